```python
import math
import jax, jax.numpy as jnp
from jax import lax
import numpy as np

D_MODEL = 1024
BATCH = 32
SEQ = 256
DEPTH = 2
DEC_BATCH = 2
DEC_SEQ = 2048
PAST_LEN = 256

GRID_W = 64
HEAD_DIM = 64
ROPE_BASE = 10000.0
N_EVEN = (DEPTH + 1) // 2
N_ODD = DEPTH // 2
MIX_WIDTH = D_MODEL
A_HEADS = D_MODEL // 128
A_KV_HEADS = A_HEADS // 4
A_GROUP = A_HEADS // A_KV_HEADS
A_WIDTH = A_HEADS * HEAD_DIM
A_KV_WIDTH = A_KV_HEADS * HEAD_DIM
WINDOW = 128
BLOCK = 128
B_HEADS = D_MODEL // 128
B_WIDTH = B_HEADS * HEAD_DIM
DECAY_LORA = 64
AAA_LORA = 64
B_SHIFT_WIDTH = 3 * B_WIDTH + DECAY_LORA + AAA_LORA
C_HEADS = D_MODEL // 128
C_QK_DIM = 64
C_V_DIM = 2 * C_QK_DIM
C_WIDTH = C_HEADS * C_V_DIM
EVEN_IN = A_WIDTH + 2 * A_KV_WIDTH + A_WIDTH + B_SHIFT_WIDTH + B_WIDTH
ODD_IN = 2 * (C_HEADS * 2 * C_QK_DIM) + C_WIDTH + C_WIDTH
NORM_EPS = 1e-6
RWKV_LN_EPS = 64e-5
SUBLN_EPS = 1e-5
NEG_INF = -1e30

kernel_name = "hybrid_diffusion_prefix_step"

F32 = jnp.float32


def split_cols(p, sizes):
    idx = np.cumsum(sizes)[:-1].tolist()
    return jnp.split(p, idx, axis=-1)


def rms_norm(x, gain, eps=NORM_EPS):
    xf = x.astype(F32)
    y = xf * lax.rsqrt(jnp.mean(xf * xf, -1, keepdims=True) + eps)
    return (y * gain.astype(F32)).astype(x.dtype)


def modulation(cvec, ada_w, ada_b):
    m = jax.nn.silu(cvec) @ ada_w + ada_b
    return split_cols(m, [D_MODEL] * 3)


def axial_rope_tables(T, d):
    nf = d // 4
    inv = 1.0 / (ROPE_BASE ** (jnp.arange(nf, dtype=F32) / nf))
    t = jnp.arange(T)
    row = (t // GRID_W).astype(F32)
    col = (t % GRID_W).astype(F32)
    ang_r = row[:, None] * inv[None]
    ang_c = col[:, None] * inv[None]
    ang = jnp.concatenate([ang_r, ang_r, ang_c, ang_c], -1)
    return jnp.cos(ang), jnp.sin(ang)


def apply_rope(x, cos, sin):
    x1, x2, x3, x4 = jnp.split(x, 4, axis=-1)
    rot = jnp.concatenate([-x2, x1, -x4, x3], -1)
    return x * cos[None, :, None, :].astype(x.dtype) + rot * sin[None, :, None, :].astype(x.dtype)


def sink_softmax(s, sink):
    m = jnp.maximum(jnp.max(s, -1, keepdims=True), sink)
    e = jnp.exp(s - m)
    return e / (jnp.sum(e, -1, keepdims=True) + jnp.exp(sink - m))


def gqa_context(q, k, v, sink):
    Bn, C = q.shape[:2]
    qg = q.reshape(Bn, C, A_KV_HEADS, A_GROUP, HEAD_DIM)
    s = jnp.einsum('bqgrd,bkgd->bgrqk', qg, k, preferred_element_type=F32) * (HEAD_DIM ** -0.5)
    p = sink_softmax(s, sink.astype(F32).reshape(1, A_KV_HEADS, A_GROUP, 1, 1))
    o = jnp.einsum('bgrqk,bkgd->bqgrd', p.astype(v.dtype), v)
    return o.reshape(Bn, C, A_WIDTH)


def gqa_window_latent(q, k, v, k_ctx, v_ctx, sink):
    Bn, T = q.shape[:2]
    nb = T // BLOCK
    C = k_ctx.shape[1]
    qb = q.reshape(Bn, nb, BLOCK, A_KV_HEADS, A_GROUP, HEAD_DIM)

    def band(t):
        tp = jnp.pad(t, ((0, 0), (BLOCK, BLOCK), (0, 0), (0, 0)))
        tp = tp.reshape(Bn, nb + 2, BLOCK, A_KV_HEADS, HEAD_DIM)
        return jnp.concatenate([tp[:, :-2], tp[:, 1:-1], tp[:, 2:]], axis=2)

    kb, vb = band(k), band(v)
    blk = jnp.arange(nb)[:, None] * BLOCK
    qpos = blk + jnp.arange(BLOCK)[None]
    kpos = blk + jnp.arange(3 * BLOCK)[None] - BLOCK
    valid = ((jnp.abs(qpos[:, :, None] - kpos[:, None, :]) <= WINDOW)
             & (kpos[:, None, :] >= 0) & (kpos[:, None, :] < T))
    scale = HEAD_DIM ** -0.5
    s_loc = jnp.einsum('bnigrd,bnjgd->bngrij', qb, kb, preferred_element_type=F32) * scale
    s_loc = jnp.where(valid[None, :, None, None], s_loc, NEG_INF)
    s_ctx = jnp.einsum('bnigrd,bcgd->bngric', qb, k_ctx, preferred_element_type=F32) * scale
    s = jnp.concatenate([s_ctx, s_loc], -1)
    p = sink_softmax(s, sink.astype(F32).reshape(1, 1, A_KV_HEADS, A_GROUP, 1, 1)).astype(v.dtype)
    o = (jnp.einsum('bngric,bcgd->bnigrd', p[..., :C], v_ctx)
         + jnp.einsum('bngrij,bnjgd->bnigrd', p[..., C:], vb))
    return o.reshape(Bn, T, A_WIDTH)


def centred_shift(p):
    z = jnp.zeros_like(p[:, :1])
    prev = jnp.concatenate([z, p[:, :-1]], 1)
    nxt = jnp.concatenate([p[:, 1:], z], 1)
    return 0.5 * (prev + nxt)


def heads_b(t):
    return t.reshape(t.shape[:-1] + (B_HEADS, HEAD_DIM))


def rwkv_scan(S0, r, decay, kk, a, k, v, reverse):
    def step(S, inp):
        r_t, w_t, kk_t, a_t, k_t, v_t = inp
        sa = jnp.einsum('bhvk,bhk->bhv', S, -kk_t)
        S = (S * w_t[:, :, None, :] + sa[..., None] * (kk_t * a_t)[:, :, None, :]
             + v_t[..., None] * k_t[:, :, None, :])
        return S, jnp.einsum('bhvk,bhk->bhv', S, r_t)

    xs = tuple(jnp.moveaxis(t, 1, 0) for t in (r, decay, kk, a, k, v))
    S, ys = lax.scan(step, S0, xs, reverse=reverse)
    return S, jnp.moveaxis(ys, 0, 1)


def rwkv_mix(p, S0_fwd, S0_bwd, mu, w0, w2, a0, a2, k_k, k_a, r_k, ln_w, ln_b):
    pf = p.astype(F32)
    xf = pf + (centred_shift(pf) - pf) * mu.astype(F32)
    r, k, v, wlo, alo = split_cols(xf, [B_WIDTH] * 3 + [DECAY_LORA, AAA_LORA])
    w = -jax.nn.softplus(-(w0.astype(F32)[:, None, None, :]
                           + jnp.einsum('btl,zlc->zbtc', jnp.tanh(wlo), w2.astype(F32)))) - 0.5
    decay = jnp.exp(-jnp.exp(w))
    a = jax.nn.sigmoid(a0.astype(F32)[:, None, None, :]
                       + jnp.einsum('btl,zlc->zbtc', alo, a2.astype(F32)))
    kk = heads_b(k * k_k.astype(F32))
    kk = kk / jnp.maximum(jnp.sqrt(jnp.sum(kk * kk, -1, keepdims=True)), 1e-12)
    k_eff = k[None] * (1.0 + (a - 1.0) * k_a.astype(F32))
    rh, vh = heads_b(r), heads_b(v)
    S_f, y_f = rwkv_scan(S0_fwd.astype(F32), rh, heads_b(decay[0]), kk, heads_b(a[0]),
                         heads_b(k_eff[0]), vh, False)
    S_b, y_b = rwkv_scan(S0_bwd.astype(F32), rh, heads_b(decay[1]), kk, heads_b(a[1]),
                         heads_b(k_eff[1]), vh, True)
    y = y_f + y_b
    mean = jnp.mean(y, -1, keepdims=True)
    var = jnp.mean(jnp.square(y - mean), -1, keepdims=True)
    y = (y - mean) * lax.rsqrt(var + RWKV_LN_EPS)
    Bn, T = p.shape[:2]
    y = y.reshape(Bn, T, B_WIDTH) * ln_w.astype(F32) + ln_b.astype(F32)
    k_bonus = heads_b(0.5 * (k_eff[0] + k_eff[1]))
    bonus = jnp.sum(rh * k_bonus * heads_b(r_k.astype(F32)), -1, keepdims=True) * vh
    y = y + bonus.reshape(Bn, T, B_WIDTH)
    return y.astype(p.dtype), jnp.stack([S_f, S_b], axis=1)


def diff_attn_context(q, k, v, lam):
    s = jnp.einsum('bqhzd,bkhzd->bhzqk', q, k, preferred_element_type=F32) * (C_QK_DIM ** -0.5)
    p = jax.nn.softmax(s, -1)
    att = p[:, :, 0] - lam * p[:, :, 1]
    return jnp.einsum('bhqk,bkhd->bqhd', att.astype(v.dtype), v)


def diff_attn_latent(q, k, v, k_ctx, v_ctx, lam):
    Bn, T = q.shape[:2]
    nb = T // BLOCK
    k_all = jnp.concatenate([k_ctx, k], 1)
    v_all = jnp.concatenate([v_ctx, v], 1)
    qb = jnp.moveaxis(q.reshape(Bn, nb, BLOCK, C_HEADS, 2, C_QK_DIM), 1, 0)

    def one_block(qblk):
        s = jnp.einsum('bqhzd,bkhzd->bhzqk', qblk, k_all, preferred_element_type=F32) * (C_QK_DIM ** -0.5)
        p = jax.nn.softmax(s, -1)
        att = p[:, :, 0] - lam * p[:, :, 1]
        return jnp.einsum('bhqk,bkhd->bqhd', att.astype(v.dtype), v_all)

    o = lax.map(one_block, qb)
    return jnp.moveaxis(o, 0, 1).reshape(Bn, T, C_HEADS, C_V_DIM)


def even_layer(x, shift, scale, gate, norm_pre, norm_post, w_out, w_in, sink, mu, w0, w2, a0, a2,
               k_k, k_a, r_k, ln_w, ln_b, ctx):
    Bn, T = x.shape[:2]
    h = rms_norm(x, norm_pre) * (1.0 + scale) + shift
    qa, ka, va, ga, pb, gb = split_cols(h @ w_in, [A_WIDTH, A_KV_WIDTH, A_KV_WIDTH, A_WIDTH,
                                                    B_SHIFT_WIDTH, B_WIDTH])
    qa = qa.reshape(Bn, T, A_HEADS, HEAD_DIM)
    ka = ka.reshape(Bn, T, A_KV_HEADS, HEAD_DIM)
    va = va.reshape(Bn, T, A_KV_HEADS, HEAD_DIM)
    if ctx is None:
        ya = gqa_context(qa, ka, va, sink)
        zeros = jnp.zeros((Bn, B_HEADS, HEAD_DIM, HEAD_DIM), F32)
        yb, st = rwkv_mix(pb, zeros, zeros, mu, w0, w2, a0, a2, k_k, k_a, r_k, ln_w, ln_b)
        new = (ka, va, st)
    else:
        k_ctx, v_ctx, st0 = ctx
        cos, sin = axial_rope_tables(T, HEAD_DIM)
        ya = gqa_window_latent(apply_rope(qa, cos, sin), apply_rope(ka, cos, sin), va,
                               k_ctx, v_ctx, sink)
        yb, _ = rwkv_mix(pb, st0[:, 0], st0[:, 1], mu, w0, w2, a0, a2, k_k, k_a, r_k, ln_w, ln_b)
        new = None
    y = jnp.concatenate([ya * jax.nn.silu(ga), yb * jax.nn.silu(gb)], -1) @ w_out
    return x + gate * rms_norm(y, norm_post), new


def odd_layer(x, shift, scale, gate, norm_pre, norm_post, w_out, w_in, lq1, lk1, lq2, lk2, subln,
              lam_init, ctx):
    Bn, T = x.shape[:2]
    h = rms_norm(x, norm_pre) * (1.0 + scale) + shift
    qc, kc, vc, gc = split_cols(h @ w_in, [C_HEADS * 2 * C_QK_DIM, C_HEADS * 2 * C_QK_DIM,
                                           C_WIDTH, C_WIDTH])
    qc = qc.reshape(Bn, T, C_HEADS, 2, C_QK_DIM)
    kc = kc.reshape(Bn, T, C_HEADS, 2, C_QK_DIM)
    vc = vc.reshape(Bn, T, C_HEADS, C_V_DIM)
    lam = (jnp.exp(jnp.sum((lq1 * lk1).astype(F32))) - jnp.exp(jnp.sum((lq2 * lk2).astype(F32)))
           + lam_init)
    if ctx is None:
        o = diff_attn_context(qc, kc, vc, lam)
        new = (kc.reshape(Bn, T, C_HEADS, 2 * C_QK_DIM), vc)
    else:
        k_ctx, v_ctx = ctx
        C = k_ctx.shape[1]
        cos, sin = axial_rope_tables(T, C_QK_DIM)
        qr = apply_rope(qc.reshape(Bn, T, C_HEADS * 2, C_QK_DIM), cos, sin).reshape(qc.shape)
        kr = apply_rope(kc.reshape(Bn, T, C_HEADS * 2, C_QK_DIM), cos, sin).reshape(kc.shape)
        o = diff_attn_latent(qr, kr, vc, k_ctx.reshape(Bn, C, C_HEADS, 2, C_QK_DIM), v_ctx, lam)
        new = None
    o = rms_norm(o, subln, SUBLN_EPS) * (1.0 - lam_init)
    y = (o.reshape(Bn, T, C_WIDTH) * jax.nn.silu(gc)) @ w_out
    return x + gate * rms_norm(y, norm_post), new


def setup_inputs(seed: int = 0) -> dict:
    key = jax.random.key(seed)
    ks = jax.random.split(key, 40)

    def nrm(k, shape, s):
        return jax.random.normal(k, shape, F32) * s

    D = D_MODEL
    return {
        "x_prompt": nrm(ks[0], (BATCH, SEQ, D), 1.0),
        "x_sample": nrm(ks[1], (DEC_BATCH, DEC_SEQ, D), 1.0),
        "cache_a_k": nrm(ks[2], (DEC_BATCH, N_EVEN, PAST_LEN, A_KV_HEADS, HEAD_DIM), 1.0),
        "cache_a_v": nrm(ks[3], (DEC_BATCH, N_EVEN, PAST_LEN, A_KV_HEADS, HEAD_DIM), 1.0),
        "state_rwkv": nrm(ks[4], (DEC_BATCH, N_EVEN, 2, B_HEADS, HEAD_DIM, HEAD_DIM), 0.5),
        "cache_c_k": nrm(ks[5], (DEC_BATCH, N_ODD, PAST_LEN, C_HEADS, 2 * C_QK_DIM), 1.0),
        "cache_c_v": nrm(ks[6], (DEC_BATCH, N_ODD, PAST_LEN, C_HEADS, C_V_DIM), 1.0),
        "c": nrm(ks[7], (DEC_BATCH, D), 1.0),
        "c_ctx": nrm(ks[8], (D,), 1.0),
        "ada_w": nrm(ks[9], (DEPTH, D, 3 * D), 0.5 * D ** -0.5),
        "ada_b": nrm(ks[10], (DEPTH, 3 * D), 0.02),
        "norm_pre": 1.0 + nrm(ks[11], (DEPTH, D), 0.05),
        "norm_post": 1.0 + nrm(ks[12], (DEPTH, D), 0.05),
        "w_out": nrm(ks[13], (DEPTH, MIX_WIDTH, D), MIX_WIDTH ** -0.5),
        "even_w_in": nrm(ks[14], (N_EVEN, D, EVEN_IN), D ** -0.5),
        "a_sink": nrm(ks[15], (N_EVEN, A_HEADS), 0.5),
        "b_mu": jax.random.uniform(ks[16], (N_EVEN, B_SHIFT_WIDTH), F32),
        "b_w0": -2.0 + nrm(ks[17], (N_EVEN, 2, B_WIDTH), 0.5),
        "b_w2": nrm(ks[18], (N_EVEN, 2, DECAY_LORA, B_WIDTH), 0.1 * DECAY_LORA ** -0.5),
        "b_a0": nrm(ks[19], (N_EVEN, 2, B_WIDTH), 0.1),
        "b_a2": nrm(ks[20], (N_EVEN, 2, AAA_LORA, B_WIDTH), AAA_LORA ** -0.5),
        "b_kk": 0.85 + nrm(ks[21], (N_EVEN, B_WIDTH), 0.05),
        "b_ka": 1.0 + nrm(ks[22], (N_EVEN, B_WIDTH), 0.05),
        "b_rk": nrm(ks[23], (N_EVEN, B_WIDTH), 0.1),
        "b_ln_w": 1.0 + nrm(ks[24], (N_EVEN, B_WIDTH), 0.05),
        "b_ln_b": nrm(ks[25], (N_EVEN, B_WIDTH), 0.02),
        "odd_w_in": nrm(ks[26], (N_ODD, D, ODD_IN), D ** -0.5),
        "c_lq1": nrm(ks[27], (N_ODD, C_QK_DIM), 0.1),
        "c_lk1": nrm(ks[28], (N_ODD, C_QK_DIM), 0.1),
        "c_lq2": nrm(ks[29], (N_ODD, C_QK_DIM), 0.1),
        "c_lk2": nrm(ks[30], (N_ODD, C_QK_DIM), 0.1),
        "c_subln": 1.0 + nrm(ks[31], (N_ODD, C_V_DIM), 0.05),
    }


def reference(x_prompt, x_sample, cache_a_k, cache_a_v, state_rwkv, cache_c_k, cache_c_v, c, c_ctx,
              ada_w, ada_b, norm_pre, norm_post, w_out, even_w_in, a_sink, b_mu, b_w0, b_w2, b_a0,
              b_a2, b_kk, b_ka, b_rk, b_ln_w, b_ln_b, odd_w_in, c_lq1, c_lk1, c_lq2, c_lk2, c_subln):
    y_p, y_s = x_prompt, x_sample
    new_ak, new_av, new_st, new_ck, new_cv = [], [], [], [], []
    for layer in range(DEPTH):
        sh_c, sc_c, g_c = modulation(c_ctx, ada_w[layer], ada_b[layer])
        sh_s, sc_s, g_s = modulation(c[:, None, :], ada_w[layer], ada_b[layer])
        common = (norm_pre[layer], norm_post[layer], w_out[layer])
        if layer % 2 == 0:
            e = layer // 2
            ep = (even_w_in[e], a_sink[e], b_mu[e], b_w0[e], b_w2[e], b_a0[e], b_a2[e], b_kk[e],
                  b_ka[e], b_rk[e], b_ln_w[e], b_ln_b[e])
            y_p, (k_a, v_a, st) = even_layer(y_p, sh_c, sc_c, g_c, *common, *ep, None)
            y_s, _ = even_layer(y_s, sh_s, sc_s, g_s, *common, *ep,
                                (cache_a_k[:, e], cache_a_v[:, e], state_rwkv[:, e]))
            new_ak.append(k_a)
            new_av.append(v_a)
            new_st.append(st)
        else:
            o = layer // 2
            lam_init = 0.8 - 0.6 * math.exp(-0.3 * layer)
            op = (odd_w_in[o], c_lq1[o], c_lk1[o], c_lq2[o], c_lk2[o], c_subln[o], lam_init)
            y_p, (k_c, v_c) = odd_layer(y_p, sh_c, sc_c, g_c, *common, *op, None)
            y_s, _ = odd_layer(y_s, sh_s, sc_s, g_s, *common, *op, (cache_c_k[:, o], cache_c_v[:, o]))
            new_ck.append(k_c)
            new_cv.append(v_c)
    new_cache_a_k = jnp.stack(new_ak, axis=1)
    new_cache_a_v = jnp.stack(new_av, axis=1)
    new_state_rwkv = jnp.stack(new_st, axis=1)
    new_cache_c_k = jnp.stack(new_ck, axis=1)
    new_cache_c_v = jnp.stack(new_cv, axis=1)
    return (y_p, y_s, new_cache_a_k, new_cache_a_v, new_state_rwkv, new_cache_c_k, new_cache_c_v)
```

```python
import functools
import math

import jax
import jax.numpy as jnp
from jax import lax
from jax.experimental import pallas as pl
from jax.experimental.pallas import tpu as pltpu

F32 = jnp.float32
BF16 = jnp.bfloat16

D_MODEL = 1024
DEPTH = 2
GRID_W = 64
HEAD_DIM = 64
ROPE_BASE = 10000.0
A_HEADS = 8
A_KV_HEADS = 2
A_WIDTH = 512
A_KV_WIDTH = 128
WINDOW = 128
BLOCK = 128
B_HEADS = 8
B_WIDTH = 512
DECAY_LORA = 64
AAA_LORA = 64
B_SHIFT_WIDTH = 3 * B_WIDTH + DECAY_LORA + AAA_LORA
C_HEADS = 8
C_QK_DIM = 64
C_V_DIM = 128
C_WIDTH = 1024
EVEN_IN = 3456
ODD_IN = 4096
NORM_EPS = 1e-6
RWKV_LN_EPS = 64e-5
SUBLN_EPS = 1e-5
NEG_INF = -1e30

LANES = 128
ROW_TILE = 256
CHUNK = 64
PAIR = 2 * CHUNK
VMEM_LIMIT = 48 * 1024 * 1024

NN = ((1,), (0,))
NT = ((1,), (1,))


def _bdot(a, b, dims=NN):
    return lax.dot_general(a.astype(BF16), b.astype(BF16), (dims, ((), ())),
                           preferred_element_type=F32)


def _hdot(a, b):
    return lax.dot_general(a, b, (NN, ((), ())), precision=lax.Precision.HIGHEST,
                           preferred_element_type=F32)


def _sigmoid(x):
    return 1.0 / (1.0 + jnp.exp(-x))


def _silu(x):
    return x * _sigmoid(x)


def _params(n_axes):
    return pltpu.CompilerParams(dimension_semantics=("arbitrary",) * n_axes,
                                vmem_limit_bytes=VMEM_LIMIT)


def _lane_lo(shape):
    return (lax.broadcasted_iota(jnp.int32, shape, len(shape) - 1) % LANES) < HEAD_DIM


def _stack_pair(x):
    lo = _lane_lo(x.shape)
    z = jnp.zeros_like(x)
    return jnp.concatenate([jnp.where(lo, x, z), jnp.where(lo, z, x)], axis=0)


def _dup_half(x, g):
    lo = _lane_lo(x.shape)
    sw = pltpu.roll(x, HEAD_DIM, 1)
    return jnp.where(lo, x, sw) if g == 0 else jnp.where(lo, sw, x)


def _rope(x, cos, sin_signed):
    out = []
    even_q = (lax.broadcasted_iota(jnp.int32, cos.shape, 1) // (HEAD_DIM // 4)) % 2 == 0
    for g in range(x.shape[1] // LANES):
        xg = x[:, g * LANES:(g + 1) * LANES]
        up = pltpu.roll(xg, LANES - HEAD_DIM // 4, 1)
        dn = pltpu.roll(xg, HEAD_DIM // 4, 1)
        out.append(xg * cos + jnp.where(even_q, up, dn) * sin_signed)
    return out


def _mod_kernel(c_ref, w_ref, b_ref, o_ref):
    s = _silu(c_ref[...])
    o_ref[...] = _hdot(s, w_ref[...]) + b_ref[...]


def _modulation(cvecs, ada_w, ada_b):
    nrow = cvecs.shape[0]
    return pl.pallas_call(
        _mod_kernel,
        grid=(DEPTH, 3),
        in_specs=[pl.BlockSpec((nrow, D_MODEL), lambda l, j: (0, 0)),
                  pl.BlockSpec((None, D_MODEL, D_MODEL), lambda l, j: (l, 0, j)),
                  pl.BlockSpec((None, 1, D_MODEL), lambda l, j: (l, 0, j))],
        out_specs=pl.BlockSpec((None, nrow, D_MODEL), lambda l, j: (l, 0, j)),
        out_shape=jax.ShapeDtypeStruct((DEPTH, nrow, 3 * D_MODEL), F32),
        compiler_params=_params(2),
        name="modulation",
    )(cvecs, ada_w, ada_b.reshape(DEPTH, 1, 3 * D_MODEL))


def _proj_in_kernel(*refs, segs, rope_segs, use_rope):
    if use_rope:
        x_ref, sh_ref, sc_ref, g_ref, w_ref, cos_ref, sin_ref = refs[:7]
        outs = refs[7:]
    else:
        x_ref, sh_ref, sc_ref, g_ref, w_ref = refs[:5]
        outs = refs[5:]
    x = x_ref[...]
    y = x * lax.rsqrt(jnp.mean(x * x, -1, keepdims=True) + NORM_EPS)
    h = (y * g_ref[...]) * (1.0 + sc_ref[...]) + sh_ref[...]
    hb = h.astype(BF16)
    for idx, (lo, hi) in enumerate(segs):
        o = jnp.dot(hb, w_ref[:, lo:hi], preferred_element_type=F32)
        if use_rope and idx in rope_segs:
            parts = _rope(o, cos_ref[...], sin_ref[...])
            for g, part in enumerate(parts):
                outs[idx][:, g * LANES:(g + 1) * LANES] = part
        else:
            outs[idx][...] = o


def _proj_in(x2d, shift, scale, gain, w_bf16, segs, rope_segs, tiles_per_seq, rope_tabs):
    n = x2d.shape[0]
    nt = n // ROW_TILE
    use_rope = rope_tabs is not None
    row = lambda i: (i, 0)
    per_b = lambda i: (i // tiles_per_seq, 0, 0)
    in_specs = [pl.BlockSpec((ROW_TILE, D_MODEL), row),
                pl.BlockSpec((None, 1, D_MODEL), per_b),
                pl.BlockSpec((None, 1, D_MODEL), per_b),
                pl.BlockSpec((1, D_MODEL), lambda i: (0, 0)),
                pl.BlockSpec(w_bf16.shape, lambda i: (0, 0))]
    args = [x2d, shift, scale, gain, w_bf16]
    if use_rope:
        pos = lambda i: (i % tiles_per_seq, 0)
        in_specs += [pl.BlockSpec((ROW_TILE, LANES), pos), pl.BlockSpec((ROW_TILE, LANES), pos)]
        args += list(rope_tabs)
    return pl.pallas_call(
        functools.partial(_proj_in_kernel, segs=segs, rope_segs=rope_segs, use_rope=use_rope),
        grid=(nt,),
        in_specs=in_specs,
        out_specs=[pl.BlockSpec((ROW_TILE, hi - lo), row) for lo, hi in segs],
        out_shape=[jax.ShapeDtypeStruct((n, hi - lo), F32) for lo, hi in segs],
        compiler_params=_params(1),
        name="proj_in",
    )(*args)


def _sink_softmax_pair(s, sink_a, sink_b, n):
    rowi = lax.broadcasted_iota(jnp.int32, (2 * n, 1), 0)
    sink = jnp.where(rowi < n, sink_a, sink_b)
    m = jnp.maximum(jnp.max(s, -1, keepdims=True), sink)
    e = jnp.exp(s - m)
    return e / (jnp.sum(e, -1, keepdims=True) + jnp.exp(sink - m))


def _gqa_ctx_kernel(sink_ref, q_ref, k_ref, v_ref, o_ref):
    t = q_ref.shape[0]
    k = k_ref[...]
    v = v_ref[...]
    for p in range(A_HEADS // 2):
        g = p // 2
        qs = _stack_pair(q_ref[:, p * LANES:(p + 1) * LANES] * (HEAD_DIM ** -0.5))
        s = _bdot(qs, _dup_half(k, g), NT)
        pr = _sink_softmax_pair(s, sink_ref[2 * p], sink_ref[2 * p + 1], t)
        o = _bdot(pr, _dup_half(v, g))
        o_ref[:, p * LANES:(p + 1) * LANES] = jnp.where(_lane_lo((t, LANES)), o[:t], o[t:])


def _gqa_context(sink, q, k, v, bn, t):
    seq = lambda b: (b, 0, 0)
    return pl.pallas_call(
        _gqa_ctx_kernel,
        grid=(bn,),
        in_specs=[pl.BlockSpec(memory_space=pltpu.SMEM),
                  pl.BlockSpec((None, t, A_WIDTH), seq),
                  pl.BlockSpec((None, t, A_KV_WIDTH), seq),
                  pl.BlockSpec((None, t, A_KV_WIDTH), seq)],
        out_specs=pl.BlockSpec((None, t, A_WIDTH), seq),
        out_shape=jax.ShapeDtypeStruct((bn, t, A_WIDTH), F32),
        compiler_params=_params(1),
        name="gqa_context",
    )(sink, q.reshape(bn, t, A_WIDTH), k.reshape(bn, t, A_KV_WIDTH), v.reshape(bn, t, A_KV_WIDTH))


def _gqa_win_kernel(sink_ref, q_ref, kp_ref, kc_ref, kn_ref, vp_ref, vc_ref, vn_ref,
                    kx_ref, vx_ref, o_ref, *, t):
    nblk = pl.program_id(1)
    nctx = kx_ref.shape[0]
    keys = jnp.concatenate([kx_ref[...], kp_ref[...], kc_ref[...], kn_ref[...]], axis=0)
    vals = jnp.concatenate([vx_ref[...], vp_ref[...], vc_ref[...], vn_ref[...]], axis=0)
    nkeys = nctx + 3 * BLOCK
    rowi = lax.broadcasted_iota(jnp.int32, (2 * BLOCK, nkeys), 0)
    coli = lax.broadcasted_iota(jnp.int32, (2 * BLOCK, nkeys), 1)
    qpos = nblk * BLOCK + rowi % BLOCK
    kpos = nblk * BLOCK + coli - nctx - BLOCK
    valid = (coli < nctx) | ((jnp.abs(qpos - kpos) <= WINDOW) & (kpos >= 0) & (kpos < t))
    for p in range(A_HEADS // 2):
        g = p // 2
        qs = _stack_pair(q_ref[:, p * LANES:(p + 1) * LANES] * (HEAD_DIM ** -0.5))
        s = _bdot(qs, _dup_half(keys, g), NT)
        s = jnp.where(valid, s, NEG_INF)
        pr = _sink_softmax_pair(s, sink_ref[2 * p], sink_ref[2 * p + 1], BLOCK)
        o = _bdot(pr, _dup_half(vals, g))
        o_ref[:, p * LANES:(p + 1) * LANES] = jnp.where(_lane_lo((BLOCK, LANES)), o[:BLOCK], o[BLOCK:])


def _gqa_window(sink, q, k, v, k_ctx, v_ctx, bn, t):
    nb = t // BLOCK
    nctx = k_ctx.shape[1]
    cur = lambda b, n: (b, n, 0)
    prev = lambda b, n: (b, jnp.maximum(n - 1, 0), 0)
    nxt = lambda b, n: (b, jnp.minimum(n + 1, nb - 1), 0)
    ctx = lambda b, n: (b, 0, 0)
    kv = lambda f: pl.BlockSpec((None, BLOCK, A_KV_WIDTH), f)
    k3 = k.reshape(bn, t, A_KV_WIDTH)
    v3 = v.reshape(bn, t, A_KV_WIDTH)
    return pl.pallas_call(
        functools.partial(_gqa_win_kernel, t=t),
        grid=(bn, nb),
        in_specs=[pl.BlockSpec(memory_space=pltpu.SMEM),
                  pl.BlockSpec((None, BLOCK, A_WIDTH), cur),
                  kv(prev), kv(cur), kv(nxt), kv(prev), kv(cur), kv(nxt),
                  pl.BlockSpec((None, nctx, A_KV_WIDTH), ctx),
                  pl.BlockSpec((None, nctx, A_KV_WIDTH), ctx)],
        out_specs=pl.BlockSpec((None, BLOCK, A_WIDTH), cur),
        out_shape=jax.ShapeDtypeStruct((bn, t, A_WIDTH), F32),
        compiler_params=_params(2),
        name="gqa_window",
    )(sink, q.reshape(bn, t, A_WIDTH), k3, k3, k3, v3, v3, v3, k_ctx, v_ctx)


def _rwkv_feat_kernel(p_ref, pp_ref, pn_ref, mu_ref, lora_ref, w0_ref, a0_ref, kk_ref, ka_ref,
                      rk_ref, ones_ref, r_out, kkn_out, v_out, lw_out, b_out, ke_out, bonus_out,
                      *, tiles_per_seq):
    i = pl.program_id(0)
    tm = p_ref.shape[0]
    pos = i % tiles_per_seq
    p = p_ref[...]
    rowi = lax.broadcasted_iota(jnp.int32, (tm, 1), 0)
    prev_row = jnp.where(pos != 0, pp_ref[7:8, :], 0.0)
    next_row = jnp.where(pos != tiles_per_seq - 1, pn_ref[0:1, :], 0.0)
    prev = jnp.where(rowi == 0, prev_row, pltpu.roll(p, 1, 0))
    nxt = jnp.where(rowi == tm - 1, next_row, pltpu.roll(p, tm - 1, 0))
    xf = p + (0.5 * (prev + nxt) - p) * mu_ref[...]
    r = xf[:, 0:B_WIDTH]
    k = xf[:, B_WIDTH:2 * B_WIDTH]
    v = xf[:, 2 * B_WIDTH:3 * B_WIDTH]
    g = xf[:, 3 * B_WIDTH:]
    g = jnp.where(_lane_lo(g.shape), jnp.tanh(g), g)
    lo = _bdot(g, lora_ref[...])
    ones = ones_ref[...]
    kkr = k * kk_ref[...]
    kkn = kkr / jnp.maximum(jnp.sqrt(_hdot(kkr * kkr, ones)), 1e-12)
    ke_sum = jnp.zeros_like(k)
    for z in range(2):
        wz = w0_ref[z] + lo[:, z * B_WIDTH:(z + 1) * B_WIDTH]
        sp = jnp.maximum(-wz, 0.0) + jnp.log(1.0 + jnp.exp(-jnp.abs(wz)))
        lw_out[z] = -jnp.exp(-sp - 0.5)
        az = _sigmoid(a0_ref[z] + lo[:, (2 + z) * B_WIDTH:(3 + z) * B_WIDTH])
        ke = k * (1.0 + (az - 1.0) * ka_ref[...])
        ke_out[z] = ke
        b_out[z] = kkn * az
        ke_sum = ke_sum + ke
    r_out[...] = r
    kkn_out[...] = kkn
    v_out[...] = v
    bonus_out[...] = _hdot(r * (0.5 * ke_sum) * rk_ref[...], ones) * v


def _rwkv_features(pb, mu, lora, w0, a0, k_k, k_a, r_k, ones, tiles_per_seq):
    n = pb.shape[0]
    nt = n // ROW_TILE
    h8 = ROW_TILE // 8
    row = lambda i: (i, 0)
    const2 = lambda i: (0, 0)
    const3 = lambda i: (0, 0, 0)
    vec = pl.BlockSpec((1, B_WIDTH), const2)
    one = jax.ShapeDtypeStruct((n, B_WIDTH), F32)
    two = jax.ShapeDtypeStruct((2, n, B_WIDTH), F32)
    ospec1 = pl.BlockSpec((ROW_TILE, B_WIDTH), row)
    ospec2 = pl.BlockSpec((2, ROW_TILE, B_WIDTH), lambda i: (0, i, 0))
    return pl.pallas_call(
        functools.partial(_rwkv_feat_kernel, tiles_per_seq=tiles_per_seq),
        grid=(nt,),
        in_specs=[pl.BlockSpec((ROW_TILE, B_SHIFT_WIDTH), row),
                  pl.BlockSpec((8, B_SHIFT_WIDTH), lambda i: (jnp.maximum(i * h8 - 1, 0), 0)),
                  pl.BlockSpec((8, B_SHIFT_WIDTH), lambda i: (jnp.minimum((i + 1) * h8, n // 8 - 1), 0)),
                  pl.BlockSpec((1, B_SHIFT_WIDTH), const2),
                  pl.BlockSpec(lora.shape, const2),
                  pl.BlockSpec((2, 1, B_WIDTH), const3),
                  pl.BlockSpec((2, 1, B_WIDTH), const3),
                  vec, vec, vec,
                  pl.BlockSpec((B_WIDTH, B_WIDTH), const2)],
        out_specs=[ospec1, ospec1, ospec1, ospec2, ospec2, ospec2, ospec1],
        out_shape=[one, one, one, two, two, two, one],
        compiler_params=_params(1),
        name="rwkv_features",
    )(pb, pb, pb, mu, lora, w0, a0, k_k, k_a, r_k, ones)


def _rwkv_scan_kernel(r_ref, kk_ref, v_ref, lw_ref, b_ref, ke_ref, s0_ref, y_ref, sout_ref, s_scr,
                      *, n_chunks):
    z = pl.program_id(1)
    c = pl.program_id(2)
    rev = z == 1

    @pl.when(c == 0)
    def _():
        s_scr[...] = s0_ref[...]

    row = lax.broadcasted_iota(jnp.int32, (PAIR, PAIR), 0)
    col = lax.broadcasted_iota(jnp.int32, (PAIR, PAIR), 1)
    diff = jnp.where(rev, col - row, row - col)
    same_head = (row // CHUNK) == (col // CHUNK)
    before = (diff > 0) & same_head
    before_eq = (diff >= 0) & same_head
    blk16 = (row // 16) == (col // 16)
    blk32 = (row // 32) == (col // 32)
    eye = jnp.where(row == col, 1.0, 0.0)
    tr = lax.broadcasted_iota(jnp.int32, (CHUNK, CHUNK), 0)
    tc = lax.broadcasted_iota(jnp.int32, (CHUNK, CHUNK), 1)
    tri = jnp.where(jnp.where(rev, tc - tr, tr - tc) >= 0, 1.0, 0.0)

    lw_all = lw_ref[...]
    cum_all = _hdot(tri, lw_all)
    for p in range(B_HEADS // 2):
        sl = slice(p * LANES, (p + 1) * LANES)
        lw = lw_all[:, sl]
        cum = cum_all[:, sl]
        tot = jnp.where(rev, cum[0:1], cum[CHUNK - 1:CHUNK])
        r = r_ref[:, sl]
        kk = kk_ref[:, sl]
        b = b_ref[:, sl]
        ke = ke_ref[:, sl]
        g_inv = jnp.exp(-cum)
        g_rest = jnp.exp(tot - cum)
        kkd_s = _stack_pair(kk * jnp.exp(cum - lw))
        rd_s = _stack_pair(r * jnp.exp(cum))
        bi_s = _stack_pair(b * g_inv)
        ki_s = _stack_pair(ke * g_inv)
        bg_s = _stack_pair(b * g_rest)
        kg_s = _stack_pair(ke * g_rest)
        v_s = _stack_pair(v_ref[:, sl])

        sc = _bdot(jnp.concatenate([kkd_s, rd_s], 0), jnp.concatenate([bi_s, ki_s], 0), NT)
        lb = jnp.where(before, sc[:PAIR, :PAIR], 0.0)
        lk = jnp.where(before, sc[:PAIR, PAIR:], 0.0)
        mb = jnp.where(before_eq, sc[PAIR:, :PAIR], 0.0)
        mk = jnp.where(before_eq, sc[PAIR:, PAIR:], 0.0)

        l16 = jnp.where(blk16, lb, 0.0)
        tinv = eye - l16
        pw = l16
        for _ in range(3):
            pw = _bdot(pw, pw)
            tinv = tinv + _bdot(tinv, pw)
        tinv = tinv - _bdot(_bdot(tinv, jnp.where(blk32 & ~blk16, lb, 0.0)), tinv)
        tinv = tinv - _bdot(_bdot(tinv, jnp.where(~blk32, lb, 0.0)), tinv)

        lv = _bdot(jnp.concatenate([lk, mk], 0), v_s)
        w12 = _bdot(tinv, jnp.concatenate([kkd_s, lv[:PAIR]], 1))

        s_old = s_scr[p]
        ws = _bdot(jnp.concatenate([w12[:, :LANES], rd_s], 0), s_old, NT)
        u = -(ws[:PAIR] + w12[:, LANES:])
        ys = ws[PAIR:] + _bdot(mb, u) + lv[PAIR:]
        y_ref[:, sl] = ys[:CHUNK] + ys[CHUNK:]
        uv_t = jnp.concatenate([u, v_s], 0).T
        s_scr[p] = s_old * jnp.exp(tot) + _bdot(uv_t, jnp.concatenate([bg_s, kg_s], 0))

    @pl.when(c == n_chunks - 1)
    def _():
        sout_ref[...] = s_scr[...]


def _rwkv_scan(r, kkn, v, lw, b, ke, s0, bn, t):
    nc = t // CHUNK
    npair = B_HEADS // 2
    cidx = lambda z, c: c + z * (nc - 1 - 2 * c)
    shared = pl.BlockSpec((None, CHUNK, B_WIDTH), lambda bb, z, c: (bb, cidx(z, c), 0))
    perdir = pl.BlockSpec((None, None, CHUNK, B_WIDTH), lambda bb, z, c: (z, bb, cidx(z, c), 0))
    state = pl.BlockSpec((None, None, npair, PAIR, PAIR), lambda bb, z, c: (bb, z, 0, 0, 0))
    r3, k3, v3 = (a.reshape(bn, t, B_WIDTH) for a in (r, kkn, v))
    lw4, b4, ke4 = (a.reshape(2, bn, t, B_WIDTH) for a in (lw, b, ke))
    return pl.pallas_call(
        functools.partial(_rwkv_scan_kernel, n_chunks=nc),
        grid=(bn, 2, nc),
        in_specs=[shared, shared, shared, perdir, perdir, perdir, state],
        out_specs=[perdir, state],
        out_shape=[jax.ShapeDtypeStruct((2, bn, t, B_WIDTH), F32),
                   jax.ShapeDtypeStruct((bn, 2, npair, PAIR, PAIR), F32)],
        scratch_shapes=[pltpu.VMEM((npair, PAIR, PAIR), F32)],
        compiler_params=_params(3),
        name="rwkv_scan",
    )(r3, k3, v3, lw4, b4, ke4, s0)


def _post(x, z, gate, gain):
    zn = z * lax.rsqrt(jnp.mean(z * z, -1, keepdims=True) + NORM_EPS)
    return x + gate * (zn * gain)


def _even_out_kernel(ya_ref, ga_ref, yf_ref, yb_ref, bonus_ref, gb_ref, lnw_ref, lnb_ref, ones_ref,
                     x_ref, gate_ref, gain_ref, w_ref, o_ref):
    y = yf_ref[...] + yb_ref[...]
    avg = ones_ref[...] * (1.0 / HEAD_DIM)
    mean = _hdot(y, avg)
    d = y - mean
    var = _hdot(d * d, avg)
    yb = (d * lax.rsqrt(var + RWKV_LN_EPS)) * lnw_ref[...] + lnb_ref[...] + bonus_ref[...]
    za = (ya_ref[...] * _silu(ga_ref[...])).astype(BF16)
    zb = (yb * _silu(gb_ref[...])).astype(BF16)
    z = (jnp.dot(za, w_ref[0:A_WIDTH, :], preferred_element_type=F32)
         + jnp.dot(zb, w_ref[A_WIDTH:, :], preferred_element_type=F32))
    o_ref[...] = _post(x_ref[...], z, gate_ref[...], gain_ref[...])


def _even_out(ya, ga, y2, bonus, gb, ln_w, ln_b, ones, x2d, gate, gain, w_bf16, tiles_per_seq):
    n = x2d.shape[0]
    row = lambda i: (i, 0)
    const2 = lambda i: (0, 0)
    half = pl.BlockSpec((ROW_TILE, B_WIDTH), row)
    full = pl.BlockSpec((ROW_TILE, D_MODEL), row)
    vec = pl.BlockSpec((1, B_WIDTH), const2)
    return pl.pallas_call(
        _even_out_kernel,
        grid=(n // ROW_TILE,),
        in_specs=[half, half,
                  pl.BlockSpec((None, ROW_TILE, B_WIDTH), lambda i: (0, i, 0)),
                  pl.BlockSpec((None, ROW_TILE, B_WIDTH), lambda i: (1, i, 0)),
                  half, half, vec, vec,
                  pl.BlockSpec((B_WIDTH, B_WIDTH), const2),
                  full,
                  pl.BlockSpec((None, 1, D_MODEL), lambda i: (i // tiles_per_seq, 0, 0)),
                  pl.BlockSpec((1, D_MODEL), const2),
                  pl.BlockSpec((D_MODEL, D_MODEL), const2)],
        out_specs=full,
        out_shape=jax.ShapeDtypeStruct((n, D_MODEL), F32),
        compiler_params=_params(1),
        name="even_out",
    )(ya, ga, y2, y2, bonus, gb, ln_w, ln_b, ones, x2d, gate, gain, w_bf16)


def _odd_out_kernel(o_ref_in, gc_ref, x_ref, gate_ref, gain_ref, w_ref, o_ref):
    zc = (o_ref_in[...] * _silu(gc_ref[...])).astype(BF16)
    z = jnp.dot(zc, w_ref[...], preferred_element_type=F32)
    o_ref[...] = _post(x_ref[...], z, gate_ref[...], gain_ref[...])


def _odd_out(o, gc, x2d, gate, gain, w_bf16, tiles_per_seq):
    n = x2d.shape[0]
    row = lambda i: (i, 0)
    const2 = lambda i: (0, 0)
    full = pl.BlockSpec((ROW_TILE, D_MODEL), row)
    return pl.pallas_call(
        _odd_out_kernel,
        grid=(n // ROW_TILE,),
        in_specs=[full, full, full,
                  pl.BlockSpec((None, 1, D_MODEL), lambda i: (i // tiles_per_seq, 0, 0)),
                  pl.BlockSpec((1, D_MODEL), const2),
                  pl.BlockSpec((D_MODEL, D_MODEL), const2)],
        out_specs=full,
        out_shape=jax.ShapeDtypeStruct((n, D_MODEL), F32),
        compiler_params=_params(1),
        name="odd_out",
    )(o, gc, x2d, gate, gain, w_bf16)


def _lambda(lq1_ref, lk1_ref, lq2_ref, lk2_ref, lam_init):
    s1 = jnp.sum(lq1_ref[...] * lk1_ref[...], -1, keepdims=True)
    s2 = jnp.sum(lq2_ref[...] * lk2_ref[...], -1, keepdims=True)
    return jnp.exp(s1) - jnp.exp(s2) + lam_init


def _subln(o, gain, lam_init):
    on = o * lax.rsqrt(jnp.mean(o * o, -1, keepdims=True) + SUBLN_EPS)
    return (on * gain) * (1.0 - lam_init)


def _softmax_rows(s):
    m = jnp.max(s, -1, keepdims=True)
    e = jnp.exp(s - m)
    return e / jnp.sum(e, -1, keepdims=True)


def _diff_ctx_kernel(lq1_ref, lk1_ref, lq2_ref, lk2_ref, sub_ref, q_ref, k_ref, v_ref, o_ref,
                     *, lam_init):
    t = q_ref.shape[0]
    lam = _lambda(lq1_ref, lk1_ref, lq2_ref, lk2_ref, lam_init)
    for h in range(C_HEADS):
        sl = slice(h * LANES, (h + 1) * LANES)
        qs = _stack_pair(q_ref[:, sl] * (C_QK_DIM ** -0.5))
        pr = _softmax_rows(_bdot(qs, k_ref[:, sl], NT))
        att = pr[:t] - lam * pr[t:]
        o = _bdot(att, v_ref[:, sl])
        o_ref[:, sl] = _subln(o, sub_ref[...], lam_init)


def _diff_context(lams, subln, q, k, v, bn, t, lam_init):
    seq = pl.BlockSpec((None, t, C_WIDTH), lambda b: (b, 0, 0))
    small = pl.BlockSpec((1, C_QK_DIM), lambda b: (0, 0))
    return pl.pallas_call(
        functools.partial(_diff_ctx_kernel, lam_init=lam_init),
        grid=(bn,),
        in_specs=[small, small, small, small, pl.BlockSpec((1, C_V_DIM), lambda b: (0, 0)),
                  seq, seq, seq],
        out_specs=seq,
        out_shape=jax.ShapeDtypeStruct((bn, t, C_WIDTH), F32),
        compiler_params=_params(1),
        name="diff_context",
    )(*lams, subln, q.reshape(bn, t, C_WIDTH), k.reshape(bn, t, C_WIDTH), v.reshape(bn, t, C_WIDTH))


def _diff_lat_kernel(lq1_ref, lk1_ref, lq2_ref, lk2_ref, sub_ref, q_ref, k_ref, v_ref, kx_ref, vx_ref,
                     o_ref, *, lam_init):
    tq = q_ref.shape[0]
    lam = _lambda(lq1_ref, lk1_ref, lq2_ref, lk2_ref, lam_init)
    qs = _stack_pair(q_ref[...] * (C_QK_DIM ** -0.5))
    sx = _bdot(qs, kx_ref[...], NT)
    sl = _bdot(qs, k_ref[...], NT)
    m = jnp.maximum(jnp.max(sx, -1, keepdims=True), jnp.max(sl, -1, keepdims=True))
    ex = jnp.exp(sx - m)
    el = jnp.exp(sl - m)
    den = jnp.sum(ex, -1, keepdims=True) + jnp.sum(el, -1, keepdims=True)
    px = ex / den
    plc = el / den
    ax = px[:tq] - lam * px[tq:]
    al = plc[:tq] - lam * plc[tq:]
    o = _bdot(ax, vx_ref[...]) + _bdot(al, v_ref[...])
    o_ref[...] = _subln(o, sub_ref[...], lam_init)


def _diff_latent(lams, subln, q, k, v, k_ctx, v_ctx, bn, t, lam_init):
    nb = t // BLOCK
    nctx = k_ctx.shape[1]
    small = pl.BlockSpec((1, C_QK_DIM), lambda b, h, n: (0, 0))
    qspec = pl.BlockSpec((None, BLOCK, LANES), lambda b, h, n: (b, n, h))
    kvspec = pl.BlockSpec((None, t, LANES), lambda b, h, n: (b, 0, h))
    cxspec = pl.BlockSpec((None, nctx, LANES), lambda b, h, n: (b, 0, h))
    return pl.pallas_call(
        functools.partial(_diff_lat_kernel, lam_init=lam_init),
        grid=(bn, C_HEADS, nb),
        in_specs=[small, small, small, small,
                  pl.BlockSpec((1, C_V_DIM), lambda b, h, n: (0, 0)),
                  qspec, kvspec, kvspec, cxspec, cxspec],
        out_specs=qspec,
        out_shape=jax.ShapeDtypeStruct((bn, t, C_WIDTH), F32),
        compiler_params=_params(3),
        name="diff_latent",
    )(*lams, subln, q.reshape(bn, t, C_WIDTH), k.reshape(bn, t, C_WIDTH), v.reshape(bn, t, C_WIDTH),
      k_ctx, v_ctx)


def _rope_tables(t):
    nf = HEAD_DIM // 4
    inv = 1.0 / (ROPE_BASE ** (jnp.arange(nf, dtype=F32) / nf))
    pos = jnp.arange(t)
    ang_r = (pos // GRID_W).astype(F32)[:, None] * inv[None]
    ang_c = (pos % GRID_W).astype(F32)[:, None] * inv[None]
    ang = jnp.concatenate([ang_r, ang_r, ang_c, ang_c], -1)
    cos, sin = jnp.cos(ang), jnp.sin(ang)
    sign = jnp.where((jnp.arange(HEAD_DIM) // nf) % 2 == 0, -1.0, 1.0).astype(F32)
    return jnp.tile(cos, (1, 2)), jnp.tile(sin * sign[None], (1, 2))


EVEN_SEGS = ((0, 512), (512, 640), (640, 768), (768, 1280), (1280, 2944), (2944, 3456))
ODD_SEGS = ((0, 1024), (1024, 2048), (2048, 3072), (3072, 4096))


def _pair_states(st):
    bn = st.shape[0]
    s = st.reshape(bn, 2, 4, 2, HEAD_DIM, HEAD_DIM)
    z = jnp.zeros_like(s[:, :, :, 0])
    top = jnp.concatenate([s[:, :, :, 0], z], -1)
    bot = jnp.concatenate([z, s[:, :, :, 1]], -1)
    return jnp.concatenate([top, bot], -2)


def _unpair_states(sp):
    bn = sp.shape[0]
    a = sp[:, :, :, :HEAD_DIM, :HEAD_DIM]
    b = sp[:, :, :, HEAD_DIM:, HEAD_DIM:]
    return jnp.stack([a, b], axis=3).reshape(bn, 2, B_HEADS, HEAD_DIM, HEAD_DIM)


def kernel(x_prompt, x_sample, cache_a_k, cache_a_v, state_rwkv, cache_c_k, cache_c_v, c, c_ctx,
           ada_w, ada_b, norm_pre, norm_post, w_out, even_w_in, a_sink, b_mu, b_w0, b_w2, b_a0,
           b_a2, b_kk, b_ka, b_rk, b_ln_w, b_ln_b, odd_w_in, c_lq1, c_lk1, c_lq2, c_lk2, c_subln):
    bp, tp, _ = x_prompt.shape
    bs, ts, _ = x_sample.shape
    past = cache_a_k.shape[2]

    cvecs = jnp.concatenate([c_ctx[None], c, jnp.zeros((8 - 1 - bs, D_MODEL), F32)], 0)
    mods = _modulation(cvecs, ada_w, ada_b)
    rope_tabs = _rope_tables(ts)
    ones = (jnp.arange(B_WIDTH)[:, None] // HEAD_DIM == jnp.arange(B_WIDTH)[None] // HEAD_DIM).astype(F32)

    xp = x_prompt.reshape(bp * tp, D_MODEL)
    xs = x_sample.reshape(bs * ts, D_MODEL)

    def mod_rows(layer, rows):
        m = mods[layer, rows[0]:rows[1]]
        if rows == (0, 1):
            m = jnp.broadcast_to(m, (bp, 3 * D_MODEL))
        m = m[:, None, :]
        return m[..., :D_MODEL], m[..., D_MODEL:2 * D_MODEL], m[..., 2 * D_MODEL:]

    e = 0
    w_in0 = even_w_in[e].astype(BF16)
    w_out0 = w_out[0].astype(BF16)
    gain_pre = norm_pre[0][None]
    gain_post = norm_post[0][None]
    zpad = jnp.zeros((DECAY_LORA, B_WIDTH), F32)
    lora = jnp.concatenate([
        jnp.concatenate([b_w2[e, 0], b_w2[e, 1], zpad, zpad], 1),
        jnp.concatenate([zpad, zpad, b_a2[e, 0], b_a2[e, 1]], 1)], 0).astype(BF16)
    feat_args = (b_mu[e][None], lora, b_w0[e][:, None, :], b_a0[e][:, None, :], b_kk[e][None],
                 b_ka[e][None], b_rk[e][None], ones)

    def even_layer(x2d, bn, t, rows, ctx):
        tps = t // ROW_TILE
        shift, scale, gate = mod_rows(0, rows)
        qa, ka, va, ga, pb, gb = _proj_in(x2d, shift, scale, gain_pre, w_in0, EVEN_SEGS, (0, 1), tps,
                                          None if ctx is None else rope_tabs)
        if ctx is None:
            ya = _gqa_context(a_sink[e], qa, ka, va, bn, t)
            s0 = jnp.zeros((bn, 2, B_HEADS // 2, PAIR, PAIR), F32)
        else:
            k_ctx, v_ctx, st0 = ctx
            ya = _gqa_window(a_sink[e], qa, ka, va, k_ctx, v_ctx, bn, t)
            s0 = _pair_states(st0)
        r, kkn, v, lw, b, ke, bonus = _rwkv_features(pb, *feat_args, tps)
        y2, s_fin = _rwkv_scan(r, kkn, v, lw, b, ke, s0, bn, t)
        y = _even_out(ya.reshape(bn * t, A_WIDTH), ga, y2.reshape(2, bn * t, B_WIDTH), bonus, gb,
                      b_ln_w[e][None], b_ln_b[e][None], ones, x2d, gate, gain_post, w_out0, tps)
        return y, ka, va, s_fin

    yp, ka_p, va_p, st_p = even_layer(xp, bp, tp, (0, 1), None)
    ctx_a = (cache_a_k[:, e].reshape(bs, past, A_KV_WIDTH), cache_a_v[:, e].reshape(bs, past, A_KV_WIDTH),
             state_rwkv[:, e])
    ys, _, _, _ = even_layer(xs, bs, ts, (1, 1 + bs), ctx_a)

    o = 0
    lam_init = 0.8 - 0.6 * math.exp(-0.3 * 1)
    w_in1 = odd_w_in[o].astype(BF16)
    w_out1 = w_out[1].astype(BF16)
    gain_pre1 = norm_pre[1][None]
    gain_post1 = norm_post[1][None]
    lams = (c_lq1[o][None], c_lk1[o][None], c_lq2[o][None], c_lk2[o][None])
    subln = c_subln[o][None]

    def odd_layer(x2d, bn, t, rows, ctx):
        tps = t // ROW_TILE
        shift, scale, gate = mod_rows(1, rows)
        qc, kc, vc, gc = _proj_in(x2d, shift, scale, gain_pre1, w_in1, ODD_SEGS, (0, 1), tps,
                                  None if ctx is None else rope_tabs)
        if ctx is None:
            oc = _diff_context(lams, subln, qc, kc, vc, bn, t, lam_init)
        else:
            k_ctx, v_ctx = ctx
            oc = _diff_latent(lams, subln, qc, kc, vc, k_ctx, v_ctx, bn, t, lam_init)
        y = _odd_out(oc.reshape(bn * t, C_WIDTH), gc, x2d, gate, gain_post1, w_out1, tps)
        return y, kc, vc

    yp2, kc_p, vc_p = odd_layer(yp, bp, tp, (0, 1), None)
    ctx_c = (cache_c_k[:, o].reshape(bs, past, C_WIDTH), cache_c_v[:, o].reshape(bs, past, C_WIDTH))
    ys2, _, _ = odd_layer(ys, bs, ts, (1, 1 + bs), ctx_c)

    return (yp2.reshape(bp, tp, D_MODEL),
            ys2.reshape(bs, ts, D_MODEL),
            ka_p.reshape(bp, 1, tp, A_KV_HEADS, HEAD_DIM),
            va_p.reshape(bp, 1, tp, A_KV_HEADS, HEAD_DIM),
            _unpair_states(st_p)[:, None],
            kc_p.reshape(bp, 1, tp, C_HEADS, 2 * C_QK_DIM),
            vc_p.reshape(bp, 1, tp, C_HEADS, C_V_DIM))
```

```python
import functools
import math

import jax
import jax.numpy as jnp
from jax import lax
from jax.experimental import pallas as pl
from jax.experimental.pallas import tpu as pltpu

F32 = jnp.float32
BF16 = jnp.bfloat16

D_MODEL = 1024
DEPTH = 2
GRID_W = 64
HEAD_DIM = 64
ROPE_BASE = 10000.0
A_HEADS = 8
A_KV_HEADS = 2
A_WIDTH = 512
A_KV_WIDTH = 128
WINDOW = 128
BLOCK = 128
B_HEADS = 8
B_WIDTH = 512
DECAY_LORA = 64
AAA_LORA = 64
B_SHIFT_WIDTH = 3 * B_WIDTH + DECAY_LORA + AAA_LORA
C_HEADS = 8
C_QK_DIM = 64
C_V_DIM = 128
C_WIDTH = 1024
EVEN_IN = 3456
ODD_IN = 4096
NORM_EPS = 1e-6
RWKV_LN_EPS = 64e-5
SUBLN_EPS = 1e-5
NEG_INF = -1e30

LANES = 128
ROW_TILE = 256
DIFF_Q_TILE = 256
CHUNK = 64
PAIR = 2 * CHUNK
VMEM_LIMIT = 48 * 1024 * 1024

LOG2E = math.log2(math.e)

NN = ((1,), (0,))
NT = ((1,), (1,))


def _bdot(a, b, dims=NN):
    return lax.dot_general(a.astype(BF16), b.astype(BF16), (dims, ((), ())),
                           preferred_element_type=F32)


def _hdot(a, b):
    return lax.dot_general(a, b, (NN, ((), ())), precision=lax.Precision.HIGHEST,
                           preferred_element_type=F32)


def _sigmoid(x):
    return 1.0 / (1.0 + jnp.exp(-x))


def _silu(x):
    return x * _sigmoid(x)


def _params(n_axes):
    return pltpu.CompilerParams(dimension_semantics=("arbitrary",) * n_axes,
                                vmem_limit_bytes=VMEM_LIMIT)


def _lane_lo(shape):
    return (lax.broadcasted_iota(jnp.int32, shape, len(shape) - 1) % LANES) < HEAD_DIM


def _stack_pair(x):
    lo = _lane_lo(x.shape)
    z = jnp.zeros_like(x)
    return jnp.concatenate([jnp.where(lo, x, z), jnp.where(lo, z, x)], axis=0)


def _dup_half(x, g):
    lo = _lane_lo(x.shape)
    sw = pltpu.roll(x, HEAD_DIM, 1)
    return jnp.where(lo, x, sw) if g == 0 else jnp.where(lo, sw, x)


def _rope(x, cos, sin_signed):
    out = []
    even_q = (lax.broadcasted_iota(jnp.int32, cos.shape, 1) // (HEAD_DIM // 4)) % 2 == 0
    for g in range(x.shape[1] // LANES):
        xg = x[:, g * LANES:(g + 1) * LANES]
        up = pltpu.roll(xg, LANES - HEAD_DIM // 4, 1)
        dn = pltpu.roll(xg, HEAD_DIM // 4, 1)
        out.append(xg * cos + jnp.where(even_q, up, dn) * sin_signed)
    return out


def _mod_kernel(c_ref, w_ref, b_ref, o_ref):
    s = _silu(c_ref[...])
    o_ref[...] = _hdot(s, w_ref[...]) + b_ref[...]


def _modulation(cvecs, ada_w, ada_b):
    nrow = cvecs.shape[0]
    return pl.pallas_call(
        _mod_kernel,
        grid=(DEPTH, 3),
        in_specs=[pl.BlockSpec((nrow, D_MODEL), lambda l, j: (0, 0)),
                  pl.BlockSpec((None, D_MODEL, D_MODEL), lambda l, j: (l, 0, j)),
                  pl.BlockSpec((None, 1, D_MODEL), lambda l, j: (l, 0, j))],
        out_specs=pl.BlockSpec((None, nrow, D_MODEL), lambda l, j: (l, 0, j)),
        out_shape=jax.ShapeDtypeStruct((DEPTH, nrow, 3 * D_MODEL), F32),
        compiler_params=_params(2),
        name="modulation",
    )(cvecs, ada_w, ada_b.reshape(DEPTH, 1, 3 * D_MODEL))


def _proj_in_kernel(*refs, segs, rope_segs, use_rope):
    if use_rope:
        x_ref, sh_ref, sc_ref, g_ref, w_ref, cos_ref, sin_ref = refs[:7]
        outs = refs[7:]
    else:
        x_ref, sh_ref, sc_ref, g_ref, w_ref = refs[:5]
        outs = refs[5:]
    x = x_ref[...]
    y = x * lax.rsqrt(jnp.mean(x * x, -1, keepdims=True) + NORM_EPS)
    h = (y * g_ref[...]) * (1.0 + sc_ref[...]) + sh_ref[...]
    hb = h.astype(BF16)
    for idx, (lo, hi) in enumerate(segs):
        o = jnp.dot(hb, w_ref[:, lo:hi], preferred_element_type=F32)
        if use_rope and idx in rope_segs:
            parts = _rope(o, cos_ref[...], sin_ref[...])
            for g, part in enumerate(parts):
                outs[idx][:, g * LANES:(g + 1) * LANES] = part
        else:
            outs[idx][...] = o


def _proj_in(x2d, shift, scale, gain, w_bf16, segs, rope_segs, tiles_per_seq, rope_tabs):
    n = x2d.shape[0]
    nt = n // ROW_TILE
    use_rope = rope_tabs is not None
    row = lambda i: (i, 0)
    per_b = lambda i: (i // tiles_per_seq, 0, 0)
    in_specs = [pl.BlockSpec((ROW_TILE, D_MODEL), row),
                pl.BlockSpec((None, 1, D_MODEL), per_b),
                pl.BlockSpec((None, 1, D_MODEL), per_b),
                pl.BlockSpec((1, D_MODEL), lambda i: (0, 0)),
                pl.BlockSpec(w_bf16.shape, lambda i: (0, 0))]
    args = [x2d, shift, scale, gain, w_bf16]
    if use_rope:
        pos = lambda i: (i % tiles_per_seq, 0)
        in_specs += [pl.BlockSpec((ROW_TILE, LANES), pos), pl.BlockSpec((ROW_TILE, LANES), pos)]
        args += list(rope_tabs)
    return pl.pallas_call(
        functools.partial(_proj_in_kernel, segs=segs, rope_segs=rope_segs, use_rope=use_rope),
        grid=(nt,),
        in_specs=in_specs,
        out_specs=[pl.BlockSpec((ROW_TILE, hi - lo), row) for lo, hi in segs],
        out_shape=[jax.ShapeDtypeStruct((n, hi - lo), F32) for lo, hi in segs],
        compiler_params=_params(1),
        name="proj_in",
    )(*args)


def _sink_exp_pair(s2, sink_a, sink_b, n):
    rowi = lax.broadcasted_iota(jnp.int32, (2 * n, 1), 0)
    sink2 = jnp.where(rowi < n, sink_a, sink_b) * LOG2E
    m = jnp.maximum(jnp.max(s2, -1, keepdims=True), sink2)
    e = jnp.exp2(s2 - m)
    return e, 1.0 / (jnp.sum(e, -1, keepdims=True) + jnp.exp2(sink2 - m))


def _gqa_ctx_kernel(sink_ref, q_ref, k_ref, v_ref, o_ref):
    t = q_ref.shape[0]
    k = k_ref[...]
    v = v_ref[...]
    for p in range(A_HEADS // 2):
        g = p // 2
        qs = _stack_pair(q_ref[:, p * LANES:(p + 1) * LANES] * (HEAD_DIM ** -0.5 * LOG2E))
        s2 = _bdot(qs, _dup_half(k, g), NT)
        e, inv = _sink_exp_pair(s2, sink_ref[2 * p], sink_ref[2 * p + 1], t)
        o = _bdot(e, _dup_half(v, g)) * inv
        o_ref[:, p * LANES:(p + 1) * LANES] = jnp.where(_lane_lo((t, LANES)), o[:t], o[t:])


def _gqa_context(sink, q, k, v, bn, t):
    seq = lambda b: (b, 0, 0)
    return pl.pallas_call(
        _gqa_ctx_kernel,
        grid=(bn,),
        in_specs=[pl.BlockSpec(memory_space=pltpu.SMEM),
                  pl.BlockSpec((None, t, A_WIDTH), seq),
                  pl.BlockSpec((None, t, A_KV_WIDTH), seq),
                  pl.BlockSpec((None, t, A_KV_WIDTH), seq)],
        out_specs=pl.BlockSpec((None, t, A_WIDTH), seq),
        out_shape=jax.ShapeDtypeStruct((bn, t, A_WIDTH), F32),
        compiler_params=_params(1),
        name="gqa_context",
    )(sink, q.reshape(bn, t, A_WIDTH), k.reshape(bn, t, A_KV_WIDTH), v.reshape(bn, t, A_KV_WIDTH))


def _gqa_win_kernel(sink_ref, q_ref, kp_ref, kc_ref, kn_ref, vp_ref, vc_ref, vn_ref,
                    kx_ref, vx_ref, o_ref, *, t):
    nblk = pl.program_id(1)
    nctx = kx_ref.shape[0]
    keys = jnp.concatenate([kx_ref[...], kp_ref[...], kc_ref[...], kn_ref[...]], axis=0)
    vals = jnp.concatenate([vx_ref[...], vp_ref[...], vc_ref[...], vn_ref[...]], axis=0)
    nkeys = nctx + 3 * BLOCK
    rowi = lax.broadcasted_iota(jnp.int32, (2 * BLOCK, nkeys), 0)
    coli = lax.broadcasted_iota(jnp.int32, (2 * BLOCK, nkeys), 1)
    qpos = nblk * BLOCK + rowi % BLOCK
    kpos = nblk * BLOCK + coli - nctx - BLOCK
    valid = (coli < nctx) | ((jnp.abs(qpos - kpos) <= WINDOW) & (kpos >= 0) & (kpos < t))
    for p in range(A_HEADS // 2):
        g = p // 2
        qs = _stack_pair(q_ref[:, p * LANES:(p + 1) * LANES] * (HEAD_DIM ** -0.5 * LOG2E))
        s2 = _bdot(qs, _dup_half(keys, g), NT)
        s2 = jnp.where(valid, s2, NEG_INF)
        e, inv = _sink_exp_pair(s2, sink_ref[2 * p], sink_ref[2 * p + 1], BLOCK)
        o = _bdot(e, _dup_half(vals, g)) * inv
        o_ref[:, p * LANES:(p + 1) * LANES] = jnp.where(_lane_lo((BLOCK, LANES)), o[:BLOCK], o[BLOCK:])


def _gqa_window(sink, q, k, v, k_ctx, v_ctx, bn, t):
    nb = t // BLOCK
    nctx = k_ctx.shape[1]
    cur = lambda b, n: (b, n, 0)
    prev = lambda b, n: (b, jnp.maximum(n - 1, 0), 0)
    nxt = lambda b, n: (b, jnp.minimum(n + 1, nb - 1), 0)
    ctx = lambda b, n: (b, 0, 0)
    kv = lambda f: pl.BlockSpec((None, BLOCK, A_KV_WIDTH), f)
    k3 = k.reshape(bn, t, A_KV_WIDTH)
    v3 = v.reshape(bn, t, A_KV_WIDTH)
    return pl.pallas_call(
        functools.partial(_gqa_win_kernel, t=t),
        grid=(bn, nb),
        in_specs=[pl.BlockSpec(memory_space=pltpu.SMEM),
                  pl.BlockSpec((None, BLOCK, A_WIDTH), cur),
                  kv(prev), kv(cur), kv(nxt), kv(prev), kv(cur), kv(nxt),
                  pl.BlockSpec((None, nctx, A_KV_WIDTH), ctx),
                  pl.BlockSpec((None, nctx, A_KV_WIDTH), ctx)],
        out_specs=pl.BlockSpec((None, BLOCK, A_WIDTH), cur),
        out_shape=jax.ShapeDtypeStruct((bn, t, A_WIDTH), F32),
        compiler_params=_params(2),
        name="gqa_window",
    )(sink, q.reshape(bn, t, A_WIDTH), k3, k3, k3, v3, v3, v3, k_ctx, v_ctx)


def _rwkv_feat_kernel(p_ref, pp_ref, pn_ref, mu_ref, lora_ref, w0_ref, a0_ref, kk_ref, ka_ref,
                      rk_ref, ones_ref, r_out, kkn_out, v_out, lw_out, b_out, ke_out, bonus_out,
                      *, tiles_per_seq):
    i = pl.program_id(0)
    tm = p_ref.shape[0]
    pos = i % tiles_per_seq
    p = p_ref[...]
    rowi = lax.broadcasted_iota(jnp.int32, (tm, 1), 0)
    prev_row = jnp.where(pos != 0, pp_ref[7:8, :], 0.0)
    next_row = jnp.where(pos != tiles_per_seq - 1, pn_ref[0:1, :], 0.0)
    prev = jnp.where(rowi == 0, prev_row, pltpu.roll(p, 1, 0))
    nxt = jnp.where(rowi == tm - 1, next_row, pltpu.roll(p, tm - 1, 0))
    xf = p + (0.5 * (prev + nxt) - p) * mu_ref[...]
    r = xf[:, 0:B_WIDTH]
    k = xf[:, B_WIDTH:2 * B_WIDTH]
    v = xf[:, 2 * B_WIDTH:3 * B_WIDTH]
    g = xf[:, 3 * B_WIDTH:]
    g = jnp.where(_lane_lo(g.shape), jnp.tanh(g), g)
    lo = _bdot(g, lora_ref[...])
    ones = ones_ref[...]
    kkr = k * kk_ref[...]
    kkn = kkr / jnp.maximum(jnp.sqrt(_hdot(kkr * kkr, ones)), 1e-12)
    ke_sum = jnp.zeros_like(k)
    for z in range(2):
        wz = w0_ref[z] + lo[:, z * B_WIDTH:(z + 1) * B_WIDTH]
        sp = jnp.maximum(-wz, 0.0) + jnp.log(1.0 + jnp.exp(-jnp.abs(wz)))
        lw_out[z] = -jnp.exp(-sp - 0.5)
        az = _sigmoid(a0_ref[z] + lo[:, (2 + z) * B_WIDTH:(3 + z) * B_WIDTH])
        ke = k * (1.0 + (az - 1.0) * ka_ref[...])
        ke_out[z] = ke
        b_out[z] = kkn * az
        ke_sum = ke_sum + ke
    r_out[...] = r
    kkn_out[...] = kkn
    v_out[...] = v
    bonus_out[...] = _hdot(r * (0.5 * ke_sum) * rk_ref[...], ones) * v


def _rwkv_features(pb, mu, lora, w0, a0, k_k, k_a, r_k, ones, tiles_per_seq):
    n = pb.shape[0]
    nt = n // ROW_TILE
    h8 = ROW_TILE // 8
    row = lambda i: (i, 0)
    const2 = lambda i: (0, 0)
    const3 = lambda i: (0, 0, 0)
    vec = pl.BlockSpec((1, B_WIDTH), const2)
    one = jax.ShapeDtypeStruct((n, B_WIDTH), F32)
    two = jax.ShapeDtypeStruct((2, n, B_WIDTH), F32)
    ospec1 = pl.BlockSpec((ROW_TILE, B_WIDTH), row)
    ospec2 = pl.BlockSpec((2, ROW_TILE, B_WIDTH), lambda i: (0, i, 0))
    return pl.pallas_call(
        functools.partial(_rwkv_feat_kernel, tiles_per_seq=tiles_per_seq),
        grid=(nt,),
        in_specs=[pl.BlockSpec((ROW_TILE, B_SHIFT_WIDTH), row),
                  pl.BlockSpec((8, B_SHIFT_WIDTH), lambda i: (jnp.maximum(i * h8 - 1, 0), 0)),
                  pl.BlockSpec((8, B_SHIFT_WIDTH), lambda i: (jnp.minimum((i + 1) * h8, n // 8 - 1), 0)),
                  pl.BlockSpec((1, B_SHIFT_WIDTH), const2),
                  pl.BlockSpec(lora.shape, const2),
                  pl.BlockSpec((2, 1, B_WIDTH), const3),
                  pl.BlockSpec((2, 1, B_WIDTH), const3),
                  vec, vec, vec,
                  pl.BlockSpec((B_WIDTH, B_WIDTH), const2)],
        out_specs=[ospec1, ospec1, ospec1, ospec2, ospec2, ospec2, ospec1],
        out_shape=[one, one, one, two, two, two, one],
        compiler_params=_params(1),
        name="rwkv_features",
    )(pb, pb, pb, mu, lora, w0, a0, k_k, k_a, r_k, ones)


def _rwkv_scan_kernel(rf_ref, kf_ref, vf_ref, rb_ref, kb_ref, vb_ref, lwf_ref, bf_ref, kef_ref,
                      lwb_ref, bb_ref, keb_ref, s0_ref, yf_ref, yb_ref, sout_ref, s_scr, *, n_chunks):
    c = pl.program_id(1)

    @pl.when(c == 0)
    def _():
        s_scr[...] = s0_ref[...]

    row = lax.broadcasted_iota(jnp.int32, (PAIR, PAIR), 0)
    col = lax.broadcasted_iota(jnp.int32, (PAIR, PAIR), 1)
    same_head = (row // CHUNK) == (col // CHUNK)
    blk16 = (row // 16) == (col // 16)
    blk32 = (row // 32) == (col // 32)
    eye = jnp.where(row == col, 1.0, 0.0)
    tr = lax.broadcasted_iota(jnp.int32, (CHUNK, CHUNK), 0)
    tc = lax.broadcasted_iota(jnp.int32, (CHUNK, CHUNK), 1)
    dirs = ((rf_ref, kf_ref, vf_ref, lwf_ref, bf_ref, kef_ref, yf_ref),
            (rb_ref, kb_ref, vb_ref, lwb_ref, bb_ref, keb_ref, yb_ref))

    chains = [(z, p) for z in range(2) for p in range(B_HEADS // 2)]
    sls = [slice(p * LANES, (p + 1) * LANES) for _, p in chains]
    before, before_eq, tot, kkd_s, rd_s, bg_s, kg_s, v_s, sc = [], [], [], [], [], [], [], [], []
    for z in range(2):
        r_ref, kk_ref, v_ref, lw_ref, b_ref, ke_ref, _ = dirs[z]
        diff = (col - row) if z else (row - col)
        bef = (diff > 0) & same_head
        bef_eq = (diff >= 0) & same_head
        tri = jnp.where(((tc - tr) if z else (tr - tc)) >= 0, 1.0, 0.0)
        lw_all = lw_ref[...]
        cum_all = _hdot(tri, lw_all)
        for p in range(B_HEADS // 2):
            sl = slice(p * LANES, (p + 1) * LANES)
            lw = lw_all[:, sl]
            cum = cum_all[:, sl]
            tt = cum[0:1] if z else cum[CHUNK - 1:CHUNK]
            b = b_ref[:, sl]
            ke = ke_ref[:, sl]
            g_inv = jnp.exp(-cum)
            g_rest = jnp.exp(tt - cum)
            before.append(bef)
            before_eq.append(bef_eq)
            tot.append(tt)
            kkd_s.append(_stack_pair(kk_ref[:, sl] * jnp.exp(cum - lw)))
            rd_s.append(_stack_pair(r_ref[:, sl] * jnp.exp(cum)))
            bg_s.append(_stack_pair(b * g_rest))
            kg_s.append(_stack_pair(ke * g_rest))
            v_s.append(_stack_pair(v_ref[:, sl]))
            sc.append(_bdot(jnp.concatenate([kkd_s[-1], rd_s[-1]], 0),
                            jnp.concatenate([_stack_pair(b * g_inv), _stack_pair(ke * g_inv)], 0), NT))
    lb = [jnp.where(m, s[:PAIR, :PAIR], 0.0) for m, s in zip(before, sc)]
    mb = [jnp.where(m, s[PAIR:, :PAIR], 0.0) for m, s in zip(before_eq, sc)]
    lv = [_bdot(jnp.concatenate([jnp.where(m, s[:PAIR, PAIR:], 0.0),
                                 jnp.where(me, s[PAIR:, PAIR:], 0.0)], 0), v)
          for m, me, s, v in zip(before, before_eq, sc, v_s)]

    pw = [jnp.where(blk16, x, 0.0) for x in lb]
    tinv = [eye - x for x in pw]
    for _ in range(3):
        pw = [_bdot(x, x) for x in pw]
        tinv = [t + _bdot(t, x) for t, x in zip(tinv, pw)]
    for off in (blk32 & ~blk16, ~blk32):
        tmp = [_bdot(t, jnp.where(off, x, 0.0)) for t, x in zip(tinv, lb)]
        tinv = [t - _bdot(m, t) for t, m in zip(tinv, tmp)]

    w12 = [_bdot(t, jnp.concatenate([k, x[:PAIR]], 1)) for t, k, x in zip(tinv, kkd_s, lv)]
    s_old = [s_scr[z, p] for z, p in chains]
    ws = [_bdot(jnp.concatenate([w[:, :LANES], r], 0), s, NT) for w, r, s in zip(w12, rd_s, s_old)]
    u = [-(a[:PAIR] + w[:, LANES:]) for a, w in zip(ws, w12)]
    mbu = [_bdot(m, x) for m, x in zip(mb, u)]
    upd = [_bdot(jnp.concatenate([x, v], 0).T, jnp.concatenate([bg, kg], 0))
           for x, v, bg, kg in zip(u, v_s, bg_s, kg_s)]
    for i, (z, p) in enumerate(chains):
        ys = ws[i][PAIR:] + mbu[i] + lv[i][PAIR:]
        dirs[z][6][:, sls[i]] = ys[:CHUNK] + ys[CHUNK:]
        s_scr[z, p] = s_old[i] * jnp.exp(tot[i]) + upd[i]

    @pl.when(c == n_chunks - 1)
    def _():
        sout_ref[...] = s_scr[...]


def _rwkv_scan(r, kkn, v, lw, b, ke, s0, bn, t):
    nc = t // CHUNK
    npair = B_HEADS // 2
    fwd = pl.BlockSpec((None, CHUNK, B_WIDTH), lambda bb, c: (bb, c, 0))
    bwd = pl.BlockSpec((None, CHUNK, B_WIDTH), lambda bb, c: (bb, nc - 1 - c, 0))
    dfwd = pl.BlockSpec((None, None, CHUNK, B_WIDTH), lambda bb, c: (0, bb, c, 0))
    dbwd = pl.BlockSpec((None, None, CHUNK, B_WIDTH), lambda bb, c: (1, bb, nc - 1 - c, 0))
    state = pl.BlockSpec((None, 2, npair, PAIR, PAIR), lambda bb, c: (bb, 0, 0, 0, 0))
    r3, k3, v3 = (a.reshape(bn, t, B_WIDTH) for a in (r, kkn, v))
    lw4, b4, ke4 = (a.reshape(2, bn, t, B_WIDTH) for a in (lw, b, ke))
    seq = jax.ShapeDtypeStruct((bn, t, B_WIDTH), F32)
    return pl.pallas_call(
        functools.partial(_rwkv_scan_kernel, n_chunks=nc),
        grid=(bn, nc),
        in_specs=[fwd, fwd, fwd, bwd, bwd, bwd, dfwd, dfwd, dfwd, dbwd, dbwd, dbwd, state],
        out_specs=[fwd, bwd, state],
        out_shape=[seq, seq, jax.ShapeDtypeStruct((bn, 2, npair, PAIR, PAIR), F32)],
        scratch_shapes=[pltpu.VMEM((2, npair, PAIR, PAIR), F32)],
        compiler_params=_params(2),
        name="rwkv_scan",
    )(r3, k3, v3, r3, k3, v3, lw4, b4, ke4, lw4, b4, ke4, s0)


def _post(x, z, gate, gain):
    zn = z * lax.rsqrt(jnp.mean(z * z, -1, keepdims=True) + NORM_EPS)
    return x + gate * (zn * gain)


def _even_out_kernel(ya_ref, ga_ref, yf_ref, yb_ref, bonus_ref, gb_ref, lnw_ref, lnb_ref, ones_ref,
                     x_ref, gate_ref, gain_ref, w_ref, o_ref):
    y = yf_ref[...] + yb_ref[...]
    avg = ones_ref[...] * (1.0 / HEAD_DIM)
    mean = _hdot(y, avg)
    d = y - mean
    var = _hdot(d * d, avg)
    yb = (d * lax.rsqrt(var + RWKV_LN_EPS)) * lnw_ref[...] + lnb_ref[...] + bonus_ref[...]
    za = (ya_ref[...] * _silu(ga_ref[...])).astype(BF16)
    zb = (yb * _silu(gb_ref[...])).astype(BF16)
    z = (jnp.dot(za, w_ref[0:A_WIDTH, :], preferred_element_type=F32)
         + jnp.dot(zb, w_ref[A_WIDTH:, :], preferred_element_type=F32))
    o_ref[...] = _post(x_ref[...], z, gate_ref[...], gain_ref[...])


def _even_out(ya, ga, yf, yb, bonus, gb, ln_w, ln_b, ones, x2d, gate, gain, w_bf16, tiles_per_seq):
    n = x2d.shape[0]
    row = lambda i: (i, 0)
    const2 = lambda i: (0, 0)
    half = pl.BlockSpec((ROW_TILE, B_WIDTH), row)
    full = pl.BlockSpec((ROW_TILE, D_MODEL), row)
    vec = pl.BlockSpec((1, B_WIDTH), const2)
    return pl.pallas_call(
        _even_out_kernel,
        grid=(n // ROW_TILE,),
        in_specs=[half, half, half, half, half, half, vec, vec,
                  pl.BlockSpec((B_WIDTH, B_WIDTH), const2),
                  full,
                  pl.BlockSpec((None, 1, D_MODEL), lambda i: (i // tiles_per_seq, 0, 0)),
                  pl.BlockSpec((1, D_MODEL), const2),
                  pl.BlockSpec((D_MODEL, D_MODEL), const2)],
        out_specs=full,
        out_shape=jax.ShapeDtypeStruct((n, D_MODEL), F32),
        compiler_params=_params(1),
        name="even_out",
    )(ya, ga, yf, yb, bonus, gb, ln_w, ln_b, ones, x2d, gate, gain, w_bf16)


def _odd_out_kernel(o_ref_in, gc_ref, x_ref, gate_ref, gain_ref, w_ref, o_ref):
    zc = (o_ref_in[...] * _silu(gc_ref[...])).astype(BF16)
    z = jnp.dot(zc, w_ref[...], preferred_element_type=F32)
    o_ref[...] = _post(x_ref[...], z, gate_ref[...], gain_ref[...])


def _odd_out(o, gc, x2d, gate, gain, w_bf16, tiles_per_seq):
    n = x2d.shape[0]
    row = lambda i: (i, 0)
    const2 = lambda i: (0, 0)
    full = pl.BlockSpec((ROW_TILE, D_MODEL), row)
    return pl.pallas_call(
        _odd_out_kernel,
        grid=(n // ROW_TILE,),
        in_specs=[full, full, full,
                  pl.BlockSpec((None, 1, D_MODEL), lambda i: (i // tiles_per_seq, 0, 0)),
                  pl.BlockSpec((1, D_MODEL), const2),
                  pl.BlockSpec((D_MODEL, D_MODEL), const2)],
        out_specs=full,
        out_shape=jax.ShapeDtypeStruct((n, D_MODEL), F32),
        compiler_params=_params(1),
        name="odd_out",
    )(o, gc, x2d, gate, gain, w_bf16)


def _lambda(lq1_ref, lk1_ref, lq2_ref, lk2_ref, lam_init):
    s1 = jnp.sum(lq1_ref[...] * lk1_ref[...], -1, keepdims=True)
    s2 = jnp.sum(lq2_ref[...] * lk2_ref[...], -1, keepdims=True)
    return jnp.exp(s1) - jnp.exp(s2) + lam_init


def _subln(o, gain, lam_init):
    on = o * lax.rsqrt(jnp.mean(o * o, -1, keepdims=True) + SUBLN_EPS)
    return (on * gain) * (1.0 - lam_init)


def _diff_ctx_kernel(lq1_ref, lk1_ref, lq2_ref, lk2_ref, sub_ref, q_ref, k_ref, v_ref, o_ref,
                     *, lam_init):
    t = q_ref.shape[0]
    lam = _lambda(lq1_ref, lk1_ref, lq2_ref, lk2_ref, lam_init)
    for h in range(C_HEADS):
        sl = slice(h * LANES, (h + 1) * LANES)
        qs = _stack_pair(q_ref[:, sl] * (C_QK_DIM ** -0.5 * LOG2E))
        s2 = _bdot(qs, k_ref[:, sl], NT)
        e = jnp.exp2(s2 - jnp.max(s2, -1, keepdims=True))
        ov = _bdot(e, v_ref[:, sl]) * (1.0 / jnp.sum(e, -1, keepdims=True))
        o_ref[:, sl] = _subln(ov[:t] - lam * ov[t:], sub_ref[...], lam_init)


def _diff_context(lams, subln, q, k, v, bn, t, lam_init):
    seq = pl.BlockSpec((None, t, C_WIDTH), lambda b: (b, 0, 0))
    small = pl.BlockSpec((1, C_QK_DIM), lambda b: (0, 0))
    return pl.pallas_call(
        functools.partial(_diff_ctx_kernel, lam_init=lam_init),
        grid=(bn,),
        in_specs=[small, small, small, small, pl.BlockSpec((1, C_V_DIM), lambda b: (0, 0)),
                  seq, seq, seq],
        out_specs=seq,
        out_shape=jax.ShapeDtypeStruct((bn, t, C_WIDTH), F32),
        compiler_params=_params(1),
        name="diff_context",
    )(*lams, subln, q.reshape(bn, t, C_WIDTH), k.reshape(bn, t, C_WIDTH), v.reshape(bn, t, C_WIDTH))


def _diff_lat_kernel(lq1_ref, lk1_ref, lq2_ref, lk2_ref, sub_ref, q_ref, k_ref, v_ref, kx_ref, vx_ref,
                     o_ref, *, lam_init):
    tq = q_ref.shape[0]
    lam = _lambda(lq1_ref, lk1_ref, lq2_ref, lk2_ref, lam_init)
    qs = _stack_pair(q_ref[...] * (C_QK_DIM ** -0.5 * LOG2E))
    sx = _bdot(qs, kx_ref[...], NT)
    sl = _bdot(qs, k_ref[...], NT)
    m = jnp.maximum(jnp.max(sx, -1, keepdims=True), jnp.max(sl, -1, keepdims=True))
    ex = jnp.exp2(sx - m)
    el = jnp.exp2(sl - m)
    inv = 1.0 / (jnp.sum(ex, -1, keepdims=True) + jnp.sum(el, -1, keepdims=True))
    ov = (_bdot(ex, vx_ref[...]) + _bdot(el, v_ref[...])) * inv
    o_ref[...] = _subln(ov[:tq] - lam * ov[tq:], sub_ref[...], lam_init)


def _diff_latent(lams, subln, q, k, v, k_ctx, v_ctx, bn, t, lam_init):
    nb = t // DIFF_Q_TILE
    nctx = k_ctx.shape[1]
    small = pl.BlockSpec((1, C_QK_DIM), lambda b, h, n: (0, 0))
    qspec = pl.BlockSpec((None, DIFF_Q_TILE, LANES), lambda b, h, n: (b, n, h))
    kvspec = pl.BlockSpec((None, t, LANES), lambda b, h, n: (b, 0, h))
    cxspec = pl.BlockSpec((None, nctx, LANES), lambda b, h, n: (b, 0, h))
    return pl.pallas_call(
        functools.partial(_diff_lat_kernel, lam_init=lam_init),
        grid=(bn, C_HEADS, nb),
        in_specs=[small, small, small, small,
                  pl.BlockSpec((1, C_V_DIM), lambda b, h, n: (0, 0)),
                  qspec, kvspec, kvspec, cxspec, cxspec],
        out_specs=qspec,
        out_shape=jax.ShapeDtypeStruct((bn, t, C_WIDTH), F32),
        compiler_params=_params(3),
        name="diff_latent",
    )(*lams, subln, q.reshape(bn, t, C_WIDTH), k.reshape(bn, t, C_WIDTH), v.reshape(bn, t, C_WIDTH),
      k_ctx, v_ctx)


def _rope_tables(t):
    nf = HEAD_DIM // 4
    inv = 1.0 / (ROPE_BASE ** (jnp.arange(nf, dtype=F32) / nf))
    pos = jnp.arange(t)
    ang_r = (pos // GRID_W).astype(F32)[:, None] * inv[None]
    ang_c = (pos % GRID_W).astype(F32)[:, None] * inv[None]
    ang = jnp.concatenate([ang_r, ang_r, ang_c, ang_c], -1)
    cos, sin = jnp.cos(ang), jnp.sin(ang)
    sign = jnp.where((jnp.arange(HEAD_DIM) // nf) % 2 == 0, -1.0, 1.0).astype(F32)
    return jnp.tile(cos, (1, 2)), jnp.tile(sin * sign[None], (1, 2))


EVEN_SEGS = ((0, 512), (512, 640), (640, 768), (768, 1280), (1280, 2944), (2944, 3456))
ODD_SEGS = ((0, 1024), (1024, 2048), (2048, 3072), (3072, 4096))


def _pair_states(st):
    bn = st.shape[0]
    s = st.reshape(bn, 2, 4, 2, HEAD_DIM, HEAD_DIM)
    z = jnp.zeros_like(s[:, :, :, 0])
    top = jnp.concatenate([s[:, :, :, 0], z], -1)
    bot = jnp.concatenate([z, s[:, :, :, 1]], -1)
    return jnp.concatenate([top, bot], -2)


def _unpair_states(sp):
    bn = sp.shape[0]
    a = sp[:, :, :, :HEAD_DIM, :HEAD_DIM]
    b = sp[:, :, :, HEAD_DIM:, HEAD_DIM:]
    return jnp.stack([a, b], axis=3).reshape(bn, 2, B_HEADS, HEAD_DIM, HEAD_DIM)


def kernel(x_prompt, x_sample, cache_a_k, cache_a_v, state_rwkv, cache_c_k, cache_c_v, c, c_ctx,
           ada_w, ada_b, norm_pre, norm_post, w_out, even_w_in, a_sink, b_mu, b_w0, b_w2, b_a0,
           b_a2, b_kk, b_ka, b_rk, b_ln_w, b_ln_b, odd_w_in, c_lq1, c_lk1, c_lq2, c_lk2, c_subln):
    bp, tp, _ = x_prompt.shape
    bs, ts, _ = x_sample.shape
    past = cache_a_k.shape[2]

    cvecs = jnp.concatenate([c_ctx[None], c, jnp.zeros((8 - 1 - bs, D_MODEL), F32)], 0)
    mods = _modulation(cvecs, ada_w, ada_b)
    rope_tabs = _rope_tables(ts)
    ones = (jnp.arange(B_WIDTH)[:, None] // HEAD_DIM == jnp.arange(B_WIDTH)[None] // HEAD_DIM).astype(F32)

    xp = x_prompt.reshape(bp * tp, D_MODEL)
    xs = x_sample.reshape(bs * ts, D_MODEL)

    def mod_rows(layer, rows):
        m = mods[layer, rows[0]:rows[1]]
        if rows == (0, 1):
            m = jnp.broadcast_to(m, (bp, 3 * D_MODEL))
        m = m[:, None, :]
        return m[..., :D_MODEL], m[..., D_MODEL:2 * D_MODEL], m[..., 2 * D_MODEL:]

    e = 0
    w_in0 = even_w_in[e].astype(BF16)
    w_out0 = w_out[0].astype(BF16)
    gain_pre = norm_pre[0][None]
    gain_post = norm_post[0][None]
    zpad = jnp.zeros((DECAY_LORA, B_WIDTH), F32)
    lora = jnp.concatenate([
        jnp.concatenate([b_w2[e, 0], b_w2[e, 1], zpad, zpad], 1),
        jnp.concatenate([zpad, zpad, b_a2[e, 0], b_a2[e, 1]], 1)], 0).astype(BF16)
    feat_args = (b_mu[e][None], lora, b_w0[e][:, None, :], b_a0[e][:, None, :], b_kk[e][None],
                 b_ka[e][None], b_rk[e][None], ones)

    def even_layer(x2d, bn, t, rows, ctx):
        tps = t // ROW_TILE
        shift, scale, gate = mod_rows(0, rows)
        qa, ka, va, ga, pb, gb = _proj_in(x2d, shift, scale, gain_pre, w_in0, EVEN_SEGS, (0, 1), tps,
                                          None if ctx is None else rope_tabs)
        if ctx is None:
            ya = _gqa_context(a_sink[e], qa, ka, va, bn, t)
            s0 = jnp.zeros((bn, 2, B_HEADS // 2, PAIR, PAIR), F32)
        else:
            k_ctx, v_ctx, st0 = ctx
            ya = _gqa_window(a_sink[e], qa, ka, va, k_ctx, v_ctx, bn, t)
            s0 = _pair_states(st0)
        r, kkn, v, lw, b, ke, bonus = _rwkv_features(pb, *feat_args, tps)
        yf, yb, s_fin = _rwkv_scan(r, kkn, v, lw, b, ke, s0, bn, t)
        y = _even_out(ya.reshape(bn * t, A_WIDTH), ga, yf.reshape(bn * t, B_WIDTH),
                      yb.reshape(bn * t, B_WIDTH), bonus, gb,
                      b_ln_w[e][None], b_ln_b[e][None], ones, x2d, gate, gain_post, w_out0, tps)
        return y, ka, va, s_fin

    yp, ka_p, va_p, st_p = even_layer(xp, bp, tp, (0, 1), None)
    ctx_a = (cache_a_k[:, e].reshape(bs, past, A_KV_WIDTH), cache_a_v[:, e].reshape(bs, past, A_KV_WIDTH),
             state_rwkv[:, e])
    ys, _, _, _ = even_layer(xs, bs, ts, (1, 1 + bs), ctx_a)

    o = 0
    lam_init = 0.8 - 0.6 * math.exp(-0.3 * 1)
    w_in1 = odd_w_in[o].astype(BF16)
    w_out1 = w_out[1].astype(BF16)
    gain_pre1 = norm_pre[1][None]
    gain_post1 = norm_post[1][None]
    lams = (c_lq1[o][None], c_lk1[o][None], c_lq2[o][None], c_lk2[o][None])
    subln = c_subln[o][None]

    def odd_layer(x2d, bn, t, rows, ctx):
        tps = t // ROW_TILE
        shift, scale, gate = mod_rows(1, rows)
        qc, kc, vc, gc = _proj_in(x2d, shift, scale, gain_pre1, w_in1, ODD_SEGS, (0, 1), tps,
                                  None if ctx is None else rope_tabs)
        if ctx is None:
            oc = _diff_context(lams, subln, qc, kc, vc, bn, t, lam_init)
        else:
            k_ctx, v_ctx = ctx
            oc = _diff_latent(lams, subln, qc, kc, vc, k_ctx, v_ctx, bn, t, lam_init)
        y = _odd_out(oc.reshape(bn * t, C_WIDTH), gc, x2d, gate, gain_post1, w_out1, tps)
        return y, kc, vc

    yp2, kc_p, vc_p = odd_layer(yp, bp, tp, (0, 1), None)
    ctx_c = (cache_c_k[:, o].reshape(bs, past, C_WIDTH), cache_c_v[:, o].reshape(bs, past, C_WIDTH))
    ys2, _, _ = odd_layer(ys, bs, ts, (1, 1 + bs), ctx_c)

    return (yp2.reshape(bp, tp, D_MODEL),
            ys2.reshape(bs, ts, D_MODEL),
            ka_p.reshape(bp, 1, tp, A_KV_HEADS, HEAD_DIM),
            va_p.reshape(bp, 1, tp, A_KV_HEADS, HEAD_DIM),
            _unpair_states(st_p)[:, None],
            kc_p.reshape(bp, 1, tp, C_HEADS, 2 * C_QK_DIM),
            vc_p.reshape(bp, 1, tp, C_HEADS, C_V_DIM))
```

```python
import functools
import math

import jax
import jax.numpy as jnp
from jax import lax
from jax.experimental import pallas as pl
from jax.experimental.pallas import tpu as pltpu

F32 = jnp.float32
BF16 = jnp.bfloat16

D_MODEL = 1024
DEPTH = 2
GRID_W = 64
HEAD_DIM = 64
ROPE_BASE = 10000.0
A_HEADS = 8
A_KV_HEADS = 2
A_WIDTH = 512
A_KV_WIDTH = 128
WINDOW = 128
BLOCK = 128
B_HEADS = 8
B_WIDTH = 512
DECAY_LORA = 64
AAA_LORA = 64
B_SHIFT_WIDTH = 3 * B_WIDTH + DECAY_LORA + AAA_LORA
C_HEADS = 8
C_QK_DIM = 64
C_V_DIM = 128
C_WIDTH = 1024
EVEN_IN = 3456
ODD_IN = 4096
NORM_EPS = 1e-6
RWKV_LN_EPS = 64e-5
SUBLN_EPS = 1e-5
NEG_INF = -1e30

LANES = 128
ROW_TILE = 256
DIFF_Q_TILE = 128
CHUNK = 64
SCAN_SEQS = 2
PAIR = 2 * CHUNK
VMEM_LIMIT = 48 * 1024 * 1024

LOG2E = math.log2(math.e)

NN = ((1,), (0,))
NT = ((1,), (1,))


def _bdot(a, b, dims=NN):
    return lax.dot_general(a.astype(BF16), b.astype(BF16), (dims, ((), ())),
                           preferred_element_type=F32)


def _hdot(a, b):
    return lax.dot_general(a, b, (NN, ((), ())), precision=lax.Precision.HIGHEST,
                           preferred_element_type=F32)


def _split_dot(sel_bf16, x):
    x1 = x.astype(BF16)
    r1 = x - x1.astype(F32)
    x2 = r1.astype(BF16)
    x3 = (r1 - x2.astype(F32)).astype(BF16)
    dot = lambda p: jnp.dot(sel_bf16, p, preferred_element_type=F32)
    return dot(x1) + dot(x2) + dot(x3)


def _head_sum(x, ones_bf16):
    hi = x.astype(BF16)
    lo = (x - hi.astype(F32)).astype(BF16)
    return (jnp.dot(hi, ones_bf16, preferred_element_type=F32)
            + jnp.dot(lo, ones_bf16, preferred_element_type=F32))


def _sigmoid(x):
    return 1.0 / (1.0 + jnp.exp(-x))


def _silu(x):
    return x * _sigmoid(x)


def _params(n_axes):
    return pltpu.CompilerParams(dimension_semantics=("arbitrary",) * n_axes,
                                vmem_limit_bytes=VMEM_LIMIT)


def _lane_lo(shape):
    return (lax.broadcasted_iota(jnp.int32, shape, len(shape) - 1) % LANES) < HEAD_DIM


def _stack_pair(x):
    lo = _lane_lo(x.shape)
    z = jnp.zeros_like(x)
    return jnp.concatenate([jnp.where(lo, x, z), jnp.where(lo, z, x)], axis=0)


def _dup_half(x, g):
    lo = _lane_lo(x.shape)
    sw = pltpu.roll(x, HEAD_DIM, 1)
    return jnp.where(lo, x, sw) if g == 0 else jnp.where(lo, sw, x)


def _rope(x, cos, sin_signed):
    out = []
    even_q = (lax.broadcasted_iota(jnp.int32, cos.shape, 1) // (HEAD_DIM // 4)) % 2 == 0
    for g in range(x.shape[1] // LANES):
        xg = x[:, g * LANES:(g + 1) * LANES]
        up = pltpu.roll(xg, LANES - HEAD_DIM // 4, 1)
        dn = pltpu.roll(xg, HEAD_DIM // 4, 1)
        out.append(xg * cos + jnp.where(even_q, up, dn) * sin_signed)
    return out


def _mod_kernel(c_ref, w_ref, b_ref, o_ref):
    s = _silu(c_ref[...])
    o_ref[...] = _hdot(s, w_ref[...]) + b_ref[...]


def _modulation(cvecs, ada_w, ada_b):
    nrow = cvecs.shape[0]
    return pl.pallas_call(
        _mod_kernel,
        grid=(DEPTH, 3),
        in_specs=[pl.BlockSpec((nrow, D_MODEL), lambda l, j: (0, 0)),
                  pl.BlockSpec((None, D_MODEL, D_MODEL), lambda l, j: (l, 0, j)),
                  pl.BlockSpec((None, 1, D_MODEL), lambda l, j: (l, 0, j))],
        out_specs=pl.BlockSpec((None, nrow, D_MODEL), lambda l, j: (l, 0, j)),
        out_shape=jax.ShapeDtypeStruct((DEPTH, nrow, 3 * D_MODEL), F32),
        compiler_params=_params(2),
        name="modulation",
    )(cvecs, ada_w, ada_b.reshape(DEPTH, 1, 3 * D_MODEL))


def _proj_in_kernel(*refs, segs, rope_segs, use_rope):
    if use_rope:
        x_ref, sh_ref, sc_ref, g_ref, w_ref, cos_ref, sin_ref = refs[:7]
        outs = refs[7:]
    else:
        x_ref, sh_ref, sc_ref, g_ref, w_ref = refs[:5]
        outs = refs[5:]
    x = x_ref[...]
    y = x * lax.rsqrt(jnp.mean(x * x, -1, keepdims=True) + NORM_EPS)
    h = (y * g_ref[...]) * (1.0 + sc_ref[...]) + sh_ref[...]
    hb = h.astype(BF16)
    for idx, (lo, hi) in enumerate(segs):
        o = jnp.dot(hb, w_ref[:, lo:hi], preferred_element_type=F32)
        if use_rope and idx in rope_segs:
            parts = _rope(o, cos_ref[...], sin_ref[...])
            for g, part in enumerate(parts):
                outs[idx][:, g * LANES:(g + 1) * LANES] = part
        else:
            outs[idx][...] = o


def _proj_in(x2d, shift, scale, gain, w_bf16, segs, rope_segs, tiles_per_seq, rope_tabs):
    n = x2d.shape[0]
    nt = n // ROW_TILE
    use_rope = rope_tabs is not None
    row = lambda i: (i, 0)
    per_b = lambda i: (i // tiles_per_seq, 0, 0)
    in_specs = [pl.BlockSpec((ROW_TILE, D_MODEL), row),
                pl.BlockSpec((None, 1, D_MODEL), per_b),
                pl.BlockSpec((None, 1, D_MODEL), per_b),
                pl.BlockSpec((1, D_MODEL), lambda i: (0, 0)),
                pl.BlockSpec(w_bf16.shape, lambda i: (0, 0))]
    args = [x2d, shift, scale, gain, w_bf16]
    if use_rope:
        pos = lambda i: (i % tiles_per_seq, 0)
        in_specs += [pl.BlockSpec((ROW_TILE, LANES), pos), pl.BlockSpec((ROW_TILE, LANES), pos)]
        args += list(rope_tabs)
    return pl.pallas_call(
        functools.partial(_proj_in_kernel, segs=segs, rope_segs=rope_segs, use_rope=use_rope),
        grid=(nt,),
        in_specs=in_specs,
        out_specs=[pl.BlockSpec((ROW_TILE, hi - lo), row) for lo, hi in segs],
        out_shape=[jax.ShapeDtypeStruct((n, hi - lo), F32) for lo, hi in segs],
        compiler_params=_params(1),
        name="proj_in",
    )(*args)


def _sink_exp_pair(s2, sink_a, sink_b, n):
    rowi = lax.broadcasted_iota(jnp.int32, (2 * n, 1), 0)
    sink2 = jnp.where(rowi < n, sink_a, sink_b) * LOG2E
    m = jnp.maximum(jnp.max(s2, -1, keepdims=True), sink2)
    e = jnp.exp2(s2 - m)
    return e, 1.0 / (jnp.sum(e, -1, keepdims=True) + jnp.exp2(sink2 - m))


def _gqa_ctx_kernel(sink_ref, q_ref, k_ref, v_ref, o_ref):
    t = q_ref.shape[0]
    k = k_ref[...]
    v = v_ref[...]
    for p in range(A_HEADS // 2):
        g = p // 2
        qs = _stack_pair(q_ref[:, p * LANES:(p + 1) * LANES] * (HEAD_DIM ** -0.5 * LOG2E))
        s2 = _bdot(qs, _dup_half(k, g), NT)
        e, inv = _sink_exp_pair(s2, sink_ref[2 * p], sink_ref[2 * p + 1], t)
        o = _bdot(e, _dup_half(v, g)) * inv
        o_ref[:, p * LANES:(p + 1) * LANES] = jnp.where(_lane_lo((t, LANES)), o[:t], o[t:])


def _gqa_context(sink, q, k, v, bn, t):
    seq = lambda b: (b, 0, 0)
    return pl.pallas_call(
        _gqa_ctx_kernel,
        grid=(bn,),
        in_specs=[pl.BlockSpec(memory_space=pltpu.SMEM),
                  pl.BlockSpec((None, t, A_WIDTH), seq),
                  pl.BlockSpec((None, t, A_KV_WIDTH), seq),
                  pl.BlockSpec((None, t, A_KV_WIDTH), seq)],
        out_specs=pl.BlockSpec((None, t, A_WIDTH), seq),
        out_shape=jax.ShapeDtypeStruct((bn, t, A_WIDTH), F32),
        compiler_params=_params(1),
        name="gqa_context",
    )(sink, q.reshape(bn, t, A_WIDTH), k.reshape(bn, t, A_KV_WIDTH), v.reshape(bn, t, A_KV_WIDTH))


def _gqa_win_kernel(sink_ref, q_ref, kp_ref, kc_ref, kn_ref, vp_ref, vc_ref, vn_ref,
                    kx_ref, vx_ref, o_ref, *, t):
    nblk = pl.program_id(1)
    nctx = kx_ref.shape[0]
    keys = jnp.concatenate([kx_ref[...], kp_ref[...], kc_ref[...], kn_ref[...]], axis=0)
    vals = jnp.concatenate([vx_ref[...], vp_ref[...], vc_ref[...], vn_ref[...]], axis=0)
    nkeys = nctx + 3 * BLOCK
    rowi = lax.broadcasted_iota(jnp.int32, (2 * BLOCK, nkeys), 0)
    coli = lax.broadcasted_iota(jnp.int32, (2 * BLOCK, nkeys), 1)
    qpos = nblk * BLOCK + rowi % BLOCK
    kpos = nblk * BLOCK + coli - nctx - BLOCK
    valid = (coli < nctx) | ((jnp.abs(qpos - kpos) <= WINDOW) & (kpos >= 0) & (kpos < t))
    for p in range(A_HEADS // 2):
        g = p // 2
        qs = _stack_pair(q_ref[:, p * LANES:(p + 1) * LANES] * (HEAD_DIM ** -0.5 * LOG2E))
        s2 = _bdot(qs, _dup_half(keys, g), NT)
        s2 = jnp.where(valid, s2, NEG_INF)
        e, inv = _sink_exp_pair(s2, sink_ref[2 * p], sink_ref[2 * p + 1], BLOCK)
        o = _bdot(e, _dup_half(vals, g)) * inv
        o_ref[:, p * LANES:(p + 1) * LANES] = jnp.where(_lane_lo((BLOCK, LANES)), o[:BLOCK], o[BLOCK:])


def _gqa_window(sink, q, k, v, k_ctx, v_ctx, bn, t):
    nb = t // BLOCK
    nctx = k_ctx.shape[1]
    cur = lambda b, n: (b, n, 0)
    prev = lambda b, n: (b, jnp.maximum(n - 1, 0), 0)
    nxt = lambda b, n: (b, jnp.minimum(n + 1, nb - 1), 0)
    ctx = lambda b, n: (b, 0, 0)
    kv = lambda f: pl.BlockSpec((None, BLOCK, A_KV_WIDTH), f)
    k3 = k.reshape(bn, t, A_KV_WIDTH)
    v3 = v.reshape(bn, t, A_KV_WIDTH)
    return pl.pallas_call(
        functools.partial(_gqa_win_kernel, t=t),
        grid=(bn, nb),
        in_specs=[pl.BlockSpec(memory_space=pltpu.SMEM),
                  pl.BlockSpec((None, BLOCK, A_WIDTH), cur),
                  kv(prev), kv(cur), kv(nxt), kv(prev), kv(cur), kv(nxt),
                  pl.BlockSpec((None, nctx, A_KV_WIDTH), ctx),
                  pl.BlockSpec((None, nctx, A_KV_WIDTH), ctx)],
        out_specs=pl.BlockSpec((None, BLOCK, A_WIDTH), cur),
        out_shape=jax.ShapeDtypeStruct((bn, t, A_WIDTH), F32),
        compiler_params=_params(2),
        name="gqa_window",
    )(sink, q.reshape(bn, t, A_WIDTH), k3, k3, k3, v3, v3, v3, k_ctx, v_ctx)


def _rwkv_feat_kernel(p_ref, pp_ref, pn_ref, mu_ref, lora_ref, w0_ref, a0_ref, kk_ref, ka_ref,
                      rk_ref, ones_ref, r_out, kkn_out, v_out, lw_out, b_out, ke_out, bonus_out,
                      *, tiles_per_seq):
    i = pl.program_id(0)
    tm = p_ref.shape[0]
    pos = i % tiles_per_seq
    p = p_ref[...]
    rowi = lax.broadcasted_iota(jnp.int32, (tm, 1), 0)
    prev_row = jnp.where(pos != 0, pp_ref[7:8, :], 0.0)
    next_row = jnp.where(pos != tiles_per_seq - 1, pn_ref[0:1, :], 0.0)
    prev = jnp.where(rowi == 0, prev_row, pltpu.roll(p, 1, 0))
    nxt = jnp.where(rowi == tm - 1, next_row, pltpu.roll(p, tm - 1, 0))
    xf = p + (0.5 * (prev + nxt) - p) * mu_ref[...]
    r = xf[:, 0:B_WIDTH]
    k = xf[:, B_WIDTH:2 * B_WIDTH]
    v = xf[:, 2 * B_WIDTH:3 * B_WIDTH]
    g = xf[:, 3 * B_WIDTH:]
    g = jnp.where(_lane_lo(g.shape), jnp.tanh(g), g)
    lo = _bdot(g, lora_ref[...])
    ones = ones_ref[...]
    kkr = k * kk_ref[...]
    kkn = kkr / jnp.maximum(jnp.sqrt(_head_sum(kkr * kkr, ones)), 1e-12)
    ke_sum = jnp.zeros_like(k)
    for z in range(2):
        wz = w0_ref[z] + lo[:, z * B_WIDTH:(z + 1) * B_WIDTH]
        sp = jnp.maximum(-wz, 0.0) + jnp.log(1.0 + jnp.exp(-jnp.abs(wz)))
        lw_out[z] = -jnp.exp(-sp - 0.5)
        az = _sigmoid(a0_ref[z] + lo[:, (2 + z) * B_WIDTH:(3 + z) * B_WIDTH])
        ke = k * (1.0 + (az - 1.0) * ka_ref[...])
        ke_out[z] = ke
        b_out[z] = kkn * az
        ke_sum = ke_sum + ke
    r_out[...] = r
    kkn_out[...] = kkn
    v_out[...] = v
    bonus_out[...] = _head_sum(r * (0.5 * ke_sum) * rk_ref[...], ones) * v


def _rwkv_features(pb, mu, lora, w0, a0, k_k, k_a, r_k, ones, tiles_per_seq):
    n = pb.shape[0]
    nt = n // ROW_TILE
    h8 = ROW_TILE // 8
    row = lambda i: (i, 0)
    const2 = lambda i: (0, 0)
    const3 = lambda i: (0, 0, 0)
    vec = pl.BlockSpec((1, B_WIDTH), const2)
    one = jax.ShapeDtypeStruct((n, B_WIDTH), F32)
    two = jax.ShapeDtypeStruct((2, n, B_WIDTH), F32)
    ospec1 = pl.BlockSpec((ROW_TILE, B_WIDTH), row)
    ospec2 = pl.BlockSpec((2, ROW_TILE, B_WIDTH), lambda i: (0, i, 0))
    return pl.pallas_call(
        functools.partial(_rwkv_feat_kernel, tiles_per_seq=tiles_per_seq),
        grid=(nt,),
        in_specs=[pl.BlockSpec((ROW_TILE, B_SHIFT_WIDTH), row),
                  pl.BlockSpec((8, B_SHIFT_WIDTH), lambda i: (jnp.maximum(i * h8 - 1, 0), 0)),
                  pl.BlockSpec((8, B_SHIFT_WIDTH), lambda i: (jnp.minimum((i + 1) * h8, n // 8 - 1), 0)),
                  pl.BlockSpec((1, B_SHIFT_WIDTH), const2),
                  pl.BlockSpec(lora.shape, const2),
                  pl.BlockSpec((2, 1, B_WIDTH), const3),
                  pl.BlockSpec((2, 1, B_WIDTH), const3),
                  vec, vec, vec,
                  pl.BlockSpec((B_WIDTH, B_WIDTH), const2)],
        out_specs=[ospec1, ospec1, ospec1, ospec2, ospec2, ospec2, ospec1],
        out_shape=[one, one, one, two, two, two, one],
        compiler_params=_params(1),
        name="rwkv_features",
    )(pb, pb, pb, mu, lora, w0, a0, k_k, k_a, r_k, ones)


def _rwkv_scan_kernel(rf_ref, kf_ref, vf_ref, rb_ref, kb_ref, vb_ref, lwf_ref, bf_ref, kef_ref,
                      lwb_ref, bb_ref, keb_ref, s0_ref, yf_ref, yb_ref, sout_ref, s_scr, *, n_chunks):
    c = pl.program_id(1)

    @pl.when(c == 0)
    def _():
        s_scr[...] = s0_ref[...]

    row = lax.broadcasted_iota(jnp.int32, (CHUNK, LANES), 0)
    li = lax.broadcasted_iota(jnp.int32, (CHUNK, LANES), 1) % CHUNK
    blk16 = (row // 16) == (li // 16)
    blk32 = (row // 32) == (li // 32)
    eye = jnp.where(row == li, 1.0, 0.0)
    srow = lax.broadcasted_iota(jnp.int32, (PAIR, PAIR), 0)
    scol = lax.broadcasted_iota(jnp.int32, (PAIR, PAIR), 1)
    same_head = (srow // CHUNK) == (scol // CHUNK)
    tr = lax.broadcasted_iota(jnp.int32, (CHUNK, CHUNK), 0)
    tc = lax.broadcasted_iota(jnp.int32, (CHUNK, CHUNK), 1)
    dirs = ((rf_ref, kf_ref, vf_ref, lwf_ref, bf_ref, kef_ref, yf_ref),
            (rb_ref, kb_ref, vb_ref, lwb_ref, bb_ref, keb_ref, yb_ref))

    nseq = rf_ref.shape[0]
    chains = [(s, z, p) for s in range(nseq) for z in range(2) for p in range(B_HEADS // 2)]
    sls = [slice(p * LANES, (p + 1) * LANES) for _, _, p in chains]
    before, before_eq, tot, kkd, rd, bg, kg, vv, sc = [], [], [], [], [], [], [], [], []
    for s, z in [(s, z) for s in range(nseq) for z in range(2)]:
        r_ref, kk_ref, v_ref, lw_ref, b_ref, ke_ref, _ = dirs[z]
        diff = (li - row) if z else (row - li)
        tri = jnp.where(((tc - tr) if z else (tr - tc)) >= 0, 1.0, 0.0).astype(BF16)
        lw_all = lw_ref[s]
        cum_all = _split_dot(tri, lw_all)
        for p in range(B_HEADS // 2):
            sl = slice(p * LANES, (p + 1) * LANES)
            lw = lw_all[:, sl]
            cum = cum_all[:, sl]
            tt = cum[0:1] if z else cum[CHUNK - 1:CHUNK]
            b = b_ref[s, :, sl]
            ke = ke_ref[s, :, sl]
            g_inv = jnp.exp(-cum)
            g_rest = jnp.exp(tt - cum)
            before.append(diff > 0)
            before_eq.append(diff >= 0)
            tot.append(tt)
            kkd.append(kk_ref[s, :, sl] * jnp.exp(cum - lw))
            rd.append(r_ref[s, :, sl] * jnp.exp(cum))
            bg.append(b * g_rest)
            kg.append(ke * g_rest)
            vv.append(v_ref[s, :, sl])
            sc.append(_bdot(jnp.concatenate([kkd[-1], rd[-1]], 0),
                            jnp.concatenate([_stack_pair(b * g_inv), _stack_pair(ke * g_inv)], 0), NT))
    lb = [jnp.where(m, s[:CHUNK, :LANES], 0.0) for m, s in zip(before, sc)]
    mb = [jnp.where(m, s[CHUNK:, :LANES], 0.0) for m, s in zip(before_eq, sc)]
    lv = [_bdot(jnp.concatenate([jnp.where(m, s[:CHUNK, LANES:], 0.0),
                                 jnp.where(me, s[CHUNK:, LANES:], 0.0)], 0), _stack_pair(v))
          for m, me, s, v in zip(before, before_eq, sc, vv)]

    pw = [jnp.where(blk16, x, 0.0) for x in lb]
    tinv = [eye - x for x in pw]
    for _ in range(3):
        pw = [_bdot(x, _stack_pair(x)) for x in pw]
        tinv = [t + _bdot(t, _stack_pair(x)) for t, x in zip(tinv, pw)]
    for off in (blk32 & ~blk16, ~blk32):
        tmp = [_bdot(t, _stack_pair(jnp.where(off, x, 0.0))) for t, x in zip(tinv, lb)]
        tinv = [t - _bdot(m, _stack_pair(t)) for t, m in zip(tinv, tmp)]

    w12 = [_bdot(t, jnp.concatenate([_stack_pair(k), _stack_pair(x[:CHUNK])], 1))
           for t, k, x in zip(tinv, kkd, lv)]
    s_old = [s_scr[s, z, p] for s, z, p in chains]
    ws = [_bdot(jnp.concatenate([w[:, :LANES], r], 0), s, NT) for w, r, s in zip(w12, rd, s_old)]
    u = [-(a[:CHUNK] + w[:, LANES:]) for a, w in zip(ws, w12)]
    mbu = [_bdot(m, _stack_pair(x)) for m, x in zip(mb, u)]
    upd = [_bdot(jnp.concatenate([x, v], 0).T, jnp.concatenate([b_, k_], 0))
           for x, v, b_, k_ in zip(u, vv, bg, kg)]
    for i, (s, z, p) in enumerate(chains):
        dirs[z][6][s, :, sls[i]] = ws[i][CHUNK:] + mbu[i] + lv[i][CHUNK:]
        s_scr[s, z, p] = s_old[i] * jnp.exp(tot[i]) + jnp.where(same_head, upd[i], 0.0)

    @pl.when(c == n_chunks - 1)
    def _():
        sout_ref[...] = s_scr[...]


def _rwkv_scan(r, kkn, v, lw, b, ke, s0, bn, t):
    nc = t // CHUNK
    npair = B_HEADS // 2
    ns = SCAN_SEQS
    fwd = pl.BlockSpec((ns, CHUNK, B_WIDTH), lambda g, c: (g, c, 0))
    bwd = pl.BlockSpec((ns, CHUNK, B_WIDTH), lambda g, c: (g, nc - 1 - c, 0))
    dfwd = pl.BlockSpec((None, ns, CHUNK, B_WIDTH), lambda g, c: (0, g, c, 0))
    dbwd = pl.BlockSpec((None, ns, CHUNK, B_WIDTH), lambda g, c: (1, g, nc - 1 - c, 0))
    state = pl.BlockSpec((ns, 2, npair, PAIR, PAIR), lambda g, c: (g, 0, 0, 0, 0))
    r3, k3, v3 = (a.reshape(bn, t, B_WIDTH) for a in (r, kkn, v))
    lw4, b4, ke4 = (a.reshape(2, bn, t, B_WIDTH) for a in (lw, b, ke))
    seq = jax.ShapeDtypeStruct((bn, t, B_WIDTH), F32)
    return pl.pallas_call(
        functools.partial(_rwkv_scan_kernel, n_chunks=nc),
        grid=(bn // ns, nc),
        in_specs=[fwd, fwd, fwd, bwd, bwd, bwd, dfwd, dfwd, dfwd, dbwd, dbwd, dbwd, state],
        out_specs=[fwd, bwd, state],
        out_shape=[seq, seq, jax.ShapeDtypeStruct((bn, 2, npair, PAIR, PAIR), F32)],
        scratch_shapes=[pltpu.VMEM((ns, 2, npair, PAIR, PAIR), F32)],
        compiler_params=_params(2),
        name="rwkv_scan",
    )(r3, k3, v3, r3, k3, v3, lw4, b4, ke4, lw4, b4, ke4, s0)


def _post(x, z, gate, gain):
    zn = z * lax.rsqrt(jnp.mean(z * z, -1, keepdims=True) + NORM_EPS)
    return x + gate * (zn * gain)


def _even_out_kernel(ya_ref, ga_ref, yf_ref, yb_ref, bonus_ref, gb_ref, lnw_ref, lnb_ref, ones_ref,
                     x_ref, gate_ref, gain_ref, w_ref, o_ref):
    y = yf_ref[...] + yb_ref[...]
    ones = ones_ref[...]
    mean = _head_sum(y, ones) * (1.0 / HEAD_DIM)
    d = y - mean
    var = _head_sum(d * d, ones) * (1.0 / HEAD_DIM)
    yb = (d * lax.rsqrt(var + RWKV_LN_EPS)) * lnw_ref[...] + lnb_ref[...] + bonus_ref[...]
    za = (ya_ref[...] * _silu(ga_ref[...])).astype(BF16)
    zb = (yb * _silu(gb_ref[...])).astype(BF16)
    z = (jnp.dot(za, w_ref[0:A_WIDTH, :], preferred_element_type=F32)
         + jnp.dot(zb, w_ref[A_WIDTH:, :], preferred_element_type=F32))
    o_ref[...] = _post(x_ref[...], z, gate_ref[...], gain_ref[...])


def _even_out(ya, ga, yf, yb, bonus, gb, ln_w, ln_b, ones, x2d, gate, gain, w_bf16, tiles_per_seq):
    n = x2d.shape[0]
    row = lambda i: (i, 0)
    const2 = lambda i: (0, 0)
    half = pl.BlockSpec((ROW_TILE, B_WIDTH), row)
    full = pl.BlockSpec((ROW_TILE, D_MODEL), row)
    vec = pl.BlockSpec((1, B_WIDTH), const2)
    return pl.pallas_call(
        _even_out_kernel,
        grid=(n // ROW_TILE,),
        in_specs=[half, half, half, half, half, half, vec, vec,
                  pl.BlockSpec((B_WIDTH, B_WIDTH), const2),
                  full,
                  pl.BlockSpec((None, 1, D_MODEL), lambda i: (i // tiles_per_seq, 0, 0)),
                  pl.BlockSpec((1, D_MODEL), const2),
                  pl.BlockSpec((D_MODEL, D_MODEL), const2)],
        out_specs=full,
        out_shape=jax.ShapeDtypeStruct((n, D_MODEL), F32),
        compiler_params=_params(1),
        name="even_out",
    )(ya, ga, yf, yb, bonus, gb, ln_w, ln_b, ones, x2d, gate, gain, w_bf16)


def _odd_out_kernel(o_ref_in, gc_ref, x_ref, gate_ref, gain_ref, w_ref, o_ref):
    zc = (o_ref_in[...] * _silu(gc_ref[...])).astype(BF16)
    z = jnp.dot(zc, w_ref[...], preferred_element_type=F32)
    o_ref[...] = _post(x_ref[...], z, gate_ref[...], gain_ref[...])


def _odd_out(o, gc, x2d, gate, gain, w_bf16, tiles_per_seq):
    n = x2d.shape[0]
    row = lambda i: (i, 0)
    const2 = lambda i: (0, 0)
    full = pl.BlockSpec((ROW_TILE, D_MODEL), row)
    return pl.pallas_call(
        _odd_out_kernel,
        grid=(n // ROW_TILE,),
        in_specs=[full, full, full,
                  pl.BlockSpec((None, 1, D_MODEL), lambda i: (i // tiles_per_seq, 0, 0)),
                  pl.BlockSpec((1, D_MODEL), const2),
                  pl.BlockSpec((D_MODEL, D_MODEL), const2)],
        out_specs=full,
        out_shape=jax.ShapeDtypeStruct((n, D_MODEL), F32),
        compiler_params=_params(1),
        name="odd_out",
    )(o, gc, x2d, gate, gain, w_bf16)


def _lambda(lq1_ref, lk1_ref, lq2_ref, lk2_ref, lam_init):
    s1 = jnp.sum(lq1_ref[...] * lk1_ref[...], -1, keepdims=True)
    s2 = jnp.sum(lq2_ref[...] * lk2_ref[...], -1, keepdims=True)
    return jnp.exp(s1) - jnp.exp(s2) + lam_init


def _subln(o, gain, lam_init):
    on = o * lax.rsqrt(jnp.mean(o * o, -1, keepdims=True) + SUBLN_EPS)
    return (on * gain) * (1.0 - lam_init)


def _diff_ctx_kernel(lq1_ref, lk1_ref, lq2_ref, lk2_ref, sub_ref, q_ref, k_ref, v_ref, o_ref,
                     *, lam_init):
    t = q_ref.shape[0]
    lam = _lambda(lq1_ref, lk1_ref, lq2_ref, lk2_ref, lam_init)
    for h in range(C_HEADS):
        sl = slice(h * LANES, (h + 1) * LANES)
        qs = _stack_pair(q_ref[:, sl] * (C_QK_DIM ** -0.5 * LOG2E))
        s2 = _bdot(qs, k_ref[:, sl], NT)
        e = jnp.exp2(s2 - jnp.max(s2, -1, keepdims=True))
        ov = _bdot(e, v_ref[:, sl]) * (1.0 / jnp.sum(e, -1, keepdims=True))
        o_ref[:, sl] = _subln(ov[:t] - lam * ov[t:], sub_ref[...], lam_init)


def _diff_context(lams, subln, q, k, v, bn, t, lam_init):
    seq = pl.BlockSpec((None, t, C_WIDTH), lambda b: (b, 0, 0))
    small = pl.BlockSpec((1, C_QK_DIM), lambda b: (0, 0))
    return pl.pallas_call(
        functools.partial(_diff_ctx_kernel, lam_init=lam_init),
        grid=(bn,),
        in_specs=[small, small, small, small, pl.BlockSpec((1, C_V_DIM), lambda b: (0, 0)),
                  seq, seq, seq],
        out_specs=seq,
        out_shape=jax.ShapeDtypeStruct((bn, t, C_WIDTH), F32),
        compiler_params=_params(1),
        name="diff_context",
    )(*lams, subln, q.reshape(bn, t, C_WIDTH), k.reshape(bn, t, C_WIDTH), v.reshape(bn, t, C_WIDTH))


def _diff_lat_kernel(lq1_ref, lk1_ref, lq2_ref, lk2_ref, sub_ref, q_ref, k_ref, v_ref, kx_ref, vx_ref,
                     o_ref, *, lam_init):
    tq = q_ref.shape[0]
    lam = _lambda(lq1_ref, lk1_ref, lq2_ref, lk2_ref, lam_init)
    qs = _stack_pair(q_ref[...] * (C_QK_DIM ** -0.5 * LOG2E))
    sx = _bdot(qs, kx_ref[...], NT)
    sl = _bdot(qs, k_ref[...], NT)
    m = jnp.maximum(jnp.max(sx, -1, keepdims=True), jnp.max(sl, -1, keepdims=True))
    ex = jnp.exp2(sx - m)
    el = jnp.exp2(sl - m)
    inv = 1.0 / (jnp.sum(ex, -1, keepdims=True) + jnp.sum(el, -1, keepdims=True))
    c0 = inv[:tq]
    c1 = lam * inv[tq:]
    o = (_bdot(ex[:tq] * c0 - ex[tq:] * c1, vx_ref[...])
         + _bdot(el[:tq] * c0 - el[tq:] * c1, v_ref[...]))
    o_ref[...] = _subln(o, sub_ref[...], lam_init)


def _diff_latent(lams, subln, q, k, v, k_ctx, v_ctx, bn, t, lam_init):
    nb = t // DIFF_Q_TILE
    nctx = k_ctx.shape[1]
    small = pl.BlockSpec((1, C_QK_DIM), lambda b, h, n: (0, 0))
    qspec = pl.BlockSpec((None, DIFF_Q_TILE, LANES), lambda b, h, n: (b, n, h))
    kvspec = pl.BlockSpec((None, t, LANES), lambda b, h, n: (b, 0, h))
    cxspec = pl.BlockSpec((None, nctx, LANES), lambda b, h, n: (b, 0, h))
    return pl.pallas_call(
        functools.partial(_diff_lat_kernel, lam_init=lam_init),
        grid=(bn, C_HEADS, nb),
        in_specs=[small, small, small, small,
                  pl.BlockSpec((1, C_V_DIM), lambda b, h, n: (0, 0)),
                  qspec, kvspec, kvspec, cxspec, cxspec],
        out_specs=qspec,
        out_shape=jax.ShapeDtypeStruct((bn, t, C_WIDTH), F32),
        compiler_params=_params(3),
        name="diff_latent",
    )(*lams, subln, q.reshape(bn, t, C_WIDTH), k.reshape(bn, t, C_WIDTH), v.reshape(bn, t, C_WIDTH),
      k_ctx, v_ctx)


def _rope_tables(t):
    nf = HEAD_DIM // 4
    inv = 1.0 / (ROPE_BASE ** (jnp.arange(nf, dtype=F32) / nf))
    pos = jnp.arange(t)
    ang_r = (pos // GRID_W).astype(F32)[:, None] * inv[None]
    ang_c = (pos % GRID_W).astype(F32)[:, None] * inv[None]
    ang = jnp.concatenate([ang_r, ang_r, ang_c, ang_c], -1)
    cos, sin = jnp.cos(ang), jnp.sin(ang)
    sign = jnp.where((jnp.arange(HEAD_DIM) // nf) % 2 == 0, -1.0, 1.0).astype(F32)
    return jnp.tile(cos, (1, 2)), jnp.tile(sin * sign[None], (1, 2))


EVEN_SEGS = ((0, 512), (512, 640), (640, 768), (768, 1280), (1280, 2944), (2944, 3456))
ODD_SEGS = ((0, 1024), (1024, 2048), (2048, 3072), (3072, 4096))


def _pair_states(st):
    bn = st.shape[0]
    s = st.reshape(bn, 2, 4, 2, HEAD_DIM, HEAD_DIM)
    z = jnp.zeros_like(s[:, :, :, 0])
    top = jnp.concatenate([s[:, :, :, 0], z], -1)
    bot = jnp.concatenate([z, s[:, :, :, 1]], -1)
    return jnp.concatenate([top, bot], -2)


def _unpair_states(sp):
    bn = sp.shape[0]
    a = sp[:, :, :, :HEAD_DIM, :HEAD_DIM]
    b = sp[:, :, :, HEAD_DIM:, HEAD_DIM:]
    return jnp.stack([a, b], axis=3).reshape(bn, 2, B_HEADS, HEAD_DIM, HEAD_DIM)


def kernel(x_prompt, x_sample, cache_a_k, cache_a_v, state_rwkv, cache_c_k, cache_c_v, c, c_ctx,
           ada_w, ada_b, norm_pre, norm_post, w_out, even_w_in, a_sink, b_mu, b_w0, b_w2, b_a0,
           b_a2, b_kk, b_ka, b_rk, b_ln_w, b_ln_b, odd_w_in, c_lq1, c_lk1, c_lq2, c_lk2, c_subln):
    bp, tp, _ = x_prompt.shape
    bs, ts, _ = x_sample.shape
    past = cache_a_k.shape[2]

    cvecs = jnp.concatenate([c_ctx[None], c, jnp.zeros((8 - 1 - bs, D_MODEL), F32)], 0)
    mods = _modulation(cvecs, ada_w, ada_b)
    rope_tabs = _rope_tables(ts)
    ones = (jnp.arange(B_WIDTH)[:, None] // HEAD_DIM == jnp.arange(B_WIDTH)[None] // HEAD_DIM).astype(BF16)

    xp = x_prompt.reshape(bp * tp, D_MODEL)
    xs = x_sample.reshape(bs * ts, D_MODEL)

    def mod_rows(layer, rows):
        m = mods[layer, rows[0]:rows[1]]
        if rows == (0, 1):
            m = jnp.broadcast_to(m, (bp, 3 * D_MODEL))
        m = m[:, None, :]
        return m[..., :D_MODEL], m[..., D_MODEL:2 * D_MODEL], m[..., 2 * D_MODEL:]

    e = 0
    w_in0 = even_w_in[e].astype(BF16)
    w_out0 = w_out[0].astype(BF16)
    gain_pre = norm_pre[0][None]
    gain_post = norm_post[0][None]
    zpad = jnp.zeros((DECAY_LORA, B_WIDTH), F32)
    lora = jnp.concatenate([
        jnp.concatenate([b_w2[e, 0], b_w2[e, 1], zpad, zpad], 1),
        jnp.concatenate([zpad, zpad, b_a2[e, 0], b_a2[e, 1]], 1)], 0).astype(BF16)
    feat_args = (b_mu[e][None], lora, b_w0[e][:, None, :], b_a0[e][:, None, :], b_kk[e][None],
                 b_ka[e][None], b_rk[e][None], ones)

    def even_layer(x2d, bn, t, rows, ctx):
        tps = t // ROW_TILE
        shift, scale, gate = mod_rows(0, rows)
        qa, ka, va, ga, pb, gb = _proj_in(x2d, shift, scale, gain_pre, w_in0, EVEN_SEGS, (0, 1), tps,
                                          None if ctx is None else rope_tabs)
        if ctx is None:
            ya = _gqa_context(a_sink[e], qa, ka, va, bn, t)
            s0 = jnp.zeros((bn, 2, B_HEADS // 2, PAIR, PAIR), F32)
        else:
            k_ctx, v_ctx, st0 = ctx
            ya = _gqa_window(a_sink[e], qa, ka, va, k_ctx, v_ctx, bn, t)
            s0 = _pair_states(st0)
        r, kkn, v, lw, b, ke, bonus = _rwkv_features(pb, *feat_args, tps)
        yf, yb, s_fin = _rwkv_scan(r, kkn, v, lw, b, ke, s0, bn, t)
        y = _even_out(ya.reshape(bn * t, A_WIDTH), ga, yf.reshape(bn * t, B_WIDTH),
                      yb.reshape(bn * t, B_WIDTH), bonus, gb,
                      b_ln_w[e][None], b_ln_b[e][None], ones, x2d, gate, gain_post, w_out0, tps)
        return y, ka, va, s_fin

    yp, ka_p, va_p, st_p = even_layer(xp, bp, tp, (0, 1), None)
    ctx_a = (cache_a_k[:, e].reshape(bs, past, A_KV_WIDTH), cache_a_v[:, e].reshape(bs, past, A_KV_WIDTH),
             state_rwkv[:, e])
    ys, _, _, _ = even_layer(xs, bs, ts, (1, 1 + bs), ctx_a)

    o = 0
    lam_init = 0.8 - 0.6 * math.exp(-0.3 * 1)
    w_in1 = odd_w_in[o].astype(BF16)
    w_out1 = w_out[1].astype(BF16)
    gain_pre1 = norm_pre[1][None]
    gain_post1 = norm_post[1][None]
    lams = (c_lq1[o][None], c_lk1[o][None], c_lq2[o][None], c_lk2[o][None])
    subln = c_subln[o][None]

    def odd_layer(x2d, bn, t, rows, ctx):
        tps = t // ROW_TILE
        shift, scale, gate = mod_rows(1, rows)
        qc, kc, vc, gc = _proj_in(x2d, shift, scale, gain_pre1, w_in1, ODD_SEGS, (0, 1), tps,
                                  None if ctx is None else rope_tabs)
        if ctx is None:
            oc = _diff_context(lams, subln, qc, kc, vc, bn, t, lam_init)
        else:
            k_ctx, v_ctx = ctx
            oc = _diff_latent(lams, subln, qc, kc, vc, k_ctx, v_ctx, bn, t, lam_init)
        y = _odd_out(oc.reshape(bn * t, C_WIDTH), gc, x2d, gate, gain_post1, w_out1, tps)
        return y, kc, vc

    yp2, kc_p, vc_p = odd_layer(yp, bp, tp, (0, 1), None)
    ctx_c = (cache_c_k[:, o].reshape(bs, past, C_WIDTH), cache_c_v[:, o].reshape(bs, past, C_WIDTH))
    ys2, _, _ = odd_layer(ys, bs, ts, (1, 1 + bs), ctx_c)

    return (yp2.reshape(bp, tp, D_MODEL),
            ys2.reshape(bs, ts, D_MODEL),
            ka_p.reshape(bp, 1, tp, A_KV_HEADS, HEAD_DIM),
            va_p.reshape(bp, 1, tp, A_KV_HEADS, HEAD_DIM),
            _unpair_states(st_p)[:, None],
            kc_p.reshape(bp, 1, tp, C_HEADS, 2 * C_QK_DIM),
            vc_p.reshape(bp, 1, tp, C_HEADS, C_V_DIM))
```

```python
import functools
import math

import jax
import jax.numpy as jnp
from jax import lax
from jax.experimental import pallas as pl
from jax.experimental.pallas import tpu as pltpu

F32 = jnp.float32
BF16 = jnp.bfloat16

D_MODEL = 1024
DEPTH = 2
GRID_W = 64
HEAD_DIM = 64
ROPE_BASE = 10000.0
A_HEADS = 8
A_KV_HEADS = 2
A_WIDTH = 512
A_KV_WIDTH = 128
WINDOW = 128
BLOCK = 128
B_HEADS = 8
B_WIDTH = 512
DECAY_LORA = 64
AAA_LORA = 64
B_SHIFT_WIDTH = 3 * B_WIDTH + DECAY_LORA + AAA_LORA
C_HEADS = 8
C_QK_DIM = 64
C_V_DIM = 128
C_WIDTH = 1024
EVEN_IN = 3456
ODD_IN = 4096
NORM_EPS = 1e-6
RWKV_LN_EPS = 64e-5
SUBLN_EPS = 1e-5
NEG_INF = -1e30

LANES = 128
ROW_TILE = 256
DIFF_Q_TILE = 256
DIFF_Q_SUB = 128
CHUNK = 64
SCAN_SEQS = 4
PAIR = 2 * CHUNK
VMEM_LIMIT = 48 * 1024 * 1024

LOG2E = math.log2(math.e)

NN = ((1,), (0,))
NT = ((1,), (1,))


def _bdot(a, b, dims=NN):
    return lax.dot_general(a.astype(BF16), b.astype(BF16), (dims, ((), ())),
                           preferred_element_type=F32)


def _hdot(a, b):
    return lax.dot_general(a, b, (NN, ((), ())), precision=lax.Precision.HIGHEST,
                           preferred_element_type=F32)


def _split_dot(sel_bf16, x):
    x1 = x.astype(BF16)
    r1 = x - x1.astype(F32)
    x2 = r1.astype(BF16)
    x3 = (r1 - x2.astype(F32)).astype(BF16)
    dot = lambda p: jnp.dot(sel_bf16, p, preferred_element_type=F32)
    return dot(x1) + dot(x2) + dot(x3)


def _head_sum(x, ones_bf16):
    hi = x.astype(BF16)
    lo = (x - hi.astype(F32)).astype(BF16)
    return (jnp.dot(hi, ones_bf16, preferred_element_type=F32)
            + jnp.dot(lo, ones_bf16, preferred_element_type=F32))


def _sigmoid(x):
    return 1.0 / (1.0 + jnp.exp(-x))


def _silu(x):
    return x * _sigmoid(x)


def _params(n_axes):
    return pltpu.CompilerParams(dimension_semantics=("arbitrary",) * n_axes,
                                vmem_limit_bytes=VMEM_LIMIT)


def _lane_lo(shape):
    return (lax.broadcasted_iota(jnp.int32, shape, len(shape) - 1) % LANES) < HEAD_DIM


def _stack_pair(x):
    lo = _lane_lo(x.shape)
    z = jnp.zeros_like(x)
    return jnp.concatenate([jnp.where(lo, x, z), jnp.where(lo, z, x)], axis=0)


def _dup_half(x, g):
    lo = _lane_lo(x.shape)
    sw = pltpu.roll(x, HEAD_DIM, 1)
    return jnp.where(lo, x, sw) if g == 0 else jnp.where(lo, sw, x)


def _rope(x, cos, sin_signed):
    out = []
    even_q = (lax.broadcasted_iota(jnp.int32, cos.shape, 1) // (HEAD_DIM // 4)) % 2 == 0
    for g in range(x.shape[1] // LANES):
        xg = x[:, g * LANES:(g + 1) * LANES]
        up = pltpu.roll(xg, LANES - HEAD_DIM // 4, 1)
        dn = pltpu.roll(xg, HEAD_DIM // 4, 1)
        out.append(xg * cos + jnp.where(even_q, up, dn) * sin_signed)
    return out


def _mod_kernel(c_ref, w_ref, b_ref, o_ref):
    s = _silu(c_ref[...])
    o_ref[...] = _hdot(s, w_ref[...]) + b_ref[...]


def _modulation(cvecs, ada_w, ada_b):
    nrow = cvecs.shape[0]
    return pl.pallas_call(
        _mod_kernel,
        grid=(DEPTH, 3),
        in_specs=[pl.BlockSpec((nrow, D_MODEL), lambda l, j: (0, 0)),
                  pl.BlockSpec((None, D_MODEL, D_MODEL), lambda l, j: (l, 0, j)),
                  pl.BlockSpec((None, 1, D_MODEL), lambda l, j: (l, 0, j))],
        out_specs=pl.BlockSpec((None, nrow, D_MODEL), lambda l, j: (l, 0, j)),
        out_shape=jax.ShapeDtypeStruct((DEPTH, nrow, 3 * D_MODEL), F32),
        compiler_params=_params(2),
        name="modulation",
    )(cvecs, ada_w, ada_b.reshape(DEPTH, 1, 3 * D_MODEL))


def _proj_in_kernel(*refs, segs, rope_segs, use_rope):
    if use_rope:
        x_ref, sh_ref, sc_ref, g_ref, w_ref, cos_ref, sin_ref = refs[:7]
        outs = refs[7:]
    else:
        x_ref, sh_ref, sc_ref, g_ref, w_ref = refs[:5]
        outs = refs[5:]
    x = x_ref[...]
    y = x * lax.rsqrt(jnp.mean(x * x, -1, keepdims=True) + NORM_EPS)
    h = (y * g_ref[...]) * (1.0 + sc_ref[...]) + sh_ref[...]
    hb = h.astype(BF16)
    for idx, (lo, hi) in enumerate(segs):
        o = jnp.dot(hb, w_ref[:, lo:hi], preferred_element_type=F32)
        if use_rope and idx in rope_segs:
            parts = _rope(o, cos_ref[...], sin_ref[...])
            for g, part in enumerate(parts):
                outs[idx][:, g * LANES:(g + 1) * LANES] = part
        else:
            outs[idx][...] = o


def _proj_in(x2d, shift, scale, gain, w_bf16, segs, rope_segs, tiles_per_seq, rope_tabs):
    n = x2d.shape[0]
    nt = n // ROW_TILE
    use_rope = rope_tabs is not None
    row = lambda i: (i, 0)
    per_b = lambda i: (i // tiles_per_seq, 0, 0)
    in_specs = [pl.BlockSpec((ROW_TILE, D_MODEL), row),
                pl.BlockSpec((None, 1, D_MODEL), per_b),
                pl.BlockSpec((None, 1, D_MODEL), per_b),
                pl.BlockSpec((1, D_MODEL), lambda i: (0, 0)),
                pl.BlockSpec(w_bf16.shape, lambda i: (0, 0))]
    args = [x2d, shift, scale, gain, w_bf16]
    if use_rope:
        pos = lambda i: (i % tiles_per_seq, 0)
        in_specs += [pl.BlockSpec((ROW_TILE, LANES), pos), pl.BlockSpec((ROW_TILE, LANES), pos)]
        args += list(rope_tabs)
    return pl.pallas_call(
        functools.partial(_proj_in_kernel, segs=segs, rope_segs=rope_segs, use_rope=use_rope),
        grid=(nt,),
        in_specs=in_specs,
        out_specs=[pl.BlockSpec((ROW_TILE, hi - lo), row) for lo, hi in segs],
        out_shape=[jax.ShapeDtypeStruct((n, hi - lo), F32) for lo, hi in segs],
        compiler_params=_params(1),
        name="proj_in",
    )(*args)


def _sink_exp_pair(s2, sink_a, sink_b, n):
    rowi = lax.broadcasted_iota(jnp.int32, (2 * n, 1), 0)
    sink2 = jnp.where(rowi < n, sink_a, sink_b) * LOG2E
    m = jnp.maximum(jnp.max(s2, -1, keepdims=True), sink2)
    e = jnp.exp2(s2 - m)
    return e, 1.0 / (jnp.sum(e, -1, keepdims=True) + jnp.exp2(sink2 - m))


def _gqa_ctx_kernel(sink_ref, q_ref, k_ref, v_ref, o_ref):
    t = q_ref.shape[0]
    k = k_ref[...]
    v = v_ref[...]
    kd = [_dup_half(k, g).astype(BF16) for g in range(A_KV_HEADS)]
    vd = [_dup_half(v, g).astype(BF16) for g in range(A_KV_HEADS)]
    s2 = [_bdot(_stack_pair(q_ref[:, p * LANES:(p + 1) * LANES] * (HEAD_DIM ** -0.5 * LOG2E)),
                kd[p // 2], NT) for p in range(A_HEADS // 2)]
    for p in range(A_HEADS // 2):
        e, inv = _sink_exp_pair(s2[p], sink_ref[2 * p], sink_ref[2 * p + 1], t)
        o = _bdot(e, vd[p // 2]) * inv
        o_ref[:, p * LANES:(p + 1) * LANES] = jnp.where(_lane_lo((t, LANES)), o[:t], o[t:])


def _gqa_context(sink, q, k, v, bn, t):
    seq = lambda b: (b, 0, 0)
    return pl.pallas_call(
        _gqa_ctx_kernel,
        grid=(bn,),
        in_specs=[pl.BlockSpec(memory_space=pltpu.SMEM),
                  pl.BlockSpec((None, t, A_WIDTH), seq),
                  pl.BlockSpec((None, t, A_KV_WIDTH), seq),
                  pl.BlockSpec((None, t, A_KV_WIDTH), seq)],
        out_specs=pl.BlockSpec((None, t, A_WIDTH), seq),
        out_shape=jax.ShapeDtypeStruct((bn, t, A_WIDTH), F32),
        compiler_params=_params(1),
        name="gqa_context",
    )(sink, q.reshape(bn, t, A_WIDTH), k.reshape(bn, t, A_KV_WIDTH), v.reshape(bn, t, A_KV_WIDTH))


def _gqa_win_kernel(sink_ref, q_ref, kp_ref, kc_ref, kn_ref, vp_ref, vc_ref, vn_ref,
                    kx_ref, vx_ref, o_ref, *, t):
    nblk = pl.program_id(1)
    nctx = kx_ref.shape[0]
    keys = jnp.concatenate([kx_ref[...], kp_ref[...], kc_ref[...], kn_ref[...]], axis=0)
    vals = jnp.concatenate([vx_ref[...], vp_ref[...], vc_ref[...], vn_ref[...]], axis=0)
    nkeys = nctx + 3 * BLOCK
    rowi = lax.broadcasted_iota(jnp.int32, (2 * BLOCK, nkeys), 0)
    coli = lax.broadcasted_iota(jnp.int32, (2 * BLOCK, nkeys), 1)
    qpos = nblk * BLOCK + rowi % BLOCK
    kpos = nblk * BLOCK + coli - nctx - BLOCK
    valid = (coli < nctx) | ((jnp.abs(qpos - kpos) <= WINDOW) & (kpos >= 0) & (kpos < t))
    kd = [_dup_half(keys, g).astype(BF16) for g in range(A_KV_HEADS)]
    vd = [_dup_half(vals, g).astype(BF16) for g in range(A_KV_HEADS)]
    s2 = [_bdot(_stack_pair(q_ref[:, p * LANES:(p + 1) * LANES] * (HEAD_DIM ** -0.5 * LOG2E)),
                kd[p // 2], NT) for p in range(A_HEADS // 2)]
    for p in range(A_HEADS // 2):
        e, inv = _sink_exp_pair(jnp.where(valid, s2[p], NEG_INF), sink_ref[2 * p], sink_ref[2 * p + 1], BLOCK)
        o = _bdot(e, vd[p // 2]) * inv
        o_ref[:, p * LANES:(p + 1) * LANES] = jnp.where(_lane_lo((BLOCK, LANES)), o[:BLOCK], o[BLOCK:])


def _gqa_window(sink, q, k, v, k_ctx, v_ctx, bn, t):
    nb = t // BLOCK
    nctx = k_ctx.shape[1]
    cur = lambda b, n: (b, n, 0)
    prev = lambda b, n: (b, jnp.maximum(n - 1, 0), 0)
    nxt = lambda b, n: (b, jnp.minimum(n + 1, nb - 1), 0)
    ctx = lambda b, n: (b, 0, 0)
    kv = lambda f: pl.BlockSpec((None, BLOCK, A_KV_WIDTH), f)
    k3 = k.reshape(bn, t, A_KV_WIDTH)
    v3 = v.reshape(bn, t, A_KV_WIDTH)
    return pl.pallas_call(
        functools.partial(_gqa_win_kernel, t=t),
        grid=(bn, nb),
        in_specs=[pl.BlockSpec(memory_space=pltpu.SMEM),
                  pl.BlockSpec((None, BLOCK, A_WIDTH), cur),
                  kv(prev), kv(cur), kv(nxt), kv(prev), kv(cur), kv(nxt),
                  pl.BlockSpec((None, nctx, A_KV_WIDTH), ctx),
                  pl.BlockSpec((None, nctx, A_KV_WIDTH), ctx)],
        out_specs=pl.BlockSpec((None, BLOCK, A_WIDTH), cur),
        out_shape=jax.ShapeDtypeStruct((bn, t, A_WIDTH), F32),
        compiler_params=_params(2),
        name="gqa_window",
    )(sink, q.reshape(bn, t, A_WIDTH), k3, k3, k3, v3, v3, v3, k_ctx, v_ctx)


def _rwkv_feat_kernel(p_ref, pp_ref, pn_ref, mu_ref, lora_ref, w0_ref, a0_ref, kk_ref, ka_ref,
                      rk_ref, ones_ref, r_out, kkn_out, v_out, lw_out, b_out, ke_out, bonus_out,
                      *, tiles_per_seq):
    i = pl.program_id(0)
    tm = p_ref.shape[0]
    pos = i % tiles_per_seq
    p = p_ref[...]
    rowi = lax.broadcasted_iota(jnp.int32, (tm, 1), 0)
    prev_row = jnp.where(pos != 0, pp_ref[7:8, :], 0.0)
    next_row = jnp.where(pos != tiles_per_seq - 1, pn_ref[0:1, :], 0.0)
    prev = jnp.where(rowi == 0, prev_row, pltpu.roll(p, 1, 0))
    nxt = jnp.where(rowi == tm - 1, next_row, pltpu.roll(p, tm - 1, 0))
    xf = p + (0.5 * (prev + nxt) - p) * mu_ref[...]
    r = xf[:, 0:B_WIDTH]
    k = xf[:, B_WIDTH:2 * B_WIDTH]
    v = xf[:, 2 * B_WIDTH:3 * B_WIDTH]
    g = xf[:, 3 * B_WIDTH:]
    g = jnp.where(_lane_lo(g.shape), jnp.tanh(g), g)
    lo = _bdot(g, lora_ref[...])
    ones = ones_ref[...]
    kkr = k * kk_ref[...]
    kkn = kkr / jnp.maximum(jnp.sqrt(_head_sum(kkr * kkr, ones)), 1e-12)
    ke_sum = jnp.zeros_like(k)
    for z in range(2):
        wz = w0_ref[z] + lo[:, z * B_WIDTH:(z + 1) * B_WIDTH]
        sp = jnp.maximum(-wz, 0.0) + jnp.log(1.0 + jnp.exp(-jnp.abs(wz)))
        lw_out[z] = -jnp.exp(-sp - 0.5)
        az = _sigmoid(a0_ref[z] + lo[:, (2 + z) * B_WIDTH:(3 + z) * B_WIDTH])
        ke = k * (1.0 + (az - 1.0) * ka_ref[...])
        ke_out[z] = ke
        b_out[z] = kkn * az
        ke_sum = ke_sum + ke
    r_out[...] = r
    kkn_out[...] = kkn
    v_out[...] = v
    bonus_out[...] = _head_sum(r * (0.5 * ke_sum) * rk_ref[...], ones) * v


def _rwkv_features(pb, mu, lora, w0, a0, k_k, k_a, r_k, ones, tiles_per_seq):
    n = pb.shape[0]
    nt = n // ROW_TILE
    h8 = ROW_TILE // 8
    row = lambda i: (i, 0)
    const2 = lambda i: (0, 0)
    const3 = lambda i: (0, 0, 0)
    vec = pl.BlockSpec((1, B_WIDTH), const2)
    one = jax.ShapeDtypeStruct((n, B_WIDTH), F32)
    two = jax.ShapeDtypeStruct((2, n, B_WIDTH), F32)
    ospec1 = pl.BlockSpec((ROW_TILE, B_WIDTH), row)
    ospec2 = pl.BlockSpec((2, ROW_TILE, B_WIDTH), lambda i: (0, i, 0))
    return pl.pallas_call(
        functools.partial(_rwkv_feat_kernel, tiles_per_seq=tiles_per_seq),
        grid=(nt,),
        in_specs=[pl.BlockSpec((ROW_TILE, B_SHIFT_WIDTH), row),
                  pl.BlockSpec((8, B_SHIFT_WIDTH), lambda i: (jnp.maximum(i * h8 - 1, 0), 0)),
                  pl.BlockSpec((8, B_SHIFT_WIDTH), lambda i: (jnp.minimum((i + 1) * h8, n // 8 - 1), 0)),
                  pl.BlockSpec((1, B_SHIFT_WIDTH), const2),
                  pl.BlockSpec(lora.shape, const2),
                  pl.BlockSpec((2, 1, B_WIDTH), const3),
                  pl.BlockSpec((2, 1, B_WIDTH), const3),
                  vec, vec, vec,
                  pl.BlockSpec((B_WIDTH, B_WIDTH), const2)],
        out_specs=[ospec1, ospec1, ospec1, ospec2, ospec2, ospec2, ospec1],
        out_shape=[one, one, one, two, two, two, one],
        compiler_params=_params(1),
        name="rwkv_features",
    )(pb, pb, pb, mu, lora, w0, a0, k_k, k_a, r_k, ones)


def _rwkv_scan_kernel(*refs, n_chunks, from_zero):
    (rf_ref, kf_ref, vf_ref, rb_ref, kb_ref, vb_ref, lwf_ref, bf_ref, kef_ref,
     lwb_ref, bb_ref, keb_ref) = refs[:12]
    if from_zero:
        yf_ref, yb_ref, sout_ref, s_scr = refs[12:]
    else:
        s0_ref, yf_ref, yb_ref, s_scr = refs[12:]
    c = pl.program_id(1)

    @pl.when(c == 0)
    def _():
        if from_zero:
            s_scr[...] = jnp.zeros_like(s_scr)
        else:
            s_scr[...] = s0_ref[...]

    row = lax.broadcasted_iota(jnp.int32, (CHUNK, LANES), 0)
    li = lax.broadcasted_iota(jnp.int32, (CHUNK, LANES), 1) % CHUNK
    blk16 = (row // 16) == (li // 16)
    blk32 = (row // 32) == (li // 32)
    eye = jnp.where(row == li, 1.0, 0.0)
    srow = lax.broadcasted_iota(jnp.int32, (PAIR, PAIR), 0)
    scol = lax.broadcasted_iota(jnp.int32, (PAIR, PAIR), 1)
    same_head = (srow // CHUNK) == (scol // CHUNK)
    tr = lax.broadcasted_iota(jnp.int32, (CHUNK, CHUNK), 0)
    tc = lax.broadcasted_iota(jnp.int32, (CHUNK, CHUNK), 1)
    dirs = ((rf_ref, kf_ref, vf_ref, lwf_ref, bf_ref, kef_ref, yf_ref),
            (rb_ref, kb_ref, vb_ref, lwb_ref, bb_ref, keb_ref, yb_ref))

    def advance(s):
        chains = [(z, p) for z in range(2) for p in range(B_HEADS // 2)]
        sls = [slice(p * LANES, (p + 1) * LANES) for _, p in chains]
        before, before_eq, tot, kkd, rd, bg, kg, vv, sc = [], [], [], [], [], [], [], [], []
        for z in range(2):
            r_ref, kk_ref, v_ref, lw_ref, b_ref, ke_ref, _ = dirs[z]
            diff = (li - row) if z else (row - li)
            tri = jnp.where(((tc - tr) if z else (tr - tc)) >= 0, 1.0, 0.0).astype(BF16)
            lw_all = lw_ref[s]
            cum_all = _split_dot(tri, lw_all)
            for p in range(B_HEADS // 2):
                sl = slice(p * LANES, (p + 1) * LANES)
                lw = lw_all[:, sl]
                cum = cum_all[:, sl]
                tt = cum[0:1] if z else cum[CHUNK - 1:CHUNK]
                b = b_ref[s, :, sl]
                ke = ke_ref[s, :, sl]
                g_inv = jnp.exp(-cum)
                g_rest = jnp.exp(tt - cum)
                before.append(diff > 0)
                before_eq.append(diff >= 0)
                tot.append(tt)
                kkd.append(kk_ref[s, :, sl] * jnp.exp(cum - lw))
                rd.append(r_ref[s, :, sl] * jnp.exp(cum))
                bg.append(b * g_rest)
                kg.append(ke * g_rest)
                vv.append(v_ref[s, :, sl])
                sc.append(_bdot(jnp.concatenate([kkd[-1], rd[-1]], 0),
                                jnp.concatenate([_stack_pair(b * g_inv), _stack_pair(ke * g_inv)], 0), NT))
            yield
        lb = [jnp.where(m, x[:CHUNK, :LANES], 0.0) for m, x in zip(before, sc)]
        mb = [jnp.where(m, x[CHUNK:, :LANES], 0.0) for m, x in zip(before_eq, sc)]
        lv = [_bdot(jnp.concatenate([jnp.where(m, x[:CHUNK, LANES:], 0.0),
                                     jnp.where(me, x[CHUNK:, LANES:], 0.0)], 0), _stack_pair(v))
              for m, me, x, v in zip(before, before_eq, sc, vv)]
        yield

        pw = [jnp.where(blk16, x, 0.0) for x in lb]
        tinv = [eye - x for x in pw]
        for _ in range(3):
            pw = [_bdot(x, _stack_pair(x)) for x in pw]
            yield
            tinv = [t + _bdot(t, _stack_pair(x)) for t, x in zip(tinv, pw)]
            yield
        for off in (blk32 & ~blk16, ~blk32):
            tmp = [_bdot(t, _stack_pair(jnp.where(off, x, 0.0))) for t, x in zip(tinv, lb)]
            yield
            tinv = [t - _bdot(m, _stack_pair(t)) for t, m in zip(tinv, tmp)]
            yield

        w12 = [_bdot(t, jnp.concatenate([_stack_pair(k), _stack_pair(x[:CHUNK])], 1))
               for t, k, x in zip(tinv, kkd, lv)]
        yield
        s_old = [s_scr[s, z, p] for z, p in chains]
        ws = [_bdot(jnp.concatenate([w[:, :LANES], r], 0), st, NT) for w, r, st in zip(w12, rd, s_old)]
        yield
        u = [-(a[:CHUNK] + w[:, LANES:]) for a, w in zip(ws, w12)]
        mbu = [_bdot(m, _stack_pair(x)) for m, x in zip(mb, u)]
        upd = [_bdot(jnp.concatenate([x, v], 0).T, jnp.concatenate([b_, k_], 0))
               for x, v, b_, k_ in zip(u, vv, bg, kg)]
        yield
        for i, (z, p) in enumerate(chains):
            dirs[z][6][s, :, sls[i]] = ws[i][CHUNK:] + mbu[i] + lv[i][CHUNK:]
            s_scr[s, z, p] = s_old[i] * jnp.exp(tot[i]) + jnp.where(same_head, upd[i], 0.0)

    live = [advance(s) for s in range(rf_ref.shape[0])]
    while live:
        live = [gen for gen in live if next(gen, "done") != "done"]

    if from_zero:
        @pl.when(c == n_chunks - 1)
        def _():
            for s in range(s_scr.shape[0]):
                for z in range(2):
                    for p in range(B_HEADS // 2):
                        st = s_scr[s, z, p]
                        sout_ref[s, z, 2 * p] = st[:HEAD_DIM, :HEAD_DIM]
                        sout_ref[s, z, 2 * p + 1] = pltpu.roll(st[HEAD_DIM:, :], HEAD_DIM, 1)[:, :HEAD_DIM]


def _rwkv_scan(r, kkn, v, lw, b, ke, s0, bn, t):
    nc = t // CHUNK
    npair = B_HEADS // 2
    ns = min(SCAN_SEQS, bn)
    fwd = pl.BlockSpec((ns, CHUNK, B_WIDTH), lambda g, c: (g, c, 0))
    bwd = pl.BlockSpec((ns, CHUNK, B_WIDTH), lambda g, c: (g, nc - 1 - c, 0))
    dfwd = pl.BlockSpec((None, ns, CHUNK, B_WIDTH), lambda g, c: (0, g, c, 0))
    dbwd = pl.BlockSpec((None, ns, CHUNK, B_WIDTH), lambda g, c: (1, g, nc - 1 - c, 0))
    r3, k3, v3 = (a.reshape(bn, t, B_WIDTH) for a in (r, kkn, v))
    lw4, b4, ke4 = (a.reshape(2, bn, t, B_WIDTH) for a in (lw, b, ke))
    seq = jax.ShapeDtypeStruct((bn, t, B_WIDTH), F32)
    in_specs = [fwd, fwd, fwd, bwd, bwd, bwd, dfwd, dfwd, dfwd, dbwd, dbwd, dbwd]
    args = [r3, k3, v3, r3, k3, v3, lw4, b4, ke4, lw4, b4, ke4]
    out_specs = [fwd, bwd]
    out_shape = [seq, seq]
    if s0 is None:
        out_specs.append(pl.BlockSpec((ns, 2, B_HEADS, HEAD_DIM, HEAD_DIM), lambda g, c: (g, 0, 0, 0, 0)))
        out_shape.append(jax.ShapeDtypeStruct((bn, 2, B_HEADS, HEAD_DIM, HEAD_DIM), F32))
    else:
        in_specs.append(pl.BlockSpec((ns, 2, npair, PAIR, PAIR), lambda g, c: (g, 0, 0, 0, 0)))
        args.append(s0)
    return pl.pallas_call(
        functools.partial(_rwkv_scan_kernel, n_chunks=nc, from_zero=s0 is None),
        grid=(bn // ns, nc),
        in_specs=in_specs,
        out_specs=out_specs,
        out_shape=out_shape,
        scratch_shapes=[pltpu.VMEM((ns, 2, npair, PAIR, PAIR), F32)],
        compiler_params=_params(2),
        name="rwkv_scan",
    )(*args)


def _post(x, z, gate, gain):
    zn = z * lax.rsqrt(jnp.mean(z * z, -1, keepdims=True) + NORM_EPS)
    return x + gate * (zn * gain)


def _even_out_kernel(ya_ref, ga_ref, yf_ref, yb_ref, bonus_ref, gb_ref, lnw_ref, lnb_ref, ones_ref,
                     x_ref, gate_ref, gain_ref, w_ref, o_ref):
    y = yf_ref[...] + yb_ref[...]
    ones = ones_ref[...]
    mean = _head_sum(y, ones) * (1.0 / HEAD_DIM)
    d = y - mean
    var = _head_sum(d * d, ones) * (1.0 / HEAD_DIM)
    yb = (d * lax.rsqrt(var + RWKV_LN_EPS)) * lnw_ref[...] + lnb_ref[...] + bonus_ref[...]
    za = (ya_ref[...] * _silu(ga_ref[...])).astype(BF16)
    zb = (yb * _silu(gb_ref[...])).astype(BF16)
    z = (jnp.dot(za, w_ref[0:A_WIDTH, :], preferred_element_type=F32)
         + jnp.dot(zb, w_ref[A_WIDTH:, :], preferred_element_type=F32))
    o_ref[...] = _post(x_ref[...], z, gate_ref[...], gain_ref[...])


def _even_out(ya, ga, yf, yb, bonus, gb, ln_w, ln_b, ones, x2d, gate, gain, w_bf16, tiles_per_seq):
    n = x2d.shape[0]
    row = lambda i: (i, 0)
    const2 = lambda i: (0, 0)
    half = pl.BlockSpec((ROW_TILE, B_WIDTH), row)
    full = pl.BlockSpec((ROW_TILE, D_MODEL), row)
    vec = pl.BlockSpec((1, B_WIDTH), const2)
    return pl.pallas_call(
        _even_out_kernel,
        grid=(n // ROW_TILE,),
        in_specs=[half, half, half, half, half, half, vec, vec,
                  pl.BlockSpec((B_WIDTH, B_WIDTH), const2),
                  full,
                  pl.BlockSpec((None, 1, D_MODEL), lambda i: (i // tiles_per_seq, 0, 0)),
                  pl.BlockSpec((1, D_MODEL), const2),
                  pl.BlockSpec((D_MODEL, D_MODEL), const2)],
        out_specs=full,
        out_shape=jax.ShapeDtypeStruct((n, D_MODEL), F32),
        compiler_params=_params(1),
        name="even_out",
    )(ya, ga, yf, yb, bonus, gb, ln_w, ln_b, ones, x2d, gate, gain, w_bf16)


def _odd_out_kernel(o_ref_in, gc_ref, x_ref, gate_ref, gain_ref, w_ref, o_ref):
    zc = (o_ref_in[...] * _silu(gc_ref[...])).astype(BF16)
    z = jnp.dot(zc, w_ref[...], preferred_element_type=F32)
    o_ref[...] = _post(x_ref[...], z, gate_ref[...], gain_ref[...])


def _odd_out(o, gc, x2d, gate, gain, w_bf16, tiles_per_seq):
    n = x2d.shape[0]
    row = lambda i: (i, 0)
    const2 = lambda i: (0, 0)
    full = pl.BlockSpec((ROW_TILE, D_MODEL), row)
    return pl.pallas_call(
        _odd_out_kernel,
        grid=(n // ROW_TILE,),
        in_specs=[full, full, full,
                  pl.BlockSpec((None, 1, D_MODEL), lambda i: (i // tiles_per_seq, 0, 0)),
                  pl.BlockSpec((1, D_MODEL), const2),
                  pl.BlockSpec((D_MODEL, D_MODEL), const2)],
        out_specs=full,
        out_shape=jax.ShapeDtypeStruct((n, D_MODEL), F32),
        compiler_params=_params(1),
        name="odd_out",
    )(o, gc, x2d, gate, gain, w_bf16)


def _lambda(lq1_ref, lk1_ref, lq2_ref, lk2_ref, lam_init):
    s1 = jnp.sum(lq1_ref[...] * lk1_ref[...], -1, keepdims=True)
    s2 = jnp.sum(lq2_ref[...] * lk2_ref[...], -1, keepdims=True)
    return jnp.exp(s1) - jnp.exp(s2) + lam_init


def _subln(o, gain, lam_init):
    on = o * lax.rsqrt(jnp.mean(o * o, -1, keepdims=True) + SUBLN_EPS)
    return (on * gain) * (1.0 - lam_init)


def _diff_ctx_kernel(lq1_ref, lk1_ref, lq2_ref, lk2_ref, sub_ref, q_ref, k_ref, v_ref, o_ref,
                     *, lam_init):
    t = q_ref.shape[0]
    lam = _lambda(lq1_ref, lk1_ref, lq2_ref, lk2_ref, lam_init)
    sls = [slice(h * LANES, (h + 1) * LANES) for h in range(C_HEADS)]
    s2 = [_bdot(_stack_pair(q_ref[:, sl] * (C_QK_DIM ** -0.5 * LOG2E)), k_ref[:, sl], NT)
          for sl in sls]
    for h, sl in enumerate(sls):
        e = jnp.exp2(s2[h] - jnp.max(s2[h], -1, keepdims=True))
        ov = _bdot(e, v_ref[:, sl]) * (1.0 / jnp.sum(e, -1, keepdims=True))
        o_ref[:, sl] = _subln(ov[:t] - lam * ov[t:], sub_ref[...], lam_init)


def _diff_context(lams, subln, q, k, v, bn, t, lam_init):
    seq = pl.BlockSpec((None, t, C_WIDTH), lambda b: (b, 0, 0))
    small = pl.BlockSpec((1, C_QK_DIM), lambda b: (0, 0))
    return pl.pallas_call(
        functools.partial(_diff_ctx_kernel, lam_init=lam_init),
        grid=(bn,),
        in_specs=[small, small, small, small, pl.BlockSpec((1, C_V_DIM), lambda b: (0, 0)),
                  seq, seq, seq],
        out_specs=seq,
        out_shape=jax.ShapeDtypeStruct((bn, t, C_WIDTH), F32),
        compiler_params=_params(1),
        name="diff_context",
    )(*lams, subln, q.reshape(bn, t, C_WIDTH), k.reshape(bn, t, C_WIDTH), v.reshape(bn, t, C_WIDTH))


def _diff_lat_kernel(lq1_ref, lk1_ref, lq2_ref, lk2_ref, sub_ref, q_ref, k_ref, v_ref, kx_ref, vx_ref,
                     o_ref, *, lam_init):
    lam = _lambda(lq1_ref, lk1_ref, lq2_ref, lk2_ref, lam_init)
    kx = kx_ref[...].astype(BF16)
    k = k_ref[...].astype(BF16)
    tq = DIFF_Q_SUB
    subs = range(q_ref.shape[0] // tq)
    scores = []
    for i in subs:
        qs = _stack_pair(q_ref[i * tq:(i + 1) * tq, :] * (C_QK_DIM ** -0.5 * LOG2E))
        scores.append((_bdot(qs, kx, NT), _bdot(qs, k, NT)))
    for i in subs:
        sx, sl = scores[i]
        m = jnp.maximum(jnp.max(sx, -1, keepdims=True), jnp.max(sl, -1, keepdims=True))
        ex = jnp.exp2(sx - m)
        el = jnp.exp2(sl - m)
        inv = 1.0 / (jnp.sum(ex, -1, keepdims=True) + jnp.sum(el, -1, keepdims=True))
        c0 = inv[:tq]
        c1 = lam * inv[tq:]
        o = (_bdot(ex[:tq] * c0 - ex[tq:] * c1, vx_ref[...])
             + _bdot(el[:tq] * c0 - el[tq:] * c1, v_ref[...]))
        o_ref[i * tq:(i + 1) * tq, :] = _subln(o, sub_ref[...], lam_init)


def _diff_latent(lams, subln, q, k, v, k_ctx, v_ctx, bn, t, lam_init):
    nb = t // DIFF_Q_TILE
    nctx = k_ctx.shape[1]
    small = pl.BlockSpec((1, C_QK_DIM), lambda b, h, n: (0, 0))
    qspec = pl.BlockSpec((None, DIFF_Q_TILE, LANES), lambda b, h, n: (b, n, h))
    kvspec = pl.BlockSpec((None, t, LANES), lambda b, h, n: (b, 0, h))
    cxspec = pl.BlockSpec((None, nctx, LANES), lambda b, h, n: (b, 0, h))
    return pl.pallas_call(
        functools.partial(_diff_lat_kernel, lam_init=lam_init),
        grid=(bn, C_HEADS, nb),
        in_specs=[small, small, small, small,
                  pl.BlockSpec((1, C_V_DIM), lambda b, h, n: (0, 0)),
                  qspec, kvspec, kvspec, cxspec, cxspec],
        out_specs=qspec,
        out_shape=jax.ShapeDtypeStruct((bn, t, C_WIDTH), F32),
        compiler_params=_params(3),
        name="diff_latent",
    )(*lams, subln, q.reshape(bn, t, C_WIDTH), k.reshape(bn, t, C_WIDTH), v.reshape(bn, t, C_WIDTH),
      k_ctx, v_ctx)


def _rope_tables(t):
    nf = HEAD_DIM // 4
    inv = 1.0 / (ROPE_BASE ** (jnp.arange(nf, dtype=F32) / nf))
    pos = jnp.arange(t)
    ang_r = (pos // GRID_W).astype(F32)[:, None] * inv[None]
    ang_c = (pos % GRID_W).astype(F32)[:, None] * inv[None]
    ang = jnp.concatenate([ang_r, ang_r, ang_c, ang_c], -1)
    cos, sin = jnp.cos(ang), jnp.sin(ang)
    sign = jnp.where((jnp.arange(HEAD_DIM) // nf) % 2 == 0, -1.0, 1.0).astype(F32)
    return jnp.tile(cos, (1, 2)), jnp.tile(sin * sign[None], (1, 2))


EVEN_SEGS = ((0, 512), (512, 640), (640, 768), (768, 1280), (1280, 2944), (2944, 3456))
ODD_SEGS = ((0, 1024), (1024, 2048), (2048, 3072), (3072, 4096))


def _pair_states(st):
    bn = st.shape[0]
    s = st.reshape(bn, 2, 4, 2, HEAD_DIM, HEAD_DIM)
    z = jnp.zeros_like(s[:, :, :, 0])
    top = jnp.concatenate([s[:, :, :, 0], z], -1)
    bot = jnp.concatenate([z, s[:, :, :, 1]], -1)
    return jnp.concatenate([top, bot], -2)


def kernel(x_prompt, x_sample, cache_a_k, cache_a_v, state_rwkv, cache_c_k, cache_c_v, c, c_ctx,
           ada_w, ada_b, norm_pre, norm_post, w_out, even_w_in, a_sink, b_mu, b_w0, b_w2, b_a0,
           b_a2, b_kk, b_ka, b_rk, b_ln_w, b_ln_b, odd_w_in, c_lq1, c_lk1, c_lq2, c_lk2, c_subln):
    bp, tp, _ = x_prompt.shape
    bs, ts, _ = x_sample.shape
    past = cache_a_k.shape[2]

    cvecs = jnp.concatenate([c_ctx[None], c, jnp.zeros((8 - 1 - bs, D_MODEL), F32)], 0)
    mods = _modulation(cvecs, ada_w, ada_b)
    rope_tabs = _rope_tables(ts)
    ones = (jnp.arange(B_WIDTH)[:, None] // HEAD_DIM == jnp.arange(B_WIDTH)[None] // HEAD_DIM).astype(BF16)

    xp = x_prompt.reshape(bp * tp, D_MODEL)
    xs = x_sample.reshape(bs * ts, D_MODEL)

    def mod_rows(layer, rows):
        m = mods[layer, rows[0]:rows[1]]
        if rows == (0, 1):
            m = jnp.broadcast_to(m, (bp, 3 * D_MODEL))
        m = m[:, None, :]
        return m[..., :D_MODEL], m[..., D_MODEL:2 * D_MODEL], m[..., 2 * D_MODEL:]

    e = 0
    w_in0 = even_w_in[e].astype(BF16)
    w_out0 = w_out[0].astype(BF16)
    gain_pre = norm_pre[0][None]
    gain_post = norm_post[0][None]
    zpad = jnp.zeros((DECAY_LORA, B_WIDTH), F32)
    lora = jnp.concatenate([
        jnp.concatenate([b_w2[e, 0], b_w2[e, 1], zpad, zpad], 1),
        jnp.concatenate([zpad, zpad, b_a2[e, 0], b_a2[e, 1]], 1)], 0).astype(BF16)
    feat_args = (b_mu[e][None], lora, b_w0[e][:, None, :], b_a0[e][:, None, :], b_kk[e][None],
                 b_ka[e][None], b_rk[e][None], ones)

    def even_layer(x2d, bn, t, rows, ctx):
        tps = t // ROW_TILE
        shift, scale, gate = mod_rows(0, rows)
        qa, ka, va, ga, pb, gb = _proj_in(x2d, shift, scale, gain_pre, w_in0, EVEN_SEGS, (0, 1), tps,
                                          None if ctx is None else rope_tabs)
        if ctx is None:
            ya = _gqa_context(a_sink[e], qa, ka, va, bn, t)
            s0 = None
        else:
            k_ctx, v_ctx, st0 = ctx
            ya = _gqa_window(a_sink[e], qa, ka, va, k_ctx, v_ctx, bn, t)
            s0 = _pair_states(st0)
        r, kkn, v, lw, b, ke, bonus = _rwkv_features(pb, *feat_args, tps)
        yf, yb, *s_fin = _rwkv_scan(r, kkn, v, lw, b, ke, s0, bn, t)
        y = _even_out(ya.reshape(bn * t, A_WIDTH), ga, yf.reshape(bn * t, B_WIDTH),
                      yb.reshape(bn * t, B_WIDTH), bonus, gb,
                      b_ln_w[e][None], b_ln_b[e][None], ones, x2d, gate, gain_post, w_out0, tps)
        return y, ka, va, s_fin

    yp, ka_p, va_p, st_p = even_layer(xp, bp, tp, (0, 1), None)
    ctx_a = (cache_a_k[:, e].reshape(bs, past, A_KV_WIDTH), cache_a_v[:, e].reshape(bs, past, A_KV_WIDTH),
             state_rwkv[:, e])
    ys, _, _, _ = even_layer(xs, bs, ts, (1, 1 + bs), ctx_a)

    o = 0
    lam_init = 0.8 - 0.6 * math.exp(-0.3 * 1)
    w_in1 = odd_w_in[o].astype(BF16)
    w_out1 = w_out[1].astype(BF16)
    gain_pre1 = norm_pre[1][None]
    gain_post1 = norm_post[1][None]
    lams = (c_lq1[o][None], c_lk1[o][None], c_lq2[o][None], c_lk2[o][None])
    subln = c_subln[o][None]

    def odd_layer(x2d, bn, t, rows, ctx):
        tps = t // ROW_TILE
        shift, scale, gate = mod_rows(1, rows)
        qc, kc, vc, gc = _proj_in(x2d, shift, scale, gain_pre1, w_in1, ODD_SEGS, (0, 1), tps,
                                  None if ctx is None else rope_tabs)
        if ctx is None:
            oc = _diff_context(lams, subln, qc, kc, vc, bn, t, lam_init)
        else:
            k_ctx, v_ctx = ctx
            oc = _diff_latent(lams, subln, qc, kc, vc, k_ctx, v_ctx, bn, t, lam_init)
        y = _odd_out(oc.reshape(bn * t, C_WIDTH), gc, x2d, gate, gain_post1, w_out1, tps)
        return y, kc, vc

    yp2, kc_p, vc_p = odd_layer(yp, bp, tp, (0, 1), None)
    ctx_c = (cache_c_k[:, o].reshape(bs, past, C_WIDTH), cache_c_v[:, o].reshape(bs, past, C_WIDTH))
    ys2, _, _ = odd_layer(ys, bs, ts, (1, 1 + bs), ctx_c)

    return (yp2.reshape(bp, tp, D_MODEL),
            ys2.reshape(bs, ts, D_MODEL),
            ka_p.reshape(bp, 1, tp, A_KV_HEADS, HEAD_DIM),
            va_p.reshape(bp, 1, tp, A_KV_HEADS, HEAD_DIM),
            st_p[0][:, None],
            kc_p.reshape(bp, 1, tp, C_HEADS, 2 * C_QK_DIM),
            vc_p.reshape(bp, 1, tp, C_HEADS, C_V_DIM))
```

```python
import functools
import math

import jax
import jax.numpy as jnp
from jax import lax
from jax.experimental import pallas as pl
from jax.experimental.pallas import tpu as pltpu

F32 = jnp.float32
BF16 = jnp.bfloat16

D_MODEL = 1024
DEPTH = 2
GRID_W = 64
HEAD_DIM = 64
ROPE_BASE = 10000.0
A_HEADS = 8
A_KV_HEADS = 2
A_WIDTH = 512
A_KV_WIDTH = 128
WINDOW = 128
BLOCK = 128
B_HEADS = 8
B_WIDTH = 512
DECAY_LORA = 64
AAA_LORA = 64
B_SHIFT_WIDTH = 3 * B_WIDTH + DECAY_LORA + AAA_LORA
C_HEADS = 8
C_QK_DIM = 64
C_V_DIM = 128
C_WIDTH = 1024
EVEN_IN = 3456
ODD_IN = 4096
NORM_EPS = 1e-6
RWKV_LN_EPS = 64e-5
SUBLN_EPS = 1e-5
NEG_INF = -1e30

LANES = 128
ROW_TILE = 256
DIFF_Q_TILE = 256
DIFF_Q_SUB = 128
CHUNK = 64
SCAN_SEQS = 4
PAIR = 2 * CHUNK
VMEM_LIMIT = 48 * 1024 * 1024

LOG2E = math.log2(math.e)

NN = ((1,), (0,))
NT = ((1,), (1,))


def _bdot(a, b, dims=NN):
    return lax.dot_general(a.astype(BF16), b.astype(BF16), (dims, ((), ())),
                           preferred_element_type=F32)


def _hdot(a, b):
    return lax.dot_general(a, b, (NN, ((), ())), precision=lax.Precision.HIGHEST,
                           preferred_element_type=F32)


def _split_dot(sel_bf16, x):
    x1 = x.astype(BF16)
    r1 = x - x1.astype(F32)
    x2 = r1.astype(BF16)
    x3 = (r1 - x2.astype(F32)).astype(BF16)
    dot = lambda p: jnp.dot(sel_bf16, p, preferred_element_type=F32)
    return dot(x1) + dot(x2) + dot(x3)


def _head_sum(x, ones_bf16):
    hi = x.astype(BF16)
    lo = (x - hi.astype(F32)).astype(BF16)
    return (jnp.dot(hi, ones_bf16, preferred_element_type=F32)
            + jnp.dot(lo, ones_bf16, preferred_element_type=F32))


def _sigmoid(x):
    return 1.0 / (1.0 + jnp.exp(-x))


def _silu(x):
    return x * _sigmoid(x)


def _params(n_axes):
    return pltpu.CompilerParams(dimension_semantics=("arbitrary",) * n_axes,
                                vmem_limit_bytes=VMEM_LIMIT)


def _lane_lo(shape):
    return (lax.broadcasted_iota(jnp.int32, shape, len(shape) - 1) % LANES) < HEAD_DIM


def _stack_pair(x):
    lo = _lane_lo(x.shape)
    z = jnp.zeros_like(x)
    return jnp.concatenate([jnp.where(lo, x, z), jnp.where(lo, z, x)], axis=0)


def _dup_half(x, g):
    lo = _lane_lo(x.shape)
    sw = pltpu.roll(x, HEAD_DIM, 1)
    return jnp.where(lo, x, sw) if g == 0 else jnp.where(lo, sw, x)


def _rope(x, cos, sin_signed):
    out = []
    even_q = (lax.broadcasted_iota(jnp.int32, cos.shape, 1) // (HEAD_DIM // 4)) % 2 == 0
    for g in range(x.shape[1] // LANES):
        xg = x[:, g * LANES:(g + 1) * LANES]
        up = pltpu.roll(xg, LANES - HEAD_DIM // 4, 1)
        dn = pltpu.roll(xg, HEAD_DIM // 4, 1)
        out.append(xg * cos + jnp.where(even_q, up, dn) * sin_signed)
    return out


def _mod_kernel(c_ref, w_ref, b_ref, o_ref):
    s = _silu(c_ref[...])
    o_ref[...] = _hdot(s, w_ref[...]) + b_ref[...]


def _modulation(cvecs, ada_w, ada_b):
    nrow = cvecs.shape[0]
    return pl.pallas_call(
        _mod_kernel,
        grid=(DEPTH, 3),
        in_specs=[pl.BlockSpec((nrow, D_MODEL), lambda l, j: (0, 0)),
                  pl.BlockSpec((None, D_MODEL, D_MODEL), lambda l, j: (l, 0, j)),
                  pl.BlockSpec((None, 1, D_MODEL), lambda l, j: (l, 0, j))],
        out_specs=pl.BlockSpec((None, nrow, D_MODEL), lambda l, j: (l, 0, j)),
        out_shape=jax.ShapeDtypeStruct((DEPTH, nrow, 3 * D_MODEL), F32),
        compiler_params=_params(2),
        name="modulation",
    )(cvecs, ada_w, ada_b.reshape(DEPTH, 1, 3 * D_MODEL))


def _proj_in_kernel(*refs, segs, rope_segs, use_rope, t_segs):
    if use_rope:
        x_ref, sh_ref, sc_ref, g_ref, w_ref, cos_ref, sin_ref = refs[:7]
        outs = refs[7:]
    else:
        x_ref, sh_ref, sc_ref, g_ref, w_ref = refs[:5]
        outs = refs[5:]
    x = x_ref[...]
    y = x * lax.rsqrt(jnp.mean(x * x, -1, keepdims=True) + NORM_EPS)
    h = (y * g_ref[...]) * (1.0 + sc_ref[...]) + sh_ref[...]
    hb = h.astype(BF16)
    for idx, (lo, hi) in enumerate(segs):
        o = jnp.dot(hb, w_ref[:, lo:hi], preferred_element_type=F32)
        if use_rope and idx in rope_segs:
            parts = _rope(o, cos_ref[...], sin_ref[...])
            for g, part in enumerate(parts):
                outs[idx][:, g * LANES:(g + 1) * LANES] = part
        else:
            outs[idx][...] = o
        if idx in t_segs:
            outs[len(segs) + t_segs.index(idx)][...] = o.T


def _proj_in(x2d, shift, scale, gain, w_bf16, segs, rope_segs, tiles_per_seq, rope_tabs, t_segs=()):
    n = x2d.shape[0]
    nt = n // ROW_TILE
    use_rope = rope_tabs is not None
    assert not t_segs or tiles_per_seq == 1
    row = lambda i: (i, 0)
    per_b = lambda i: (i // tiles_per_seq, 0, 0)
    in_specs = [pl.BlockSpec((ROW_TILE, D_MODEL), row),
                pl.BlockSpec((None, 1, D_MODEL), per_b),
                pl.BlockSpec((None, 1, D_MODEL), per_b),
                pl.BlockSpec((1, D_MODEL), lambda i: (0, 0)),
                pl.BlockSpec(w_bf16.shape, lambda i: (0, 0))]
    args = [x2d, shift, scale, gain, w_bf16]
    if use_rope:
        pos = lambda i: (i % tiles_per_seq, 0)
        in_specs += [pl.BlockSpec((ROW_TILE, LANES), pos), pl.BlockSpec((ROW_TILE, LANES), pos)]
        args += list(rope_tabs)
    widths = [hi - lo for lo, hi in segs]
    return pl.pallas_call(
        functools.partial(_proj_in_kernel, segs=segs, rope_segs=rope_segs, use_rope=use_rope,
                          t_segs=tuple(t_segs)),
        grid=(nt,),
        in_specs=in_specs,
        out_specs=([pl.BlockSpec((ROW_TILE, w), row) for w in widths]
                   + [pl.BlockSpec((None, widths[i], ROW_TILE), lambda i_: (i_, 0, 0)) for i in t_segs]),
        out_shape=([jax.ShapeDtypeStruct((n, w), F32) for w in widths]
                   + [jax.ShapeDtypeStruct((nt, widths[i], ROW_TILE), F32) for i in t_segs]),
        compiler_params=_params(1),
        name="proj_in",
    )(*args)


def _sink_exp_pair(s2, sink_a, sink_b, n):
    rowi = lax.broadcasted_iota(jnp.int32, (2 * n, 1), 0)
    sink2 = jnp.where(rowi < n, sink_a, sink_b) * LOG2E
    m = jnp.maximum(jnp.max(s2, -1, keepdims=True), sink2)
    e = jnp.exp2(s2 - m)
    return e, 1.0 / (jnp.sum(e, -1, keepdims=True) + jnp.exp2(sink2 - m))


def _gqa_ctx_kernel(sink_ref, q_ref, k_ref, v_ref, o_ref):
    t = q_ref.shape[0]
    k = k_ref[...]
    v = v_ref[...]
    kd = [_dup_half(k, g).astype(BF16) for g in range(A_KV_HEADS)]
    vd = [_dup_half(v, g).astype(BF16) for g in range(A_KV_HEADS)]
    s2 = [_bdot(_stack_pair(q_ref[:, p * LANES:(p + 1) * LANES] * (HEAD_DIM ** -0.5 * LOG2E)),
                kd[p // 2], NT) for p in range(A_HEADS // 2)]
    for p in range(A_HEADS // 2):
        e, inv = _sink_exp_pair(s2[p], sink_ref[2 * p], sink_ref[2 * p + 1], t)
        o = _bdot(e, vd[p // 2]) * inv
        o_ref[:, p * LANES:(p + 1) * LANES] = jnp.where(_lane_lo((t, LANES)), o[:t], o[t:])


def _gqa_context(sink, q, k, v, bn, t):
    seq = lambda b: (b, 0, 0)
    return pl.pallas_call(
        _gqa_ctx_kernel,
        grid=(bn,),
        in_specs=[pl.BlockSpec(memory_space=pltpu.SMEM),
                  pl.BlockSpec((None, t, A_WIDTH), seq),
                  pl.BlockSpec((None, t, A_KV_WIDTH), seq),
                  pl.BlockSpec((None, t, A_KV_WIDTH), seq)],
        out_specs=pl.BlockSpec((None, t, A_WIDTH), seq),
        out_shape=jax.ShapeDtypeStruct((bn, t, A_WIDTH), F32),
        compiler_params=_params(1),
        name="gqa_context",
    )(sink, q.reshape(bn, t, A_WIDTH), k.reshape(bn, t, A_KV_WIDTH), v.reshape(bn, t, A_KV_WIDTH))


def _gqa_win_kernel(sink_ref, q_ref, kp_ref, kc_ref, kn_ref, vp_ref, vc_ref, vn_ref,
                    kx_ref, vx_ref, o_ref, *, t):
    nblk = pl.program_id(1)
    nctx = kx_ref.shape[0]
    keys = jnp.concatenate([kx_ref[...], kp_ref[...], kc_ref[...], kn_ref[...]], axis=0)
    vals = jnp.concatenate([vx_ref[...], vp_ref[...], vc_ref[...], vn_ref[...]], axis=0)
    nkeys = nctx + 3 * BLOCK
    rowi = lax.broadcasted_iota(jnp.int32, (2 * BLOCK, nkeys), 0)
    coli = lax.broadcasted_iota(jnp.int32, (2 * BLOCK, nkeys), 1)
    qpos = nblk * BLOCK + rowi % BLOCK
    kpos = nblk * BLOCK + coli - nctx - BLOCK
    valid = (coli < nctx) | ((jnp.abs(qpos - kpos) <= WINDOW) & (kpos >= 0) & (kpos < t))
    kd = [_dup_half(keys, g).astype(BF16) for g in range(A_KV_HEADS)]
    vd = [_dup_half(vals, g).astype(BF16) for g in range(A_KV_HEADS)]
    s2 = [_bdot(_stack_pair(q_ref[:, p * LANES:(p + 1) * LANES] * (HEAD_DIM ** -0.5 * LOG2E)),
                kd[p // 2], NT) for p in range(A_HEADS // 2)]
    for p in range(A_HEADS // 2):
        e, inv = _sink_exp_pair(jnp.where(valid, s2[p], NEG_INF), sink_ref[2 * p], sink_ref[2 * p + 1], BLOCK)
        o = _bdot(e, vd[p // 2]) * inv
        o_ref[:, p * LANES:(p + 1) * LANES] = jnp.where(_lane_lo((BLOCK, LANES)), o[:BLOCK], o[BLOCK:])


def _gqa_window(sink, q, k, v, k_ctx, v_ctx, bn, t):
    nb = t // BLOCK
    nctx = k_ctx.shape[1]
    cur = lambda b, n: (b, n, 0)
    prev = lambda b, n: (b, jnp.maximum(n - 1, 0), 0)
    nxt = lambda b, n: (b, jnp.minimum(n + 1, nb - 1), 0)
    ctx = lambda b, n: (b, 0, 0)
    kv = lambda f: pl.BlockSpec((None, BLOCK, A_KV_WIDTH), f)
    k3 = k.reshape(bn, t, A_KV_WIDTH)
    v3 = v.reshape(bn, t, A_KV_WIDTH)
    return pl.pallas_call(
        functools.partial(_gqa_win_kernel, t=t),
        grid=(bn, nb),
        in_specs=[pl.BlockSpec(memory_space=pltpu.SMEM),
                  pl.BlockSpec((None, BLOCK, A_WIDTH), cur),
                  kv(prev), kv(cur), kv(nxt), kv(prev), kv(cur), kv(nxt),
                  pl.BlockSpec((None, nctx, A_KV_WIDTH), ctx),
                  pl.BlockSpec((None, nctx, A_KV_WIDTH), ctx)],
        out_specs=pl.BlockSpec((None, BLOCK, A_WIDTH), cur),
        out_shape=jax.ShapeDtypeStruct((bn, t, A_WIDTH), F32),
        compiler_params=_params(2),
        name="gqa_window",
    )(sink, q.reshape(bn, t, A_WIDTH), k3, k3, k3, v3, v3, v3, k_ctx, v_ctx)


def _rwkv_feat_kernel(p_ref, pp_ref, pn_ref, mu_ref, lora_ref, w0_ref, a0_ref, kk_ref, ka_ref,
                      rk_ref, ones_ref, r_out, kkn_out, v_out, lw_out, b_out, ke_out, bonus_out,
                      *, tiles_per_seq):
    i = pl.program_id(0)
    tm = p_ref.shape[0]
    pos = i % tiles_per_seq
    p = p_ref[...]
    rowi = lax.broadcasted_iota(jnp.int32, (tm, 1), 0)
    prev_row = jnp.where(pos != 0, pp_ref[7:8, :], 0.0)
    next_row = jnp.where(pos != tiles_per_seq - 1, pn_ref[0:1, :], 0.0)
    prev = jnp.where(rowi == 0, prev_row, pltpu.roll(p, 1, 0))
    nxt = jnp.where(rowi == tm - 1, next_row, pltpu.roll(p, tm - 1, 0))
    xf = p + (0.5 * (prev + nxt) - p) * mu_ref[...]
    r = xf[:, 0:B_WIDTH]
    k = xf[:, B_WIDTH:2 * B_WIDTH]
    v = xf[:, 2 * B_WIDTH:3 * B_WIDTH]
    g = xf[:, 3 * B_WIDTH:]
    g = jnp.where(_lane_lo(g.shape), jnp.tanh(g), g)
    lo = _bdot(g, lora_ref[...])
    ones = ones_ref[...]
    kkr = k * kk_ref[...]
    kkn = kkr / jnp.maximum(jnp.sqrt(_head_sum(kkr * kkr, ones)), 1e-12)
    ke_sum = jnp.zeros_like(k)
    for z in range(2):
        wz = w0_ref[z] + lo[:, z * B_WIDTH:(z + 1) * B_WIDTH]
        sp = jnp.maximum(-wz, 0.0) + jnp.log(1.0 + jnp.exp(-jnp.abs(wz)))
        lw_out[z] = -jnp.exp(-sp - 0.5)
        az = _sigmoid(a0_ref[z] + lo[:, (2 + z) * B_WIDTH:(3 + z) * B_WIDTH])
        ke = k * (1.0 + (az - 1.0) * ka_ref[...])
        ke_out[z] = ke
        b_out[z] = kkn * az
        ke_sum = ke_sum + ke
    r_out[...] = r
    kkn_out[...] = kkn
    v_out[...] = v
    bonus_out[...] = _head_sum(r * (0.5 * ke_sum) * rk_ref[...], ones) * v


def _rwkv_features(pb, mu, lora, w0, a0, k_k, k_a, r_k, ones, tiles_per_seq):
    n = pb.shape[0]
    nt = n // ROW_TILE
    h8 = ROW_TILE // 8
    row = lambda i: (i, 0)
    const2 = lambda i: (0, 0)
    const3 = lambda i: (0, 0, 0)
    vec = pl.BlockSpec((1, B_WIDTH), const2)
    one = jax.ShapeDtypeStruct((n, B_WIDTH), F32)
    two = jax.ShapeDtypeStruct((2, n, B_WIDTH), F32)
    ospec1 = pl.BlockSpec((ROW_TILE, B_WIDTH), row)
    ospec2 = pl.BlockSpec((2, ROW_TILE, B_WIDTH), lambda i: (0, i, 0))
    return pl.pallas_call(
        functools.partial(_rwkv_feat_kernel, tiles_per_seq=tiles_per_seq),
        grid=(nt,),
        in_specs=[pl.BlockSpec((ROW_TILE, B_SHIFT_WIDTH), row),
                  pl.BlockSpec((8, B_SHIFT_WIDTH), lambda i: (jnp.maximum(i * h8 - 1, 0), 0)),
                  pl.BlockSpec((8, B_SHIFT_WIDTH), lambda i: (jnp.minimum((i + 1) * h8, n // 8 - 1), 0)),
                  pl.BlockSpec((1, B_SHIFT_WIDTH), const2),
                  pl.BlockSpec(lora.shape, const2),
                  pl.BlockSpec((2, 1, B_WIDTH), const3),
                  pl.BlockSpec((2, 1, B_WIDTH), const3),
                  vec, vec, vec,
                  pl.BlockSpec((B_WIDTH, B_WIDTH), const2)],
        out_specs=[ospec1, ospec1, ospec1, ospec2, ospec2, ospec2, ospec1],
        out_shape=[one, one, one, two, two, two, one],
        compiler_params=_params(1),
        name="rwkv_features",
    )(pb, pb, pb, mu, lora, w0, a0, k_k, k_a, r_k, ones)


def _rwkv_scan_kernel(*refs, n_chunks, from_zero):
    (rf_ref, kf_ref, vf_ref, rb_ref, kb_ref, vb_ref, lwf_ref, bf_ref, kef_ref,
     lwb_ref, bb_ref, keb_ref) = refs[:12]
    if from_zero:
        yf_ref, yb_ref, sout_ref, s_scr = refs[12:]
    else:
        s0_ref, yf_ref, yb_ref, s_scr = refs[12:]
    c = pl.program_id(1)

    @pl.when(c == 0)
    def _():
        if from_zero:
            s_scr[...] = jnp.zeros_like(s_scr)
        else:
            s_scr[...] = s0_ref[...]

    row = lax.broadcasted_iota(jnp.int32, (CHUNK, LANES), 0)
    li = lax.broadcasted_iota(jnp.int32, (CHUNK, LANES), 1) % CHUNK
    blk16 = (row // 16) == (li // 16)
    blk32 = (row // 32) == (li // 32)
    eye = jnp.where(row == li, 1.0, 0.0)
    srow = lax.broadcasted_iota(jnp.int32, (PAIR, PAIR), 0)
    scol = lax.broadcasted_iota(jnp.int32, (PAIR, PAIR), 1)
    same_head = (srow // CHUNK) == (scol // CHUNK)
    tr = lax.broadcasted_iota(jnp.int32, (CHUNK, CHUNK), 0)
    tc = lax.broadcasted_iota(jnp.int32, (CHUNK, CHUNK), 1)
    dirs = ((rf_ref, kf_ref, vf_ref, lwf_ref, bf_ref, kef_ref, yf_ref),
            (rb_ref, kb_ref, vb_ref, lwb_ref, bb_ref, keb_ref, yb_ref))

    def advance(s):
        chains = [(z, p) for z in range(2) for p in range(B_HEADS // 2)]
        sls = [slice(p * LANES, (p + 1) * LANES) for _, p in chains]
        before, before_eq, tot, kkd, rd, bg, kg, vv, sc = [], [], [], [], [], [], [], [], []
        for z in range(2):
            r_ref, kk_ref, v_ref, lw_ref, b_ref, ke_ref, _ = dirs[z]
            diff = (li - row) if z else (row - li)
            tri = jnp.where(((tc - tr) if z else (tr - tc)) >= 0, 1.0, 0.0).astype(BF16)
            lw_all = lw_ref[s]
            cum_all = _split_dot(tri, lw_all)
            for p in range(B_HEADS // 2):
                sl = slice(p * LANES, (p + 1) * LANES)
                lw = lw_all[:, sl]
                cum = cum_all[:, sl]
                tt = cum[0:1] if z else cum[CHUNK - 1:CHUNK]
                b = b_ref[s, :, sl]
                ke = ke_ref[s, :, sl]
                g_inv = jnp.exp(-cum)
                g_rest = jnp.exp(tt - cum)
                before.append(diff > 0)
                before_eq.append(diff >= 0)
                tot.append(tt)
                kkd.append(kk_ref[s, :, sl] * jnp.exp(cum - lw))
                rd.append(r_ref[s, :, sl] * jnp.exp(cum))
                bg.append(b * g_rest)
                kg.append(ke * g_rest)
                vv.append(v_ref[s, :, sl])
                sc.append(_bdot(jnp.concatenate([kkd[-1], rd[-1]], 0),
                                jnp.concatenate([_stack_pair(b * g_inv), _stack_pair(ke * g_inv)], 0), NT))
            yield
        lb = [jnp.where(m, x[:CHUNK, :LANES], 0.0) for m, x in zip(before, sc)]
        mb = [jnp.where(m, x[CHUNK:, :LANES], 0.0) for m, x in zip(before_eq, sc)]
        lv = [_bdot(jnp.concatenate([jnp.where(m, x[:CHUNK, LANES:], 0.0),
                                     jnp.where(me, x[CHUNK:, LANES:], 0.0)], 0), _stack_pair(v))
              for m, me, x, v in zip(before, before_eq, sc, vv)]
        yield

        pw = [jnp.where(blk16, x, 0.0) for x in lb]
        tinv = [eye - x for x in pw]
        for _ in range(3):
            pw = [_bdot(x, _stack_pair(x)) for x in pw]
            yield
            tinv = [t + _bdot(t, _stack_pair(x)) for t, x in zip(tinv, pw)]
            yield
        for off in (blk32 & ~blk16, ~blk32):
            tmp = [_bdot(t, _stack_pair(jnp.where(off, x, 0.0))) for t, x in zip(tinv, lb)]
            yield
            tinv = [t - _bdot(m, _stack_pair(t)) for t, m in zip(tinv, tmp)]
            yield

        w12 = [_bdot(t, jnp.concatenate([_stack_pair(k), _stack_pair(x[:CHUNK])], 1))
               for t, k, x in zip(tinv, kkd, lv)]
        yield
        s_old = [s_scr[s, z, p] for z, p in chains]
        ws = [_bdot(jnp.concatenate([w[:, :LANES], r], 0), st, NT) for w, r, st in zip(w12, rd, s_old)]
        yield
        u = [-(a[:CHUNK] + w[:, LANES:]) for a, w in zip(ws, w12)]
        mbu = [_bdot(m, _stack_pair(x)) for m, x in zip(mb, u)]
        upd = [_bdot(jnp.concatenate([x, v], 0).T, jnp.concatenate([b_, k_], 0))
               for x, v, b_, k_ in zip(u, vv, bg, kg)]
        yield
        for i, (z, p) in enumerate(chains):
            dirs[z][6][s, :, sls[i]] = ws[i][CHUNK:] + mbu[i] + lv[i][CHUNK:]
            s_scr[s, z, p] = s_old[i] * jnp.exp(tot[i]) + jnp.where(same_head, upd[i], 0.0)

    live = [advance(s) for s in range(rf_ref.shape[0])]
    while live:
        live = [gen for gen in live if next(gen, "done") != "done"]

    if from_zero:
        @pl.when(c == n_chunks - 1)
        def _():
            for s in range(s_scr.shape[0]):
                for z in range(2):
                    for p in range(B_HEADS // 2):
                        st = s_scr[s, z, p]
                        sout_ref[s, z, 2 * p] = st[:HEAD_DIM, :HEAD_DIM]
                        sout_ref[s, z, 2 * p + 1] = pltpu.roll(st[HEAD_DIM:, :], HEAD_DIM, 1)[:, :HEAD_DIM]


def _rwkv_scan(r, kkn, v, lw, b, ke, s0, bn, t):
    nc = t // CHUNK
    npair = B_HEADS // 2
    ns = min(SCAN_SEQS, bn)
    fwd = pl.BlockSpec((ns, CHUNK, B_WIDTH), lambda g, c: (g, c, 0))
    bwd = pl.BlockSpec((ns, CHUNK, B_WIDTH), lambda g, c: (g, nc - 1 - c, 0))
    dfwd = pl.BlockSpec((None, ns, CHUNK, B_WIDTH), lambda g, c: (0, g, c, 0))
    dbwd = pl.BlockSpec((None, ns, CHUNK, B_WIDTH), lambda g, c: (1, g, nc - 1 - c, 0))
    r3, k3, v3 = (a.reshape(bn, t, B_WIDTH) for a in (r, kkn, v))
    lw4, b4, ke4 = (a.reshape(2, bn, t, B_WIDTH) for a in (lw, b, ke))
    seq = jax.ShapeDtypeStruct((bn, t, B_WIDTH), F32)
    in_specs = [fwd, fwd, fwd, bwd, bwd, bwd, dfwd, dfwd, dfwd, dbwd, dbwd, dbwd]
    args = [r3, k3, v3, r3, k3, v3, lw4, b4, ke4, lw4, b4, ke4]
    out_specs = [fwd, bwd]
    out_shape = [seq, seq]
    if s0 is None:
        out_specs.append(pl.BlockSpec((ns, 2, B_HEADS, HEAD_DIM, HEAD_DIM), lambda g, c: (g, 0, 0, 0, 0)))
        out_shape.append(jax.ShapeDtypeStruct((bn, 2, B_HEADS, HEAD_DIM, HEAD_DIM), F32))
    else:
        in_specs.append(pl.BlockSpec((ns, 2, npair, PAIR, PAIR), lambda g, c: (g, 0, 0, 0, 0)))
        args.append(s0)
    return pl.pallas_call(
        functools.partial(_rwkv_scan_kernel, n_chunks=nc, from_zero=s0 is None),
        grid=(bn // ns, nc),
        in_specs=in_specs,
        out_specs=out_specs,
        out_shape=out_shape,
        scratch_shapes=[pltpu.VMEM((ns, 2, npair, PAIR, PAIR), F32)],
        compiler_params=_params(2),
        name="rwkv_scan",
    )(*args)


def _post(x, z, gate, gain):
    zn = z * lax.rsqrt(jnp.mean(z * z, -1, keepdims=True) + NORM_EPS)
    return x + gate * (zn * gain)


def _even_out_kernel(ya_ref, ga_ref, yf_ref, yb_ref, bonus_ref, gb_ref, lnw_ref, lnb_ref, ones_ref,
                     x_ref, gate_ref, gain_ref, w_ref, o_ref):
    y = yf_ref[...] + yb_ref[...]
    ones = ones_ref[...]
    mean = _head_sum(y, ones) * (1.0 / HEAD_DIM)
    d = y - mean
    var = _head_sum(d * d, ones) * (1.0 / HEAD_DIM)
    yb = (d * lax.rsqrt(var + RWKV_LN_EPS)) * lnw_ref[...] + lnb_ref[...] + bonus_ref[...]
    za = (ya_ref[...] * _silu(ga_ref[...])).astype(BF16)
    zb = (yb * _silu(gb_ref[...])).astype(BF16)
    z = (jnp.dot(za, w_ref[0:A_WIDTH, :], preferred_element_type=F32)
         + jnp.dot(zb, w_ref[A_WIDTH:, :], preferred_element_type=F32))
    o_ref[...] = _post(x_ref[...], z, gate_ref[...], gain_ref[...])


def _even_out(ya, ga, yf, yb, bonus, gb, ln_w, ln_b, ones, x2d, gate, gain, w_bf16, tiles_per_seq):
    n = x2d.shape[0]
    row = lambda i: (i, 0)
    const2 = lambda i: (0, 0)
    half = pl.BlockSpec((ROW_TILE, B_WIDTH), row)
    full = pl.BlockSpec((ROW_TILE, D_MODEL), row)
    vec = pl.BlockSpec((1, B_WIDTH), const2)
    return pl.pallas_call(
        _even_out_kernel,
        grid=(n // ROW_TILE,),
        in_specs=[half, half, half, half, half, half, vec, vec,
                  pl.BlockSpec((B_WIDTH, B_WIDTH), const2),
                  full,
                  pl.BlockSpec((None, 1, D_MODEL), lambda i: (i // tiles_per_seq, 0, 0)),
                  pl.BlockSpec((1, D_MODEL), const2),
                  pl.BlockSpec((D_MODEL, D_MODEL), const2)],
        out_specs=full,
        out_shape=jax.ShapeDtypeStruct((n, D_MODEL), F32),
        compiler_params=_params(1),
        name="even_out",
    )(ya, ga, yf, yb, bonus, gb, ln_w, ln_b, ones, x2d, gate, gain, w_bf16)


def _odd_out_kernel(o_ref_in, gc_ref, x_ref, gate_ref, gain_ref, w_ref, o_ref):
    zc = (o_ref_in[...] * _silu(gc_ref[...])).astype(BF16)
    z = jnp.dot(zc, w_ref[...], preferred_element_type=F32)
    o_ref[...] = _post(x_ref[...], z, gate_ref[...], gain_ref[...])


def _odd_out(o, gc, x2d, gate, gain, w_bf16, tiles_per_seq):
    n = x2d.shape[0]
    row = lambda i: (i, 0)
    const2 = lambda i: (0, 0)
    full = pl.BlockSpec((ROW_TILE, D_MODEL), row)
    return pl.pallas_call(
        _odd_out_kernel,
        grid=(n // ROW_TILE,),
        in_specs=[full, full, full,
                  pl.BlockSpec((None, 1, D_MODEL), lambda i: (i // tiles_per_seq, 0, 0)),
                  pl.BlockSpec((1, D_MODEL), const2),
                  pl.BlockSpec((D_MODEL, D_MODEL), const2)],
        out_specs=full,
        out_shape=jax.ShapeDtypeStruct((n, D_MODEL), F32),
        compiler_params=_params(1),
        name="odd_out",
    )(o, gc, x2d, gate, gain, w_bf16)


def _lambda(lq1_ref, lk1_ref, lq2_ref, lk2_ref, lam_init):
    s1 = jnp.sum(lq1_ref[...] * lk1_ref[...], -1, keepdims=True)
    s2 = jnp.sum(lq2_ref[...] * lk2_ref[...], -1, keepdims=True)
    return jnp.exp(s1) - jnp.exp(s2) + lam_init


def _subln(o, gain, lam_init):
    on = o * lax.rsqrt(jnp.mean(o * o, -1, keepdims=True) + SUBLN_EPS)
    return (on * gain) * (1.0 - lam_init)


def _diff_ctx_kernel(lq1_ref, lk1_ref, lq2_ref, lk2_ref, sub_ref, q_ref, k_ref, v_ref, o_ref,
                     *, lam_init):
    t = q_ref.shape[0]
    lam = _lambda(lq1_ref, lk1_ref, lq2_ref, lk2_ref, lam_init)
    sls = [slice(h * LANES, (h + 1) * LANES) for h in range(C_HEADS)]
    s2 = [_bdot(_stack_pair(q_ref[:, sl] * (C_QK_DIM ** -0.5 * LOG2E)), k_ref[:, sl], NT)
          for sl in sls]
    for h, sl in enumerate(sls):
        e = jnp.exp2(s2[h] - jnp.max(s2[h], -1, keepdims=True))
        ov = _bdot(e, v_ref[:, sl]) * (1.0 / jnp.sum(e, -1, keepdims=True))
        o_ref[:, sl] = _subln(ov[:t] - lam * ov[t:], sub_ref[...], lam_init)


def _diff_context(lams, subln, q, k, v, bn, t, lam_init):
    seq = pl.BlockSpec((None, t, C_WIDTH), lambda b: (b, 0, 0))
    small = pl.BlockSpec((1, C_QK_DIM), lambda b: (0, 0))
    return pl.pallas_call(
        functools.partial(_diff_ctx_kernel, lam_init=lam_init),
        grid=(bn,),
        in_specs=[small, small, small, small, pl.BlockSpec((1, C_V_DIM), lambda b: (0, 0)),
                  seq, seq, seq],
        out_specs=seq,
        out_shape=jax.ShapeDtypeStruct((bn, t, C_WIDTH), F32),
        compiler_params=_params(1),
        name="diff_context",
    )(*lams, subln, q.reshape(bn, t, C_WIDTH), k.reshape(bn, t, C_WIDTH), v.reshape(bn, t, C_WIDTH))


def _diff_lat_kernel(lq1_ref, lk1_ref, lq2_ref, lk2_ref, sub_ref, q_ref, k_ref, v_ref, kx_ref, vx_ref,
                     o_ref, *, lam_init):
    lam = _lambda(lq1_ref, lk1_ref, lq2_ref, lk2_ref, lam_init)
    kx = kx_ref[...].astype(BF16)
    k = k_ref[...].astype(BF16)
    tq = DIFF_Q_SUB
    subs = range(q_ref.shape[0] // tq)
    scores = []
    for i in subs:
        qs = _stack_pair(q_ref[i * tq:(i + 1) * tq, :] * (C_QK_DIM ** -0.5 * LOG2E))
        scores.append((_bdot(qs, kx, NT), _bdot(qs, k, NT)))
    for i in subs:
        sx, sl = scores[i]
        m = jnp.maximum(jnp.max(sx, -1, keepdims=True), jnp.max(sl, -1, keepdims=True))
        ex = jnp.exp2(sx - m)
        el = jnp.exp2(sl - m)
        inv = 1.0 / (jnp.sum(ex, -1, keepdims=True) + jnp.sum(el, -1, keepdims=True))
        c0 = inv[:tq]
        c1 = lam * inv[tq:]
        o = (_bdot(ex[:tq] * c0 - ex[tq:] * c1, vx_ref[...])
             + _bdot(el[:tq] * c0 - el[tq:] * c1, v_ref[...]))
        o_ref[i * tq:(i + 1) * tq, :] = _subln(o, sub_ref[...], lam_init)


def _diff_latent(lams, subln, q, k, v, k_ctx, v_ctx, bn, t, lam_init):
    nb = t // DIFF_Q_TILE
    nctx = k_ctx.shape[1]
    small = pl.BlockSpec((1, C_QK_DIM), lambda b, h, n: (0, 0))
    qspec = pl.BlockSpec((None, DIFF_Q_TILE, LANES), lambda b, h, n: (b, n, h))
    kvspec = pl.BlockSpec((None, t, LANES), lambda b, h, n: (b, 0, h))
    cxspec = pl.BlockSpec((None, nctx, LANES), lambda b, h, n: (b, 0, h))
    return pl.pallas_call(
        functools.partial(_diff_lat_kernel, lam_init=lam_init),
        grid=(bn, C_HEADS, nb),
        in_specs=[small, small, small, small,
                  pl.BlockSpec((1, C_V_DIM), lambda b, h, n: (0, 0)),
                  qspec, kvspec, kvspec, cxspec, cxspec],
        out_specs=qspec,
        out_shape=jax.ShapeDtypeStruct((bn, t, C_WIDTH), F32),
        compiler_params=_params(3),
        name="diff_latent",
    )(*lams, subln, q.reshape(bn, t, C_WIDTH), k.reshape(bn, t, C_WIDTH), v.reshape(bn, t, C_WIDTH),
      k_ctx, v_ctx)


def _rope_tables(t):
    nf = HEAD_DIM // 4
    inv = 1.0 / (ROPE_BASE ** (jnp.arange(nf, dtype=F32) / nf))
    pos = jnp.arange(t)
    ang_r = (pos // GRID_W).astype(F32)[:, None] * inv[None]
    ang_c = (pos % GRID_W).astype(F32)[:, None] * inv[None]
    ang = jnp.concatenate([ang_r, ang_r, ang_c, ang_c], -1)
    cos, sin = jnp.cos(ang), jnp.sin(ang)
    sign = jnp.where((jnp.arange(HEAD_DIM) // nf) % 2 == 0, -1.0, 1.0).astype(F32)
    return jnp.tile(cos, (1, 2)), jnp.tile(sin * sign[None], (1, 2))


EVEN_SEGS = ((0, 512), (512, 640), (640, 768), (768, 1280), (1280, 2944), (2944, 3456))
ODD_SEGS = ((0, 1024), (1024, 2048), (2048, 3072), (3072, 4096))


def _pair_states(st):
    bn = st.shape[0]
    s = st.reshape(bn, 2, 4, 2, HEAD_DIM, HEAD_DIM)
    z = jnp.zeros_like(s[:, :, :, 0])
    top = jnp.concatenate([s[:, :, :, 0], z], -1)
    bot = jnp.concatenate([z, s[:, :, :, 1]], -1)
    return jnp.concatenate([top, bot], -2)


def kernel(x_prompt, x_sample, cache_a_k, cache_a_v, state_rwkv, cache_c_k, cache_c_v, c, c_ctx,
           ada_w, ada_b, norm_pre, norm_post, w_out, even_w_in, a_sink, b_mu, b_w0, b_w2, b_a0,
           b_a2, b_kk, b_ka, b_rk, b_ln_w, b_ln_b, odd_w_in, c_lq1, c_lk1, c_lq2, c_lk2, c_subln):
    bp, tp, _ = x_prompt.shape
    bs, ts, _ = x_sample.shape
    past = cache_a_k.shape[2]

    cvecs = jnp.concatenate([c_ctx[None], c, jnp.zeros((8 - 1 - bs, D_MODEL), F32)], 0)
    mods = _modulation(cvecs, ada_w, ada_b)
    rope_tabs = _rope_tables(ts)
    ones = (jnp.arange(B_WIDTH)[:, None] // HEAD_DIM == jnp.arange(B_WIDTH)[None] // HEAD_DIM).astype(BF16)

    xp = x_prompt.reshape(bp * tp, D_MODEL)
    xs = x_sample.reshape(bs * ts, D_MODEL)

    def mod_rows(layer, rows):
        m = mods[layer, rows[0]:rows[1]]
        if rows == (0, 1):
            m = jnp.broadcast_to(m, (bp, 3 * D_MODEL))
        m = m[:, None, :]
        return m[..., :D_MODEL], m[..., D_MODEL:2 * D_MODEL], m[..., 2 * D_MODEL:]

    e = 0
    w_in0 = even_w_in[e].astype(BF16)
    w_out0 = w_out[0].astype(BF16)
    gain_pre = norm_pre[0][None]
    gain_post = norm_post[0][None]
    zpad = jnp.zeros((DECAY_LORA, B_WIDTH), F32)
    lora = jnp.concatenate([
        jnp.concatenate([b_w2[e, 0], b_w2[e, 1], zpad, zpad], 1),
        jnp.concatenate([zpad, zpad, b_a2[e, 0], b_a2[e, 1]], 1)], 0).astype(BF16)
    feat_args = (b_mu[e][None], lora, b_w0[e][:, None, :], b_a0[e][:, None, :], b_kk[e][None],
                 b_ka[e][None], b_rk[e][None], ones)

    def even_layer(x2d, bn, t, rows, ctx):
        tps = t // ROW_TILE
        shift, scale, gate = mod_rows(0, rows)
        qa, ka, va, ga, pb, gb, *kv_t = _proj_in(x2d, shift, scale, gain_pre, w_in0, EVEN_SEGS, (0, 1), tps,
                                                 None if ctx is None else rope_tabs,
                                                 t_segs=(1, 2) if ctx is None else ())
        if ctx is None:
            ya = _gqa_context(a_sink[e], qa, ka, va, bn, t)
            s0 = None
        else:
            k_ctx, v_ctx, st0 = ctx
            ya = _gqa_window(a_sink[e], qa, ka, va, k_ctx, v_ctx, bn, t)
            s0 = _pair_states(st0)
        r, kkn, v, lw, b, ke, bonus = _rwkv_features(pb, *feat_args, tps)
        yf, yb, *s_fin = _rwkv_scan(r, kkn, v, lw, b, ke, s0, bn, t)
        y = _even_out(ya.reshape(bn * t, A_WIDTH), ga, yf.reshape(bn * t, B_WIDTH),
                      yb.reshape(bn * t, B_WIDTH), bonus, gb,
                      b_ln_w[e][None], b_ln_b[e][None], ones, x2d, gate, gain_post, w_out0, tps)
        return y, kv_t, s_fin

    yp, (ka_t, va_t), st_p = even_layer(xp, bp, tp, (0, 1), None)
    ctx_a = (cache_a_k[:, e].reshape(bs, past, A_KV_WIDTH), cache_a_v[:, e].reshape(bs, past, A_KV_WIDTH),
             state_rwkv[:, e])
    ys, _, _ = even_layer(xs, bs, ts, (1, 1 + bs), ctx_a)

    def cache_a(x_t):
        return jnp.transpose(x_t.reshape(bp, A_KV_HEADS, HEAD_DIM, tp), (0, 3, 1, 2))[:, None]

    o = 0
    lam_init = 0.8 - 0.6 * math.exp(-0.3 * 1)
    w_in1 = odd_w_in[o].astype(BF16)
    w_out1 = w_out[1].astype(BF16)
    gain_pre1 = norm_pre[1][None]
    gain_post1 = norm_post[1][None]
    lams = (c_lq1[o][None], c_lk1[o][None], c_lq2[o][None], c_lk2[o][None])
    subln = c_subln[o][None]

    def odd_layer(x2d, bn, t, rows, ctx):
        tps = t // ROW_TILE
        shift, scale, gate = mod_rows(1, rows)
        qc, kc, vc, gc = _proj_in(x2d, shift, scale, gain_pre1, w_in1, ODD_SEGS, (0, 1), tps,
                                  None if ctx is None else rope_tabs)
        if ctx is None:
            oc = _diff_context(lams, subln, qc, kc, vc, bn, t, lam_init)
        else:
            k_ctx, v_ctx = ctx
            oc = _diff_latent(lams, subln, qc, kc, vc, k_ctx, v_ctx, bn, t, lam_init)
        y = _odd_out(oc.reshape(bn * t, C_WIDTH), gc, x2d, gate, gain_post1, w_out1, tps)
        return y, kc, vc

    yp2, kc_p, vc_p = odd_layer(yp, bp, tp, (0, 1), None)
    ctx_c = (cache_c_k[:, o].reshape(bs, past, C_WIDTH), cache_c_v[:, o].reshape(bs, past, C_WIDTH))
    ys2, _, _ = odd_layer(ys, bs, ts, (1, 1 + bs), ctx_c)

    return (yp2.reshape(bp, tp, D_MODEL),
            ys2.reshape(bs, ts, D_MODEL),
            cache_a(ka_t),
            cache_a(va_t),
            st_p[0][:, None],
            kc_p.reshape(bp, 1, tp, C_HEADS, 2 * C_QK_DIM),
            vc_p.reshape(bp, 1, tp, C_HEADS, C_V_DIM))
```

```python
import functools
import math

import jax
import jax.numpy as jnp
from jax import lax
from jax.experimental import pallas as pl
from jax.experimental.pallas import tpu as pltpu

F32 = jnp.float32
BF16 = jnp.bfloat16

D_MODEL = 1024
DEPTH = 2
GRID_W = 64
HEAD_DIM = 64
ROPE_BASE = 10000.0
A_HEADS = 8
A_KV_HEADS = 2
A_WIDTH = 512
A_KV_WIDTH = 128
WINDOW = 128
BLOCK = 128
B_HEADS = 8
B_WIDTH = 512
DECAY_LORA = 64
AAA_LORA = 64
B_SHIFT_WIDTH = 3 * B_WIDTH + DECAY_LORA + AAA_LORA
C_HEADS = 8
C_QK_DIM = 64
C_V_DIM = 128
C_WIDTH = 1024
EVEN_IN = 3456
ODD_IN = 4096
NORM_EPS = 1e-6
RWKV_LN_EPS = 64e-5
SUBLN_EPS = 1e-5
NEG_INF = -1e30

LANES = 128
ROW_TILE = 256
OUT_TILE = 512
DIFF_Q_TILE = 256
DIFF_Q_SUB = 128
CHUNK = 64
SCAN_SEQS = 4
PAIR = 2 * CHUNK
VMEM_LIMIT = 48 * 1024 * 1024

LOG2E = math.log2(math.e)
QUERY_SCALE = HEAD_DIM ** -0.5 * LOG2E

NN = ((1,), (0,))
NT = ((1,), (1,))


def _bdot(a, b, dims=NN):
    return lax.dot_general(a.astype(BF16), b.astype(BF16), (dims, ((), ())),
                           preferred_element_type=F32)


def _hdot(a, b):
    return lax.dot_general(a, b, (NN, ((), ())), precision=lax.Precision.HIGHEST,
                           preferred_element_type=F32)


def _split_dot(sel_bf16, x):
    x1 = x.astype(BF16)
    r1 = x - x1.astype(F32)
    x2 = r1.astype(BF16)
    x3 = (r1 - x2.astype(F32)).astype(BF16)
    dot = lambda p: jnp.dot(sel_bf16, p, preferred_element_type=F32)
    return dot(x1) + dot(x2) + dot(x3)


def _head_sum(x, ones_bf16):
    hi = x.astype(BF16)
    lo = (x - hi.astype(F32)).astype(BF16)
    return (jnp.dot(hi, ones_bf16, preferred_element_type=F32)
            + jnp.dot(lo, ones_bf16, preferred_element_type=F32))


def _sigmoid(x):
    return 1.0 / (1.0 + jnp.exp(-x))


def _silu(x):
    return x * _sigmoid(x)


def _params(n_axes):
    return pltpu.CompilerParams(dimension_semantics=("arbitrary",) * n_axes,
                                vmem_limit_bytes=VMEM_LIMIT)


def _lane_lo(shape):
    return (lax.broadcasted_iota(jnp.int32, shape, len(shape) - 1) % LANES) < HEAD_DIM


def _stack_pair(x):
    lo = _lane_lo(x.shape)
    z = jnp.zeros_like(x)
    return jnp.concatenate([jnp.where(lo, x, z), jnp.where(lo, z, x)], axis=0)


def _dup_half(x, g):
    lo = _lane_lo(x.shape)
    sw = pltpu.roll(x, HEAD_DIM, 1)
    return jnp.where(lo, x, sw) if g == 0 else jnp.where(lo, sw, x)


def _rope(x, cos, sin_signed):
    out = []
    even_q = (lax.broadcasted_iota(jnp.int32, cos.shape, 1) // (HEAD_DIM // 4)) % 2 == 0
    for g in range(x.shape[1] // LANES):
        xg = x[:, g * LANES:(g + 1) * LANES]
        up = pltpu.roll(xg, LANES - HEAD_DIM // 4, 1)
        dn = pltpu.roll(xg, HEAD_DIM // 4, 1)
        out.append(xg * cos + jnp.where(even_q, up, dn) * sin_signed)
    return out


def _mod_kernel(c_ref, w_ref, b_ref, o_ref):
    s = _silu(c_ref[...])
    o_ref[...] = _hdot(s, w_ref[...]) + b_ref[...]


def _modulation(cvecs, ada_w, ada_b):
    nrow = cvecs.shape[0]
    return pl.pallas_call(
        _mod_kernel,
        grid=(DEPTH, 3),
        in_specs=[pl.BlockSpec((nrow, D_MODEL), lambda l, j: (0, 0)),
                  pl.BlockSpec((None, D_MODEL, D_MODEL), lambda l, j: (l, 0, j)),
                  pl.BlockSpec((None, 1, D_MODEL), lambda l, j: (l, 0, j))],
        out_specs=pl.BlockSpec((None, nrow, D_MODEL), lambda l, j: (l, 0, j)),
        out_shape=jax.ShapeDtypeStruct((DEPTH, nrow, 3 * D_MODEL), F32),
        compiler_params=_params(2),
        name="modulation",
    )(cvecs, ada_w, ada_b.reshape(DEPTH, 1, 3 * D_MODEL))


def _mod_index(mod, tiles_per_seq):
    if mod.shape[0] == 1:
        return lambda i: (0, 0, 0)
    return lambda i: (i // tiles_per_seq, 0, 0)


def _proj_in_kernel(*refs, segs, rope_segs, use_rope, t_segs):
    if use_rope:
        x_ref, sh_ref, sc_ref, g_ref, w_ref, cos_ref, sin_ref = refs[:7]
        outs = refs[7:]
    else:
        x_ref, sh_ref, sc_ref, g_ref, w_ref = refs[:5]
        outs = refs[5:]
    x = x_ref[...]
    y = x * lax.rsqrt(jnp.mean(x * x, -1, keepdims=True) + NORM_EPS)
    h = (y * g_ref[...]) * (1.0 + sc_ref[...]) + sh_ref[...]
    hb = h.astype(BF16)
    for idx, (lo, hi) in enumerate(segs):
        o = jnp.dot(hb, w_ref[:, lo:hi], preferred_element_type=F32)
        post = (lambda a: (a * QUERY_SCALE).astype(BF16)) if idx == 0 else (lambda a: a)
        if use_rope and idx in rope_segs:
            parts = _rope(o, cos_ref[...], sin_ref[...])
            for g, part in enumerate(parts):
                outs[idx][:, g * LANES:(g + 1) * LANES] = post(part)
        else:
            outs[idx][...] = post(o)
        if idx in t_segs:
            outs[len(segs) + t_segs.index(idx)][...] = o.T


def _proj_in(x2d, shift, scale, gain, w_bf16, segs, rope_segs, tiles_per_seq, rope_tabs, t_segs=()):
    n = x2d.shape[0]
    nt = n // ROW_TILE
    use_rope = rope_tabs is not None
    assert not t_segs or tiles_per_seq == 1
    row = lambda i: (i, 0)
    per_b = _mod_index(shift, tiles_per_seq)
    in_specs = [pl.BlockSpec((ROW_TILE, D_MODEL), row),
                pl.BlockSpec((None, 1, D_MODEL), per_b),
                pl.BlockSpec((None, 1, D_MODEL), per_b),
                pl.BlockSpec((1, D_MODEL), lambda i: (0, 0)),
                pl.BlockSpec(w_bf16.shape, lambda i: (0, 0))]
    args = [x2d, shift, scale, gain, w_bf16]
    if use_rope:
        pos = lambda i: (i % tiles_per_seq, 0)
        in_specs += [pl.BlockSpec((ROW_TILE, LANES), pos), pl.BlockSpec((ROW_TILE, LANES), pos)]
        args += list(rope_tabs)
    widths = [hi - lo for lo, hi in segs]
    return pl.pallas_call(
        functools.partial(_proj_in_kernel, segs=segs, rope_segs=rope_segs, use_rope=use_rope,
                          t_segs=tuple(t_segs)),
        grid=(nt,),
        in_specs=in_specs,
        out_specs=([pl.BlockSpec((ROW_TILE, w), row) for w in widths]
                   + [pl.BlockSpec((None, widths[i], ROW_TILE), lambda i_: (i_, 0, 0)) for i in t_segs]),
        out_shape=([jax.ShapeDtypeStruct((n, w), BF16 if i == 0 else F32) for i, w in enumerate(widths)]
                   + [jax.ShapeDtypeStruct((nt, widths[i], ROW_TILE), F32) for i in t_segs]),
        compiler_params=_params(1),
        name="proj_in",
    )(*args)


def _sink_exp_pair(s2, sink_a, sink_b, n):
    rowi = lax.broadcasted_iota(jnp.int32, (2 * n, 1), 0)
    sink2 = jnp.where(rowi < n, sink_a, sink_b) * LOG2E
    m = jnp.maximum(jnp.max(s2, -1, keepdims=True), sink2)
    e = jnp.exp2(s2 - m)
    return e, 1.0 / (jnp.sum(e, -1, keepdims=True) + jnp.exp2(sink2 - m))


def _gqa_ctx_kernel(sink_ref, q_ref, k_ref, v_ref, o_ref):
    t = q_ref.shape[0]
    k = k_ref[...]
    v = v_ref[...]
    kd = [_dup_half(k, g).astype(BF16) for g in range(A_KV_HEADS)]
    vd = [_dup_half(v, g).astype(BF16) for g in range(A_KV_HEADS)]
    s2 = [_bdot(_stack_pair(q_ref[:, p * LANES:(p + 1) * LANES]),
                kd[p // 2], NT) for p in range(A_HEADS // 2)]
    for p in range(A_HEADS // 2):
        e, inv = _sink_exp_pair(s2[p], sink_ref[2 * p], sink_ref[2 * p + 1], t)
        o = _bdot(e, vd[p // 2]) * inv
        o_ref[:, p * LANES:(p + 1) * LANES] = jnp.where(_lane_lo((t, LANES)), o[:t], o[t:])


def _gqa_context(sink, q, k, v, bn, t):
    seq = lambda b: (b, 0, 0)
    return pl.pallas_call(
        _gqa_ctx_kernel,
        grid=(bn,),
        in_specs=[pl.BlockSpec(memory_space=pltpu.SMEM),
                  pl.BlockSpec((None, t, A_WIDTH), seq),
                  pl.BlockSpec((None, t, A_KV_WIDTH), seq),
                  pl.BlockSpec((None, t, A_KV_WIDTH), seq)],
        out_specs=pl.BlockSpec((None, t, A_WIDTH), seq),
        out_shape=jax.ShapeDtypeStruct((bn, t, A_WIDTH), F32),
        compiler_params=_params(1),
        name="gqa_context",
    )(sink, q.reshape(bn, t, A_WIDTH), k.reshape(bn, t, A_KV_WIDTH), v.reshape(bn, t, A_KV_WIDTH))


def _gqa_win_kernel(sink_ref, q_ref, kp_ref, kc_ref, kn_ref, vp_ref, vc_ref, vn_ref,
                    kx_ref, vx_ref, o_ref, *, t):
    nblk = pl.program_id(1)
    nctx = kx_ref.shape[0]
    keys = jnp.concatenate([kx_ref[...], kp_ref[...], kc_ref[...], kn_ref[...]], axis=0)
    vals = jnp.concatenate([vx_ref[...], vp_ref[...], vc_ref[...], vn_ref[...]], axis=0)
    nkeys = nctx + 3 * BLOCK
    rowi = lax.broadcasted_iota(jnp.int32, (2 * BLOCK, nkeys), 0)
    coli = lax.broadcasted_iota(jnp.int32, (2 * BLOCK, nkeys), 1)
    qpos = nblk * BLOCK + rowi % BLOCK
    kpos = nblk * BLOCK + coli - nctx - BLOCK
    valid = (coli < nctx) | ((jnp.abs(qpos - kpos) <= WINDOW) & (kpos >= 0) & (kpos < t))
    kd = [_dup_half(keys, g).astype(BF16) for g in range(A_KV_HEADS)]
    vd = [_dup_half(vals, g).astype(BF16) for g in range(A_KV_HEADS)]
    s2 = [_bdot(_stack_pair(q_ref[:, p * LANES:(p + 1) * LANES]),
                kd[p // 2], NT) for p in range(A_HEADS // 2)]
    for p in range(A_HEADS // 2):
        e, inv = _sink_exp_pair(jnp.where(valid, s2[p], NEG_INF), sink_ref[2 * p], sink_ref[2 * p + 1], BLOCK)
        o = _bdot(e, vd[p // 2]) * inv
        o_ref[:, p * LANES:(p + 1) * LANES] = jnp.where(_lane_lo((BLOCK, LANES)), o[:BLOCK], o[BLOCK:])


def _gqa_window(sink, q, k, v, k_ctx, v_ctx, bn, t):
    nb = t // BLOCK
    nctx = k_ctx.shape[1]
    cur = lambda b, n: (b, n, 0)
    prev = lambda b, n: (b, jnp.maximum(n - 1, 0), 0)
    nxt = lambda b, n: (b, jnp.minimum(n + 1, nb - 1), 0)
    ctx = lambda b, n: (b, 0, 0)
    kv = lambda f: pl.BlockSpec((None, BLOCK, A_KV_WIDTH), f)
    k3 = k.reshape(bn, t, A_KV_WIDTH)
    v3 = v.reshape(bn, t, A_KV_WIDTH)
    return pl.pallas_call(
        functools.partial(_gqa_win_kernel, t=t),
        grid=(bn, nb),
        in_specs=[pl.BlockSpec(memory_space=pltpu.SMEM),
                  pl.BlockSpec((None, BLOCK, A_WIDTH), cur),
                  kv(prev), kv(cur), kv(nxt), kv(prev), kv(cur), kv(nxt),
                  pl.BlockSpec((None, nctx, A_KV_WIDTH), ctx),
                  pl.BlockSpec((None, nctx, A_KV_WIDTH), ctx)],
        out_specs=pl.BlockSpec((None, BLOCK, A_WIDTH), cur),
        out_shape=jax.ShapeDtypeStruct((bn, t, A_WIDTH), F32),
        compiler_params=_params(2),
        name="gqa_window",
    )(sink, q.reshape(bn, t, A_WIDTH), k3, k3, k3, v3, v3, v3, k_ctx, v_ctx)


def _rwkv_feat_kernel(p_ref, pp_ref, pn_ref, mu_ref, lora_ref, w0_ref, a0_ref, kk_ref, ka_ref,
                      rk_ref, ones_ref, r_out, kkn_out, v_out, lw_out, b_out, ke_out, bonus_out,
                      *, tiles_per_seq):
    i = pl.program_id(0)
    tm = p_ref.shape[0]
    pos = i % tiles_per_seq
    p = p_ref[...]
    rowi = lax.broadcasted_iota(jnp.int32, (tm, 1), 0)
    prev_row = jnp.where(pos != 0, pp_ref[7:8, :], 0.0)
    next_row = jnp.where(pos != tiles_per_seq - 1, pn_ref[0:1, :], 0.0)
    prev = jnp.where(rowi == 0, prev_row, pltpu.roll(p, 1, 0))
    nxt = jnp.where(rowi == tm - 1, next_row, pltpu.roll(p, tm - 1, 0))
    xf = p + (0.5 * (prev + nxt) - p) * mu_ref[...]
    r = xf[:, 0:B_WIDTH]
    k = xf[:, B_WIDTH:2 * B_WIDTH]
    v = xf[:, 2 * B_WIDTH:3 * B_WIDTH]
    g = xf[:, 3 * B_WIDTH:]
    g = jnp.where(_lane_lo(g.shape), jnp.tanh(g), g)
    lo = _bdot(g, lora_ref[...])
    ones = ones_ref[...]
    kkr = k * kk_ref[...]
    kkn = kkr / jnp.maximum(jnp.sqrt(_head_sum(kkr * kkr, ones)), 1e-12)
    ke_sum = jnp.zeros_like(k)
    for z in range(2):
        wz = w0_ref[z] + lo[:, z * B_WIDTH:(z + 1) * B_WIDTH]
        sp = jnp.maximum(-wz, 0.0) + jnp.log(1.0 + jnp.exp(-jnp.abs(wz)))
        lw_out[z] = -jnp.exp(-sp - 0.5)
        az = _sigmoid(a0_ref[z] + lo[:, (2 + z) * B_WIDTH:(3 + z) * B_WIDTH])
        ke = k * (1.0 + (az - 1.0) * ka_ref[...])
        ke_out[z] = ke
        b_out[z] = kkn * az
        ke_sum = ke_sum + ke
    r_out[...] = r
    kkn_out[...] = kkn
    v_out[...] = v
    bonus_out[...] = _head_sum(r * (0.5 * ke_sum) * rk_ref[...], ones) * v


def _rwkv_features(pb, mu, lora, w0, a0, k_k, k_a, r_k, ones, tiles_per_seq):
    n = pb.shape[0]
    nt = n // ROW_TILE
    h8 = ROW_TILE // 8
    row = lambda i: (i, 0)
    const2 = lambda i: (0, 0)
    const3 = lambda i: (0, 0, 0)
    vec = pl.BlockSpec((1, B_WIDTH), const2)
    one = jax.ShapeDtypeStruct((n, B_WIDTH), F32)
    two = jax.ShapeDtypeStruct((2, n, B_WIDTH), F32)
    ospec1 = pl.BlockSpec((ROW_TILE, B_WIDTH), row)
    ospec2 = pl.BlockSpec((2, ROW_TILE, B_WIDTH), lambda i: (0, i, 0))
    return pl.pallas_call(
        functools.partial(_rwkv_feat_kernel, tiles_per_seq=tiles_per_seq),
        grid=(nt,),
        in_specs=[pl.BlockSpec((ROW_TILE, B_SHIFT_WIDTH), row),
                  pl.BlockSpec((8, B_SHIFT_WIDTH), lambda i: (jnp.maximum(i * h8 - 1, 0), 0)),
                  pl.BlockSpec((8, B_SHIFT_WIDTH), lambda i: (jnp.minimum((i + 1) * h8, n // 8 - 1), 0)),
                  pl.BlockSpec((1, B_SHIFT_WIDTH), const2),
                  pl.BlockSpec(lora.shape, const2),
                  pl.BlockSpec((2, 1, B_WIDTH), const3),
                  pl.BlockSpec((2, 1, B_WIDTH), const3),
                  vec, vec, vec,
                  pl.BlockSpec((B_WIDTH, B_WIDTH), const2)],
        out_specs=[ospec1, ospec1, ospec1, ospec2, ospec2, ospec2, ospec1],
        out_shape=[one, one, one, two, two, two, one],
        compiler_params=_params(1),
        name="rwkv_features",
    )(pb, pb, pb, mu, lora, w0, a0, k_k, k_a, r_k, ones)


def _rwkv_scan_kernel(*refs, n_chunks, from_zero):
    (rf_ref, kf_ref, vf_ref, rb_ref, kb_ref, vb_ref, lwf_ref, bf_ref, kef_ref,
     lwb_ref, bb_ref, keb_ref) = refs[:12]
    if from_zero:
        yf_ref, yb_ref, sout_ref, s_scr = refs[12:]
    else:
        s0_ref, yf_ref, yb_ref, s_scr = refs[12:]
    c = pl.program_id(1)

    @pl.when(c == 0)
    def _():
        if from_zero:
            s_scr[...] = jnp.zeros_like(s_scr)
        else:
            s_scr[...] = s0_ref[...]

    row = lax.broadcasted_iota(jnp.int32, (CHUNK, LANES), 0)
    li = lax.broadcasted_iota(jnp.int32, (CHUNK, LANES), 1) % CHUNK
    blk16 = (row // 16) == (li // 16)
    blk32 = (row // 32) == (li // 32)
    eye = jnp.where(row == li, 1.0, 0.0)
    srow = lax.broadcasted_iota(jnp.int32, (PAIR, PAIR), 0)
    scol = lax.broadcasted_iota(jnp.int32, (PAIR, PAIR), 1)
    same_head = (srow // CHUNK) == (scol // CHUNK)
    tr = lax.broadcasted_iota(jnp.int32, (CHUNK, CHUNK), 0)
    tc = lax.broadcasted_iota(jnp.int32, (CHUNK, CHUNK), 1)
    dirs = ((rf_ref, kf_ref, vf_ref, lwf_ref, bf_ref, kef_ref, yf_ref),
            (rb_ref, kb_ref, vb_ref, lwb_ref, bb_ref, keb_ref, yb_ref))

    def advance(s):
        chains = [(z, p) for z in range(2) for p in range(B_HEADS // 2)]
        sls = [slice(p * LANES, (p + 1) * LANES) for _, p in chains]
        before, before_eq, tot, kkd, rd, bg, kg, vv, sc = [], [], [], [], [], [], [], [], []
        for z in range(2):
            r_ref, kk_ref, v_ref, lw_ref, b_ref, ke_ref, _ = dirs[z]
            diff = (li - row) if z else (row - li)
            tri = jnp.where(((tc - tr) if z else (tr - tc)) >= 0, 1.0, 0.0).astype(BF16)
            lw_all = lw_ref[s]
            cum_all = _split_dot(tri, lw_all)
            for p in range(B_HEADS // 2):
                sl = slice(p * LANES, (p + 1) * LANES)
                lw = lw_all[:, sl]
                cum = cum_all[:, sl]
                tt = cum[0:1] if z else cum[CHUNK - 1:CHUNK]
                b = b_ref[s, :, sl]
                ke = ke_ref[s, :, sl]
                g_inv = jnp.exp(-cum)
                g_rest = jnp.exp(tt - cum)
                before.append(diff > 0)
                before_eq.append(diff >= 0)
                tot.append(tt)
                kkd.append(kk_ref[s, :, sl] * jnp.exp(cum - lw))
                rd.append(r_ref[s, :, sl] * jnp.exp(cum))
                bg.append(b * g_rest)
                kg.append(ke * g_rest)
                vv.append(v_ref[s, :, sl])
                sc.append(_bdot(jnp.concatenate([kkd[-1], rd[-1]], 0),
                                jnp.concatenate([_stack_pair(b * g_inv), _stack_pair(ke * g_inv)], 0), NT))
            yield
        lb = [jnp.where(m, x[:CHUNK, :LANES], 0.0) for m, x in zip(before, sc)]
        mb = [jnp.where(m, x[CHUNK:, :LANES], 0.0) for m, x in zip(before_eq, sc)]
        lv = [_bdot(jnp.concatenate([jnp.where(m, x[:CHUNK, LANES:], 0.0),
                                     jnp.where(me, x[CHUNK:, LANES:], 0.0)], 0), _stack_pair(v))
              for m, me, x, v in zip(before, before_eq, sc, vv)]
        yield

        pw = [jnp.where(blk16, x, 0.0) for x in lb]
        tinv = [eye - x for x in pw]
        for _ in range(3):
            pw = [_bdot(x, _stack_pair(x)) for x in pw]
            yield
            tinv = [t + _bdot(t, _stack_pair(x)) for t, x in zip(tinv, pw)]
            yield
        for off in (blk32 & ~blk16, ~blk32):
            tmp = [_bdot(t, _stack_pair(jnp.where(off, x, 0.0))) for t, x in zip(tinv, lb)]
            yield
            tinv = [t - _bdot(m, _stack_pair(t)) for t, m in zip(tinv, tmp)]
            yield

        w12 = [_bdot(t, jnp.concatenate([_stack_pair(k), _stack_pair(x[:CHUNK])], 1))
               for t, k, x in zip(tinv, kkd, lv)]
        yield
        s_old = [s_scr[s, z, p] for z, p in chains]
        ws = [_bdot(jnp.concatenate([w[:, :LANES], r], 0), st, NT) for w, r, st in zip(w12, rd, s_old)]
        yield
        u = [-(a[:CHUNK] + w[:, LANES:]) for a, w in zip(ws, w12)]
        mbu = [_bdot(m, _stack_pair(x)) for m, x in zip(mb, u)]
        upd = [_bdot(jnp.concatenate([x, v], 0).T, jnp.concatenate([b_, k_], 0))
               for x, v, b_, k_ in zip(u, vv, bg, kg)]
        yield
        for i, (z, p) in enumerate(chains):
            dirs[z][6][s, :, sls[i]] = ws[i][CHUNK:] + mbu[i] + lv[i][CHUNK:]
            s_scr[s, z, p] = s_old[i] * jnp.exp(tot[i]) + jnp.where(same_head, upd[i], 0.0)

    live = [advance(s) for s in range(rf_ref.shape[0])]
    while live:
        live = [gen for gen in live if next(gen, "done") != "done"]

    if from_zero:
        @pl.when(c == n_chunks - 1)
        def _():
            for s in range(s_scr.shape[0]):
                for z in range(2):
                    for p in range(B_HEADS // 2):
                        st = s_scr[s, z, p]
                        sout_ref[s, z, 2 * p] = st[:HEAD_DIM, :HEAD_DIM]
                        sout_ref[s, z, 2 * p + 1] = pltpu.roll(st[HEAD_DIM:, :], HEAD_DIM, 1)[:, :HEAD_DIM]


def _rwkv_scan(r, kkn, v, lw, b, ke, s0, bn, t):
    nc = t // CHUNK
    npair = B_HEADS // 2
    ns = min(SCAN_SEQS, bn)
    fwd = pl.BlockSpec((ns, CHUNK, B_WIDTH), lambda g, c: (g, c, 0))
    bwd = pl.BlockSpec((ns, CHUNK, B_WIDTH), lambda g, c: (g, nc - 1 - c, 0))
    dfwd = pl.BlockSpec((None, ns, CHUNK, B_WIDTH), lambda g, c: (0, g, c, 0))
    dbwd = pl.BlockSpec((None, ns, CHUNK, B_WIDTH), lambda g, c: (1, g, nc - 1 - c, 0))
    r3, k3, v3 = (a.reshape(bn, t, B_WIDTH) for a in (r, kkn, v))
    lw4, b4, ke4 = (a.reshape(2, bn, t, B_WIDTH) for a in (lw, b, ke))
    seq = jax.ShapeDtypeStruct((bn, t, B_WIDTH), F32)
    in_specs = [fwd, fwd, fwd, bwd, bwd, bwd, dfwd, dfwd, dfwd, dbwd, dbwd, dbwd]
    args = [r3, k3, v3, r3, k3, v3, lw4, b4, ke4, lw4, b4, ke4]
    out_specs = [fwd, bwd]
    out_shape = [seq, seq]
    if s0 is None:
        out_specs.append(pl.BlockSpec((ns, 2, B_HEADS, HEAD_DIM, HEAD_DIM), lambda g, c: (g, 0, 0, 0, 0)))
        out_shape.append(jax.ShapeDtypeStruct((bn, 2, B_HEADS, HEAD_DIM, HEAD_DIM), F32))
    else:
        in_specs.append(pl.BlockSpec((ns, 2, npair, PAIR, PAIR), lambda g, c: (g, 0, 0, 0, 0)))
        args.append(s0)
    return pl.pallas_call(
        functools.partial(_rwkv_scan_kernel, n_chunks=nc, from_zero=s0 is None),
        grid=(bn // ns, nc),
        in_specs=in_specs,
        out_specs=out_specs,
        out_shape=out_shape,
        scratch_shapes=[pltpu.VMEM((ns, 2, npair, PAIR, PAIR), F32)],
        compiler_params=_params(2),
        name="rwkv_scan",
    )(*args)


def _post(x, z, gate, gain):
    zn = z * lax.rsqrt(jnp.mean(z * z, -1, keepdims=True) + NORM_EPS)
    return x + gate * (zn * gain)


def _even_out_kernel(ya_ref, ga_ref, yf_ref, yb_ref, bonus_ref, gb_ref, lnw_ref, lnb_ref, ones_ref,
                     x_ref, gate_ref, gain_ref, w_ref, o_ref):
    y = yf_ref[...] + yb_ref[...]
    ones = ones_ref[...]
    mean = _head_sum(y, ones) * (1.0 / HEAD_DIM)
    d = y - mean
    var = _head_sum(d * d, ones) * (1.0 / HEAD_DIM)
    yb = (d * lax.rsqrt(var + RWKV_LN_EPS)) * lnw_ref[...] + lnb_ref[...] + bonus_ref[...]
    za = (ya_ref[...] * _silu(ga_ref[...])).astype(BF16)
    zb = (yb * _silu(gb_ref[...])).astype(BF16)
    z = (jnp.dot(za, w_ref[0:A_WIDTH, :], preferred_element_type=F32)
         + jnp.dot(zb, w_ref[A_WIDTH:, :], preferred_element_type=F32))
    o_ref[...] = _post(x_ref[...], z, gate_ref[...], gain_ref[...])


def _even_out(ya, ga, yf, yb, bonus, gb, ln_w, ln_b, ones, x2d, gate, gain, w_bf16, t):
    n = x2d.shape[0]
    row = lambda i: (i, 0)
    const2 = lambda i: (0, 0)
    half = pl.BlockSpec((OUT_TILE, B_WIDTH), row)
    full = pl.BlockSpec((OUT_TILE, D_MODEL), row)
    vec = pl.BlockSpec((1, B_WIDTH), const2)
    assert gate.shape[0] == 1 or t % OUT_TILE == 0
    return pl.pallas_call(
        _even_out_kernel,
        grid=(n // OUT_TILE,),
        in_specs=[half, half, half, half, half, half, vec, vec,
                  pl.BlockSpec((B_WIDTH, B_WIDTH), const2),
                  full,
                  pl.BlockSpec((None, 1, D_MODEL), _mod_index(gate, t // OUT_TILE)),
                  pl.BlockSpec((1, D_MODEL), const2),
                  pl.BlockSpec((D_MODEL, D_MODEL), const2)],
        out_specs=full,
        out_shape=jax.ShapeDtypeStruct((n, D_MODEL), F32),
        compiler_params=_params(1),
        name="even_out",
    )(ya, ga, yf, yb, bonus, gb, ln_w, ln_b, ones, x2d, gate, gain, w_bf16)


def _odd_out_kernel(o_ref_in, gc_ref, x_ref, gate_ref, gain_ref, w_ref, o_ref):
    zc = (o_ref_in[...] * _silu(gc_ref[...])).astype(BF16)
    z = jnp.dot(zc, w_ref[...], preferred_element_type=F32)
    o_ref[...] = _post(x_ref[...], z, gate_ref[...], gain_ref[...])


def _odd_out(o, gc, x2d, gate, gain, w_bf16, t):
    n = x2d.shape[0]
    row = lambda i: (i, 0)
    const2 = lambda i: (0, 0)
    full = pl.BlockSpec((OUT_TILE, D_MODEL), row)
    assert gate.shape[0] == 1 or t % OUT_TILE == 0
    return pl.pallas_call(
        _odd_out_kernel,
        grid=(n // OUT_TILE,),
        in_specs=[full, full, full,
                  pl.BlockSpec((None, 1, D_MODEL), _mod_index(gate, t // OUT_TILE)),
                  pl.BlockSpec((1, D_MODEL), const2),
                  pl.BlockSpec((D_MODEL, D_MODEL), const2)],
        out_specs=full,
        out_shape=jax.ShapeDtypeStruct((n, D_MODEL), F32),
        compiler_params=_params(1),
        name="odd_out",
    )(o, gc, x2d, gate, gain, w_bf16)


def _lambda(lq1_ref, lk1_ref, lq2_ref, lk2_ref, lam_init):
    s1 = jnp.sum(lq1_ref[...] * lk1_ref[...], -1, keepdims=True)
    s2 = jnp.sum(lq2_ref[...] * lk2_ref[...], -1, keepdims=True)
    return jnp.exp(s1) - jnp.exp(s2) + lam_init


def _subln(o, gain, lam_init):
    on = o * lax.rsqrt(jnp.mean(o * o, -1, keepdims=True) + SUBLN_EPS)
    return (on * gain) * (1.0 - lam_init)


def _diff_ctx_kernel(lq1_ref, lk1_ref, lq2_ref, lk2_ref, sub_ref, q_ref, k_ref, v_ref,
                     gc_ref, x_ref, gate_ref, gain_ref, w_ref, y_ref, o_scr, *, lam_init):
    t = q_ref.shape[0]
    lam = _lambda(lq1_ref, lk1_ref, lq2_ref, lk2_ref, lam_init)
    sls = [slice(h * LANES, (h + 1) * LANES) for h in range(C_HEADS)]
    s2 = [_bdot(_stack_pair(q_ref[:, sl]), k_ref[:, sl], NT)
          for sl in sls]
    for h, sl in enumerate(sls):
        e = jnp.exp2(s2[h] - jnp.max(s2[h], -1, keepdims=True))
        ov = _bdot(e, v_ref[:, sl]) * (1.0 / jnp.sum(e, -1, keepdims=True))
        o_scr[:, sl] = _subln(ov[:t] - lam * ov[t:], sub_ref[...], lam_init)
    zc = (o_scr[...] * _silu(gc_ref[...])).astype(BF16)
    z = jnp.dot(zc, w_ref[...], preferred_element_type=F32)
    y_ref[...] = _post(x_ref[...], z, gate_ref[...], gain_ref[...])


def _diff_context_out(lams, subln, q, k, v, gc, x2d, gate, gain, w_bf16, bn, t, lam_init):
    seq = pl.BlockSpec((None, t, C_WIDTH), lambda b: (b, 0, 0))
    small = pl.BlockSpec((1, C_QK_DIM), lambda b: (0, 0))
    const2 = lambda b: (0, 0)
    to3 = lambda a: a.reshape(bn, t, a.shape[-1])
    return pl.pallas_call(
        functools.partial(_diff_ctx_kernel, lam_init=lam_init),
        grid=(bn,),
        in_specs=[small, small, small, small, pl.BlockSpec((1, C_V_DIM), const2),
                  seq, seq, seq, seq, seq,
                  pl.BlockSpec((None, 1, D_MODEL), lambda b: (0, 0, 0)),
                  pl.BlockSpec((1, D_MODEL), const2),
                  pl.BlockSpec((D_MODEL, D_MODEL), const2)],
        out_specs=seq,
        out_shape=jax.ShapeDtypeStruct((bn, t, D_MODEL), F32),
        scratch_shapes=[pltpu.VMEM((t, C_WIDTH), F32)],
        compiler_params=_params(1),
        name="diff_context",
    )(*lams, subln, to3(q), to3(k), to3(v), to3(gc), to3(x2d), gate, gain, w_bf16).reshape(bn * t, D_MODEL)


def _diff_lat_kernel(lq1_ref, lk1_ref, lq2_ref, lk2_ref, sub_ref, q_ref, k_ref, v_ref, kx_ref, vx_ref,
                     o_ref, *, lam_init):
    lam = _lambda(lq1_ref, lk1_ref, lq2_ref, lk2_ref, lam_init)
    kx = kx_ref[...].astype(BF16)
    k = k_ref[...].astype(BF16)
    tq = DIFF_Q_SUB
    subs = range(q_ref.shape[0] // tq)
    scores = []
    for i in subs:
        qs = _stack_pair(q_ref[i * tq:(i + 1) * tq, :])
        scores.append((_bdot(qs, kx, NT), _bdot(qs, k, NT)))
    for i in subs:
        sx, sl = scores[i]
        m = jnp.maximum(jnp.max(sx, -1, keepdims=True), jnp.max(sl, -1, keepdims=True))
        ex = jnp.exp2(sx - m)
        el = jnp.exp2(sl - m)
        den = jnp.sum(ex, -1, keepdims=True) + jnp.sum(el, -1, keepdims=True)
        ratio = lam * den[:tq] / den[tq:]
        o = (_bdot(ex[:tq] - ex[tq:] * ratio, vx_ref[...])
             + _bdot(el[:tq] - el[tq:] * ratio, v_ref[...])) * (1.0 / den[:tq])
        o_ref[i * tq:(i + 1) * tq, :] = _subln(o, sub_ref[...], lam_init)


def _diff_latent(lams, subln, q, k, v, k_ctx, v_ctx, bn, t, lam_init):
    nb = t // DIFF_Q_TILE
    nctx = k_ctx.shape[1]
    small = pl.BlockSpec((1, C_QK_DIM), lambda b, h, n: (0, 0))
    qspec = pl.BlockSpec((None, DIFF_Q_TILE, LANES), lambda b, h, n: (b, n, h))
    kvspec = pl.BlockSpec((None, t, LANES), lambda b, h, n: (b, 0, h))
    cxspec = pl.BlockSpec((None, nctx, LANES), lambda b, h, n: (b, 0, h))
    return pl.pallas_call(
        functools.partial(_diff_lat_kernel, lam_init=lam_init),
        grid=(bn, C_HEADS, nb),
        in_specs=[small, small, small, small,
                  pl.BlockSpec((1, C_V_DIM), lambda b, h, n: (0, 0)),
                  qspec, kvspec, kvspec, cxspec, cxspec],
        out_specs=qspec,
        out_shape=jax.ShapeDtypeStruct((bn, t, C_WIDTH), F32),
        compiler_params=_params(3),
        name="diff_latent",
    )(*lams, subln, q.reshape(bn, t, C_WIDTH), k.reshape(bn, t, C_WIDTH), v.reshape(bn, t, C_WIDTH),
      k_ctx, v_ctx)


def _rope_tables(t):
    nf = HEAD_DIM // 4
    inv = 1.0 / (ROPE_BASE ** (jnp.arange(nf, dtype=F32) / nf))
    pos = jnp.arange(t)
    ang_r = (pos // GRID_W).astype(F32)[:, None] * inv[None]
    ang_c = (pos % GRID_W).astype(F32)[:, None] * inv[None]
    ang = jnp.concatenate([ang_r, ang_r, ang_c, ang_c], -1)
    cos, sin = jnp.cos(ang), jnp.sin(ang)
    sign = jnp.where((jnp.arange(HEAD_DIM) // nf) % 2 == 0, -1.0, 1.0).astype(F32)
    return jnp.tile(cos, (1, 2)), jnp.tile(sin * sign[None], (1, 2))


EVEN_SEGS = ((0, 512), (512, 640), (640, 768), (768, 1280), (1280, 2944), (2944, 3456))
ODD_SEGS = ((0, 1024), (1024, 2048), (2048, 3072), (3072, 4096))


def _pair_states(st):
    bn = st.shape[0]
    s = st.reshape(bn, 2, 4, 2, HEAD_DIM, HEAD_DIM)
    z = jnp.zeros_like(s[:, :, :, 0])
    top = jnp.concatenate([s[:, :, :, 0], z], -1)
    bot = jnp.concatenate([z, s[:, :, :, 1]], -1)
    return jnp.concatenate([top, bot], -2)


def kernel(x_prompt, x_sample, cache_a_k, cache_a_v, state_rwkv, cache_c_k, cache_c_v, c, c_ctx,
           ada_w, ada_b, norm_pre, norm_post, w_out, even_w_in, a_sink, b_mu, b_w0, b_w2, b_a0,
           b_a2, b_kk, b_ka, b_rk, b_ln_w, b_ln_b, odd_w_in, c_lq1, c_lk1, c_lq2, c_lk2, c_subln):
    bp, tp, _ = x_prompt.shape
    bs, ts, _ = x_sample.shape
    past = cache_a_k.shape[2]

    cvecs = jnp.concatenate([c_ctx[None], c, jnp.zeros((8 - 1 - bs, D_MODEL), F32)], 0)
    mods = _modulation(cvecs, ada_w, ada_b)
    rope_tabs = _rope_tables(ts)
    ones = (jnp.arange(B_WIDTH)[:, None] // HEAD_DIM == jnp.arange(B_WIDTH)[None] // HEAD_DIM).astype(BF16)

    xp = x_prompt.reshape(bp * tp, D_MODEL)
    xs = x_sample.reshape(bs * ts, D_MODEL)

    def mod_rows(layer, rows):
        m = mods[layer, rows[0]:rows[1]][:, None, :]
        return m[..., :D_MODEL], m[..., D_MODEL:2 * D_MODEL], m[..., 2 * D_MODEL:]

    e = 0
    w_in0 = even_w_in[e].astype(BF16)
    w_out0 = w_out[0].astype(BF16)
    gain_pre = norm_pre[0][None]
    gain_post = norm_post[0][None]
    zpad = jnp.zeros((DECAY_LORA, B_WIDTH), F32)
    lora = jnp.concatenate([
        jnp.concatenate([b_w2[e, 0], b_w2[e, 1], zpad, zpad], 1),
        jnp.concatenate([zpad, zpad, b_a2[e, 0], b_a2[e, 1]], 1)], 0).astype(BF16)
    feat_args = (b_mu[e][None], lora, b_w0[e][:, None, :], b_a0[e][:, None, :], b_kk[e][None],
                 b_ka[e][None], b_rk[e][None], ones)

    def even_layer(x2d, bn, t, rows, ctx):
        tps = t // ROW_TILE
        shift, scale, gate = mod_rows(0, rows)
        qa, ka, va, ga, pb, gb, *kv_t = _proj_in(x2d, shift, scale, gain_pre, w_in0, EVEN_SEGS, (0, 1), tps,
                                                 None if ctx is None else rope_tabs,
                                                 t_segs=(1, 2) if ctx is None else ())
        if ctx is None:
            ya = _gqa_context(a_sink[e], qa, ka, va, bn, t)
            s0 = None
        else:
            k_ctx, v_ctx, st0 = ctx
            ya = _gqa_window(a_sink[e], qa, ka, va, k_ctx, v_ctx, bn, t)
            s0 = _pair_states(st0)
        r, kkn, v, lw, b, ke, bonus = _rwkv_features(pb, *feat_args, tps)
        yf, yb, *s_fin = _rwkv_scan(r, kkn, v, lw, b, ke, s0, bn, t)
        y = _even_out(ya.reshape(bn * t, A_WIDTH), ga, yf.reshape(bn * t, B_WIDTH),
                      yb.reshape(bn * t, B_WIDTH), bonus, gb,
                      b_ln_w[e][None], b_ln_b[e][None], ones, x2d, gate, gain_post, w_out0, t)
        return y, kv_t, s_fin

    yp, (ka_t, va_t), st_p = even_layer(xp, bp, tp, (0, 1), None)
    ctx_a = (cache_a_k[:, e].reshape(bs, past, A_KV_WIDTH), cache_a_v[:, e].reshape(bs, past, A_KV_WIDTH),
             state_rwkv[:, e])
    ys, _, _ = even_layer(xs, bs, ts, (1, 1 + bs), ctx_a)

    def cache_a(x_t):
        return jnp.transpose(x_t.reshape(bp, A_KV_HEADS, HEAD_DIM, tp), (0, 3, 1, 2))[:, None]

    o = 0
    lam_init = 0.8 - 0.6 * math.exp(-0.3 * 1)
    w_in1 = odd_w_in[o].astype(BF16)
    w_out1 = w_out[1].astype(BF16)
    gain_pre1 = norm_pre[1][None]
    gain_post1 = norm_post[1][None]
    lams = (c_lq1[o][None], c_lk1[o][None], c_lq2[o][None], c_lk2[o][None])
    subln = c_subln[o][None]

    def odd_layer(x2d, bn, t, rows, ctx):
        tps = t // ROW_TILE
        shift, scale, gate = mod_rows(1, rows)
        qc, kc, vc, gc = _proj_in(x2d, shift, scale, gain_pre1, w_in1, ODD_SEGS, (0, 1), tps,
                                  None if ctx is None else rope_tabs)
        if ctx is None:
            y = _diff_context_out(lams, subln, qc, kc, vc, gc, x2d, gate, gain_post1, w_out1, bn, t, lam_init)
        else:
            k_ctx, v_ctx = ctx
            oc = _diff_latent(lams, subln, qc, kc, vc, k_ctx, v_ctx, bn, t, lam_init)
            y = _odd_out(oc.reshape(bn * t, C_WIDTH), gc, x2d, gate, gain_post1, w_out1, t)
        return y, kc, vc

    yp2, kc_p, vc_p = odd_layer(yp, bp, tp, (0, 1), None)
    ctx_c = (cache_c_k[:, o].reshape(bs, past, C_WIDTH), cache_c_v[:, o].reshape(bs, past, C_WIDTH))
    ys2, _, _ = odd_layer(ys, bs, ts, (1, 1 + bs), ctx_c)

    return (yp2.reshape(bp, tp, D_MODEL),
            ys2.reshape(bs, ts, D_MODEL),
            cache_a(ka_t),
            cache_a(va_t),
            st_p[0][:, None],
            kc_p.reshape(bp, 1, tp, C_HEADS, 2 * C_QK_DIM),
            vc_p.reshape(bp, 1, tp, C_HEADS, C_V_DIM))
```

```python
import functools
import math

import jax
import jax.numpy as jnp
from jax import lax
from jax.experimental import pallas as pl
from jax.experimental.pallas import tpu as pltpu

F32 = jnp.float32
BF16 = jnp.bfloat16

D_MODEL = 1024
DEPTH = 2
GRID_W = 64
HEAD_DIM = 64
ROPE_BASE = 10000.0
A_HEADS = 8
A_KV_HEADS = 2
A_WIDTH = 512
A_KV_WIDTH = 128
WINDOW = 128
BLOCK = 128
B_HEADS = 8
B_WIDTH = 512
DECAY_LORA = 64
AAA_LORA = 64
B_SHIFT_WIDTH = 3 * B_WIDTH + DECAY_LORA + AAA_LORA
C_HEADS = 8
C_QK_DIM = 64
C_V_DIM = 128
C_WIDTH = 1024
EVEN_IN = 3456
ODD_IN = 4096
NORM_EPS = 1e-6
RWKV_LN_EPS = 64e-5
SUBLN_EPS = 1e-5
NEG_INF = -1e30

LANES = 128
ROW_TILE = 256
OUT_TILE = 512
DIFF_Q_TILE = 1024
DIFF_Q_SUB = 128
CHUNK = 64
SCAN_SEQS = 4
PAIR = 2 * CHUNK
VMEM_LIMIT = 48 * 1024 * 1024

LOG2E = math.log2(math.e)
QUERY_SCALE = HEAD_DIM ** -0.5 * LOG2E

NN = ((1,), (0,))
NT = ((1,), (1,))


def _bdot(a, b, dims=NN):
    return lax.dot_general(a.astype(BF16), b.astype(BF16), (dims, ((), ())),
                           preferred_element_type=F32)


def _hdot(a, b):
    return lax.dot_general(a, b, (NN, ((), ())), precision=lax.Precision.HIGHEST,
                           preferred_element_type=F32)


def _split_dot(sel_bf16, x):
    x1 = x.astype(BF16)
    r1 = x - x1.astype(F32)
    x2 = r1.astype(BF16)
    x3 = (r1 - x2.astype(F32)).astype(BF16)
    dot = lambda p: jnp.dot(sel_bf16, p, preferred_element_type=F32)
    return dot(x1) + dot(x2) + dot(x3)


def _head_sum(x, ones_bf16):
    hi = x.astype(BF16)
    lo = (x - hi.astype(F32)).astype(BF16)
    return (jnp.dot(hi, ones_bf16, preferred_element_type=F32)
            + jnp.dot(lo, ones_bf16, preferred_element_type=F32))


def _sigmoid(x):
    return 1.0 / (1.0 + jnp.exp(-x))


def _silu(x):
    return x * _sigmoid(x)


def _params(n_axes):
    return pltpu.CompilerParams(dimension_semantics=("arbitrary",) * n_axes,
                                vmem_limit_bytes=VMEM_LIMIT)


def _lane_lo(shape):
    return (lax.broadcasted_iota(jnp.int32, shape, len(shape) - 1) % LANES) < HEAD_DIM


def _stack_pair(x):
    lo = _lane_lo(x.shape)
    z = jnp.zeros_like(x)
    return jnp.concatenate([jnp.where(lo, x, z), jnp.where(lo, z, x)], axis=0)


def _dup_half(x, g):
    lo = _lane_lo(x.shape)
    sw = pltpu.roll(x, HEAD_DIM, 1)
    return jnp.where(lo, x, sw) if g == 0 else jnp.where(lo, sw, x)


def _rope(x, cos, sin_signed):
    out = []
    even_q = (lax.broadcasted_iota(jnp.int32, cos.shape, 1) // (HEAD_DIM // 4)) % 2 == 0
    for g in range(x.shape[1] // LANES):
        xg = x[:, g * LANES:(g + 1) * LANES]
        up = pltpu.roll(xg, LANES - HEAD_DIM // 4, 1)
        dn = pltpu.roll(xg, HEAD_DIM // 4, 1)
        out.append(xg * cos + jnp.where(even_q, up, dn) * sin_signed)
    return out


def _mod_kernel(c_ref, w_ref, b_ref, o_ref):
    s = _silu(c_ref[...])
    o_ref[...] = _hdot(s, w_ref[...]) + b_ref[...]


def _modulation(cvecs, ada_w, ada_b):
    nrow = cvecs.shape[0]
    return pl.pallas_call(
        _mod_kernel,
        grid=(DEPTH, 3),
        in_specs=[pl.BlockSpec((nrow, D_MODEL), lambda l, j: (0, 0)),
                  pl.BlockSpec((None, D_MODEL, D_MODEL), lambda l, j: (l, 0, j)),
                  pl.BlockSpec((None, 1, D_MODEL), lambda l, j: (l, 0, j))],
        out_specs=pl.BlockSpec((None, nrow, D_MODEL), lambda l, j: (l, 0, j)),
        out_shape=jax.ShapeDtypeStruct((DEPTH, nrow, 3 * D_MODEL), F32),
        compiler_params=_params(2),
        name="modulation",
    )(cvecs, ada_w, ada_b.reshape(DEPTH, 1, 3 * D_MODEL))


def _mod_index(mod, tiles_per_seq):
    if mod.shape[0] == 1:
        return lambda i: (0, 0, 0)
    return lambda i: (i // tiles_per_seq, 0, 0)


def _proj_in_kernel(*refs, segs, rope_segs, use_rope, t_segs):
    if use_rope:
        x_ref, sh_ref, sc_ref, g_ref, w_ref, cos_ref, sin_ref = refs[:7]
        outs = refs[7:]
    else:
        x_ref, sh_ref, sc_ref, g_ref, w_ref = refs[:5]
        outs = refs[5:]
    x = x_ref[...]
    y = x * lax.rsqrt(jnp.mean(x * x, -1, keepdims=True) + NORM_EPS)
    h = (y * g_ref[...]) * (1.0 + sc_ref[...]) + sh_ref[...]
    hb = h.astype(BF16)
    for idx, (lo, hi) in enumerate(segs):
        o = jnp.dot(hb, w_ref[:, lo:hi], preferred_element_type=F32)
        post = (lambda a: (a * QUERY_SCALE).astype(BF16)) if idx == 0 else (lambda a: a)
        if use_rope and idx in rope_segs:
            parts = _rope(o, cos_ref[...], sin_ref[...])
            for g, part in enumerate(parts):
                outs[idx][:, g * LANES:(g + 1) * LANES] = post(part)
        else:
            outs[idx][...] = post(o)
        if idx in t_segs:
            outs[len(segs) + t_segs.index(idx)][...] = o.T


def _proj_in(x2d, shift, scale, gain, w_bf16, segs, rope_segs, t, rope_tabs, t_segs=()):
    n = x2d.shape[0]
    tile = t if t_segs else OUT_TILE
    assert shift.shape[0] == 1 or t % tile == 0
    nt = n // tile
    tiles_per_seq = max(t // tile, 1)
    use_rope = rope_tabs is not None
    row = lambda i: (i, 0)
    per_b = _mod_index(shift, tiles_per_seq)
    in_specs = [pl.BlockSpec((tile, D_MODEL), row),
                pl.BlockSpec((None, 1, D_MODEL), per_b),
                pl.BlockSpec((None, 1, D_MODEL), per_b),
                pl.BlockSpec((1, D_MODEL), lambda i: (0, 0)),
                pl.BlockSpec(w_bf16.shape, lambda i: (0, 0))]
    args = [x2d, shift, scale, gain, w_bf16]
    if use_rope:
        pos = lambda i: (i % tiles_per_seq, 0)
        in_specs += [pl.BlockSpec((tile, LANES), pos), pl.BlockSpec((tile, LANES), pos)]
        args += list(rope_tabs)
    widths = [hi - lo for lo, hi in segs]
    return pl.pallas_call(
        functools.partial(_proj_in_kernel, segs=segs, rope_segs=rope_segs, use_rope=use_rope,
                          t_segs=tuple(t_segs)),
        grid=(nt,),
        in_specs=in_specs,
        out_specs=([pl.BlockSpec((tile, w), row) for w in widths]
                   + [pl.BlockSpec((None, widths[i], tile), lambda i_: (i_, 0, 0)) for i in t_segs]),
        out_shape=([jax.ShapeDtypeStruct((n, w), BF16 if i == 0 else F32) for i, w in enumerate(widths)]
                   + [jax.ShapeDtypeStruct((nt, widths[i], tile), F32) for i in t_segs]),
        compiler_params=_params(1),
        name="proj_in",
    )(*args)


def _sink_exp_pair(s2, sink_a, sink_b, n):
    rowi = lax.broadcasted_iota(jnp.int32, (2 * n, 1), 0)
    sink2 = jnp.where(rowi < n, sink_a, sink_b) * LOG2E
    m = jnp.maximum(jnp.max(s2, -1, keepdims=True), sink2)
    e = jnp.exp2(s2 - m)
    return e, 1.0 / (jnp.sum(e, -1, keepdims=True) + jnp.exp2(sink2 - m))


def _gqa_ctx_kernel(sink_ref, q_ref, k_ref, v_ref, o_ref):
    t = q_ref.shape[0]
    k = k_ref[...]
    v = v_ref[...]
    kd = [_dup_half(k, g).astype(BF16) for g in range(A_KV_HEADS)]
    vd = [_dup_half(v, g).astype(BF16) for g in range(A_KV_HEADS)]
    s2 = [_bdot(_stack_pair(q_ref[:, p * LANES:(p + 1) * LANES]),
                kd[p // 2], NT) for p in range(A_HEADS // 2)]
    for p in range(A_HEADS // 2):
        e, inv = _sink_exp_pair(s2[p], sink_ref[2 * p], sink_ref[2 * p + 1], t)
        o = _bdot(e, vd[p // 2]) * inv
        o_ref[:, p * LANES:(p + 1) * LANES] = jnp.where(_lane_lo((t, LANES)), o[:t], o[t:])


def _gqa_context(sink, q, k, v, bn, t):
    seq = lambda b: (b, 0, 0)
    return pl.pallas_call(
        _gqa_ctx_kernel,
        grid=(bn,),
        in_specs=[pl.BlockSpec(memory_space=pltpu.SMEM),
                  pl.BlockSpec((None, t, A_WIDTH), seq),
                  pl.BlockSpec((None, t, A_KV_WIDTH), seq),
                  pl.BlockSpec((None, t, A_KV_WIDTH), seq)],
        out_specs=pl.BlockSpec((None, t, A_WIDTH), seq),
        out_shape=jax.ShapeDtypeStruct((bn, t, A_WIDTH), F32),
        compiler_params=_params(1),
        name="gqa_context",
    )(sink, q.reshape(bn, t, A_WIDTH), k.reshape(bn, t, A_KV_WIDTH), v.reshape(bn, t, A_KV_WIDTH))


def _gqa_win_kernel(sink_ref, q_ref, kp_ref, kc_ref, kn_ref, vp_ref, vc_ref, vn_ref,
                    kx_ref, vx_ref, o_ref, *, t):
    nblk = pl.program_id(1)
    nctx = kx_ref.shape[0]
    keys = jnp.concatenate([kx_ref[...], kp_ref[...], kc_ref[...], kn_ref[...]], axis=0)
    vals = jnp.concatenate([vx_ref[...], vp_ref[...], vc_ref[...], vn_ref[...]], axis=0)
    nkeys = nctx + 3 * BLOCK
    rowi = lax.broadcasted_iota(jnp.int32, (2 * BLOCK, nkeys), 0)
    coli = lax.broadcasted_iota(jnp.int32, (2 * BLOCK, nkeys), 1)
    qpos = nblk * BLOCK + rowi % BLOCK
    kpos = nblk * BLOCK + coli - nctx - BLOCK
    valid = (coli < nctx) | ((jnp.abs(qpos - kpos) <= WINDOW) & (kpos >= 0) & (kpos < t))
    kd = [_dup_half(keys, g).astype(BF16) for g in range(A_KV_HEADS)]
    vd = [_dup_half(vals, g).astype(BF16) for g in range(A_KV_HEADS)]
    s2 = [_bdot(_stack_pair(q_ref[:, p * LANES:(p + 1) * LANES]),
                kd[p // 2], NT) for p in range(A_HEADS // 2)]
    for p in range(A_HEADS // 2):
        e, inv = _sink_exp_pair(jnp.where(valid, s2[p], NEG_INF), sink_ref[2 * p], sink_ref[2 * p + 1], BLOCK)
        o = _bdot(e, vd[p // 2]) * inv
        o_ref[:, p * LANES:(p + 1) * LANES] = jnp.where(_lane_lo((BLOCK, LANES)), o[:BLOCK], o[BLOCK:])


def _gqa_window(sink, q, k, v, k_ctx, v_ctx, bn, t):
    nb = t // BLOCK
    nctx = k_ctx.shape[1]
    cur = lambda b, n: (b, n, 0)
    prev = lambda b, n: (b, jnp.maximum(n - 1, 0), 0)
    nxt = lambda b, n: (b, jnp.minimum(n + 1, nb - 1), 0)
    ctx = lambda b, n: (b, 0, 0)
    kv = lambda f: pl.BlockSpec((None, BLOCK, A_KV_WIDTH), f)
    k3 = k.reshape(bn, t, A_KV_WIDTH)
    v3 = v.reshape(bn, t, A_KV_WIDTH)
    return pl.pallas_call(
        functools.partial(_gqa_win_kernel, t=t),
        grid=(bn, nb),
        in_specs=[pl.BlockSpec(memory_space=pltpu.SMEM),
                  pl.BlockSpec((None, BLOCK, A_WIDTH), cur),
                  kv(prev), kv(cur), kv(nxt), kv(prev), kv(cur), kv(nxt),
                  pl.BlockSpec((None, nctx, A_KV_WIDTH), ctx),
                  pl.BlockSpec((None, nctx, A_KV_WIDTH), ctx)],
        out_specs=pl.BlockSpec((None, BLOCK, A_WIDTH), cur),
        out_shape=jax.ShapeDtypeStruct((bn, t, A_WIDTH), F32),
        compiler_params=_params(2),
        name="gqa_window",
    )(sink, q.reshape(bn, t, A_WIDTH), k3, k3, k3, v3, v3, v3, k_ctx, v_ctx)


def _rwkv_feat_kernel(p_ref, pp_ref, pn_ref, mu_ref, lora_ref, w0_ref, a0_ref, kk_ref, ka_ref,
                      rk_ref, ones_ref, r_out, kkn_out, v_out, lw_out, b_out, ke_out, bonus_out,
                      *, tiles_per_seq):
    i = pl.program_id(0)
    tm = p_ref.shape[0]
    pos = i % tiles_per_seq
    p = p_ref[...]
    rowi = lax.broadcasted_iota(jnp.int32, (tm, 1), 0)
    prev_row = jnp.where(pos != 0, pp_ref[7:8, :], 0.0)
    next_row = jnp.where(pos != tiles_per_seq - 1, pn_ref[0:1, :], 0.0)
    prev = jnp.where(rowi == 0, prev_row, pltpu.roll(p, 1, 0))
    nxt = jnp.where(rowi == tm - 1, next_row, pltpu.roll(p, tm - 1, 0))
    xf = p + (0.5 * (prev + nxt) - p) * mu_ref[...]
    r = xf[:, 0:B_WIDTH]
    k = xf[:, B_WIDTH:2 * B_WIDTH]
    v = xf[:, 2 * B_WIDTH:3 * B_WIDTH]
    g = xf[:, 3 * B_WIDTH:]
    g = jnp.where(_lane_lo(g.shape), jnp.tanh(g), g)
    lo = _bdot(g, lora_ref[...])
    ones = ones_ref[...]
    kkr = k * kk_ref[...]
    kkn = kkr / jnp.maximum(jnp.sqrt(_head_sum(kkr * kkr, ones)), 1e-12)
    ke_sum = jnp.zeros_like(k)
    for z in range(2):
        wz = w0_ref[z] + lo[:, z * B_WIDTH:(z + 1) * B_WIDTH]
        sp = jnp.maximum(-wz, 0.0) + jnp.log(1.0 + jnp.exp(-jnp.abs(wz)))
        lw_out[z] = -jnp.exp(-sp - 0.5)
        az = _sigmoid(a0_ref[z] + lo[:, (2 + z) * B_WIDTH:(3 + z) * B_WIDTH])
        ke = k * (1.0 + (az - 1.0) * ka_ref[...])
        ke_out[z] = ke
        b_out[z] = kkn * az
        ke_sum = ke_sum + ke
    r_out[...] = r
    kkn_out[...] = kkn
    v_out[...] = v
    bonus_out[...] = _head_sum(r * (0.5 * ke_sum) * rk_ref[...], ones) * v


def _rwkv_features(pb, mu, lora, w0, a0, k_k, k_a, r_k, ones, tiles_per_seq):
    n = pb.shape[0]
    nt = n // ROW_TILE
    h8 = ROW_TILE // 8
    row = lambda i: (i, 0)
    const2 = lambda i: (0, 0)
    const3 = lambda i: (0, 0, 0)
    vec = pl.BlockSpec((1, B_WIDTH), const2)
    one = jax.ShapeDtypeStruct((n, B_WIDTH), F32)
    two = jax.ShapeDtypeStruct((2, n, B_WIDTH), F32)
    ospec1 = pl.BlockSpec((ROW_TILE, B_WIDTH), row)
    ospec2 = pl.BlockSpec((2, ROW_TILE, B_WIDTH), lambda i: (0, i, 0))
    return pl.pallas_call(
        functools.partial(_rwkv_feat_kernel, tiles_per_seq=tiles_per_seq),
        grid=(nt,),
        in_specs=[pl.BlockSpec((ROW_TILE, B_SHIFT_WIDTH), row),
                  pl.BlockSpec((8, B_SHIFT_WIDTH), lambda i: (jnp.maximum(i * h8 - 1, 0), 0)),
                  pl.BlockSpec((8, B_SHIFT_WIDTH), lambda i: (jnp.minimum((i + 1) * h8, n // 8 - 1), 0)),
                  pl.BlockSpec((1, B_SHIFT_WIDTH), const2),
                  pl.BlockSpec(lora.shape, const2),
                  pl.BlockSpec((2, 1, B_WIDTH), const3),
                  pl.BlockSpec((2, 1, B_WIDTH), const3),
                  vec, vec, vec,
                  pl.BlockSpec((B_WIDTH, B_WIDTH), const2)],
        out_specs=[ospec1, ospec1, ospec1, ospec2, ospec2, ospec2, ospec1],
        out_shape=[one, one, one, two, two, two, one],
        compiler_params=_params(1),
        name="rwkv_features",
    )(pb, pb, pb, mu, lora, w0, a0, k_k, k_a, r_k, ones)


def _rwkv_scan_kernel(*refs, n_chunks, from_zero):
    (rf_ref, kf_ref, vf_ref, rb_ref, kb_ref, vb_ref, lwf_ref, bf_ref, kef_ref,
     lwb_ref, bb_ref, keb_ref) = refs[:12]
    if from_zero:
        yf_ref, yb_ref, sout_ref, s_scr = refs[12:]
    else:
        s0_ref, yf_ref, yb_ref, s_scr = refs[12:]
    c = pl.program_id(1)

    @pl.when(c == 0)
    def _():
        if from_zero:
            s_scr[...] = jnp.zeros_like(s_scr)
        else:
            s_scr[...] = s0_ref[...]

    row = lax.broadcasted_iota(jnp.int32, (CHUNK, LANES), 0)
    li = lax.broadcasted_iota(jnp.int32, (CHUNK, LANES), 1) % CHUNK
    blk16 = (row // 16) == (li // 16)
    blk32 = (row // 32) == (li // 32)
    eye = jnp.where(row == li, 1.0, 0.0)
    srow = lax.broadcasted_iota(jnp.int32, (PAIR, PAIR), 0)
    scol = lax.broadcasted_iota(jnp.int32, (PAIR, PAIR), 1)
    same_head = (srow // CHUNK) == (scol // CHUNK)
    tr = lax.broadcasted_iota(jnp.int32, (CHUNK, CHUNK), 0)
    tc = lax.broadcasted_iota(jnp.int32, (CHUNK, CHUNK), 1)
    dirs = ((rf_ref, kf_ref, vf_ref, lwf_ref, bf_ref, kef_ref, yf_ref),
            (rb_ref, kb_ref, vb_ref, lwb_ref, bb_ref, keb_ref, yb_ref))

    def advance(s):
        chains = [(z, p) for z in range(2) for p in range(B_HEADS // 2)]
        sls = [slice(p * LANES, (p + 1) * LANES) for _, p in chains]
        before, before_eq, tot, kkd, rd, bg, kg, vv, sc = [], [], [], [], [], [], [], [], []
        for z in range(2):
            r_ref, kk_ref, v_ref, lw_ref, b_ref, ke_ref, _ = dirs[z]
            diff = (li - row) if z else (row - li)
            tri = jnp.where(((tc - tr) if z else (tr - tc)) >= 0, 1.0, 0.0).astype(BF16)
            lw_all = lw_ref[s]
            cum_all = _split_dot(tri, lw_all)
            for p in range(B_HEADS // 2):
                sl = slice(p * LANES, (p + 1) * LANES)
                lw = lw_all[:, sl]
                cum = cum_all[:, sl]
                tt = cum[0:1] if z else cum[CHUNK - 1:CHUNK]
                b = b_ref[s, :, sl]
                ke = ke_ref[s, :, sl]
                g_inv = jnp.exp(-cum)
                g_rest = jnp.exp(tt - cum)
                before.append(diff > 0)
                before_eq.append(diff >= 0)
                tot.append(tt)
                kkd.append(kk_ref[s, :, sl] * jnp.exp(cum - lw))
                rd.append(r_ref[s, :, sl] * jnp.exp(cum))
                bg.append(b * g_rest)
                kg.append(ke * g_rest)
                vv.append(v_ref[s, :, sl])
                sc.append(_bdot(jnp.concatenate([kkd[-1], rd[-1]], 0),
                                jnp.concatenate([_stack_pair(b * g_inv), _stack_pair(ke * g_inv)], 0), NT))
            yield
        lb = [jnp.where(m, x[:CHUNK, :LANES], 0.0) for m, x in zip(before, sc)]
        mb = [jnp.where(m, x[CHUNK:, :LANES], 0.0) for m, x in zip(before_eq, sc)]
        lv = [_bdot(jnp.concatenate([jnp.where(m, x[:CHUNK, LANES:], 0.0),
                                     jnp.where(me, x[CHUNK:, LANES:], 0.0)], 0), _stack_pair(v))
              for m, me, x, v in zip(before, before_eq, sc, vv)]
        yield

        pw = [jnp.where(blk16, x, 0.0) for x in lb]
        tinv = [eye - x for x in pw]
        for _ in range(3):
            pw = [_bdot(x, _stack_pair(x)) for x in pw]
            yield
            tinv = [t + _bdot(t, _stack_pair(x)) for t, x in zip(tinv, pw)]
            yield
        for off in (blk32 & ~blk16, ~blk32):
            tmp = [_bdot(t, _stack_pair(jnp.where(off, x, 0.0))) for t, x in zip(tinv, lb)]
            yield
            tinv = [t - _bdot(m, _stack_pair(t)) for t, m in zip(tinv, tmp)]
            yield

        w12 = [_bdot(t, jnp.concatenate([_stack_pair(k), _stack_pair(x[:CHUNK])], 1))
               for t, k, x in zip(tinv, kkd, lv)]
        yield
        s_old = [s_scr[s, z, p] for z, p in chains]
        ws = [_bdot(jnp.concatenate([w[:, :LANES], r], 0), st, NT) for w, r, st in zip(w12, rd, s_old)]
        yield
        u = [-(a[:CHUNK] + w[:, LANES:]) for a, w in zip(ws, w12)]
        mbu = [_bdot(m, _stack_pair(x)) for m, x in zip(mb, u)]
        upd = [_bdot(jnp.concatenate([x, v], 0).T, jnp.concatenate([b_, k_], 0))
               for x, v, b_, k_ in zip(u, vv, bg, kg)]
        yield
        for i, (z, p) in enumerate(chains):
            dirs[z][6][s, :, sls[i]] = ws[i][CHUNK:] + mbu[i] + lv[i][CHUNK:]
            s_scr[s, z, p] = s_old[i] * jnp.exp(tot[i]) + jnp.where(same_head, upd[i], 0.0)

    live = [advance(s) for s in range(rf_ref.shape[0])]
    while live:
        live = [gen for gen in live if next(gen, "done") != "done"]

    if from_zero:
        @pl.when(c == n_chunks - 1)
        def _():
            for s in range(s_scr.shape[0]):
                for z in range(2):
                    for p in range(B_HEADS // 2):
                        st = s_scr[s, z, p]
                        sout_ref[s, z, 2 * p] = st[:HEAD_DIM, :HEAD_DIM]
                        sout_ref[s, z, 2 * p + 1] = pltpu.roll(st[HEAD_DIM:, :], HEAD_DIM, 1)[:, :HEAD_DIM]


def _rwkv_scan(r, kkn, v, lw, b, ke, s0, bn, t):
    nc = t // CHUNK
    npair = B_HEADS // 2
    ns = min(SCAN_SEQS, bn)
    fwd = pl.BlockSpec((ns, CHUNK, B_WIDTH), lambda g, c: (g, c, 0))
    bwd = pl.BlockSpec((ns, CHUNK, B_WIDTH), lambda g, c: (g, nc - 1 - c, 0))
    dfwd = pl.BlockSpec((None, ns, CHUNK, B_WIDTH), lambda g, c: (0, g, c, 0))
    dbwd = pl.BlockSpec((None, ns, CHUNK, B_WIDTH), lambda g, c: (1, g, nc - 1 - c, 0))
    r3, k3, v3 = (a.reshape(bn, t, B_WIDTH) for a in (r, kkn, v))
    lw4, b4, ke4 = (a.reshape(2, bn, t, B_WIDTH) for a in (lw, b, ke))
    seq = jax.ShapeDtypeStruct((bn, t, B_WIDTH), F32)
    in_specs = [fwd, fwd, fwd, bwd, bwd, bwd, dfwd, dfwd, dfwd, dbwd, dbwd, dbwd]
    args = [r3, k3, v3, r3, k3, v3, lw4, b4, ke4, lw4, b4, ke4]
    out_specs = [fwd, bwd]
    out_shape = [seq, seq]
    if s0 is None:
        out_specs.append(pl.BlockSpec((ns, 2, B_HEADS, HEAD_DIM, HEAD_DIM), lambda g, c: (g, 0, 0, 0, 0)))
        out_shape.append(jax.ShapeDtypeStruct((bn, 2, B_HEADS, HEAD_DIM, HEAD_DIM), F32))
    else:
        in_specs.append(pl.BlockSpec((ns, 2, npair, PAIR, PAIR), lambda g, c: (g, 0, 0, 0, 0)))
        args.append(s0)
    return pl.pallas_call(
        functools.partial(_rwkv_scan_kernel, n_chunks=nc, from_zero=s0 is None),
        grid=(bn // ns, nc),
        in_specs=in_specs,
        out_specs=out_specs,
        out_shape=out_shape,
        scratch_shapes=[pltpu.VMEM((ns, 2, npair, PAIR, PAIR), F32)],
        compiler_params=_params(2),
        name="rwkv_scan",
    )(*args)


def _post(x, z, gate, gain):
    zn = z * lax.rsqrt(jnp.mean(z * z, -1, keepdims=True) + NORM_EPS)
    return x + gate * (zn * gain)


def _even_out_kernel(ya_ref, ga_ref, yf_ref, yb_ref, bonus_ref, gb_ref, lnw_ref, lnb_ref, ones_ref,
                     x_ref, gate_ref, gain_ref, w_ref, o_ref):
    y = yf_ref[...] + yb_ref[...]
    ones = ones_ref[...]
    mean = _head_sum(y, ones) * (1.0 / HEAD_DIM)
    d = y - mean
    var = _head_sum(d * d, ones) * (1.0 / HEAD_DIM)
    yb = (d * lax.rsqrt(var + RWKV_LN_EPS)) * lnw_ref[...] + lnb_ref[...] + bonus_ref[...]
    za = (ya_ref[...] * _silu(ga_ref[...])).astype(BF16)
    zb = (yb * _silu(gb_ref[...])).astype(BF16)
    z = (jnp.dot(za, w_ref[0:A_WIDTH, :], preferred_element_type=F32)
         + jnp.dot(zb, w_ref[A_WIDTH:, :], preferred_element_type=F32))
    o_ref[...] = _post(x_ref[...], z, gate_ref[...], gain_ref[...])


def _even_out(ya, ga, yf, yb, bonus, gb, ln_w, ln_b, ones, x2d, gate, gain, w_bf16, t):
    n = x2d.shape[0]
    row = lambda i: (i, 0)
    const2 = lambda i: (0, 0)
    half = pl.BlockSpec((OUT_TILE, B_WIDTH), row)
    full = pl.BlockSpec((OUT_TILE, D_MODEL), row)
    vec = pl.BlockSpec((1, B_WIDTH), const2)
    assert gate.shape[0] == 1 or t % OUT_TILE == 0
    return pl.pallas_call(
        _even_out_kernel,
        grid=(n // OUT_TILE,),
        in_specs=[half, half, half, half, half, half, vec, vec,
                  pl.BlockSpec((B_WIDTH, B_WIDTH), const2),
                  full,
                  pl.BlockSpec((None, 1, D_MODEL), _mod_index(gate, t // OUT_TILE)),
                  pl.BlockSpec((1, D_MODEL), const2),
                  pl.BlockSpec((D_MODEL, D_MODEL), const2)],
        out_specs=full,
        out_shape=jax.ShapeDtypeStruct((n, D_MODEL), F32),
        compiler_params=_params(1),
        name="even_out",
    )(ya, ga, yf, yb, bonus, gb, ln_w, ln_b, ones, x2d, gate, gain, w_bf16)


def _odd_out_kernel(o_ref_in, gc_ref, x_ref, gate_ref, gain_ref, w_ref, o_ref):
    zc = (o_ref_in[...] * _silu(gc_ref[...])).astype(BF16)
    z = jnp.dot(zc, w_ref[...], preferred_element_type=F32)
    o_ref[...] = _post(x_ref[...], z, gate_ref[...], gain_ref[...])


def _odd_out(o, gc, x2d, gate, gain, w_bf16, t):
    n = x2d.shape[0]
    row = lambda i: (i, 0)
    const2 = lambda i: (0, 0)
    full = pl.BlockSpec((OUT_TILE, D_MODEL), row)
    assert gate.shape[0] == 1 or t % OUT_TILE == 0
    return pl.pallas_call(
        _odd_out_kernel,
        grid=(n // OUT_TILE,),
        in_specs=[full, full, full,
                  pl.BlockSpec((None, 1, D_MODEL), _mod_index(gate, t // OUT_TILE)),
                  pl.BlockSpec((1, D_MODEL), const2),
                  pl.BlockSpec((D_MODEL, D_MODEL), const2)],
        out_specs=full,
        out_shape=jax.ShapeDtypeStruct((n, D_MODEL), F32),
        compiler_params=_params(1),
        name="odd_out",
    )(o, gc, x2d, gate, gain, w_bf16)


def _lambda(lq1_ref, lk1_ref, lq2_ref, lk2_ref, lam_init):
    s1 = jnp.sum(lq1_ref[...] * lk1_ref[...], -1, keepdims=True)
    s2 = jnp.sum(lq2_ref[...] * lk2_ref[...], -1, keepdims=True)
    return jnp.exp(s1) - jnp.exp(s2) + lam_init


def _subln(o, gain, lam_init):
    on = o * lax.rsqrt(jnp.mean(o * o, -1, keepdims=True) + SUBLN_EPS)
    return (on * gain) * (1.0 - lam_init)


def _diff_ctx_kernel(lq1_ref, lk1_ref, lq2_ref, lk2_ref, sub_ref, q_ref, k_ref, v_ref,
                     gc_ref, x_ref, gate_ref, gain_ref, w_ref, y_ref, o_scr, *, lam_init):
    t = q_ref.shape[0]
    lam = _lambda(lq1_ref, lk1_ref, lq2_ref, lk2_ref, lam_init)
    sls = [slice(h * LANES, (h + 1) * LANES) for h in range(C_HEADS)]
    s2 = [_bdot(_stack_pair(q_ref[:, sl]), k_ref[:, sl], NT)
          for sl in sls]
    for h, sl in enumerate(sls):
        e = jnp.exp2(s2[h] - jnp.max(s2[h], -1, keepdims=True))
        ov = _bdot(e, v_ref[:, sl]) * (1.0 / jnp.sum(e, -1, keepdims=True))
        o_scr[:, sl] = _subln(ov[:t] - lam * ov[t:], sub_ref[...], lam_init)
    zc = (o_scr[...] * _silu(gc_ref[...])).astype(BF16)
    z = jnp.dot(zc, w_ref[...], preferred_element_type=F32)
    y_ref[...] = _post(x_ref[...], z, gate_ref[...], gain_ref[...])


def _diff_context_out(lams, subln, q, k, v, gc, x2d, gate, gain, w_bf16, bn, t, lam_init):
    seq = pl.BlockSpec((None, t, C_WIDTH), lambda b: (b, 0, 0))
    small = pl.BlockSpec((1, C_QK_DIM), lambda b: (0, 0))
    const2 = lambda b: (0, 0)
    to3 = lambda a: a.reshape(bn, t, a.shape[-1])
    return pl.pallas_call(
        functools.partial(_diff_ctx_kernel, lam_init=lam_init),
        grid=(bn,),
        in_specs=[small, small, small, small, pl.BlockSpec((1, C_V_DIM), const2),
                  seq, seq, seq, seq, seq,
                  pl.BlockSpec((None, 1, D_MODEL), lambda b: (0, 0, 0)),
                  pl.BlockSpec((1, D_MODEL), const2),
                  pl.BlockSpec((D_MODEL, D_MODEL), const2)],
        out_specs=seq,
        out_shape=jax.ShapeDtypeStruct((bn, t, D_MODEL), F32),
        scratch_shapes=[pltpu.VMEM((t, C_WIDTH), F32)],
        compiler_params=_params(1),
        name="diff_context",
    )(*lams, subln, to3(q), to3(k), to3(v), to3(gc), to3(x2d), gate, gain, w_bf16).reshape(bn * t, D_MODEL)


def _diff_lat_kernel(lq1_ref, lk1_ref, lq2_ref, lk2_ref, sub_ref, q_ref, k_ref, v_ref, kx_ref, vx_ref,
                     o_ref, *, lam_init):
    lam = _lambda(lq1_ref, lk1_ref, lq2_ref, lk2_ref, lam_init)
    kx = kx_ref[...].astype(BF16)
    k = k_ref[...].astype(BF16)
    tq = DIFF_Q_SUB
    subs = range(q_ref.shape[0] // tq)
    scores = []
    for i in subs:
        qs = _stack_pair(q_ref[i * tq:(i + 1) * tq, :])
        scores.append((_bdot(qs, kx, NT), _bdot(qs, k, NT)))
    for i in subs:
        sx, sl = scores[i]
        m = jnp.maximum(jnp.max(sx, -1, keepdims=True), jnp.max(sl, -1, keepdims=True))
        ex = jnp.exp2(sx - m)
        el = jnp.exp2(sl - m)
        den = jnp.sum(ex, -1, keepdims=True) + jnp.sum(el, -1, keepdims=True)
        ratio = lam * den[:tq] / den[tq:]
        o = (_bdot(ex[:tq] - ex[tq:] * ratio, vx_ref[...])
             + _bdot(el[:tq] - el[tq:] * ratio, v_ref[...])) * (1.0 / den[:tq])
        o_ref[i * tq:(i + 1) * tq, :] = _subln(o, sub_ref[...], lam_init)


def _diff_latent(lams, subln, q, k, v, k_ctx, v_ctx, bn, t, lam_init):
    nb = t // DIFF_Q_TILE
    nctx = k_ctx.shape[1]
    small = pl.BlockSpec((1, C_QK_DIM), lambda b, h, n: (0, 0))
    qspec = pl.BlockSpec((None, DIFF_Q_TILE, LANES), lambda b, h, n: (b, n, h))
    kvspec = pl.BlockSpec((None, t, LANES), lambda b, h, n: (b, 0, h))
    cxspec = pl.BlockSpec((None, nctx, LANES), lambda b, h, n: (b, 0, h))
    return pl.pallas_call(
        functools.partial(_diff_lat_kernel, lam_init=lam_init),
        grid=(bn, C_HEADS, nb),
        in_specs=[small, small, small, small,
                  pl.BlockSpec((1, C_V_DIM), lambda b, h, n: (0, 0)),
                  qspec, kvspec, kvspec, cxspec, cxspec],
        out_specs=qspec,
        out_shape=jax.ShapeDtypeStruct((bn, t, C_WIDTH), F32),
        compiler_params=_params(3),
        name="diff_latent",
    )(*lams, subln, q.reshape(bn, t, C_WIDTH), k.reshape(bn, t, C_WIDTH), v.reshape(bn, t, C_WIDTH),
      k_ctx, v_ctx)


def _rope_tables(t):
    nf = HEAD_DIM // 4
    inv = 1.0 / (ROPE_BASE ** (jnp.arange(nf, dtype=F32) / nf))
    pos = jnp.arange(t)
    ang_r = (pos // GRID_W).astype(F32)[:, None] * inv[None]
    ang_c = (pos % GRID_W).astype(F32)[:, None] * inv[None]
    ang = jnp.concatenate([ang_r, ang_r, ang_c, ang_c], -1)
    cos, sin = jnp.cos(ang), jnp.sin(ang)
    sign = jnp.where((jnp.arange(HEAD_DIM) // nf) % 2 == 0, -1.0, 1.0).astype(F32)
    return jnp.tile(cos, (1, 2)), jnp.tile(sin * sign[None], (1, 2))


EVEN_SEGS = ((0, 512), (512, 640), (640, 768), (768, 1280), (1280, 2944), (2944, 3456))
ODD_SEGS = ((0, 1024), (1024, 2048), (2048, 3072), (3072, 4096))


def _pair_states(st):
    bn = st.shape[0]
    s = st.reshape(bn, 2, 4, 2, HEAD_DIM, HEAD_DIM)
    z = jnp.zeros_like(s[:, :, :, 0])
    top = jnp.concatenate([s[:, :, :, 0], z], -1)
    bot = jnp.concatenate([z, s[:, :, :, 1]], -1)
    return jnp.concatenate([top, bot], -2)


def kernel(x_prompt, x_sample, cache_a_k, cache_a_v, state_rwkv, cache_c_k, cache_c_v, c, c_ctx,
           ada_w, ada_b, norm_pre, norm_post, w_out, even_w_in, a_sink, b_mu, b_w0, b_w2, b_a0,
           b_a2, b_kk, b_ka, b_rk, b_ln_w, b_ln_b, odd_w_in, c_lq1, c_lk1, c_lq2, c_lk2, c_subln):
    bp, tp, _ = x_prompt.shape
    bs, ts, _ = x_sample.shape
    past = cache_a_k.shape[2]

    cvecs = jnp.concatenate([c_ctx[None], c, jnp.zeros((8 - 1 - bs, D_MODEL), F32)], 0)
    mods = _modulation(cvecs, ada_w, ada_b)
    rope_tabs = _rope_tables(ts)
    ones = (jnp.arange(B_WIDTH)[:, None] // HEAD_DIM == jnp.arange(B_WIDTH)[None] // HEAD_DIM).astype(BF16)

    xp = x_prompt.reshape(bp * tp, D_MODEL)
    xs = x_sample.reshape(bs * ts, D_MODEL)

    def mod_rows(layer, rows):
        m = mods[layer, rows[0]:rows[1]][:, None, :]
        return m[..., :D_MODEL], m[..., D_MODEL:2 * D_MODEL], m[..., 2 * D_MODEL:]

    e = 0
    w_in0 = even_w_in[e].astype(BF16)
    w_out0 = w_out[0].astype(BF16)
    gain_pre = norm_pre[0][None]
    gain_post = norm_post[0][None]
    zpad = jnp.zeros((DECAY_LORA, B_WIDTH), F32)
    lora = jnp.concatenate([
        jnp.concatenate([b_w2[e, 0], b_w2[e, 1], zpad, zpad], 1),
        jnp.concatenate([zpad, zpad, b_a2[e, 0], b_a2[e, 1]], 1)], 0).astype(BF16)
    feat_args = (b_mu[e][None], lora, b_w0[e][:, None, :], b_a0[e][:, None, :], b_kk[e][None],
                 b_ka[e][None], b_rk[e][None], ones)

    def even_layer(x2d, bn, t, rows, ctx):
        tps = t // ROW_TILE
        shift, scale, gate = mod_rows(0, rows)
        qa, ka, va, ga, pb, gb, *kv_t = _proj_in(x2d, shift, scale, gain_pre, w_in0, EVEN_SEGS, (0, 1), t,
                                                 None if ctx is None else rope_tabs,
                                                 t_segs=(1, 2) if ctx is None else ())
        if ctx is None:
            ya = _gqa_context(a_sink[e], qa, ka, va, bn, t)
            s0 = None
        else:
            k_ctx, v_ctx, st0 = ctx
            ya = _gqa_window(a_sink[e], qa, ka, va, k_ctx, v_ctx, bn, t)
            s0 = _pair_states(st0)
        r, kkn, v, lw, b, ke, bonus = _rwkv_features(pb, *feat_args, tps)
        yf, yb, *s_fin = _rwkv_scan(r, kkn, v, lw, b, ke, s0, bn, t)
        y = _even_out(ya.reshape(bn * t, A_WIDTH), ga, yf.reshape(bn * t, B_WIDTH),
                      yb.reshape(bn * t, B_WIDTH), bonus, gb,
                      b_ln_w[e][None], b_ln_b[e][None], ones, x2d, gate, gain_post, w_out0, t)
        return y, kv_t, s_fin

    yp, (ka_t, va_t), st_p = even_layer(xp, bp, tp, (0, 1), None)
    ctx_a = (cache_a_k[:, e].reshape(bs, past, A_KV_WIDTH), cache_a_v[:, e].reshape(bs, past, A_KV_WIDTH),
             state_rwkv[:, e])
    ys, _, _ = even_layer(xs, bs, ts, (1, 1 + bs), ctx_a)

    def cache_a(x_t):
        return jnp.transpose(x_t.reshape(bp, A_KV_HEADS, HEAD_DIM, tp), (0, 3, 1, 2))[:, None]

    o = 0
    lam_init = 0.8 - 0.6 * math.exp(-0.3 * 1)
    w_in1 = odd_w_in[o].astype(BF16)
    w_out1 = w_out[1].astype(BF16)
    gain_pre1 = norm_pre[1][None]
    gain_post1 = norm_post[1][None]
    lams = (c_lq1[o][None], c_lk1[o][None], c_lq2[o][None], c_lk2[o][None])
    subln = c_subln[o][None]

    def odd_layer(x2d, bn, t, rows, ctx):
        shift, scale, gate = mod_rows(1, rows)
        qc, kc, vc, gc = _proj_in(x2d, shift, scale, gain_pre1, w_in1, ODD_SEGS, (0, 1), t,
                                  None if ctx is None else rope_tabs)
        if ctx is None:
            y = _diff_context_out(lams, subln, qc, kc, vc, gc, x2d, gate, gain_post1, w_out1, bn, t, lam_init)
        else:
            k_ctx, v_ctx = ctx
            oc = _diff_latent(lams, subln, qc, kc, vc, k_ctx, v_ctx, bn, t, lam_init)
            y = _odd_out(oc.reshape(bn * t, C_WIDTH), gc, x2d, gate, gain_post1, w_out1, t)
        return y, kc, vc

    yp2, kc_p, vc_p = odd_layer(yp, bp, tp, (0, 1), None)
    ctx_c = (cache_c_k[:, o].reshape(bs, past, C_WIDTH), cache_c_v[:, o].reshape(bs, past, C_WIDTH))
    ys2, _, _ = odd_layer(ys, bs, ts, (1, 1 + bs), ctx_c)

    return (yp2.reshape(bp, tp, D_MODEL),
            ys2.reshape(bs, ts, D_MODEL),
            cache_a(ka_t),
            cache_a(va_t),
            st_p[0][:, None],
            kc_p.reshape(bp, 1, tp, C_HEADS, 2 * C_QK_DIM),
            vc_p.reshape(bp, 1, tp, C_HEADS, C_V_DIM))
```

```python
import functools
import math

import jax
import jax.numpy as jnp
from jax import lax
from jax.experimental import pallas as pl
from jax.experimental.pallas import tpu as pltpu

F32 = jnp.float32
BF16 = jnp.bfloat16

D_MODEL = 1024
DEPTH = 2
GRID_W = 64
HEAD_DIM = 64
ROPE_BASE = 10000.0
A_HEADS = 8
A_KV_HEADS = 2
A_WIDTH = 512
A_KV_WIDTH = 128
WINDOW = 128
BLOCK = 128
B_HEADS = 8
B_WIDTH = 512
DECAY_LORA = 64
AAA_LORA = 64
B_SHIFT_WIDTH = 3 * B_WIDTH + DECAY_LORA + AAA_LORA
C_HEADS = 8
C_QK_DIM = 64
C_V_DIM = 128
C_WIDTH = 1024
EVEN_IN = 3456
ODD_IN = 4096
NORM_EPS = 1e-6
RWKV_LN_EPS = 64e-5
SUBLN_EPS = 1e-5
NEG_INF = -1e30

LANES = 128
ROW_TILE = 256
OUT_TILE = 512
DIFF_Q_TILE = 1024
DIFF_Q_SUB = 128
CHUNK = 64
SCAN_SEQS = 4
PAIR = 2 * CHUNK
VMEM_LIMIT = 48 * 1024 * 1024

LOG2E = math.log2(math.e)
QUERY_SCALE = HEAD_DIM ** -0.5 * LOG2E

NN = ((1,), (0,))
NT = ((1,), (1,))


def _bdot(a, b, dims=NN):
    return lax.dot_general(a.astype(BF16), b.astype(BF16), (dims, ((), ())),
                           preferred_element_type=F32)


def _hdot(a, b):
    return lax.dot_general(a, b, (NN, ((), ())), precision=lax.Precision.HIGHEST,
                           preferred_element_type=F32)


def _split_dot(sel_bf16, x):
    x1 = x.astype(BF16)
    r1 = x - x1.astype(F32)
    x2 = r1.astype(BF16)
    x3 = (r1 - x2.astype(F32)).astype(BF16)
    dot = lambda p: jnp.dot(sel_bf16, p, preferred_element_type=F32)
    return dot(x1) + dot(x2) + dot(x3)


def _head_sum(x, ones_bf16):
    hi = x.astype(BF16)
    lo = (x - hi.astype(F32)).astype(BF16)
    return (jnp.dot(hi, ones_bf16, preferred_element_type=F32)
            + jnp.dot(lo, ones_bf16, preferred_element_type=F32))


def _sigmoid(x):
    return 1.0 / (1.0 + jnp.exp(-x))


def _silu(x):
    return x * _sigmoid(x)


def _params(n_axes):
    return pltpu.CompilerParams(dimension_semantics=("arbitrary",) * n_axes,
                                vmem_limit_bytes=VMEM_LIMIT)


def _lane_lo(shape):
    return (lax.broadcasted_iota(jnp.int32, shape, len(shape) - 1) % LANES) < HEAD_DIM


def _stack_pair(x):
    lo = _lane_lo(x.shape)
    z = jnp.zeros_like(x)
    return jnp.concatenate([jnp.where(lo, x, z), jnp.where(lo, z, x)], axis=0)


def _dup_half(x, g):
    lo = _lane_lo(x.shape)
    sw = pltpu.roll(x, HEAD_DIM, 1)
    return jnp.where(lo, x, sw) if g == 0 else jnp.where(lo, sw, x)


def _rope(x, cos, sin_signed):
    out = []
    even_q = (lax.broadcasted_iota(jnp.int32, cos.shape, 1) // (HEAD_DIM // 4)) % 2 == 0
    for g in range(x.shape[1] // LANES):
        xg = x[:, g * LANES:(g + 1) * LANES]
        up = pltpu.roll(xg, LANES - HEAD_DIM // 4, 1)
        dn = pltpu.roll(xg, HEAD_DIM // 4, 1)
        out.append(xg * cos + jnp.where(even_q, up, dn) * sin_signed)
    return out


def _mod_kernel(c_ref, w_ref, b_ref, o_ref):
    s = _silu(c_ref[...])
    o_ref[...] = _hdot(s, w_ref[...]) + b_ref[...]


def _modulation(cvecs, ada_w, ada_b):
    nrow = cvecs.shape[0]
    return pl.pallas_call(
        _mod_kernel,
        grid=(DEPTH, 3),
        in_specs=[pl.BlockSpec((nrow, D_MODEL), lambda l, j: (0, 0)),
                  pl.BlockSpec((None, D_MODEL, D_MODEL), lambda l, j: (l, 0, j)),
                  pl.BlockSpec((None, 1, D_MODEL), lambda l, j: (l, 0, j))],
        out_specs=pl.BlockSpec((None, nrow, D_MODEL), lambda l, j: (l, 0, j)),
        out_shape=jax.ShapeDtypeStruct((DEPTH, nrow, 3 * D_MODEL), F32),
        compiler_params=_params(2),
        name="modulation",
    )(cvecs, ada_w, ada_b.reshape(DEPTH, 1, 3 * D_MODEL))


def _mod_index(mod, tiles_per_seq):
    if mod.shape[0] == 1:
        return lambda i: (0, 0, 0)
    return lambda i: (i // tiles_per_seq, 0, 0)


def _proj_in_kernel(*refs, segs, rope_segs, use_rope, t_segs):
    if use_rope:
        x_ref, sh_ref, sc_ref, g_ref, w_ref, cos_ref, sin_ref = refs[:7]
        outs = refs[7:]
    else:
        x_ref, sh_ref, sc_ref, g_ref, w_ref = refs[:5]
        outs = refs[5:]
    x = x_ref[...]
    y = x * lax.rsqrt(jnp.mean(x * x, -1, keepdims=True) + NORM_EPS)
    h = (y * g_ref[...]) * (1.0 + sc_ref[...]) + sh_ref[...]
    hb = h.astype(BF16)
    for idx, (lo, hi) in enumerate(segs):
        o = jnp.dot(hb, w_ref[:, lo:hi], preferred_element_type=F32)
        post = (lambda a: (a * QUERY_SCALE).astype(BF16)) if idx == 0 else (lambda a: a)
        if use_rope and idx in rope_segs:
            parts = _rope(o, cos_ref[...], sin_ref[...])
            for g, part in enumerate(parts):
                outs[idx][:, g * LANES:(g + 1) * LANES] = post(part)
        else:
            outs[idx][...] = post(o)
        if idx in t_segs:
            outs[len(segs) + t_segs.index(idx)][...] = o.T


def _proj_in(x2d, shift, scale, gain, w_bf16, segs, rope_segs, t, rope_tabs, t_segs=()):
    n = x2d.shape[0]
    tile = t if t_segs else OUT_TILE
    assert shift.shape[0] == 1 or t % tile == 0
    nt = n // tile
    tiles_per_seq = max(t // tile, 1)
    use_rope = rope_tabs is not None
    row = lambda i: (i, 0)
    per_b = _mod_index(shift, tiles_per_seq)
    in_specs = [pl.BlockSpec((tile, D_MODEL), row),
                pl.BlockSpec((None, 1, D_MODEL), per_b),
                pl.BlockSpec((None, 1, D_MODEL), per_b),
                pl.BlockSpec((1, D_MODEL), lambda i: (0, 0)),
                pl.BlockSpec(w_bf16.shape, lambda i: (0, 0))]
    args = [x2d, shift, scale, gain, w_bf16]
    if use_rope:
        pos = lambda i: (i % tiles_per_seq, 0)
        in_specs += [pl.BlockSpec((tile, LANES), pos), pl.BlockSpec((tile, LANES), pos)]
        args += list(rope_tabs)
    widths = [hi - lo for lo, hi in segs]
    return pl.pallas_call(
        functools.partial(_proj_in_kernel, segs=segs, rope_segs=rope_segs, use_rope=use_rope,
                          t_segs=tuple(t_segs)),
        grid=(nt,),
        in_specs=in_specs,
        out_specs=([pl.BlockSpec((tile, w), row) for w in widths]
                   + [pl.BlockSpec((None, widths[i], tile), lambda i_: (i_, 0, 0)) for i in t_segs]),
        out_shape=([jax.ShapeDtypeStruct((n, w), BF16 if i == 0 else F32) for i, w in enumerate(widths)]
                   + [jax.ShapeDtypeStruct((nt, widths[i], tile), F32) for i in t_segs]),
        compiler_params=_params(1),
        name="proj_in",
    )(*args)


def _sink_exp_pair(s2, sink_a, sink_b, n):
    rowi = lax.broadcasted_iota(jnp.int32, (2 * n, 1), 0)
    sink2 = jnp.where(rowi < n, sink_a, sink_b) * LOG2E
    m = jnp.maximum(jnp.max(s2, -1, keepdims=True), sink2)
    e = jnp.exp2(s2 - m)
    return e, 1.0 / (jnp.sum(e, -1, keepdims=True) + jnp.exp2(sink2 - m))


def _gqa_ctx_kernel(sink_ref, q_ref, k_ref, v_ref, o_ref):
    t = q_ref.shape[0]
    k = k_ref[...]
    v = v_ref[...]
    kd = [_dup_half(k, g).astype(BF16) for g in range(A_KV_HEADS)]
    vd = [_dup_half(v, g).astype(BF16) for g in range(A_KV_HEADS)]
    s2 = [_bdot(_stack_pair(q_ref[:, p * LANES:(p + 1) * LANES]),
                kd[p // 2], NT) for p in range(A_HEADS // 2)]
    for p in range(A_HEADS // 2):
        e, inv = _sink_exp_pair(s2[p], sink_ref[2 * p], sink_ref[2 * p + 1], t)
        o = _bdot(e, vd[p // 2]) * inv
        o_ref[:, p * LANES:(p + 1) * LANES] = jnp.where(_lane_lo((t, LANES)), o[:t], o[t:])


def _gqa_context(sink, q, k, v, bn, t):
    seq = lambda b: (b, 0, 0)
    return pl.pallas_call(
        _gqa_ctx_kernel,
        grid=(bn,),
        in_specs=[pl.BlockSpec(memory_space=pltpu.SMEM),
                  pl.BlockSpec((None, t, A_WIDTH), seq),
                  pl.BlockSpec((None, t, A_KV_WIDTH), seq),
                  pl.BlockSpec((None, t, A_KV_WIDTH), seq)],
        out_specs=pl.BlockSpec((None, t, A_WIDTH), seq),
        out_shape=jax.ShapeDtypeStruct((bn, t, A_WIDTH), F32),
        compiler_params=_params(1),
        name="gqa_context",
    )(sink, q.reshape(bn, t, A_WIDTH), k.reshape(bn, t, A_KV_WIDTH), v.reshape(bn, t, A_KV_WIDTH))


def _gqa_win_kernel(sink_ref, q_ref, kp_ref, kc_ref, kn_ref, vp_ref, vc_ref, vn_ref,
                    kx_ref, vx_ref, o_ref, *, t):
    nblk = pl.program_id(1)
    nctx = kx_ref.shape[0]
    keys = jnp.concatenate([kx_ref[...], kp_ref[...], kc_ref[...], kn_ref[...]], axis=0)
    vals = jnp.concatenate([vx_ref[...], vp_ref[...], vc_ref[...], vn_ref[...]], axis=0)
    nkeys = nctx + 3 * BLOCK
    rowi = lax.broadcasted_iota(jnp.int32, (2 * BLOCK, nkeys), 0)
    coli = lax.broadcasted_iota(jnp.int32, (2 * BLOCK, nkeys), 1)
    qpos = nblk * BLOCK + rowi % BLOCK
    kpos = nblk * BLOCK + coli - nctx - BLOCK
    valid = (coli < nctx) | ((jnp.abs(qpos - kpos) <= WINDOW) & (kpos >= 0) & (kpos < t))
    kd = [_dup_half(keys, g).astype(BF16) for g in range(A_KV_HEADS)]
    vd = [_dup_half(vals, g).astype(BF16) for g in range(A_KV_HEADS)]
    s2 = [_bdot(_stack_pair(q_ref[:, p * LANES:(p + 1) * LANES]),
                kd[p // 2], NT) for p in range(A_HEADS // 2)]
    for p in range(A_HEADS // 2):
        e, inv = _sink_exp_pair(jnp.where(valid, s2[p], NEG_INF), sink_ref[2 * p], sink_ref[2 * p + 1], BLOCK)
        o = _bdot(e, vd[p // 2]) * inv
        o_ref[:, p * LANES:(p + 1) * LANES] = jnp.where(_lane_lo((BLOCK, LANES)), o[:BLOCK], o[BLOCK:])


def _gqa_window(sink, q, k, v, k_ctx, v_ctx, bn, t):
    nb = t // BLOCK
    nctx = k_ctx.shape[1]
    cur = lambda b, n: (b, n, 0)
    prev = lambda b, n: (b, jnp.maximum(n - 1, 0), 0)
    nxt = lambda b, n: (b, jnp.minimum(n + 1, nb - 1), 0)
    ctx = lambda b, n: (b, 0, 0)
    kv = lambda f: pl.BlockSpec((None, BLOCK, A_KV_WIDTH), f)
    k3 = k.reshape(bn, t, A_KV_WIDTH)
    v3 = v.reshape(bn, t, A_KV_WIDTH)
    return pl.pallas_call(
        functools.partial(_gqa_win_kernel, t=t),
        grid=(bn, nb),
        in_specs=[pl.BlockSpec(memory_space=pltpu.SMEM),
                  pl.BlockSpec((None, BLOCK, A_WIDTH), cur),
                  kv(prev), kv(cur), kv(nxt), kv(prev), kv(cur), kv(nxt),
                  pl.BlockSpec((None, nctx, A_KV_WIDTH), ctx),
                  pl.BlockSpec((None, nctx, A_KV_WIDTH), ctx)],
        out_specs=pl.BlockSpec((None, BLOCK, A_WIDTH), cur),
        out_shape=jax.ShapeDtypeStruct((bn, t, A_WIDTH), F32),
        compiler_params=_params(2),
        name="gqa_window",
    )(sink, q.reshape(bn, t, A_WIDTH), k3, k3, k3, v3, v3, v3, k_ctx, v_ctx)


def _even_in_kernel(*refs, use_rope, emit_t, tiles_per_seq):
    x_ref, xp_ref, xn_ref, sh_ref, sc_ref, g_ref, w_ref = refs[:7]
    refs = refs[7:]
    if use_rope:
        cos_ref, sin_ref = refs[:2]
        refs = refs[2:]
    mu_ref, lora_ref, w0_ref, a0_ref, kk_ref, ka_ref, rk_ref, ones_ref = refs[:8]
    (q_out, k_out, v_out, ga_out, gb_out, r_out, kkn_out, vv_out, lw_out, b_out, ke_out,
     bonus_out) = refs[8:20]
    i = pl.program_id(0)
    tm = x_ref.shape[0]
    pos = i % tiles_per_seq

    def norm_mod(x):
        y = x * lax.rsqrt(jnp.mean(x * x, -1, keepdims=True) + NORM_EPS)
        return (y * g_ref[...]) * (1.0 + sc_ref[...]) + sh_ref[...]

    h = norm_mod(x_ref[...])
    hb = h.astype(BF16)
    seg = dict(zip(("q", "k", "v", "ga", "pb", "gb"), EVEN_SEGS))
    proj = lambda name, lhs=hb: jnp.dot(lhs, w_ref[:, seg[name][0]:seg[name][1]], preferred_element_type=F32)
    h_ext = jnp.concatenate([norm_mod(xp_ref[...]), h, norm_mod(xn_ref[...])], 0).astype(BF16)
    p_ext = proj("pb", h_ext)
    q = proj("q")
    k_att = proj("k")
    if use_rope:
        q = jnp.concatenate(_rope(q, cos_ref[...], sin_ref[...]), 1)
        k_att = _rope(k_att, cos_ref[...], sin_ref[...])[0]
    q_out[...] = (q * QUERY_SCALE).astype(BF16)
    k_out[...] = k_att
    v_att = proj("v")
    v_out[...] = v_att
    ga_out[...] = proj("ga")
    gb_out[...] = proj("gb")
    if emit_t:
        refs[20][...] = k_att.T
        refs[21][...] = v_att.T

    p = p_ext[8:8 + tm]
    p_ext = jnp.concatenate([jnp.where(pos != 0, p_ext[:8], 0.0), p,
                             jnp.where(pos != tiles_per_seq - 1, p_ext[8 + tm:], 0.0)], 0)
    prev = pltpu.roll(p_ext, 1, 0)[8:8 + tm]
    nxt = pltpu.roll(p_ext, tm + 15, 0)[8:8 + tm]
    xf = p + (0.5 * (prev + nxt) - p) * mu_ref[...]
    r = xf[:, 0:B_WIDTH]
    k = xf[:, B_WIDTH:2 * B_WIDTH]
    v = xf[:, 2 * B_WIDTH:3 * B_WIDTH]
    g = xf[:, 3 * B_WIDTH:]
    g = jnp.where(_lane_lo(g.shape), jnp.tanh(g), g)
    lo = _bdot(g, lora_ref[...])
    ones = ones_ref[...]
    kkr = k * kk_ref[...]
    kkn = kkr / jnp.maximum(jnp.sqrt(_head_sum(kkr * kkr, ones)), 1e-12)
    ke_sum = jnp.zeros_like(k)
    for z in range(2):
        wz = w0_ref[z] + lo[:, z * B_WIDTH:(z + 1) * B_WIDTH]
        lw_out[z] = -math.exp(-0.5) * _sigmoid(wz)
        az = _sigmoid(a0_ref[z] + lo[:, (2 + z) * B_WIDTH:(3 + z) * B_WIDTH])
        ke = k * (1.0 + (az - 1.0) * ka_ref[...])
        ke_out[z] = ke
        b_out[z] = kkn * az
        ke_sum = ke_sum + ke
    r_out[...] = r
    kkn_out[...] = kkn
    vv_out[...] = v
    bonus_out[...] = _head_sum(r * (0.5 * ke_sum) * rk_ref[...], ones) * v


def _even_in(x2d, shift, scale, gain, w_bf16, t, rope_tabs, emit_t, mu, lora, w0, a0, k_k, k_a, r_k, ones):
    n = x2d.shape[0]
    tile = ROW_TILE
    assert t % tile == 0 and (not emit_t or t == tile)
    nt = n // tile
    tiles_per_seq = t // tile
    h8 = tile // 8
    use_rope = rope_tabs is not None
    row = lambda i: (i, 0)
    const2 = lambda i: (0, 0)
    const3 = lambda i: (0, 0, 0)
    per_b = _mod_index(shift, tiles_per_seq)
    vec = pl.BlockSpec((1, B_WIDTH), const2)
    in_specs = [pl.BlockSpec((tile, D_MODEL), row),
                pl.BlockSpec((8, D_MODEL), lambda i: (jnp.maximum(i * h8 - 1, 0), 0)),
                pl.BlockSpec((8, D_MODEL), lambda i: (jnp.minimum((i + 1) * h8, n // 8 - 1), 0)),
                pl.BlockSpec((None, 1, D_MODEL), per_b),
                pl.BlockSpec((None, 1, D_MODEL), per_b),
                pl.BlockSpec((1, D_MODEL), const2),
                pl.BlockSpec(w_bf16.shape, const2)]
    args = [x2d, x2d, x2d, shift, scale, gain, w_bf16]
    if use_rope:
        pos = lambda i: (i % tiles_per_seq, 0)
        in_specs += [pl.BlockSpec((tile, LANES), pos), pl.BlockSpec((tile, LANES), pos)]
        args += list(rope_tabs)
    in_specs += [pl.BlockSpec((1, B_SHIFT_WIDTH), const2),
                 pl.BlockSpec(lora.shape, const2),
                 pl.BlockSpec((2, 1, B_WIDTH), const3),
                 pl.BlockSpec((2, 1, B_WIDTH), const3),
                 vec, vec, vec,
                 pl.BlockSpec((B_WIDTH, B_WIDTH), const2)]
    args += [mu, lora, w0, a0, k_k, k_a, r_k, ones]
    wide = pl.BlockSpec((tile, B_WIDTH), row)
    narrow = pl.BlockSpec((tile, A_KV_WIDTH), row)
    dual = pl.BlockSpec((2, tile, B_WIDTH), lambda i: (0, i, 0))
    f_wide = jax.ShapeDtypeStruct((n, B_WIDTH), F32)
    f_narrow = jax.ShapeDtypeStruct((n, A_KV_WIDTH), F32)
    f_dual = jax.ShapeDtypeStruct((2, n, B_WIDTH), F32)
    out_specs = [wide, narrow, narrow, wide, wide, wide, wide, wide, dual, dual, dual, wide]
    out_shape = [jax.ShapeDtypeStruct((n, A_WIDTH), BF16), f_narrow, f_narrow, f_wide, f_wide,
                 f_wide, f_wide, f_wide, f_dual, f_dual, f_dual, f_wide]
    if emit_t:
        out_specs += [pl.BlockSpec((None, A_KV_WIDTH, tile), lambda i: (i, 0, 0))] * 2
        out_shape += [jax.ShapeDtypeStruct((nt, A_KV_WIDTH, tile), F32)] * 2
    return pl.pallas_call(
        functools.partial(_even_in_kernel, use_rope=use_rope, emit_t=emit_t, tiles_per_seq=tiles_per_seq),
        grid=(nt,),
        in_specs=in_specs,
        out_specs=out_specs,
        out_shape=out_shape,
        compiler_params=_params(1),
        name="even_in",
    )(*args)


def _rwkv_scan_kernel(*refs, n_chunks, from_zero):
    (rf_ref, kf_ref, vf_ref, rb_ref, kb_ref, vb_ref, lwf_ref, bf_ref, kef_ref,
     lwb_ref, bb_ref, keb_ref) = refs[:12]
    if from_zero:
        yf_ref, yb_ref, sout_ref, s_scr = refs[12:]
    else:
        s0_ref, yf_ref, yb_ref, s_scr = refs[12:]
    c = pl.program_id(1)

    @pl.when(c == 0)
    def _():
        if from_zero:
            s_scr[...] = jnp.zeros_like(s_scr)
        else:
            s_scr[...] = s0_ref[...]

    row = lax.broadcasted_iota(jnp.int32, (CHUNK, LANES), 0)
    li = lax.broadcasted_iota(jnp.int32, (CHUNK, LANES), 1) % CHUNK
    blk16 = (row // 16) == (li // 16)
    blk32 = (row // 32) == (li // 32)
    eye = jnp.where(row == li, 1.0, 0.0)
    srow = lax.broadcasted_iota(jnp.int32, (PAIR, PAIR), 0)
    scol = lax.broadcasted_iota(jnp.int32, (PAIR, PAIR), 1)
    same_head = (srow // CHUNK) == (scol // CHUNK)
    tr = lax.broadcasted_iota(jnp.int32, (CHUNK, CHUNK), 0)
    tc = lax.broadcasted_iota(jnp.int32, (CHUNK, CHUNK), 1)
    dirs = ((rf_ref, kf_ref, vf_ref, lwf_ref, bf_ref, kef_ref, yf_ref),
            (rb_ref, kb_ref, vb_ref, lwb_ref, bb_ref, keb_ref, yb_ref))

    def advance(s):
        chains = [(z, p) for z in range(2) for p in range(B_HEADS // 2)]
        sls = [slice(p * LANES, (p + 1) * LANES) for _, p in chains]
        before, before_eq, tot, kkd, rd, bg, kg, vv, sc = [], [], [], [], [], [], [], [], []
        for z in range(2):
            r_ref, kk_ref, v_ref, lw_ref, b_ref, ke_ref, _ = dirs[z]
            diff = (li - row) if z else (row - li)
            tri = jnp.where(((tc - tr) if z else (tr - tc)) >= 0, 1.0, 0.0).astype(BF16)
            lw_all = lw_ref[s]
            cum_all = _split_dot(tri, lw_all)
            for p in range(B_HEADS // 2):
                sl = slice(p * LANES, (p + 1) * LANES)
                lw = lw_all[:, sl]
                cum = cum_all[:, sl]
                tt = cum[0:1] if z else cum[CHUNK - 1:CHUNK]
                b = b_ref[s, :, sl]
                ke = ke_ref[s, :, sl]
                g_inv = jnp.exp(-cum)
                g_rest = jnp.exp(tt - cum)
                before.append(diff > 0)
                before_eq.append(diff >= 0)
                tot.append(tt)
                kkd.append(kk_ref[s, :, sl] * jnp.exp(cum - lw))
                rd.append(r_ref[s, :, sl] * jnp.exp(cum))
                bg.append(b * g_rest)
                kg.append(ke * g_rest)
                vv.append(v_ref[s, :, sl])
                sc.append(_bdot(jnp.concatenate([kkd[-1], rd[-1]], 0),
                                jnp.concatenate([_stack_pair(b * g_inv), _stack_pair(ke * g_inv)], 0), NT))
            yield
        lb = [jnp.where(m, x[:CHUNK, :LANES], 0.0) for m, x in zip(before, sc)]
        mb = [jnp.where(m, x[CHUNK:, :LANES], 0.0) for m, x in zip(before_eq, sc)]
        lv = [_bdot(jnp.concatenate([jnp.where(m, x[:CHUNK, LANES:], 0.0),
                                     jnp.where(me, x[CHUNK:, LANES:], 0.0)], 0), _stack_pair(v))
              for m, me, x, v in zip(before, before_eq, sc, vv)]
        yield

        pw = [jnp.where(blk16, x, 0.0) for x in lb]
        tinv = [eye - x for x in pw]
        for _ in range(3):
            pw = [_bdot(x, _stack_pair(x)) for x in pw]
            yield
            tinv = [t + _bdot(t, _stack_pair(x)) for t, x in zip(tinv, pw)]
            yield
        for off in (blk32 & ~blk16, ~blk32):
            tmp = [_bdot(t, _stack_pair(jnp.where(off, x, 0.0))) for t, x in zip(tinv, lb)]
            yield
            tinv = [t - _bdot(m, _stack_pair(t)) for t, m in zip(tinv, tmp)]
            yield

        w12 = [_bdot(t, jnp.concatenate([_stack_pair(k), _stack_pair(x[:CHUNK])], 1))
               for t, k, x in zip(tinv, kkd, lv)]
        yield
        s_old = [s_scr[s, z, p] for z, p in chains]
        ws = [_bdot(jnp.concatenate([w[:, :LANES], r], 0), st, NT) for w, r, st in zip(w12, rd, s_old)]
        yield
        u = [-(a[:CHUNK] + w[:, LANES:]) for a, w in zip(ws, w12)]
        mbu = [_bdot(m, _stack_pair(x)) for m, x in zip(mb, u)]
        upd = [_bdot(jnp.concatenate([x, v], 0).T, jnp.concatenate([b_, k_], 0))
               for x, v, b_, k_ in zip(u, vv, bg, kg)]
        yield
        for i, (z, p) in enumerate(chains):
            dirs[z][6][s, :, sls[i]] = ws[i][CHUNK:] + mbu[i] + lv[i][CHUNK:]
            s_scr[s, z, p] = s_old[i] * jnp.exp(tot[i]) + jnp.where(same_head, upd[i], 0.0)

    live = [advance(s) for s in range(rf_ref.shape[0])]
    while live:
        live = [gen for gen in live if next(gen, "done") != "done"]

    if from_zero:
        @pl.when(c == n_chunks - 1)
        def _():
            for s in range(s_scr.shape[0]):
                for z in range(2):
                    for p in range(B_HEADS // 2):
                        st = s_scr[s, z, p]
                        sout_ref[s, z, 2 * p] = st[:HEAD_DIM, :HEAD_DIM]
                        sout_ref[s, z, 2 * p + 1] = pltpu.roll(st[HEAD_DIM:, :], HEAD_DIM, 1)[:, :HEAD_DIM]


def _rwkv_scan(r, kkn, v, lw, b, ke, s0, bn, t):
    nc = t // CHUNK
    npair = B_HEADS // 2
    ns = min(SCAN_SEQS, bn)
    fwd = pl.BlockSpec((ns, CHUNK, B_WIDTH), lambda g, c: (g, c, 0))
    bwd = pl.BlockSpec((ns, CHUNK, B_WIDTH), lambda g, c: (g, nc - 1 - c, 0))
    dfwd = pl.BlockSpec((None, ns, CHUNK, B_WIDTH), lambda g, c: (0, g, c, 0))
    dbwd = pl.BlockSpec((None, ns, CHUNK, B_WIDTH), lambda g, c: (1, g, nc - 1 - c, 0))
    r3, k3, v3 = (a.reshape(bn, t, B_WIDTH) for a in (r, kkn, v))
    lw4, b4, ke4 = (a.reshape(2, bn, t, B_WIDTH) for a in (lw, b, ke))
    seq = jax.ShapeDtypeStruct((bn, t, B_WIDTH), F32)
    in_specs = [fwd, fwd, fwd, bwd, bwd, bwd, dfwd, dfwd, dfwd, dbwd, dbwd, dbwd]
    args = [r3, k3, v3, r3, k3, v3, lw4, b4, ke4, lw4, b4, ke4]
    out_specs = [fwd, bwd]
    out_shape = [seq, seq]
    if s0 is None:
        out_specs.append(pl.BlockSpec((ns, 2, B_HEADS, HEAD_DIM, HEAD_DIM), lambda g, c: (g, 0, 0, 0, 0)))
        out_shape.append(jax.ShapeDtypeStruct((bn, 2, B_HEADS, HEAD_DIM, HEAD_DIM), F32))
    else:
        in_specs.append(pl.BlockSpec((ns, 2, npair, PAIR, PAIR), lambda g, c: (g, 0, 0, 0, 0)))
        args.append(s0)
    return pl.pallas_call(
        functools.partial(_rwkv_scan_kernel, n_chunks=nc, from_zero=s0 is None),
        grid=(bn // ns, nc),
        in_specs=in_specs,
        out_specs=out_specs,
        out_shape=out_shape,
        scratch_shapes=[pltpu.VMEM((ns, 2, npair, PAIR, PAIR), F32)],
        compiler_params=_params(2),
        name="rwkv_scan",
    )(*args)


def _post(x, z, gate, gain):
    zn = z * lax.rsqrt(jnp.mean(z * z, -1, keepdims=True) + NORM_EPS)
    return x + gate * (zn * gain)


def _even_out_kernel(ya_ref, ga_ref, yf_ref, yb_ref, bonus_ref, gb_ref, lnw_ref, lnb_ref, ones_ref,
                     x_ref, gate_ref, gain_ref, w_ref, o_ref):
    y = yf_ref[...] + yb_ref[...]
    ones = ones_ref[...]
    mean = _head_sum(y, ones) * (1.0 / HEAD_DIM)
    d = y - mean
    var = _head_sum(d * d, ones) * (1.0 / HEAD_DIM)
    yb = (d * lax.rsqrt(var + RWKV_LN_EPS)) * lnw_ref[...] + lnb_ref[...] + bonus_ref[...]
    za = (ya_ref[...] * _silu(ga_ref[...])).astype(BF16)
    zb = (yb * _silu(gb_ref[...])).astype(BF16)
    z = (jnp.dot(za, w_ref[0:A_WIDTH, :], preferred_element_type=F32)
         + jnp.dot(zb, w_ref[A_WIDTH:, :], preferred_element_type=F32))
    o_ref[...] = _post(x_ref[...], z, gate_ref[...], gain_ref[...])


def _even_out(ya, ga, yf, yb, bonus, gb, ln_w, ln_b, ones, x2d, gate, gain, w_bf16, t):
    n = x2d.shape[0]
    row = lambda i: (i, 0)
    const2 = lambda i: (0, 0)
    half = pl.BlockSpec((OUT_TILE, B_WIDTH), row)
    full = pl.BlockSpec((OUT_TILE, D_MODEL), row)
    vec = pl.BlockSpec((1, B_WIDTH), const2)
    assert gate.shape[0] == 1 or t % OUT_TILE == 0
    return pl.pallas_call(
        _even_out_kernel,
        grid=(n // OUT_TILE,),
        in_specs=[half, half, half, half, half, half, vec, vec,
                  pl.BlockSpec((B_WIDTH, B_WIDTH), const2),
                  full,
                  pl.BlockSpec((None, 1, D_MODEL), _mod_index(gate, t // OUT_TILE)),
                  pl.BlockSpec((1, D_MODEL), const2),
                  pl.BlockSpec((D_MODEL, D_MODEL), const2)],
        out_specs=full,
        out_shape=jax.ShapeDtypeStruct((n, D_MODEL), F32),
        compiler_params=_params(1),
        name="even_out",
    )(ya, ga, yf, yb, bonus, gb, ln_w, ln_b, ones, x2d, gate, gain, w_bf16)


def _odd_out_kernel(o_ref_in, gc_ref, x_ref, gate_ref, gain_ref, w_ref, o_ref):
    zc = (o_ref_in[...] * _silu(gc_ref[...])).astype(BF16)
    z = jnp.dot(zc, w_ref[...], preferred_element_type=F32)
    o_ref[...] = _post(x_ref[...], z, gate_ref[...], gain_ref[...])


def _odd_out(o, gc, x2d, gate, gain, w_bf16, t):
    n = x2d.shape[0]
    row = lambda i: (i, 0)
    const2 = lambda i: (0, 0)
    full = pl.BlockSpec((OUT_TILE, D_MODEL), row)
    assert gate.shape[0] == 1 or t % OUT_TILE == 0
    return pl.pallas_call(
        _odd_out_kernel,
        grid=(n // OUT_TILE,),
        in_specs=[full, full, full,
                  pl.BlockSpec((None, 1, D_MODEL), _mod_index(gate, t // OUT_TILE)),
                  pl.BlockSpec((1, D_MODEL), const2),
                  pl.BlockSpec((D_MODEL, D_MODEL), const2)],
        out_specs=full,
        out_shape=jax.ShapeDtypeStruct((n, D_MODEL), F32),
        compiler_params=_params(1),
        name="odd_out",
    )(o, gc, x2d, gate, gain, w_bf16)


def _lambda(lq1_ref, lk1_ref, lq2_ref, lk2_ref, lam_init):
    s1 = jnp.sum(lq1_ref[...] * lk1_ref[...], -1, keepdims=True)
    s2 = jnp.sum(lq2_ref[...] * lk2_ref[...], -1, keepdims=True)
    return jnp.exp(s1) - jnp.exp(s2) + lam_init


def _subln(o, gain, lam_init):
    on = o * lax.rsqrt(jnp.mean(o * o, -1, keepdims=True) + SUBLN_EPS)
    return (on * gain) * (1.0 - lam_init)


def _diff_ctx_kernel(lq1_ref, lk1_ref, lq2_ref, lk2_ref, sub_ref, q_ref, k_ref, v_ref,
                     gc_ref, x_ref, gate_ref, gain_ref, w_ref, y_ref, o_scr, *, lam_init):
    t = q_ref.shape[0]
    lam = _lambda(lq1_ref, lk1_ref, lq2_ref, lk2_ref, lam_init)
    sls = [slice(h * LANES, (h + 1) * LANES) for h in range(C_HEADS)]
    s2 = [_bdot(_stack_pair(q_ref[:, sl]), k_ref[:, sl], NT)
          for sl in sls]
    for h, sl in enumerate(sls):
        e = jnp.exp2(s2[h] - jnp.max(s2[h], -1, keepdims=True))
        ov = _bdot(e, v_ref[:, sl]) * (1.0 / jnp.sum(e, -1, keepdims=True))
        o_scr[:, sl] = _subln(ov[:t] - lam * ov[t:], sub_ref[...], lam_init)
    zc = (o_scr[...] * _silu(gc_ref[...])).astype(BF16)
    z = jnp.dot(zc, w_ref[...], preferred_element_type=F32)
    y_ref[...] = _post(x_ref[...], z, gate_ref[...], gain_ref[...])


def _diff_context_out(lams, subln, q, k, v, gc, x2d, gate, gain, w_bf16, bn, t, lam_init):
    seq = pl.BlockSpec((None, t, C_WIDTH), lambda b: (b, 0, 0))
    small = pl.BlockSpec((1, C_QK_DIM), lambda b: (0, 0))
    const2 = lambda b: (0, 0)
    to3 = lambda a: a.reshape(bn, t, a.shape[-1])
    return pl.pallas_call(
        functools.partial(_diff_ctx_kernel, lam_init=lam_init),
        grid=(bn,),
        in_specs=[small, small, small, small, pl.BlockSpec((1, C_V_DIM), const2),
                  seq, seq, seq, seq, seq,
                  pl.BlockSpec((None, 1, D_MODEL), lambda b: (0, 0, 0)),
                  pl.BlockSpec((1, D_MODEL), const2),
                  pl.BlockSpec((D_MODEL, D_MODEL), const2)],
        out_specs=seq,
        out_shape=jax.ShapeDtypeStruct((bn, t, D_MODEL), F32),
        scratch_shapes=[pltpu.VMEM((t, C_WIDTH), F32)],
        compiler_params=_params(1),
        name="diff_context",
    )(*lams, subln, to3(q), to3(k), to3(v), to3(gc), to3(x2d), gate, gain, w_bf16).reshape(bn * t, D_MODEL)


def _diff_lat_kernel(lq1_ref, lk1_ref, lq2_ref, lk2_ref, sub_ref, q_ref, k_ref, v_ref, kx_ref, vx_ref,
                     o_ref, *, lam_init):
    lam = _lambda(lq1_ref, lk1_ref, lq2_ref, lk2_ref, lam_init)
    kx = kx_ref[...].astype(BF16)
    k = k_ref[...].astype(BF16)
    tq = DIFF_Q_SUB
    subs = range(q_ref.shape[0] // tq)
    scores = []
    for i in subs:
        qs = _stack_pair(q_ref[i * tq:(i + 1) * tq, :])
        scores.append((_bdot(qs, kx, NT), _bdot(qs, k, NT)))
    for i in subs:
        sx, sl = scores[i]
        m = jnp.maximum(jnp.max(sx, -1, keepdims=True), jnp.max(sl, -1, keepdims=True))
        ex = jnp.exp2(sx - m)
        el = jnp.exp2(sl - m)
        den = jnp.sum(ex, -1, keepdims=True) + jnp.sum(el, -1, keepdims=True)
        ratio = lam * den[:tq] / den[tq:]
        o = (_bdot(ex[:tq] - ex[tq:] * ratio, vx_ref[...])
             + _bdot(el[:tq] - el[tq:] * ratio, v_ref[...])) * (1.0 / den[:tq])
        o_ref[i * tq:(i + 1) * tq, :] = _subln(o, sub_ref[...], lam_init)


def _diff_latent(lams, subln, q, k, v, k_ctx, v_ctx, bn, t, lam_init):
    nb = t // DIFF_Q_TILE
    nctx = k_ctx.shape[1]
    small = pl.BlockSpec((1, C_QK_DIM), lambda b, h, n: (0, 0))
    qspec = pl.BlockSpec((None, DIFF_Q_TILE, LANES), lambda b, h, n: (b, n, h))
    kvspec = pl.BlockSpec((None, t, LANES), lambda b, h, n: (b, 0, h))
    cxspec = pl.BlockSpec((None, nctx, LANES), lambda b, h, n: (b, 0, h))
    return pl.pallas_call(
        functools.partial(_diff_lat_kernel, lam_init=lam_init),
        grid=(bn, C_HEADS, nb),
        in_specs=[small, small, small, small,
                  pl.BlockSpec((1, C_V_DIM), lambda b, h, n: (0, 0)),
                  qspec, kvspec, kvspec, cxspec, cxspec],
        out_specs=qspec,
        out_shape=jax.ShapeDtypeStruct((bn, t, C_WIDTH), F32),
        compiler_params=_params(3),
        name="diff_latent",
    )(*lams, subln, q.reshape(bn, t, C_WIDTH), k.reshape(bn, t, C_WIDTH), v.reshape(bn, t, C_WIDTH),
      k_ctx, v_ctx)


def _rope_tables(t):
    nf = HEAD_DIM // 4
    inv = 1.0 / (ROPE_BASE ** (jnp.arange(nf, dtype=F32) / nf))
    pos = jnp.arange(t)
    ang_r = (pos // GRID_W).astype(F32)[:, None] * inv[None]
    ang_c = (pos % GRID_W).astype(F32)[:, None] * inv[None]
    ang = jnp.concatenate([ang_r, ang_r, ang_c, ang_c], -1)
    cos, sin = jnp.cos(ang), jnp.sin(ang)
    sign = jnp.where((jnp.arange(HEAD_DIM) // nf) % 2 == 0, -1.0, 1.0).astype(F32)
    return jnp.tile(cos, (1, 2)), jnp.tile(sin * sign[None], (1, 2))


EVEN_SEGS = ((0, 512), (512, 640), (640, 768), (768, 1280), (1280, 2944), (2944, 3456))
ODD_SEGS = ((0, 1024), (1024, 2048), (2048, 3072), (3072, 4096))


def _pair_states(st):
    bn = st.shape[0]
    s = st.reshape(bn, 2, 4, 2, HEAD_DIM, HEAD_DIM)
    z = jnp.zeros_like(s[:, :, :, 0])
    top = jnp.concatenate([s[:, :, :, 0], z], -1)
    bot = jnp.concatenate([z, s[:, :, :, 1]], -1)
    return jnp.concatenate([top, bot], -2)


def kernel(x_prompt, x_sample, cache_a_k, cache_a_v, state_rwkv, cache_c_k, cache_c_v, c, c_ctx,
           ada_w, ada_b, norm_pre, norm_post, w_out, even_w_in, a_sink, b_mu, b_w0, b_w2, b_a0,
           b_a2, b_kk, b_ka, b_rk, b_ln_w, b_ln_b, odd_w_in, c_lq1, c_lk1, c_lq2, c_lk2, c_subln):
    bp, tp, _ = x_prompt.shape
    bs, ts, _ = x_sample.shape
    past = cache_a_k.shape[2]

    cvecs = jnp.concatenate([c_ctx[None], c, jnp.zeros((8 - 1 - bs, D_MODEL), F32)], 0)
    mods = _modulation(cvecs, ada_w, ada_b)
    rope_tabs = _rope_tables(ts)
    ones = (jnp.arange(B_WIDTH)[:, None] // HEAD_DIM == jnp.arange(B_WIDTH)[None] // HEAD_DIM).astype(BF16)

    xp = x_prompt.reshape(bp * tp, D_MODEL)
    xs = x_sample.reshape(bs * ts, D_MODEL)

    def mod_rows(layer, rows):
        m = mods[layer, rows[0]:rows[1]][:, None, :]
        return m[..., :D_MODEL], m[..., D_MODEL:2 * D_MODEL], m[..., 2 * D_MODEL:]

    e = 0
    w_in0 = even_w_in[e].astype(BF16)
    w_out0 = w_out[0].astype(BF16)
    gain_pre = norm_pre[0][None]
    gain_post = norm_post[0][None]
    zpad = jnp.zeros((DECAY_LORA, B_WIDTH), F32)
    lora = jnp.concatenate([
        jnp.concatenate([b_w2[e, 0], b_w2[e, 1], zpad, zpad], 1),
        jnp.concatenate([zpad, zpad, b_a2[e, 0], b_a2[e, 1]], 1)], 0).astype(BF16)
    feat_args = (b_mu[e][None], lora, b_w0[e][:, None, :], b_a0[e][:, None, :], b_kk[e][None],
                 b_ka[e][None], b_rk[e][None], ones)

    def even_layer(x2d, bn, t, rows, ctx):
        shift, scale, gate = mod_rows(0, rows)
        (qa, ka, va, ga, gb, r, kkn, v, lw, b, ke, bonus, *kv_t) = _even_in(
            x2d, shift, scale, gain_pre, w_in0, t, None if ctx is None else rope_tabs, ctx is None,
            *feat_args)
        if ctx is None:
            ya = _gqa_context(a_sink[e], qa, ka, va, bn, t)
            s0 = None
        else:
            k_ctx, v_ctx, st0 = ctx
            ya = _gqa_window(a_sink[e], qa, ka, va, k_ctx, v_ctx, bn, t)
            s0 = _pair_states(st0)
        yf, yb, *s_fin = _rwkv_scan(r, kkn, v, lw, b, ke, s0, bn, t)
        y = _even_out(ya.reshape(bn * t, A_WIDTH), ga, yf.reshape(bn * t, B_WIDTH),
                      yb.reshape(bn * t, B_WIDTH), bonus, gb,
                      b_ln_w[e][None], b_ln_b[e][None], ones, x2d, gate, gain_post, w_out0, t)
        return y, kv_t, s_fin

    yp, (ka_t, va_t), st_p = even_layer(xp, bp, tp, (0, 1), None)
    ctx_a = (cache_a_k[:, e].reshape(bs, past, A_KV_WIDTH), cache_a_v[:, e].reshape(bs, past, A_KV_WIDTH),
             state_rwkv[:, e])
    ys, _, _ = even_layer(xs, bs, ts, (1, 1 + bs), ctx_a)

    def cache_a(x_t):
        return jnp.transpose(x_t.reshape(bp, A_KV_HEADS, HEAD_DIM, tp), (0, 3, 1, 2))[:, None]

    o = 0
    lam_init = 0.8 - 0.6 * math.exp(-0.3 * 1)
    w_in1 = odd_w_in[o].astype(BF16)
    w_out1 = w_out[1].astype(BF16)
    gain_pre1 = norm_pre[1][None]
    gain_post1 = norm_post[1][None]
    lams = (c_lq1[o][None], c_lk1[o][None], c_lq2[o][None], c_lk2[o][None])
    subln = c_subln[o][None]

    def odd_layer(x2d, bn, t, rows, ctx):
        shift, scale, gate = mod_rows(1, rows)
        qc, kc, vc, gc = _proj_in(x2d, shift, scale, gain_pre1, w_in1, ODD_SEGS, (0, 1), t,
                                  None if ctx is None else rope_tabs)
        if ctx is None:
            y = _diff_context_out(lams, subln, qc, kc, vc, gc, x2d, gate, gain_post1, w_out1, bn, t, lam_init)
        else:
            k_ctx, v_ctx = ctx
            oc = _diff_latent(lams, subln, qc, kc, vc, k_ctx, v_ctx, bn, t, lam_init)
            y = _odd_out(oc.reshape(bn * t, C_WIDTH), gc, x2d, gate, gain_post1, w_out1, t)
        return y, kc, vc

    yp2, kc_p, vc_p = odd_layer(yp, bp, tp, (0, 1), None)
    ctx_c = (cache_c_k[:, o].reshape(bs, past, C_WIDTH), cache_c_v[:, o].reshape(bs, past, C_WIDTH))
    ys2, _, _ = odd_layer(ys, bs, ts, (1, 1 + bs), ctx_c)

    return (yp2.reshape(bp, tp, D_MODEL),
            ys2.reshape(bs, ts, D_MODEL),
            cache_a(ka_t),
            cache_a(va_t),
            st_p[0][:, None],
            kc_p.reshape(bp, 1, tp, C_HEADS, 2 * C_QK_DIM),
            vc_p.reshape(bp, 1, tp, C_HEADS, C_V_DIM))
```

```python
import functools
import math

import jax
import jax.numpy as jnp
from jax import lax
from jax.experimental import pallas as pl
from jax.experimental.pallas import tpu as pltpu

F32 = jnp.float32
BF16 = jnp.bfloat16

D_MODEL = 1024
DEPTH = 2
GRID_W = 64
HEAD_DIM = 64
ROPE_BASE = 10000.0
A_HEADS = 8
A_KV_HEADS = 2
A_WIDTH = 512
A_KV_WIDTH = 128
WINDOW = 128
BLOCK = 128
B_HEADS = 8
B_WIDTH = 512
DECAY_LORA = 64
AAA_LORA = 64
B_SHIFT_WIDTH = 3 * B_WIDTH + DECAY_LORA + AAA_LORA
C_HEADS = 8
C_QK_DIM = 64
C_V_DIM = 128
C_WIDTH = 1024
EVEN_IN = 3456
ODD_IN = 4096
NORM_EPS = 1e-6
RWKV_LN_EPS = 64e-5
SUBLN_EPS = 1e-5
NEG_INF = -1e30

LANES = 128
ROW_TILE = 256
OUT_TILE = 512
DIFF_Q_TILE = 1024
DIFF_Q_SUB = 128
CHUNK = 64
SCAN_SEQS = 4
PAIR = 2 * CHUNK
VMEM_LIMIT = 40 * 1024 * 1024

LOG2E = math.log2(math.e)
QUERY_SCALE = HEAD_DIM ** -0.5 * LOG2E

NN = ((1,), (0,))
NT = ((1,), (1,))


def _bdot(a, b, dims=NN):
    return lax.dot_general(a.astype(BF16), b.astype(BF16), (dims, ((), ())),
                           preferred_element_type=F32)


def _hdot(a, b):
    return lax.dot_general(a, b, (NN, ((), ())), precision=lax.Precision.HIGHEST,
                           preferred_element_type=F32)


def _split_dot(sel_bf16, x):
    x1 = x.astype(BF16)
    r1 = x - x1.astype(F32)
    x2 = r1.astype(BF16)
    x3 = (r1 - x2.astype(F32)).astype(BF16)
    dot = lambda p: jnp.dot(sel_bf16, p, preferred_element_type=F32)
    return dot(x1) + dot(x2) + dot(x3)


def _head_sum(x, ones_bf16):
    hi = x.astype(BF16)
    lo = (x - hi.astype(F32)).astype(BF16)
    return (jnp.dot(hi, ones_bf16, preferred_element_type=F32)
            + jnp.dot(lo, ones_bf16, preferred_element_type=F32))


def _sigmoid(x):
    return 1.0 / (1.0 + jnp.exp(-x))


def _silu(x):
    return x * _sigmoid(x)


def _params(n_axes):
    return pltpu.CompilerParams(dimension_semantics=("arbitrary",) * n_axes,
                                vmem_limit_bytes=VMEM_LIMIT)


def _lane_lo(shape):
    return (lax.broadcasted_iota(jnp.int32, shape, len(shape) - 1) % LANES) < HEAD_DIM


def _stack_pair(x):
    lo = _lane_lo(x.shape)
    z = jnp.zeros_like(x)
    return jnp.concatenate([jnp.where(lo, x, z), jnp.where(lo, z, x)], axis=0)


def _dup_half(x, g):
    lo = _lane_lo(x.shape)
    sw = pltpu.roll(x, HEAD_DIM, 1)
    return jnp.where(lo, x, sw) if g == 0 else jnp.where(lo, sw, x)


def _rope(x, cos, sin_signed):
    out = []
    even_q = (lax.broadcasted_iota(jnp.int32, cos.shape, 1) // (HEAD_DIM // 4)) % 2 == 0
    for g in range(x.shape[1] // LANES):
        xg = x[:, g * LANES:(g + 1) * LANES]
        up = pltpu.roll(xg, LANES - HEAD_DIM // 4, 1)
        dn = pltpu.roll(xg, HEAD_DIM // 4, 1)
        out.append(xg * cos + jnp.where(even_q, up, dn) * sin_signed)
    return out


def _mod_kernel(c_ref, w_ref, b_ref, o_ref):
    s = _silu(c_ref[...])
    o_ref[...] = _hdot(s, w_ref[...]) + b_ref[...]


def _modulation(cvecs, ada_w, ada_b):
    nrow = cvecs.shape[0]
    return pl.pallas_call(
        _mod_kernel,
        grid=(DEPTH, 3),
        in_specs=[pl.BlockSpec((nrow, D_MODEL), lambda l, j: (0, 0)),
                  pl.BlockSpec((None, D_MODEL, D_MODEL), lambda l, j: (l, 0, j)),
                  pl.BlockSpec((None, 1, D_MODEL), lambda l, j: (l, 0, j))],
        out_specs=pl.BlockSpec((None, nrow, D_MODEL), lambda l, j: (l, 0, j)),
        out_shape=jax.ShapeDtypeStruct((DEPTH, nrow, 3 * D_MODEL), F32),
        compiler_params=_params(2),
        name="modulation",
    )(cvecs, ada_w, ada_b.reshape(DEPTH, 1, 3 * D_MODEL))


def _mod_index(mod, tiles_per_seq):
    if mod.shape[0] == 1:
        return lambda i: (0, 0, 0)
    return lambda i: (i // tiles_per_seq, 0, 0)


def _proj_in_kernel(*refs, segs, rope_segs, use_rope, t_segs):
    if use_rope:
        x_ref, sh_ref, sc_ref, g_ref, w_ref, cos_ref, sin_ref = refs[:7]
        outs = refs[7:]
    else:
        x_ref, sh_ref, sc_ref, g_ref, w_ref = refs[:5]
        outs = refs[5:]
    x = x_ref[...]
    y = x * lax.rsqrt(jnp.mean(x * x, -1, keepdims=True) + NORM_EPS)
    h = (y * g_ref[...]) * (1.0 + sc_ref[...]) + sh_ref[...]
    hb = h.astype(BF16)
    for idx, (lo, hi) in enumerate(segs):
        o = jnp.dot(hb, w_ref[:, lo:hi], preferred_element_type=F32)
        post = (lambda a: (a * QUERY_SCALE).astype(BF16)) if idx == 0 else (lambda a: a)
        if use_rope and idx in rope_segs:
            parts = _rope(o, cos_ref[...], sin_ref[...])
            for g, part in enumerate(parts):
                outs[idx][:, g * LANES:(g + 1) * LANES] = post(part)
        else:
            outs[idx][...] = post(o)
        if idx in t_segs:
            outs[len(segs) + t_segs.index(idx)][...] = o.T


def _proj_in(x2d, shift, scale, gain, w_bf16, segs, rope_segs, t, rope_tabs, t_segs=()):
    n = x2d.shape[0]
    tile = t if t_segs else OUT_TILE
    assert shift.shape[0] == 1 or t % tile == 0
    nt = n // tile
    tiles_per_seq = max(t // tile, 1)
    use_rope = rope_tabs is not None
    row = lambda i: (i, 0)
    per_b = _mod_index(shift, tiles_per_seq)
    in_specs = [pl.BlockSpec((tile, D_MODEL), row),
                pl.BlockSpec((None, 1, D_MODEL), per_b),
                pl.BlockSpec((None, 1, D_MODEL), per_b),
                pl.BlockSpec((1, D_MODEL), lambda i: (0, 0)),
                pl.BlockSpec(w_bf16.shape, lambda i: (0, 0))]
    args = [x2d, shift, scale, gain, w_bf16]
    if use_rope:
        pos = lambda i: (i % tiles_per_seq, 0)
        in_specs += [pl.BlockSpec((tile, LANES), pos), pl.BlockSpec((tile, LANES), pos)]
        args += list(rope_tabs)
    widths = [hi - lo for lo, hi in segs]
    return pl.pallas_call(
        functools.partial(_proj_in_kernel, segs=segs, rope_segs=rope_segs, use_rope=use_rope,
                          t_segs=tuple(t_segs)),
        grid=(nt,),
        in_specs=in_specs,
        out_specs=([pl.BlockSpec((tile, w), row) for w in widths]
                   + [pl.BlockSpec((None, widths[i], tile), lambda i_: (i_, 0, 0)) for i in t_segs]),
        out_shape=([jax.ShapeDtypeStruct((n, w), BF16 if i == 0 else F32) for i, w in enumerate(widths)]
                   + [jax.ShapeDtypeStruct((nt, widths[i], tile), F32) for i in t_segs]),
        compiler_params=_params(1),
        name="proj_in",
    )(*args)


def _sink_exp_pair(s2, sink_a, sink_b, n):
    rowi = lax.broadcasted_iota(jnp.int32, (2 * n, 1), 0)
    sink2 = jnp.where(rowi < n, sink_a, sink_b) * LOG2E
    m = jnp.maximum(jnp.max(s2, -1, keepdims=True), sink2)
    e = jnp.exp2(s2 - m)
    return e, 1.0 / (jnp.sum(e, -1, keepdims=True) + jnp.exp2(sink2 - m))


def _gqa_ctx_kernel(sink_ref, q_ref, k_ref, v_ref, o_ref):
    t = q_ref.shape[0]
    k = k_ref[...]
    v = v_ref[...]
    kd = [_dup_half(k, g).astype(BF16) for g in range(A_KV_HEADS)]
    vd = [_dup_half(v, g).astype(BF16) for g in range(A_KV_HEADS)]
    s2 = [_bdot(_stack_pair(q_ref[:, p * LANES:(p + 1) * LANES]),
                kd[p // 2], NT) for p in range(A_HEADS // 2)]
    for p in range(A_HEADS // 2):
        e, inv = _sink_exp_pair(s2[p], sink_ref[2 * p], sink_ref[2 * p + 1], t)
        o = _bdot(e, vd[p // 2]) * inv
        o_ref[:, p * LANES:(p + 1) * LANES] = jnp.where(_lane_lo((t, LANES)), o[:t], o[t:])


def _gqa_context(sink, q, k, v, bn, t):
    seq = lambda b: (b, 0, 0)
    return pl.pallas_call(
        _gqa_ctx_kernel,
        grid=(bn,),
        in_specs=[pl.BlockSpec(memory_space=pltpu.SMEM),
                  pl.BlockSpec((None, t, A_WIDTH), seq),
                  pl.BlockSpec((None, t, A_KV_WIDTH), seq),
                  pl.BlockSpec((None, t, A_KV_WIDTH), seq)],
        out_specs=pl.BlockSpec((None, t, A_WIDTH), seq),
        out_shape=jax.ShapeDtypeStruct((bn, t, A_WIDTH), F32),
        compiler_params=_params(1),
        name="gqa_context",
    )(sink, q.reshape(bn, t, A_WIDTH), k.reshape(bn, t, A_KV_WIDTH), v.reshape(bn, t, A_KV_WIDTH))


def _gqa_win_kernel(sink_ref, q_ref, kp_ref, kc_ref, kn_ref, vp_ref, vc_ref, vn_ref,
                    kx_ref, vx_ref, o_ref, *, t):
    nblk = pl.program_id(1)
    nctx = kx_ref.shape[0]
    keys = jnp.concatenate([kx_ref[...], kp_ref[...], kc_ref[...], kn_ref[...]], axis=0)
    vals = jnp.concatenate([vx_ref[...], vp_ref[...], vc_ref[...], vn_ref[...]], axis=0)
    nkeys = nctx + 3 * BLOCK
    rowi = lax.broadcasted_iota(jnp.int32, (2 * BLOCK, nkeys), 0)
    coli = lax.broadcasted_iota(jnp.int32, (2 * BLOCK, nkeys), 1)
    qpos = nblk * BLOCK + rowi % BLOCK
    kpos = nblk * BLOCK + coli - nctx - BLOCK
    valid = (coli < nctx) | ((jnp.abs(qpos - kpos) <= WINDOW) & (kpos >= 0) & (kpos < t))
    kd = [_dup_half(keys, g).astype(BF16) for g in range(A_KV_HEADS)]
    vd = [_dup_half(vals, g).astype(BF16) for g in range(A_KV_HEADS)]
    s2 = [_bdot(_stack_pair(q_ref[:, p * LANES:(p + 1) * LANES]),
                kd[p // 2], NT) for p in range(A_HEADS // 2)]
    for p in range(A_HEADS // 2):
        e, inv = _sink_exp_pair(jnp.where(valid, s2[p], NEG_INF), sink_ref[2 * p], sink_ref[2 * p + 1], BLOCK)
        o = _bdot(e, vd[p // 2]) * inv
        o_ref[:, p * LANES:(p + 1) * LANES] = jnp.where(_lane_lo((BLOCK, LANES)), o[:BLOCK], o[BLOCK:])


def _gqa_window(sink, q, k, v, k_ctx, v_ctx, bn, t):
    nb = t // BLOCK
    nctx = k_ctx.shape[1]
    cur = lambda b, n: (b, n, 0)
    prev = lambda b, n: (b, jnp.maximum(n - 1, 0), 0)
    nxt = lambda b, n: (b, jnp.minimum(n + 1, nb - 1), 0)
    ctx = lambda b, n: (b, 0, 0)
    kv = lambda f: pl.BlockSpec((None, BLOCK, A_KV_WIDTH), f)
    k3 = k.reshape(bn, t, A_KV_WIDTH)
    v3 = v.reshape(bn, t, A_KV_WIDTH)
    return pl.pallas_call(
        functools.partial(_gqa_win_kernel, t=t),
        grid=(bn, nb),
        in_specs=[pl.BlockSpec(memory_space=pltpu.SMEM),
                  pl.BlockSpec((None, BLOCK, A_WIDTH), cur),
                  kv(prev), kv(cur), kv(nxt), kv(prev), kv(cur), kv(nxt),
                  pl.BlockSpec((None, nctx, A_KV_WIDTH), ctx),
                  pl.BlockSpec((None, nctx, A_KV_WIDTH), ctx)],
        out_specs=pl.BlockSpec((None, BLOCK, A_WIDTH), cur),
        out_shape=jax.ShapeDtypeStruct((bn, t, A_WIDTH), F32),
        compiler_params=_params(2),
        name="gqa_window",
    )(sink, q.reshape(bn, t, A_WIDTH), k3, k3, k3, v3, v3, v3, k_ctx, v_ctx)


def _even_in_kernel(*refs, use_rope, emit_t, tiles_per_seq):
    x_ref, xp_ref, xn_ref, sh_ref, sc_ref, g_ref, w_ref = refs[:7]
    refs = refs[7:]
    if use_rope:
        cos_ref, sin_ref = refs[:2]
        refs = refs[2:]
    mu_ref, lora_ref, w0_ref, a0_ref, kk_ref, ka_ref, rk_ref, ones_ref = refs[:8]
    (q_out, k_out, v_out, ga_out, gb_out, r_out, kkn_out, vv_out, lw_out, b_out, ke_out,
     bonus_out) = refs[8:20]
    i = pl.program_id(0)
    tm = x_ref.shape[0]
    pos = i % tiles_per_seq

    def norm_mod(x):
        y = x * lax.rsqrt(jnp.mean(x * x, -1, keepdims=True) + NORM_EPS)
        return (y * g_ref[...]) * (1.0 + sc_ref[...]) + sh_ref[...]

    h = norm_mod(x_ref[...])
    hb = h.astype(BF16)
    seg = dict(zip(("q", "k", "v", "ga", "pb", "gb"), EVEN_SEGS))
    proj = lambda name, lhs=hb: jnp.dot(lhs, w_ref[:, seg[name][0]:seg[name][1]], preferred_element_type=F32)
    h_ext = jnp.concatenate([norm_mod(xp_ref[...]), h, norm_mod(xn_ref[...])], 0).astype(BF16)
    p_ext = proj("pb", h_ext)
    q = proj("q")
    k_att = proj("k")
    if use_rope:
        q = jnp.concatenate(_rope(q, cos_ref[...], sin_ref[...]), 1)
        k_att = _rope(k_att, cos_ref[...], sin_ref[...])[0]
    q_out[...] = (q * QUERY_SCALE).astype(BF16)
    k_out[...] = k_att
    v_att = proj("v")
    v_out[...] = v_att
    ga_out[...] = proj("ga")
    gb_out[...] = proj("gb")
    if emit_t:
        refs[20][...] = k_att.T
        refs[21][...] = v_att.T

    p = p_ext[8:8 + tm]
    p_ext = jnp.concatenate([jnp.where(pos != 0, p_ext[:8], 0.0), p,
                             jnp.where(pos != tiles_per_seq - 1, p_ext[8 + tm:], 0.0)], 0)
    prev = pltpu.roll(p_ext, 1, 0)[8:8 + tm]
    nxt = pltpu.roll(p_ext, tm + 15, 0)[8:8 + tm]
    xf = p + (0.5 * (prev + nxt) - p) * mu_ref[...]
    r = xf[:, 0:B_WIDTH]
    k = xf[:, B_WIDTH:2 * B_WIDTH]
    v = xf[:, 2 * B_WIDTH:3 * B_WIDTH]
    g = xf[:, 3 * B_WIDTH:]
    g = jnp.where(_lane_lo(g.shape), jnp.tanh(g), g)
    lo = _bdot(g, lora_ref[...])
    ones = ones_ref[...]
    kkr = k * kk_ref[...]
    kkn = kkr / jnp.maximum(jnp.sqrt(_head_sum(kkr * kkr, ones)), 1e-12)
    ke_sum = jnp.zeros_like(k)
    for z in range(2):
        wz = w0_ref[z] + lo[:, z * B_WIDTH:(z + 1) * B_WIDTH]
        lw_out[z] = -math.exp(-0.5) * _sigmoid(wz)
        az = _sigmoid(a0_ref[z] + lo[:, (2 + z) * B_WIDTH:(3 + z) * B_WIDTH])
        ke = k * (1.0 + (az - 1.0) * ka_ref[...])
        ke_out[z] = ke
        b_out[z] = kkn * az
        ke_sum = ke_sum + ke
    r_out[...] = r
    kkn_out[...] = kkn
    vv_out[...] = v
    bonus_out[...] = _head_sum(r * (0.5 * ke_sum) * rk_ref[...], ones) * v


def _even_in(x2d, shift, scale, gain, w_bf16, t, rope_tabs, emit_t, mu, lora, w0, a0, k_k, k_a, r_k, ones):
    n = x2d.shape[0]
    tile = ROW_TILE
    assert t % tile == 0 and (not emit_t or t == tile)
    nt = n // tile
    tiles_per_seq = t // tile
    h8 = tile // 8
    use_rope = rope_tabs is not None
    row = lambda i: (i, 0)
    const2 = lambda i: (0, 0)
    const3 = lambda i: (0, 0, 0)
    per_b = _mod_index(shift, tiles_per_seq)
    vec = pl.BlockSpec((1, B_WIDTH), const2)
    in_specs = [pl.BlockSpec((tile, D_MODEL), row),
                pl.BlockSpec((8, D_MODEL), lambda i: (jnp.maximum(i * h8 - 1, 0), 0)),
                pl.BlockSpec((8, D_MODEL), lambda i: (jnp.minimum((i + 1) * h8, n // 8 - 1), 0)),
                pl.BlockSpec((None, 1, D_MODEL), per_b),
                pl.BlockSpec((None, 1, D_MODEL), per_b),
                pl.BlockSpec((1, D_MODEL), const2),
                pl.BlockSpec(w_bf16.shape, const2)]
    args = [x2d, x2d, x2d, shift, scale, gain, w_bf16]
    if use_rope:
        pos = lambda i: (i % tiles_per_seq, 0)
        in_specs += [pl.BlockSpec((tile, LANES), pos), pl.BlockSpec((tile, LANES), pos)]
        args += list(rope_tabs)
    in_specs += [pl.BlockSpec((1, B_SHIFT_WIDTH), const2),
                 pl.BlockSpec(lora.shape, const2),
                 pl.BlockSpec((2, 1, B_WIDTH), const3),
                 pl.BlockSpec((2, 1, B_WIDTH), const3),
                 vec, vec, vec,
                 pl.BlockSpec((B_WIDTH, B_WIDTH), const2)]
    args += [mu, lora, w0, a0, k_k, k_a, r_k, ones]
    wide = pl.BlockSpec((tile, B_WIDTH), row)
    narrow = pl.BlockSpec((tile, A_KV_WIDTH), row)
    dual = pl.BlockSpec((2, tile, B_WIDTH), lambda i: (0, i, 0))
    f_wide = jax.ShapeDtypeStruct((n, B_WIDTH), F32)
    f_narrow = jax.ShapeDtypeStruct((n, A_KV_WIDTH), F32)
    f_dual = jax.ShapeDtypeStruct((2, n, B_WIDTH), F32)
    out_specs = [wide, narrow, narrow, wide, wide, wide, wide, wide, dual, dual, dual, wide]
    out_shape = [jax.ShapeDtypeStruct((n, A_WIDTH), BF16), f_narrow, f_narrow, f_wide, f_wide,
                 f_wide, f_wide, f_wide, f_dual, f_dual, f_dual, f_wide]
    if emit_t:
        out_specs += [pl.BlockSpec((None, A_KV_WIDTH, tile), lambda i: (i, 0, 0))] * 2
        out_shape += [jax.ShapeDtypeStruct((nt, A_KV_WIDTH, tile), F32)] * 2
    return pl.pallas_call(
        functools.partial(_even_in_kernel, use_rope=use_rope, emit_t=emit_t, tiles_per_seq=tiles_per_seq),
        grid=(nt,),
        in_specs=in_specs,
        out_specs=out_specs,
        out_shape=out_shape,
        compiler_params=_params(1),
        name="even_in",
    )(*args)


def _rwkv_scan_kernel(*refs, n_chunks, from_zero):
    (rf_ref, kf_ref, vf_ref, rb_ref, kb_ref, vb_ref, lwf_ref, bf_ref, kef_ref,
     lwb_ref, bb_ref, keb_ref) = refs[:12]
    if from_zero:
        y_ref, sout_ref, s_scr = refs[12:]
    else:
        s0_ref, y_ref, s_scr = refs[12:]
    c = pl.program_id(1)

    @pl.when(c == 0)
    def _():
        y_ref[...] = jnp.zeros_like(y_ref)
        if from_zero:
            s_scr[...] = jnp.zeros_like(s_scr)
        else:
            s_scr[...] = s0_ref[...]

    row = lax.broadcasted_iota(jnp.int32, (CHUNK, LANES), 0)
    li = lax.broadcasted_iota(jnp.int32, (CHUNK, LANES), 1) % CHUNK
    blk16 = (row // 16) == (li // 16)
    blk32 = (row // 32) == (li // 32)
    eye = jnp.where(row == li, 1.0, 0.0)
    srow = lax.broadcasted_iota(jnp.int32, (PAIR, PAIR), 0)
    scol = lax.broadcasted_iota(jnp.int32, (PAIR, PAIR), 1)
    same_head = (srow // CHUNK) == (scol // CHUNK)
    tr = lax.broadcasted_iota(jnp.int32, (CHUNK, CHUNK), 0)
    tc = lax.broadcasted_iota(jnp.int32, (CHUNK, CHUNK), 1)
    dirs = ((rf_ref, kf_ref, vf_ref, lwf_ref, bf_ref, kef_ref, pl.multiple_of(c * CHUNK, CHUNK)),
            (rb_ref, kb_ref, vb_ref, lwb_ref, bb_ref, keb_ref,
             pl.multiple_of((n_chunks - 1 - c) * CHUNK, CHUNK)))

    def advance(s):
        chains = [(z, p) for z in range(2) for p in range(B_HEADS // 2)]
        sls = [slice(p * LANES, (p + 1) * LANES) for _, p in chains]
        before, before_eq, tot, kkd, rd, bg, kg, vv, sc = [], [], [], [], [], [], [], [], []
        for z in range(2):
            r_ref, kk_ref, v_ref, lw_ref, b_ref, ke_ref, _ = dirs[z]
            diff = (li - row) if z else (row - li)
            tri = jnp.where(((tc - tr) if z else (tr - tc)) >= 0, 1.0, 0.0).astype(BF16)
            lw_all = lw_ref[s]
            cum_all = _split_dot(tri, lw_all)
            for p in range(B_HEADS // 2):
                sl = slice(p * LANES, (p + 1) * LANES)
                lw = lw_all[:, sl]
                cum = cum_all[:, sl]
                tt = cum[0:1] if z else cum[CHUNK - 1:CHUNK]
                b = b_ref[s, :, sl]
                ke = ke_ref[s, :, sl]
                g_inv = jnp.exp(-cum)
                g_rest = jnp.exp(tt - cum)
                before.append(diff > 0)
                before_eq.append(diff >= 0)
                tot.append(tt)
                kkd.append(kk_ref[s, :, sl] * jnp.exp(cum - lw))
                rd.append(r_ref[s, :, sl] * jnp.exp(cum))
                bg.append(b * g_rest)
                kg.append(ke * g_rest)
                vv.append(v_ref[s, :, sl])
                sc.append(_bdot(jnp.concatenate([kkd[-1], rd[-1]], 0),
                                jnp.concatenate([_stack_pair(b * g_inv), _stack_pair(ke * g_inv)], 0), NT))
            yield
        lb = [jnp.where(m, x[:CHUNK, :LANES], 0.0) for m, x in zip(before, sc)]
        mb = [jnp.where(m, x[CHUNK:, :LANES], 0.0) for m, x in zip(before_eq, sc)]
        lv = [_bdot(jnp.concatenate([jnp.where(m, x[:CHUNK, LANES:], 0.0),
                                     jnp.where(me, x[CHUNK:, LANES:], 0.0)], 0), _stack_pair(v))
              for m, me, x, v in zip(before, before_eq, sc, vv)]
        yield

        pw = [jnp.where(blk16, x, 0.0) for x in lb]
        tinv = [eye - x for x in pw]
        for _ in range(3):
            pw = [_bdot(x, _stack_pair(x)) for x in pw]
            yield
            tinv = [t + _bdot(t, _stack_pair(x)) for t, x in zip(tinv, pw)]
            yield
        for off in (blk32 & ~blk16, ~blk32):
            tmp = [_bdot(t, _stack_pair(jnp.where(off, x, 0.0))) for t, x in zip(tinv, lb)]
            yield
            tinv = [t - _bdot(m, _stack_pair(t)) for t, m in zip(tinv, tmp)]
            yield

        w12 = [_bdot(t, jnp.concatenate([_stack_pair(k), _stack_pair(x[:CHUNK])], 1))
               for t, k, x in zip(tinv, kkd, lv)]
        yield
        s_old = [s_scr[s, z, p] for z, p in chains]
        ws = [_bdot(jnp.concatenate([w[:, :LANES], r], 0), st, NT) for w, r, st in zip(w12, rd, s_old)]
        yield
        u = [-(a[:CHUNK] + w[:, LANES:]) for a, w in zip(ws, w12)]
        mbu = [_bdot(m, _stack_pair(x)) for m, x in zip(mb, u)]
        upd = [_bdot(jnp.concatenate([x, v], 0).T, jnp.concatenate([b_, k_], 0))
               for x, v, b_, k_ in zip(u, vv, bg, kg)]
        yield
        for i, (z, p) in enumerate(chains):
            y_ref[s, pl.ds(dirs[z][6], CHUNK), sls[i]] += ws[i][CHUNK:] + mbu[i] + lv[i][CHUNK:]
            s_scr[s, z, p] = s_old[i] * jnp.exp(tot[i]) + jnp.where(same_head, upd[i], 0.0)

    live = [advance(s) for s in range(rf_ref.shape[0])]
    while live:
        live = [gen for gen in live if next(gen, "done") != "done"]

    if from_zero:
        @pl.when(c == n_chunks - 1)
        def _():
            for s in range(s_scr.shape[0]):
                for z in range(2):
                    for p in range(B_HEADS // 2):
                        st = s_scr[s, z, p]
                        sout_ref[s, z, 2 * p] = st[:HEAD_DIM, :HEAD_DIM]
                        sout_ref[s, z, 2 * p + 1] = pltpu.roll(st[HEAD_DIM:, :], HEAD_DIM, 1)[:, :HEAD_DIM]


def _rwkv_scan(r, kkn, v, lw, b, ke, s0, bn, t):
    nc = t // CHUNK
    npair = B_HEADS // 2
    ns = min(SCAN_SEQS, bn)
    fwd = pl.BlockSpec((ns, CHUNK, B_WIDTH), lambda g, c: (g, c, 0))
    bwd = pl.BlockSpec((ns, CHUNK, B_WIDTH), lambda g, c: (g, nc - 1 - c, 0))
    dfwd = pl.BlockSpec((None, ns, CHUNK, B_WIDTH), lambda g, c: (0, g, c, 0))
    dbwd = pl.BlockSpec((None, ns, CHUNK, B_WIDTH), lambda g, c: (1, g, nc - 1 - c, 0))
    r3, k3, v3 = (a.reshape(bn, t, B_WIDTH) for a in (r, kkn, v))
    lw4, b4, ke4 = (a.reshape(2, bn, t, B_WIDTH) for a in (lw, b, ke))
    seq = jax.ShapeDtypeStruct((bn, t, B_WIDTH), F32)
    in_specs = [fwd, fwd, fwd, bwd, bwd, bwd, dfwd, dfwd, dfwd, dbwd, dbwd, dbwd]
    args = [r3, k3, v3, r3, k3, v3, lw4, b4, ke4, lw4, b4, ke4]
    out_specs = [pl.BlockSpec((ns, t, B_WIDTH), lambda g, c: (g, 0, 0))]
    out_shape = [seq]
    if s0 is None:
        out_specs.append(pl.BlockSpec((ns, 2, B_HEADS, HEAD_DIM, HEAD_DIM), lambda g, c: (g, 0, 0, 0, 0)))
        out_shape.append(jax.ShapeDtypeStruct((bn, 2, B_HEADS, HEAD_DIM, HEAD_DIM), F32))
    else:
        in_specs.append(pl.BlockSpec((ns, 2, npair, PAIR, PAIR), lambda g, c: (g, 0, 0, 0, 0)))
        args.append(s0)
    return pl.pallas_call(
        functools.partial(_rwkv_scan_kernel, n_chunks=nc, from_zero=s0 is None),
        grid=(bn // ns, nc),
        in_specs=in_specs,
        out_specs=out_specs,
        out_shape=out_shape,
        scratch_shapes=[pltpu.VMEM((ns, 2, npair, PAIR, PAIR), F32)],
        compiler_params=_params(2),
        name="rwkv_scan",
    )(*args)


def _post(x, z, gate, gain):
    zn = z * lax.rsqrt(jnp.mean(z * z, -1, keepdims=True) + NORM_EPS)
    return x + gate * (zn * gain)


def _even_out_kernel(ya_ref, ga_ref, y_ref, bonus_ref, gb_ref, lnw_ref, lnb_ref, ones_ref,
                     x_ref, gate_ref, gain_ref, w_ref, o_ref):
    y = y_ref[...]
    ones = ones_ref[...]
    mean = _head_sum(y, ones) * (1.0 / HEAD_DIM)
    d = y - mean
    var = _head_sum(d * d, ones) * (1.0 / HEAD_DIM)
    yb = (d * lax.rsqrt(var + RWKV_LN_EPS)) * lnw_ref[...] + lnb_ref[...] + bonus_ref[...]
    za = (ya_ref[...] * _silu(ga_ref[...])).astype(BF16)
    zb = (yb * _silu(gb_ref[...])).astype(BF16)
    z = (jnp.dot(za, w_ref[0:A_WIDTH, :], preferred_element_type=F32)
         + jnp.dot(zb, w_ref[A_WIDTH:, :], preferred_element_type=F32))
    o_ref[...] = _post(x_ref[...], z, gate_ref[...], gain_ref[...])


def _even_out(ya, ga, y, bonus, gb, ln_w, ln_b, ones, x2d, gate, gain, w_bf16, t):
    n = x2d.shape[0]
    row = lambda i: (i, 0)
    const2 = lambda i: (0, 0)
    half = pl.BlockSpec((OUT_TILE, B_WIDTH), row)
    full = pl.BlockSpec((OUT_TILE, D_MODEL), row)
    vec = pl.BlockSpec((1, B_WIDTH), const2)
    assert gate.shape[0] == 1 or t % OUT_TILE == 0
    return pl.pallas_call(
        _even_out_kernel,
        grid=(n // OUT_TILE,),
        in_specs=[half, half, half, half, half, vec, vec,
                  pl.BlockSpec((B_WIDTH, B_WIDTH), const2),
                  full,
                  pl.BlockSpec((None, 1, D_MODEL), _mod_index(gate, t // OUT_TILE)),
                  pl.BlockSpec((1, D_MODEL), const2),
                  pl.BlockSpec((D_MODEL, D_MODEL), const2)],
        out_specs=full,
        out_shape=jax.ShapeDtypeStruct((n, D_MODEL), F32),
        compiler_params=_params(1),
        name="even_out",
    )(ya, ga, y, bonus, gb, ln_w, ln_b, ones, x2d, gate, gain, w_bf16)


def _odd_out_kernel(o_ref_in, gc_ref, x_ref, gate_ref, gain_ref, w_ref, o_ref):
    zc = (o_ref_in[...] * _silu(gc_ref[...])).astype(BF16)
    z = jnp.dot(zc, w_ref[...], preferred_element_type=F32)
    o_ref[...] = _post(x_ref[...], z, gate_ref[...], gain_ref[...])


def _odd_out(o, gc, x2d, gate, gain, w_bf16, t):
    n = x2d.shape[0]
    row = lambda i: (i, 0)
    const2 = lambda i: (0, 0)
    full = pl.BlockSpec((OUT_TILE, D_MODEL), row)
    assert gate.shape[0] == 1 or t % OUT_TILE == 0
    return pl.pallas_call(
        _odd_out_kernel,
        grid=(n // OUT_TILE,),
        in_specs=[full, full, full,
                  pl.BlockSpec((None, 1, D_MODEL), _mod_index(gate, t // OUT_TILE)),
                  pl.BlockSpec((1, D_MODEL), const2),
                  pl.BlockSpec((D_MODEL, D_MODEL), const2)],
        out_specs=full,
        out_shape=jax.ShapeDtypeStruct((n, D_MODEL), F32),
        compiler_params=_params(1),
        name="odd_out",
    )(o, gc, x2d, gate, gain, w_bf16)


def _lambda(lq1_ref, lk1_ref, lq2_ref, lk2_ref, lam_init):
    s1 = jnp.sum(lq1_ref[...] * lk1_ref[...], -1, keepdims=True)
    s2 = jnp.sum(lq2_ref[...] * lk2_ref[...], -1, keepdims=True)
    return jnp.exp(s1) - jnp.exp(s2) + lam_init


def _subln(o, gain, lam_init):
    on = o * lax.rsqrt(jnp.mean(o * o, -1, keepdims=True) + SUBLN_EPS)
    return (on * gain) * (1.0 - lam_init)


def _diff_ctx_kernel(lq1_ref, lk1_ref, lq2_ref, lk2_ref, sub_ref, q_ref, k_ref, v_ref,
                     gc_ref, x_ref, gate_ref, gain_ref, w_ref, y_ref, o_scr, *, lam_init):
    t = q_ref.shape[0]
    lam = _lambda(lq1_ref, lk1_ref, lq2_ref, lk2_ref, lam_init)
    sls = [slice(h * LANES, (h + 1) * LANES) for h in range(C_HEADS)]
    s2 = [_bdot(_stack_pair(q_ref[:, sl]), k_ref[:, sl], NT)
          for sl in sls]
    for h, sl in enumerate(sls):
        e = jnp.exp2(s2[h] - jnp.max(s2[h], -1, keepdims=True))
        ov = _bdot(e, v_ref[:, sl]) * (1.0 / jnp.sum(e, -1, keepdims=True))
        o_scr[:, sl] = _subln(ov[:t] - lam * ov[t:], sub_ref[...], lam_init)
    zc = (o_scr[...] * _silu(gc_ref[...])).astype(BF16)
    z = jnp.dot(zc, w_ref[...], preferred_element_type=F32)
    y_ref[...] = _post(x_ref[...], z, gate_ref[...], gain_ref[...])


def _diff_context_out(lams, subln, q, k, v, gc, x2d, gate, gain, w_bf16, bn, t, lam_init):
    seq = pl.BlockSpec((None, t, C_WIDTH), lambda b: (b, 0, 0))
    small = pl.BlockSpec((1, C_QK_DIM), lambda b: (0, 0))
    const2 = lambda b: (0, 0)
    to3 = lambda a: a.reshape(bn, t, a.shape[-1])
    return pl.pallas_call(
        functools.partial(_diff_ctx_kernel, lam_init=lam_init),
        grid=(bn,),
        in_specs=[small, small, small, small, pl.BlockSpec((1, C_V_DIM), const2),
                  seq, seq, seq, seq, seq,
                  pl.BlockSpec((None, 1, D_MODEL), lambda b: (0, 0, 0)),
                  pl.BlockSpec((1, D_MODEL), const2),
                  pl.BlockSpec((D_MODEL, D_MODEL), const2)],
        out_specs=seq,
        out_shape=jax.ShapeDtypeStruct((bn, t, D_MODEL), F32),
        scratch_shapes=[pltpu.VMEM((t, C_WIDTH), F32)],
        compiler_params=_params(1),
        name="diff_context",
    )(*lams, subln, to3(q), to3(k), to3(v), to3(gc), to3(x2d), gate, gain, w_bf16).reshape(bn * t, D_MODEL)


def _diff_lat_kernel(lq1_ref, lk1_ref, lq2_ref, lk2_ref, sub_ref, q_ref, k_ref, v_ref, kx_ref, vx_ref,
                     o_ref, *, lam_init):
    lam = _lambda(lq1_ref, lk1_ref, lq2_ref, lk2_ref, lam_init)
    kx = kx_ref[...].astype(BF16)
    k = k_ref[...].astype(BF16)
    tq = DIFF_Q_SUB
    subs = range(q_ref.shape[0] // tq)
    scores = []
    for i in subs:
        qs = _stack_pair(q_ref[i * tq:(i + 1) * tq, :])
        scores.append((_bdot(qs, kx, NT), _bdot(qs, k, NT)))
    for i in subs:
        sx, sl = scores[i]
        m = jnp.maximum(jnp.max(sx, -1, keepdims=True), jnp.max(sl, -1, keepdims=True))
        ex = jnp.exp2(sx - m)
        el = jnp.exp2(sl - m)
        den = jnp.sum(ex, -1, keepdims=True) + jnp.sum(el, -1, keepdims=True)
        ratio = lam * den[:tq] / den[tq:]
        o = (_bdot(ex[:tq] - ex[tq:] * ratio, vx_ref[...])
             + _bdot(el[:tq] - el[tq:] * ratio, v_ref[...])) * (1.0 / den[:tq])
        o_ref[i * tq:(i + 1) * tq, :] = _subln(o, sub_ref[...], lam_init)


def _diff_latent(lams, subln, q, k, v, k_ctx, v_ctx, bn, t, lam_init):
    nb = t // DIFF_Q_TILE
    nctx = k_ctx.shape[1]
    small = pl.BlockSpec((1, C_QK_DIM), lambda b, h, n: (0, 0))
    qspec = pl.BlockSpec((None, DIFF_Q_TILE, LANES), lambda b, h, n: (b, n, h))
    kvspec = pl.BlockSpec((None, t, LANES), lambda b, h, n: (b, 0, h))
    cxspec = pl.BlockSpec((None, nctx, LANES), lambda b, h, n: (b, 0, h))
    return pl.pallas_call(
        functools.partial(_diff_lat_kernel, lam_init=lam_init),
        grid=(bn, C_HEADS, nb),
        in_specs=[small, small, small, small,
                  pl.BlockSpec((1, C_V_DIM), lambda b, h, n: (0, 0)),
                  qspec, kvspec, kvspec, cxspec, cxspec],
        out_specs=qspec,
        out_shape=jax.ShapeDtypeStruct((bn, t, C_WIDTH), F32),
        compiler_params=_params(3),
        name="diff_latent",
    )(*lams, subln, q.reshape(bn, t, C_WIDTH), k.reshape(bn, t, C_WIDTH), v.reshape(bn, t, C_WIDTH),
      k_ctx, v_ctx)


def _rope_tables(t):
    nf = HEAD_DIM // 4
    inv = 1.0 / (ROPE_BASE ** (jnp.arange(nf, dtype=F32) / nf))
    pos = jnp.arange(t)
    ang_r = (pos // GRID_W).astype(F32)[:, None] * inv[None]
    ang_c = (pos % GRID_W).astype(F32)[:, None] * inv[None]
    ang = jnp.concatenate([ang_r, ang_r, ang_c, ang_c], -1)
    cos, sin = jnp.cos(ang), jnp.sin(ang)
    sign = jnp.where((jnp.arange(HEAD_DIM) // nf) % 2 == 0, -1.0, 1.0).astype(F32)
    return jnp.tile(cos, (1, 2)), jnp.tile(sin * sign[None], (1, 2))


EVEN_SEGS = ((0, 512), (512, 640), (640, 768), (768, 1280), (1280, 2944), (2944, 3456))
ODD_SEGS = ((0, 1024), (1024, 2048), (2048, 3072), (3072, 4096))


def _pair_states(st):
    bn = st.shape[0]
    s = st.reshape(bn, 2, 4, 2, HEAD_DIM, HEAD_DIM)
    z = jnp.zeros_like(s[:, :, :, 0])
    top = jnp.concatenate([s[:, :, :, 0], z], -1)
    bot = jnp.concatenate([z, s[:, :, :, 1]], -1)
    return jnp.concatenate([top, bot], -2)


def kernel(x_prompt, x_sample, cache_a_k, cache_a_v, state_rwkv, cache_c_k, cache_c_v, c, c_ctx,
           ada_w, ada_b, norm_pre, norm_post, w_out, even_w_in, a_sink, b_mu, b_w0, b_w2, b_a0,
           b_a2, b_kk, b_ka, b_rk, b_ln_w, b_ln_b, odd_w_in, c_lq1, c_lk1, c_lq2, c_lk2, c_subln):
    bp, tp, _ = x_prompt.shape
    bs, ts, _ = x_sample.shape
    past = cache_a_k.shape[2]

    cvecs = jnp.concatenate([c_ctx[None], c, jnp.zeros((8 - 1 - bs, D_MODEL), F32)], 0)
    mods = _modulation(cvecs, ada_w, ada_b)
    rope_tabs = _rope_tables(ts)
    ones = (jnp.arange(B_WIDTH)[:, None] // HEAD_DIM == jnp.arange(B_WIDTH)[None] // HEAD_DIM).astype(BF16)

    xp = x_prompt.reshape(bp * tp, D_MODEL)
    xs = x_sample.reshape(bs * ts, D_MODEL)

    def mod_rows(layer, rows):
        m = mods[layer, rows[0]:rows[1]][:, None, :]
        return m[..., :D_MODEL], m[..., D_MODEL:2 * D_MODEL], m[..., 2 * D_MODEL:]

    e = 0
    w_in0 = even_w_in[e].astype(BF16)
    w_out0 = w_out[0].astype(BF16)
    gain_pre = norm_pre[0][None]
    gain_post = norm_post[0][None]
    zpad = jnp.zeros((DECAY_LORA, B_WIDTH), F32)
    lora = jnp.concatenate([
        jnp.concatenate([b_w2[e, 0], b_w2[e, 1], zpad, zpad], 1),
        jnp.concatenate([zpad, zpad, b_a2[e, 0], b_a2[e, 1]], 1)], 0).astype(BF16)
    feat_args = (b_mu[e][None], lora, b_w0[e][:, None, :], b_a0[e][:, None, :], b_kk[e][None],
                 b_ka[e][None], b_rk[e][None], ones)

    def even_layer(x2d, bn, t, rows, ctx):
        shift, scale, gate = mod_rows(0, rows)
        (qa, ka, va, ga, gb, r, kkn, v, lw, b, ke, bonus, *kv_t) = _even_in(
            x2d, shift, scale, gain_pre, w_in0, t, None if ctx is None else rope_tabs, ctx is None,
            *feat_args)
        if ctx is None:
            ya = _gqa_context(a_sink[e], qa, ka, va, bn, t)
            s0 = None
        else:
            k_ctx, v_ctx, st0 = ctx
            ya = _gqa_window(a_sink[e], qa, ka, va, k_ctx, v_ctx, bn, t)
            s0 = _pair_states(st0)
        yb, *s_fin = _rwkv_scan(r, kkn, v, lw, b, ke, s0, bn, t)
        y = _even_out(ya.reshape(bn * t, A_WIDTH), ga, yb.reshape(bn * t, B_WIDTH), bonus, gb,
                      b_ln_w[e][None], b_ln_b[e][None], ones, x2d, gate, gain_post, w_out0, t)
        return y, kv_t, s_fin

    yp, (ka_t, va_t), st_p = even_layer(xp, bp, tp, (0, 1), None)
    ctx_a = (cache_a_k[:, e].reshape(bs, past, A_KV_WIDTH), cache_a_v[:, e].reshape(bs, past, A_KV_WIDTH),
             state_rwkv[:, e])
    ys, _, _ = even_layer(xs, bs, ts, (1, 1 + bs), ctx_a)

    def cache_a(x_t):
        return jnp.transpose(x_t.reshape(bp, A_KV_HEADS, HEAD_DIM, tp), (0, 3, 1, 2))[:, None]

    o = 0
    lam_init = 0.8 - 0.6 * math.exp(-0.3 * 1)
    w_in1 = odd_w_in[o].astype(BF16)
    w_out1 = w_out[1].astype(BF16)
    gain_pre1 = norm_pre[1][None]
    gain_post1 = norm_post[1][None]
    lams = (c_lq1[o][None], c_lk1[o][None], c_lq2[o][None], c_lk2[o][None])
    subln = c_subln[o][None]

    def odd_layer(x2d, bn, t, rows, ctx):
        shift, scale, gate = mod_rows(1, rows)
        qc, kc, vc, gc = _proj_in(x2d, shift, scale, gain_pre1, w_in1, ODD_SEGS, (0, 1), t,
                                  None if ctx is None else rope_tabs)
        if ctx is None:
            y = _diff_context_out(lams, subln, qc, kc, vc, gc, x2d, gate, gain_post1, w_out1, bn, t, lam_init)
        else:
            k_ctx, v_ctx = ctx
            oc = _diff_latent(lams, subln, qc, kc, vc, k_ctx, v_ctx, bn, t, lam_init)
            y = _odd_out(oc.reshape(bn * t, C_WIDTH), gc, x2d, gate, gain_post1, w_out1, t)
        return y, kc, vc

    yp2, kc_p, vc_p = odd_layer(yp, bp, tp, (0, 1), None)
    ctx_c = (cache_c_k[:, o].reshape(bs, past, C_WIDTH), cache_c_v[:, o].reshape(bs, past, C_WIDTH))
    ys2, _, _ = odd_layer(ys, bs, ts, (1, 1 + bs), ctx_c)

    return (yp2.reshape(bp, tp, D_MODEL),
            ys2.reshape(bs, ts, D_MODEL),
            cache_a(ka_t),
            cache_a(va_t),
            st_p[0][:, None],
            kc_p.reshape(bp, 1, tp, C_HEADS, 2 * C_QK_DIM),
            vc_p.reshape(bp, 1, tp, C_HEADS, C_V_DIM))
```

```python
import functools
import math

import jax
import jax.numpy as jnp
from jax import lax
from jax.experimental import pallas as pl
from jax.experimental.pallas import tpu as pltpu

F32 = jnp.float32
BF16 = jnp.bfloat16

D_MODEL = 1024
DEPTH = 2
GRID_W = 64
HEAD_DIM = 64
ROPE_BASE = 10000.0
A_HEADS = 8
A_KV_HEADS = 2
A_WIDTH = 512
A_KV_WIDTH = 128
WINDOW = 128
BLOCK = 128
B_HEADS = 8
B_WIDTH = 512
DECAY_LORA = 64
AAA_LORA = 64
B_SHIFT_WIDTH = 3 * B_WIDTH + DECAY_LORA + AAA_LORA
C_HEADS = 8
C_QK_DIM = 64
C_V_DIM = 128
C_WIDTH = 1024
EVEN_IN = 3456
ODD_IN = 4096
NORM_EPS = 1e-6
RWKV_LN_EPS = 64e-5
SUBLN_EPS = 1e-5
NEG_INF = -1e30

LANES = 128
ROW_TILE = 256
OUT_TILE = 512
DIFF_Q_TILE = 1024
DIFF_Q_SUB = 128
CHUNK = 64
SCAN_SEQS = 4
PAIR = 2 * CHUNK
VMEM_LIMIT = 48 * 1024 * 1024

LOG2E = math.log2(math.e)
QUERY_SCALE = HEAD_DIM ** -0.5 * LOG2E

NN = ((1,), (0,))
NT = ((1,), (1,))


def _bdot(a, b, dims=NN):
    return lax.dot_general(a.astype(BF16), b.astype(BF16), (dims, ((), ())),
                           preferred_element_type=F32)


def _hdot(a, b):
    return lax.dot_general(a, b, (NN, ((), ())), precision=lax.Precision.HIGHEST,
                           preferred_element_type=F32)


def _split_dot(sel_bf16, x):
    x1 = x.astype(BF16)
    r1 = x - x1.astype(F32)
    x2 = r1.astype(BF16)
    x3 = (r1 - x2.astype(F32)).astype(BF16)
    dot = lambda p: jnp.dot(sel_bf16, p, preferred_element_type=F32)
    return dot(x1) + dot(x2) + dot(x3)


def _head_sum(x, ones_bf16):
    hi = x.astype(BF16)
    lo = (x - hi.astype(F32)).astype(BF16)
    return (jnp.dot(hi, ones_bf16, preferred_element_type=F32)
            + jnp.dot(lo, ones_bf16, preferred_element_type=F32))


def _sigmoid(x):
    return 1.0 / (1.0 + jnp.exp(-x))


def _silu(x):
    return x * _sigmoid(x)


def _params(n_axes):
    return pltpu.CompilerParams(dimension_semantics=("arbitrary",) * n_axes,
                                vmem_limit_bytes=VMEM_LIMIT)


def _lane_lo(shape):
    return (lax.broadcasted_iota(jnp.int32, shape, len(shape) - 1) % LANES) < HEAD_DIM


def _stack_pair(x):
    lo = _lane_lo(x.shape)
    z = jnp.zeros_like(x)
    return jnp.concatenate([jnp.where(lo, x, z), jnp.where(lo, z, x)], axis=0)


def _dup_half(x, g):
    lo = _lane_lo(x.shape)
    sw = pltpu.roll(x, HEAD_DIM, 1)
    return jnp.where(lo, x, sw) if g == 0 else jnp.where(lo, sw, x)


def _rope(x, cos, sin_signed):
    out = []
    even_q = (lax.broadcasted_iota(jnp.int32, cos.shape, 1) // (HEAD_DIM // 4)) % 2 == 0
    for g in range(x.shape[1] // LANES):
        xg = x[:, g * LANES:(g + 1) * LANES]
        up = pltpu.roll(xg, LANES - HEAD_DIM // 4, 1)
        dn = pltpu.roll(xg, HEAD_DIM // 4, 1)
        out.append(xg * cos + jnp.where(even_q, up, dn) * sin_signed)
    return out


def _mod_kernel(c_ref, w_ref, b_ref, o_ref):
    s = _silu(c_ref[...])
    o_ref[...] = _hdot(s, w_ref[...]) + b_ref[...]


def _modulation(cvecs, ada_w, ada_b):
    nrow = cvecs.shape[0]
    return pl.pallas_call(
        _mod_kernel,
        grid=(DEPTH, 3),
        in_specs=[pl.BlockSpec((nrow, D_MODEL), lambda l, j: (0, 0)),
                  pl.BlockSpec((None, D_MODEL, D_MODEL), lambda l, j: (l, 0, j)),
                  pl.BlockSpec((None, 1, D_MODEL), lambda l, j: (l, 0, j))],
        out_specs=pl.BlockSpec((None, nrow, D_MODEL), lambda l, j: (l, 0, j)),
        out_shape=jax.ShapeDtypeStruct((DEPTH, nrow, 3 * D_MODEL), F32),
        compiler_params=_params(2),
        name="modulation",
    )(cvecs, ada_w, ada_b.reshape(DEPTH, 1, 3 * D_MODEL))


def _mod_index(mod, tiles_per_seq):
    if mod.shape[0] == 1:
        return lambda i: (0, 0, 0)
    return lambda i: (i // tiles_per_seq, 0, 0)


def _proj_in_kernel(*refs, segs, rope_segs, use_rope, t_segs):
    if use_rope:
        x_ref, sh_ref, sc_ref, g_ref, w_ref, cos_ref, sin_ref = refs[:7]
        outs = refs[7:]
    else:
        x_ref, sh_ref, sc_ref, g_ref, w_ref = refs[:5]
        outs = refs[5:]
    x = x_ref[...]
    y = x * lax.rsqrt(jnp.mean(x * x, -1, keepdims=True) + NORM_EPS)
    h = (y * g_ref[...]) * (1.0 + sc_ref[...]) + sh_ref[...]
    hb = h.astype(BF16)
    for idx, (lo, hi) in enumerate(segs):
        o = jnp.dot(hb, w_ref[:, lo:hi], preferred_element_type=F32)
        post = (lambda a: (a * QUERY_SCALE).astype(BF16)) if idx == 0 else (lambda a: a)
        if use_rope and idx in rope_segs:
            parts = _rope(o, cos_ref[...], sin_ref[...])
            for g, part in enumerate(parts):
                outs[idx][:, g * LANES:(g + 1) * LANES] = post(part)
        else:
            outs[idx][...] = post(o)
        if idx in t_segs:
            outs[len(segs) + t_segs.index(idx)][...] = o.T


def _proj_in(x2d, shift, scale, gain, w_bf16, segs, rope_segs, t, rope_tabs, t_segs=()):
    n = x2d.shape[0]
    tile = t if t_segs else OUT_TILE
    assert shift.shape[0] == 1 or t % tile == 0
    nt = n // tile
    tiles_per_seq = max(t // tile, 1)
    use_rope = rope_tabs is not None
    row = lambda i: (i, 0)
    per_b = _mod_index(shift, tiles_per_seq)
    in_specs = [pl.BlockSpec((tile, D_MODEL), row),
                pl.BlockSpec((None, 1, D_MODEL), per_b),
                pl.BlockSpec((None, 1, D_MODEL), per_b),
                pl.BlockSpec((1, D_MODEL), lambda i: (0, 0)),
                pl.BlockSpec(w_bf16.shape, lambda i: (0, 0))]
    args = [x2d, shift, scale, gain, w_bf16]
    if use_rope:
        pos = lambda i: (i % tiles_per_seq, 0)
        in_specs += [pl.BlockSpec((tile, LANES), pos), pl.BlockSpec((tile, LANES), pos)]
        args += list(rope_tabs)
    widths = [hi - lo for lo, hi in segs]
    return pl.pallas_call(
        functools.partial(_proj_in_kernel, segs=segs, rope_segs=rope_segs, use_rope=use_rope,
                          t_segs=tuple(t_segs)),
        grid=(nt,),
        in_specs=in_specs,
        out_specs=([pl.BlockSpec((tile, w), row) for w in widths]
                   + [pl.BlockSpec((None, widths[i], tile), lambda i_: (i_, 0, 0)) for i in t_segs]),
        out_shape=([jax.ShapeDtypeStruct((n, w), BF16 if i == 0 else F32) for i, w in enumerate(widths)]
                   + [jax.ShapeDtypeStruct((nt, widths[i], tile), F32) for i in t_segs]),
        compiler_params=_params(1),
        name="proj_in",
    )(*args)


def _sink_exp_pair(s2, sink_a, sink_b, n):
    rowi = lax.broadcasted_iota(jnp.int32, (2 * n, 1), 0)
    sink2 = jnp.where(rowi < n, sink_a, sink_b) * LOG2E
    m = jnp.maximum(jnp.max(s2, -1, keepdims=True), sink2)
    e = jnp.exp2(s2 - m)
    return e, 1.0 / (jnp.sum(e, -1, keepdims=True) + jnp.exp2(sink2 - m))


def _gqa_ctx_kernel(sink_ref, q_ref, k_ref, v_ref, o_ref):
    t = q_ref.shape[0]
    k = k_ref[...]
    v = v_ref[...]
    kd = [_dup_half(k, g).astype(BF16) for g in range(A_KV_HEADS)]
    vd = [_dup_half(v, g).astype(BF16) for g in range(A_KV_HEADS)]
    s2 = [_bdot(_stack_pair(q_ref[:, p * LANES:(p + 1) * LANES]),
                kd[p // 2], NT) for p in range(A_HEADS // 2)]
    for p in range(A_HEADS // 2):
        e, inv = _sink_exp_pair(s2[p], sink_ref[2 * p], sink_ref[2 * p + 1], t)
        o = _bdot(e, vd[p // 2]) * inv
        o_ref[:, p * LANES:(p + 1) * LANES] = jnp.where(_lane_lo((t, LANES)), o[:t], o[t:])


def _gqa_context(sink, q, k, v, bn, t):
    seq = lambda b: (b, 0, 0)
    return pl.pallas_call(
        _gqa_ctx_kernel,
        grid=(bn,),
        in_specs=[pl.BlockSpec(memory_space=pltpu.SMEM),
                  pl.BlockSpec((None, t, A_WIDTH), seq),
                  pl.BlockSpec((None, t, A_KV_WIDTH), seq),
                  pl.BlockSpec((None, t, A_KV_WIDTH), seq)],
        out_specs=pl.BlockSpec((None, t, A_WIDTH), seq),
        out_shape=jax.ShapeDtypeStruct((bn, t, A_WIDTH), F32),
        compiler_params=_params(1),
        name="gqa_context",
    )(sink, q.reshape(bn, t, A_WIDTH), k.reshape(bn, t, A_KV_WIDTH), v.reshape(bn, t, A_KV_WIDTH))


def _gqa_win_kernel(sink_ref, q_ref, kp_ref, kc_ref, kn_ref, vp_ref, vc_ref, vn_ref,
                    kx_ref, vx_ref, o_ref, *, t):
    nblk = pl.program_id(1)
    nctx = kx_ref.shape[0]
    keys = jnp.concatenate([kx_ref[...], kp_ref[...], kc_ref[...], kn_ref[...]], axis=0)
    vals = jnp.concatenate([vx_ref[...], vp_ref[...], vc_ref[...], vn_ref[...]], axis=0)
    nkeys = nctx + 3 * BLOCK
    rowi = lax.broadcasted_iota(jnp.int32, (2 * BLOCK, nkeys), 0)
    coli = lax.broadcasted_iota(jnp.int32, (2 * BLOCK, nkeys), 1)
    qpos = nblk * BLOCK + rowi % BLOCK
    kpos = nblk * BLOCK + coli - nctx - BLOCK
    valid = (coli < nctx) | ((jnp.abs(qpos - kpos) <= WINDOW) & (kpos >= 0) & (kpos < t))
    kd = [_dup_half(keys, g).astype(BF16) for g in range(A_KV_HEADS)]
    vd = [_dup_half(vals, g).astype(BF16) for g in range(A_KV_HEADS)]
    s2 = [_bdot(_stack_pair(q_ref[:, p * LANES:(p + 1) * LANES]),
                kd[p // 2], NT) for p in range(A_HEADS // 2)]
    for p in range(A_HEADS // 2):
        e, inv = _sink_exp_pair(jnp.where(valid, s2[p], NEG_INF), sink_ref[2 * p], sink_ref[2 * p + 1], BLOCK)
        o = _bdot(e, vd[p // 2]) * inv
        o_ref[:, p * LANES:(p + 1) * LANES] = jnp.where(_lane_lo((BLOCK, LANES)), o[:BLOCK], o[BLOCK:])


def _gqa_window(sink, q, k, v, k_ctx, v_ctx, bn, t):
    nb = t // BLOCK
    nctx = k_ctx.shape[1]
    cur = lambda b, n: (b, n, 0)
    prev = lambda b, n: (b, jnp.maximum(n - 1, 0), 0)
    nxt = lambda b, n: (b, jnp.minimum(n + 1, nb - 1), 0)
    ctx = lambda b, n: (b, 0, 0)
    kv = lambda f: pl.BlockSpec((None, BLOCK, A_KV_WIDTH), f)
    k3 = k.reshape(bn, t, A_KV_WIDTH)
    v3 = v.reshape(bn, t, A_KV_WIDTH)
    return pl.pallas_call(
        functools.partial(_gqa_win_kernel, t=t),
        grid=(bn, nb),
        in_specs=[pl.BlockSpec(memory_space=pltpu.SMEM),
                  pl.BlockSpec((None, BLOCK, A_WIDTH), cur),
                  kv(prev), kv(cur), kv(nxt), kv(prev), kv(cur), kv(nxt),
                  pl.BlockSpec((None, nctx, A_KV_WIDTH), ctx),
                  pl.BlockSpec((None, nctx, A_KV_WIDTH), ctx)],
        out_specs=pl.BlockSpec((None, BLOCK, A_WIDTH), cur),
        out_shape=jax.ShapeDtypeStruct((bn, t, A_WIDTH), F32),
        compiler_params=_params(2),
        name="gqa_window",
    )(sink, q.reshape(bn, t, A_WIDTH), k3, k3, k3, v3, v3, v3, k_ctx, v_ctx)


def _even_in_kernel(*refs, use_rope, emit_t, tiles_per_seq):
    x_ref, xp_ref, xn_ref, sh_ref, sc_ref, g_ref, w_ref = refs[:7]
    refs = refs[7:]
    if use_rope:
        cos_ref, sin_ref = refs[:2]
        refs = refs[2:]
    mu_ref, lora_ref, w0_ref, a0_ref, kk_ref, ka_ref, rk_ref, ones_ref = refs[:8]
    (q_out, k_out, v_out, ga_out, gb_out, r_out, kkn_out, vv_out, lw_out, b_out, ke_out,
     bonus_out) = refs[8:20]
    i = pl.program_id(0)
    tm = x_ref.shape[0]
    pos = i % tiles_per_seq

    def norm_mod(x):
        y = x * lax.rsqrt(jnp.mean(x * x, -1, keepdims=True) + NORM_EPS)
        return (y * g_ref[...]) * (1.0 + sc_ref[...]) + sh_ref[...]

    h = norm_mod(x_ref[...])
    hb = h.astype(BF16)
    seg = dict(zip(("q", "k", "v", "ga", "pb", "gb"), EVEN_SEGS))
    proj = lambda name, lhs=hb: jnp.dot(lhs, w_ref[:, seg[name][0]:seg[name][1]], preferred_element_type=F32)
    h_ext = jnp.concatenate([norm_mod(xp_ref[...]), h, norm_mod(xn_ref[...])], 0).astype(BF16)
    p_ext = proj("pb", h_ext)

    q = proj("q")
    if use_rope:
        q = jnp.concatenate(_rope(q, cos_ref[...], sin_ref[...]), 1)
    q_out[...] = (q * QUERY_SCALE).astype(BF16)

    p = p_ext[8:8 + tm]
    p_ext = jnp.concatenate([jnp.where(pos != 0, p_ext[:8], 0.0), p,
                             jnp.where(pos != tiles_per_seq - 1, p_ext[8 + tm:], 0.0)], 0)
    prev = pltpu.roll(p_ext, 1, 0)[8:8 + tm]
    nxt = pltpu.roll(p_ext, tm + 15, 0)[8:8 + tm]
    xf = p + (0.5 * (prev + nxt) - p) * mu_ref[...]
    r = xf[:, 0:B_WIDTH]
    k = xf[:, B_WIDTH:2 * B_WIDTH]
    v = xf[:, 2 * B_WIDTH:3 * B_WIDTH]
    g = xf[:, 3 * B_WIDTH:]
    r_out[...] = r
    vv_out[...] = v

    k_att = proj("k")
    if use_rope:
        k_att = _rope(k_att, cos_ref[...], sin_ref[...])[0]
    k_out[...] = k_att
    v_att = proj("v")
    v_out[...] = v_att
    if emit_t:
        refs[20][...] = k_att.T
        refs[21][...] = v_att.T

    g = jnp.where(_lane_lo(g.shape), jnp.tanh(g), g)
    lo = _bdot(g, lora_ref[...])
    ones = ones_ref[...]
    kkr = k * kk_ref[...]
    kkn = kkr / jnp.maximum(jnp.sqrt(_head_sum(kkr * kkr, ones)), 1e-12)
    kkn_out[...] = kkn

    ga_out[...] = proj("ga")

    ke_sum = jnp.zeros_like(k)
    for z in range(2):
        wz = w0_ref[z] + lo[:, z * B_WIDTH:(z + 1) * B_WIDTH]
        lw_out[z] = -math.exp(-0.5) * _sigmoid(wz)
        az = _sigmoid(a0_ref[z] + lo[:, (2 + z) * B_WIDTH:(3 + z) * B_WIDTH])
        ke = k * (1.0 + (az - 1.0) * ka_ref[...])
        ke_out[z] = ke
        b_out[z] = kkn * az
        ke_sum = ke_sum + ke
        if z == 0:
            gb_out[...] = proj("gb")
    bonus_out[...] = _head_sum(r * (0.5 * ke_sum) * rk_ref[...], ones) * v


def _even_in(x2d, shift, scale, gain, w_bf16, t, rope_tabs, emit_t, mu, lora, w0, a0, k_k, k_a, r_k, ones):
    n = x2d.shape[0]
    tile = ROW_TILE if emit_t else OUT_TILE
    assert t % tile == 0 and (not emit_t or t == tile)
    nt = n // tile
    tiles_per_seq = t // tile
    h8 = tile // 8
    use_rope = rope_tabs is not None
    row = lambda i: (i, 0)
    const2 = lambda i: (0, 0)
    const3 = lambda i: (0, 0, 0)
    per_b = _mod_index(shift, tiles_per_seq)
    vec = pl.BlockSpec((1, B_WIDTH), const2)
    in_specs = [pl.BlockSpec((tile, D_MODEL), row),
                pl.BlockSpec((8, D_MODEL), lambda i: (jnp.maximum(i * h8 - 1, 0), 0)),
                pl.BlockSpec((8, D_MODEL), lambda i: (jnp.minimum((i + 1) * h8, n // 8 - 1), 0)),
                pl.BlockSpec((None, 1, D_MODEL), per_b),
                pl.BlockSpec((None, 1, D_MODEL), per_b),
                pl.BlockSpec((1, D_MODEL), const2),
                pl.BlockSpec(w_bf16.shape, const2)]
    args = [x2d, x2d, x2d, shift, scale, gain, w_bf16]
    if use_rope:
        pos = lambda i: (i % tiles_per_seq, 0)
        in_specs += [pl.BlockSpec((tile, LANES), pos), pl.BlockSpec((tile, LANES), pos)]
        args += list(rope_tabs)
    in_specs += [pl.BlockSpec((1, B_SHIFT_WIDTH), const2),
                 pl.BlockSpec(lora.shape, const2),
                 pl.BlockSpec((2, 1, B_WIDTH), const3),
                 pl.BlockSpec((2, 1, B_WIDTH), const3),
                 vec, vec, vec,
                 pl.BlockSpec((B_WIDTH, B_WIDTH), const2)]
    args += [mu, lora, w0, a0, k_k, k_a, r_k, ones]
    wide = pl.BlockSpec((tile, B_WIDTH), row)
    narrow = pl.BlockSpec((tile, A_KV_WIDTH), row)
    dual = pl.BlockSpec((2, tile, B_WIDTH), lambda i: (0, i, 0))
    f_wide = jax.ShapeDtypeStruct((n, B_WIDTH), F32)
    f_narrow = jax.ShapeDtypeStruct((n, A_KV_WIDTH), F32)
    f_dual = jax.ShapeDtypeStruct((2, n, B_WIDTH), F32)
    out_specs = [wide, narrow, narrow, wide, wide, wide, wide, wide, dual, dual, dual, wide]
    out_shape = [jax.ShapeDtypeStruct((n, A_WIDTH), BF16), f_narrow, f_narrow, f_wide, f_wide,
                 f_wide, f_wide, f_wide, f_dual, f_dual, f_dual, f_wide]
    if emit_t:
        out_specs += [pl.BlockSpec((None, A_KV_WIDTH, tile), lambda i: (i, 0, 0))] * 2
        out_shape += [jax.ShapeDtypeStruct((nt, A_KV_WIDTH, tile), F32)] * 2
    return pl.pallas_call(
        functools.partial(_even_in_kernel, use_rope=use_rope, emit_t=emit_t, tiles_per_seq=tiles_per_seq),
        grid=(nt,),
        in_specs=in_specs,
        out_specs=out_specs,
        out_shape=out_shape,
        compiler_params=_params(1),
        name="even_in",
    )(*args)


def _rwkv_scan_kernel(*refs, n_chunks, from_zero):
    (rf_ref, kf_ref, vf_ref, rb_ref, kb_ref, vb_ref, lwf_ref, bf_ref, kef_ref,
     lwb_ref, bb_ref, keb_ref) = refs[:12]
    if from_zero:
        yf_ref, yb_ref, sout_ref, s_scr = refs[12:]
    else:
        s0_ref, yf_ref, yb_ref, s_scr = refs[12:]
    c = pl.program_id(1)

    @pl.when(c == 0)
    def _():
        if from_zero:
            s_scr[...] = jnp.zeros_like(s_scr)
        else:
            s_scr[...] = s0_ref[...]

    row = lax.broadcasted_iota(jnp.int32, (CHUNK, LANES), 0)
    li = lax.broadcasted_iota(jnp.int32, (CHUNK, LANES), 1) % CHUNK
    blk16 = (row // 16) == (li // 16)
    blk32 = (row // 32) == (li // 32)
    eye = jnp.where(row == li, 1.0, 0.0)
    srow = lax.broadcasted_iota(jnp.int32, (PAIR, PAIR), 0)
    scol = lax.broadcasted_iota(jnp.int32, (PAIR, PAIR), 1)
    same_head = (srow // CHUNK) == (scol // CHUNK)
    tr = lax.broadcasted_iota(jnp.int32, (CHUNK, CHUNK), 0)
    tc = lax.broadcasted_iota(jnp.int32, (CHUNK, CHUNK), 1)
    dirs = ((rf_ref, kf_ref, vf_ref, lwf_ref, bf_ref, kef_ref, yf_ref),
            (rb_ref, kb_ref, vb_ref, lwb_ref, bb_ref, keb_ref, yb_ref))

    def advance(s):
        chains = [(z, p) for z in range(2) for p in range(B_HEADS // 2)]
        sls = [slice(p * LANES, (p + 1) * LANES) for _, p in chains]
        before, before_eq, tot, kkd, rd, bg, kg, vv, sc = [], [], [], [], [], [], [], [], []
        for z in range(2):
            r_ref, kk_ref, v_ref, lw_ref, b_ref, ke_ref, _ = dirs[z]
            diff = (li - row) if z else (row - li)
            tri = jnp.where(((tc - tr) if z else (tr - tc)) >= 0, 1.0, 0.0).astype(BF16)
            lw_all = lw_ref[s]
            cum_all = _split_dot(tri, lw_all)
            for p in range(B_HEADS // 2):
                sl = slice(p * LANES, (p + 1) * LANES)
                lw = lw_all[:, sl]
                cum = cum_all[:, sl]
                tt = cum[0:1] if z else cum[CHUNK - 1:CHUNK]
                b = b_ref[s, :, sl]
                ke = ke_ref[s, :, sl]
                g_inv = jnp.exp(-cum)
                g_rest = jnp.exp(tt - cum)
                before.append(diff > 0)
                before_eq.append(diff >= 0)
                tot.append(tt)
                kkd.append(kk_ref[s, :, sl] * jnp.exp(cum - lw))
                rd.append(r_ref[s, :, sl] * jnp.exp(cum))
                bg.append(b * g_rest)
                kg.append(ke * g_rest)
                vv.append(v_ref[s, :, sl])
                sc.append(_bdot(jnp.concatenate([kkd[-1], rd[-1]], 0),
                                jnp.concatenate([_stack_pair(b * g_inv), _stack_pair(ke * g_inv)], 0), NT))
            yield
        lb = [jnp.where(m, x[:CHUNK, :LANES], 0.0) for m, x in zip(before, sc)]
        mb = [jnp.where(m, x[CHUNK:, :LANES], 0.0) for m, x in zip(before_eq, sc)]
        lv = [_bdot(jnp.concatenate([jnp.where(m, x[:CHUNK, LANES:], 0.0),
                                     jnp.where(me, x[CHUNK:, LANES:], 0.0)], 0), _stack_pair(v))
              for m, me, x, v in zip(before, before_eq, sc, vv)]
        yield

        pw = [jnp.where(blk16, x, 0.0) for x in lb]
        tinv = [eye - x for x in pw]
        for _ in range(3):
            pw = [_bdot(x, _stack_pair(x)) for x in pw]
            yield
            tinv = [t + _bdot(t, _stack_pair(x)) for t, x in zip(tinv, pw)]
            yield
        for off in (blk32 & ~blk16, ~blk32):
            tmp = [_bdot(t, _stack_pair(jnp.where(off, x, 0.0))) for t, x in zip(tinv, lb)]
            yield
            tinv = [t - _bdot(m, _stack_pair(t)) for t, m in zip(tinv, tmp)]
            yield

        w12 = [_bdot(t, jnp.concatenate([_stack_pair(k), _stack_pair(x[:CHUNK])], 1))
               for t, k, x in zip(tinv, kkd, lv)]
        yield
        s_old = [s_scr[s, z, p] for z, p in chains]
        ws = [_bdot(jnp.concatenate([w[:, :LANES], r], 0), st, NT) for w, r, st in zip(w12, rd, s_old)]
        yield
        u = [-(a[:CHUNK] + w[:, LANES:]) for a, w in zip(ws, w12)]
        mbu = [_bdot(m, _stack_pair(x)) for m, x in zip(mb, u)]
        upd = [_bdot(jnp.concatenate([x, v], 0).T, jnp.concatenate([b_, k_], 0))
               for x, v, b_, k_ in zip(u, vv, bg, kg)]
        yield
        for i, (z, p) in enumerate(chains):
            dirs[z][6][s, :, sls[i]] = ws[i][CHUNK:] + mbu[i] + lv[i][CHUNK:]
            s_scr[s, z, p] = s_old[i] * jnp.exp(tot[i]) + jnp.where(same_head, upd[i], 0.0)

    live = [advance(s) for s in range(rf_ref.shape[0])]
    while live:
        live = [gen for gen in live if next(gen, "done") != "done"]

    if from_zero:
        @pl.when(c == n_chunks - 1)
        def _():
            for s in range(s_scr.shape[0]):
                for z in range(2):
                    for p in range(B_HEADS // 2):
                        st = s_scr[s, z, p]
                        sout_ref[s, z, 2 * p] = st[:HEAD_DIM, :HEAD_DIM]
                        sout_ref[s, z, 2 * p + 1] = pltpu.roll(st[HEAD_DIM:, :], HEAD_DIM, 1)[:, :HEAD_DIM]


def _rwkv_scan(r, kkn, v, lw, b, ke, s0, bn, t):
    nc = t // CHUNK
    npair = B_HEADS // 2
    ns = min(SCAN_SEQS, bn)
    fwd = pl.BlockSpec((ns, CHUNK, B_WIDTH), lambda g, c: (g, c, 0))
    bwd = pl.BlockSpec((ns, CHUNK, B_WIDTH), lambda g, c: (g, nc - 1 - c, 0))
    dfwd = pl.BlockSpec((None, ns, CHUNK, B_WIDTH), lambda g, c: (0, g, c, 0))
    dbwd = pl.BlockSpec((None, ns, CHUNK, B_WIDTH), lambda g, c: (1, g, nc - 1 - c, 0))
    r3, k3, v3 = (a.reshape(bn, t, B_WIDTH) for a in (r, kkn, v))
    lw4, b4, ke4 = (a.reshape(2, bn, t, B_WIDTH) for a in (lw, b, ke))
    seq = jax.ShapeDtypeStruct((bn, t, B_WIDTH), F32)
    in_specs = [fwd, fwd, fwd, bwd, bwd, bwd, dfwd, dfwd, dfwd, dbwd, dbwd, dbwd]
    args = [r3, k3, v3, r3, k3, v3, lw4, b4, ke4, lw4, b4, ke4]
    out_specs = [fwd, bwd]
    out_shape = [seq, seq]
    if s0 is None:
        out_specs.append(pl.BlockSpec((ns, 2, B_HEADS, HEAD_DIM, HEAD_DIM), lambda g, c: (g, 0, 0, 0, 0)))
        out_shape.append(jax.ShapeDtypeStruct((bn, 2, B_HEADS, HEAD_DIM, HEAD_DIM), F32))
    else:
        in_specs.append(pl.BlockSpec((ns, 2, npair, PAIR, PAIR), lambda g, c: (g, 0, 0, 0, 0)))
        args.append(s0)
    return pl.pallas_call(
        functools.partial(_rwkv_scan_kernel, n_chunks=nc, from_zero=s0 is None),
        grid=(bn // ns, nc),
        in_specs=in_specs,
        out_specs=out_specs,
        out_shape=out_shape,
        scratch_shapes=[pltpu.VMEM((ns, 2, npair, PAIR, PAIR), F32)],
        compiler_params=_params(2),
        name="rwkv_scan",
    )(*args)


def _post(x, z, gate, gain):
    zn = z * lax.rsqrt(jnp.mean(z * z, -1, keepdims=True) + NORM_EPS)
    return x + gate * (zn * gain)


def _even_out_kernel(ya_ref, ga_ref, yf_ref, yb_ref, bonus_ref, gb_ref, lnw_ref, lnb_ref, ones_ref,
                     x_ref, gate_ref, gain_ref, w_ref, o_ref):
    y = yf_ref[...] + yb_ref[...]
    ones = ones_ref[...]
    mean = _head_sum(y, ones) * (1.0 / HEAD_DIM)
    d = y - mean
    var = _head_sum(d * d, ones) * (1.0 / HEAD_DIM)
    yb = (d * lax.rsqrt(var + RWKV_LN_EPS)) * lnw_ref[...] + lnb_ref[...] + bonus_ref[...]
    za = (ya_ref[...] * _silu(ga_ref[...])).astype(BF16)
    zb = (yb * _silu(gb_ref[...])).astype(BF16)
    z = (jnp.dot(za, w_ref[0:A_WIDTH, :], preferred_element_type=F32)
         + jnp.dot(zb, w_ref[A_WIDTH:, :], preferred_element_type=F32))
    o_ref[...] = _post(x_ref[...], z, gate_ref[...], gain_ref[...])


def _even_out(ya, ga, yf, yb, bonus, gb, ln_w, ln_b, ones, x2d, gate, gain, w_bf16, t):
    n = x2d.shape[0]
    row = lambda i: (i, 0)
    const2 = lambda i: (0, 0)
    half = pl.BlockSpec((OUT_TILE, B_WIDTH), row)
    full = pl.BlockSpec((OUT_TILE, D_MODEL), row)
    vec = pl.BlockSpec((1, B_WIDTH), const2)
    assert gate.shape[0] == 1 or t % OUT_TILE == 0
    return pl.pallas_call(
        _even_out_kernel,
        grid=(n // OUT_TILE,),
        in_specs=[half, half, half, half, half, half, vec, vec,
                  pl.BlockSpec((B_WIDTH, B_WIDTH), const2),
                  full,
                  pl.BlockSpec((None, 1, D_MODEL), _mod_index(gate, t // OUT_TILE)),
                  pl.BlockSpec((1, D_MODEL), const2),
                  pl.BlockSpec((D_MODEL, D_MODEL), const2)],
        out_specs=full,
        out_shape=jax.ShapeDtypeStruct((n, D_MODEL), F32),
        compiler_params=_params(1),
        name="even_out",
    )(ya, ga, yf, yb, bonus, gb, ln_w, ln_b, ones, x2d, gate, gain, w_bf16)


def _odd_out_kernel(o_ref_in, gc_ref, x_ref, gate_ref, gain_ref, w_ref, o_ref):
    zc = (o_ref_in[...] * _silu(gc_ref[...])).astype(BF16)
    z = jnp.dot(zc, w_ref[...], preferred_element_type=F32)
    o_ref[...] = _post(x_ref[...], z, gate_ref[...], gain_ref[...])


def _odd_out(o, gc, x2d, gate, gain, w_bf16, t):
    n = x2d.shape[0]
    row = lambda i: (i, 0)
    const2 = lambda i: (0, 0)
    full = pl.BlockSpec((OUT_TILE, D_MODEL), row)
    assert gate.shape[0] == 1 or t % OUT_TILE == 0
    return pl.pallas_call(
        _odd_out_kernel,
        grid=(n // OUT_TILE,),
        in_specs=[full, full, full,
                  pl.BlockSpec((None, 1, D_MODEL), _mod_index(gate, t // OUT_TILE)),
                  pl.BlockSpec((1, D_MODEL), const2),
                  pl.BlockSpec((D_MODEL, D_MODEL), const2)],
        out_specs=full,
        out_shape=jax.ShapeDtypeStruct((n, D_MODEL), F32),
        compiler_params=_params(1),
        name="odd_out",
    )(o, gc, x2d, gate, gain, w_bf16)


def _lambda(lq1_ref, lk1_ref, lq2_ref, lk2_ref, lam_init):
    s1 = jnp.sum(lq1_ref[...] * lk1_ref[...], -1, keepdims=True)
    s2 = jnp.sum(lq2_ref[...] * lk2_ref[...], -1, keepdims=True)
    return jnp.exp(s1) - jnp.exp(s2) + lam_init


def _subln(o, gain, lam_init):
    on = o * lax.rsqrt(jnp.mean(o * o, -1, keepdims=True) + SUBLN_EPS)
    return (on * gain) * (1.0 - lam_init)


def _diff_ctx_kernel(lq1_ref, lk1_ref, lq2_ref, lk2_ref, sub_ref, q_ref, k_ref, v_ref,
                     gc_ref, x_ref, gate_ref, gain_ref, w_ref, y_ref, o_scr, *, lam_init):
    t = q_ref.shape[0]
    lam = _lambda(lq1_ref, lk1_ref, lq2_ref, lk2_ref, lam_init)
    sls = [slice(h * LANES, (h + 1) * LANES) for h in range(C_HEADS)]
    s2 = [_bdot(_stack_pair(q_ref[:, sl]), k_ref[:, sl], NT)
          for sl in sls]
    for h, sl in enumerate(sls):
        e = jnp.exp2(s2[h] - jnp.max(s2[h], -1, keepdims=True))
        ov = _bdot(e, v_ref[:, sl]) * (1.0 / jnp.sum(e, -1, keepdims=True))
        o_scr[:, sl] = _subln(ov[:t] - lam * ov[t:], sub_ref[...], lam_init)
    zc = (o_scr[...] * _silu(gc_ref[...])).astype(BF16)
    z = jnp.dot(zc, w_ref[...], preferred_element_type=F32)
    y_ref[...] = _post(x_ref[...], z, gate_ref[...], gain_ref[...])


def _diff_context_out(lams, subln, q, k, v, gc, x2d, gate, gain, w_bf16, bn, t, lam_init):
    seq = pl.BlockSpec((None, t, C_WIDTH), lambda b: (b, 0, 0))
    small = pl.BlockSpec((1, C_QK_DIM), lambda b: (0, 0))
    const2 = lambda b: (0, 0)
    to3 = lambda a: a.reshape(bn, t, a.shape[-1])
    return pl.pallas_call(
        functools.partial(_diff_ctx_kernel, lam_init=lam_init),
        grid=(bn,),
        in_specs=[small, small, small, small, pl.BlockSpec((1, C_V_DIM), const2),
                  seq, seq, seq, seq, seq,
                  pl.BlockSpec((None, 1, D_MODEL), lambda b: (0, 0, 0)),
                  pl.BlockSpec((1, D_MODEL), const2),
                  pl.BlockSpec((D_MODEL, D_MODEL), const2)],
        out_specs=seq,
        out_shape=jax.ShapeDtypeStruct((bn, t, D_MODEL), F32),
        scratch_shapes=[pltpu.VMEM((t, C_WIDTH), F32)],
        compiler_params=_params(1),
        name="diff_context",
    )(*lams, subln, to3(q), to3(k), to3(v), to3(gc), to3(x2d), gate, gain, w_bf16).reshape(bn * t, D_MODEL)


def _diff_lat_kernel(lq1_ref, lk1_ref, lq2_ref, lk2_ref, sub_ref, q_ref, k_ref, v_ref, kx_ref, vx_ref,
                     o_ref, *, lam_init):
    lam = _lambda(lq1_ref, lk1_ref, lq2_ref, lk2_ref, lam_init)
    kx = kx_ref[...].astype(BF16)
    k = k_ref[...].astype(BF16)
    tq = DIFF_Q_SUB
    subs = range(q_ref.shape[0] // tq)
    scores = []
    for i in subs:
        qs = _stack_pair(q_ref[i * tq:(i + 1) * tq, :])
        scores.append((_bdot(qs, kx, NT), _bdot(qs, k, NT)))
    for i in subs:
        sx, sl = scores[i]
        m = jnp.maximum(jnp.max(sx, -1, keepdims=True), jnp.max(sl, -1, keepdims=True))
        ex = jnp.exp2(sx - m)
        el = jnp.exp2(sl - m)
        den = jnp.sum(ex, -1, keepdims=True) + jnp.sum(el, -1, keepdims=True)
        ratio = lam * den[:tq] / den[tq:]
        o = (_bdot(ex[:tq] - ex[tq:] * ratio, vx_ref[...])
             + _bdot(el[:tq] - el[tq:] * ratio, v_ref[...])) * (1.0 / den[:tq])
        o_ref[i * tq:(i + 1) * tq, :] = _subln(o, sub_ref[...], lam_init)


def _diff_latent(lams, subln, q, k, v, k_ctx, v_ctx, bn, t, lam_init):
    nb = t // DIFF_Q_TILE
    nctx = k_ctx.shape[1]
    small = pl.BlockSpec((1, C_QK_DIM), lambda b, h, n: (0, 0))
    qspec = pl.BlockSpec((None, DIFF_Q_TILE, LANES), lambda b, h, n: (b, n, h))
    kvspec = pl.BlockSpec((None, t, LANES), lambda b, h, n: (b, 0, h))
    cxspec = pl.BlockSpec((None, nctx, LANES), lambda b, h, n: (b, 0, h))
    return pl.pallas_call(
        functools.partial(_diff_lat_kernel, lam_init=lam_init),
        grid=(bn, C_HEADS, nb),
        in_specs=[small, small, small, small,
                  pl.BlockSpec((1, C_V_DIM), lambda b, h, n: (0, 0)),
                  qspec, kvspec, kvspec, cxspec, cxspec],
        out_specs=qspec,
        out_shape=jax.ShapeDtypeStruct((bn, t, C_WIDTH), F32),
        compiler_params=_params(3),
        name="diff_latent",
    )(*lams, subln, q.reshape(bn, t, C_WIDTH), k.reshape(bn, t, C_WIDTH), v.reshape(bn, t, C_WIDTH),
      k_ctx, v_ctx)


def _rope_tables(t):
    nf = HEAD_DIM // 4
    inv = 1.0 / (ROPE_BASE ** (jnp.arange(nf, dtype=F32) / nf))
    pos = jnp.arange(t)
    ang_r = (pos // GRID_W).astype(F32)[:, None] * inv[None]
    ang_c = (pos % GRID_W).astype(F32)[:, None] * inv[None]
    ang = jnp.concatenate([ang_r, ang_r, ang_c, ang_c], -1)
    cos, sin = jnp.cos(ang), jnp.sin(ang)
    sign = jnp.where((jnp.arange(HEAD_DIM) // nf) % 2 == 0, -1.0, 1.0).astype(F32)
    return jnp.tile(cos, (1, 2)), jnp.tile(sin * sign[None], (1, 2))


EVEN_SEGS = ((0, 512), (512, 640), (640, 768), (768, 1280), (1280, 2944), (2944, 3456))
ODD_SEGS = ((0, 1024), (1024, 2048), (2048, 3072), (3072, 4096))


def _pair_states(st):
    bn = st.shape[0]
    s = st.reshape(bn, 2, 4, 2, HEAD_DIM, HEAD_DIM)
    z = jnp.zeros_like(s[:, :, :, 0])
    top = jnp.concatenate([s[:, :, :, 0], z], -1)
    bot = jnp.concatenate([z, s[:, :, :, 1]], -1)
    return jnp.concatenate([top, bot], -2)


def kernel(x_prompt, x_sample, cache_a_k, cache_a_v, state_rwkv, cache_c_k, cache_c_v, c, c_ctx,
           ada_w, ada_b, norm_pre, norm_post, w_out, even_w_in, a_sink, b_mu, b_w0, b_w2, b_a0,
           b_a2, b_kk, b_ka, b_rk, b_ln_w, b_ln_b, odd_w_in, c_lq1, c_lk1, c_lq2, c_lk2, c_subln):
    bp, tp, _ = x_prompt.shape
    bs, ts, _ = x_sample.shape
    past = cache_a_k.shape[2]

    cvecs = jnp.concatenate([c_ctx[None], c, jnp.zeros((8 - 1 - bs, D_MODEL), F32)], 0)
    mods = _modulation(cvecs, ada_w, ada_b)
    rope_tabs = _rope_tables(ts)
    ones = (jnp.arange(B_WIDTH)[:, None] // HEAD_DIM == jnp.arange(B_WIDTH)[None] // HEAD_DIM).astype(BF16)

    xp = x_prompt.reshape(bp * tp, D_MODEL)
    xs = x_sample.reshape(bs * ts, D_MODEL)

    def mod_rows(layer, rows):
        m = mods[layer, rows[0]:rows[1]][:, None, :]
        return m[..., :D_MODEL], m[..., D_MODEL:2 * D_MODEL], m[..., 2 * D_MODEL:]

    e = 0
    w_in0 = even_w_in[e].astype(BF16)
    w_out0 = w_out[0].astype(BF16)
    gain_pre = norm_pre[0][None]
    gain_post = norm_post[0][None]
    zpad = jnp.zeros((DECAY_LORA, B_WIDTH), F32)
    lora = jnp.concatenate([
        jnp.concatenate([b_w2[e, 0], b_w2[e, 1], zpad, zpad], 1),
        jnp.concatenate([zpad, zpad, b_a2[e, 0], b_a2[e, 1]], 1)], 0).astype(BF16)
    feat_args = (b_mu[e][None], lora, b_w0[e][:, None, :], b_a0[e][:, None, :], b_kk[e][None],
                 b_ka[e][None], b_rk[e][None], ones)

    def even_layer(x2d, bn, t, rows, ctx):
        shift, scale, gate = mod_rows(0, rows)
        (qa, ka, va, ga, gb, r, kkn, v, lw, b, ke, bonus, *kv_t) = _even_in(
            x2d, shift, scale, gain_pre, w_in0, t, None if ctx is None else rope_tabs, ctx is None,
            *feat_args)
        if ctx is None:
            ya = _gqa_context(a_sink[e], qa, ka, va, bn, t)
            s0 = None
        else:
            k_ctx, v_ctx, st0 = ctx
            ya = _gqa_window(a_sink[e], qa, ka, va, k_ctx, v_ctx, bn, t)
            s0 = _pair_states(st0)
        yf, yb, *s_fin = _rwkv_scan(r, kkn, v, lw, b, ke, s0, bn, t)
        y = _even_out(ya.reshape(bn * t, A_WIDTH), ga, yf.reshape(bn * t, B_WIDTH),
                      yb.reshape(bn * t, B_WIDTH), bonus, gb,
                      b_ln_w[e][None], b_ln_b[e][None], ones, x2d, gate, gain_post, w_out0, t)
        return y, kv_t, s_fin

    yp, (ka_t, va_t), st_p = even_layer(xp, bp, tp, (0, 1), None)
    ctx_a = (cache_a_k[:, e].reshape(bs, past, A_KV_WIDTH), cache_a_v[:, e].reshape(bs, past, A_KV_WIDTH),
             state_rwkv[:, e])
    ys, _, _ = even_layer(xs, bs, ts, (1, 1 + bs), ctx_a)

    def cache_a(x_t):
        return jnp.transpose(x_t.reshape(bp, A_KV_HEADS, HEAD_DIM, tp), (0, 3, 1, 2))[:, None]

    o = 0
    lam_init = 0.8 - 0.6 * math.exp(-0.3 * 1)
    w_in1 = odd_w_in[o].astype(BF16)
    w_out1 = w_out[1].astype(BF16)
    gain_pre1 = norm_pre[1][None]
    gain_post1 = norm_post[1][None]
    lams = (c_lq1[o][None], c_lk1[o][None], c_lq2[o][None], c_lk2[o][None])
    subln = c_subln[o][None]

    def odd_layer(x2d, bn, t, rows, ctx):
        shift, scale, gate = mod_rows(1, rows)
        qc, kc, vc, gc = _proj_in(x2d, shift, scale, gain_pre1, w_in1, ODD_SEGS, (0, 1), t,
                                  None if ctx is None else rope_tabs)
        if ctx is None:
            y = _diff_context_out(lams, subln, qc, kc, vc, gc, x2d, gate, gain_post1, w_out1, bn, t, lam_init)
        else:
            k_ctx, v_ctx = ctx
            oc = _diff_latent(lams, subln, qc, kc, vc, k_ctx, v_ctx, bn, t, lam_init)
            y = _odd_out(oc.reshape(bn * t, C_WIDTH), gc, x2d, gate, gain_post1, w_out1, t)
        return y, kc, vc

    yp2, kc_p, vc_p = odd_layer(yp, bp, tp, (0, 1), None)
    ctx_c = (cache_c_k[:, o].reshape(bs, past, C_WIDTH), cache_c_v[:, o].reshape(bs, past, C_WIDTH))
    ys2, _, _ = odd_layer(ys, bs, ts, (1, 1 + bs), ctx_c)

    return (yp2.reshape(bp, tp, D_MODEL),
            ys2.reshape(bs, ts, D_MODEL),
            cache_a(ka_t),
            cache_a(va_t),
            st_p[0][:, None],
            kc_p.reshape(bp, 1, tp, C_HEADS, 2 * C_QK_DIM),
            vc_p.reshape(bp, 1, tp, C_HEADS, C_V_DIM))
```

```python
import functools
import math

import jax
import jax.numpy as jnp
from jax import lax
from jax.experimental import pallas as pl
from jax.experimental.pallas import tpu as pltpu

F32 = jnp.float32
BF16 = jnp.bfloat16

D_MODEL = 1024
DEPTH = 2
GRID_W = 64
HEAD_DIM = 64
ROPE_BASE = 10000.0
A_HEADS = 8
A_KV_HEADS = 2
A_WIDTH = 512
A_KV_WIDTH = 128
WINDOW = 128
BLOCK = 128
B_HEADS = 8
B_WIDTH = 512
DECAY_LORA = 64
AAA_LORA = 64
B_SHIFT_WIDTH = 3 * B_WIDTH + DECAY_LORA + AAA_LORA
C_HEADS = 8
C_QK_DIM = 64
C_V_DIM = 128
C_WIDTH = 1024
EVEN_IN = 3456
ODD_IN = 4096
NORM_EPS = 1e-6
RWKV_LN_EPS = 64e-5
SUBLN_EPS = 1e-5
NEG_INF = -1e30

LANES = 128
ROW_TILE = 256
OUT_TILE = 512
DIFF_Q_TILE = 1024
DIFF_Q_SUB = 128
CHUNK = 64
SCAN_SEQS = 4
PAIR = 2 * CHUNK
VMEM_LIMIT = 48 * 1024 * 1024

LOG2E = math.log2(math.e)
QUERY_SCALE = HEAD_DIM ** -0.5 * LOG2E

NN = ((1,), (0,))
NT = ((1,), (1,))


def _bdot(a, b, dims=NN):
    return lax.dot_general(a.astype(BF16), b.astype(BF16), (dims, ((), ())),
                           preferred_element_type=F32)


def _hdot(a, b):
    return lax.dot_general(a, b, (NN, ((), ())), precision=lax.Precision.HIGHEST,
                           preferred_element_type=F32)


def _split_dot(sel_bf16, x):
    x1 = x.astype(BF16)
    r1 = x - x1.astype(F32)
    x2 = r1.astype(BF16)
    x3 = (r1 - x2.astype(F32)).astype(BF16)
    dot = lambda p: jnp.dot(sel_bf16, p, preferred_element_type=F32)
    return dot(x1) + dot(x2) + dot(x3)


def _head_sum(x, ones_bf16):
    hi = x.astype(BF16)
    lo = (x - hi.astype(F32)).astype(BF16)
    return (jnp.dot(hi, ones_bf16, preferred_element_type=F32)
            + jnp.dot(lo, ones_bf16, preferred_element_type=F32))


def _sigmoid(x):
    return 1.0 / (1.0 + jnp.exp(-x))


def _silu(x):
    return x * _sigmoid(x)


def _params(n_axes):
    return pltpu.CompilerParams(dimension_semantics=("arbitrary",) * n_axes,
                                vmem_limit_bytes=VMEM_LIMIT)


def _lane_lo(shape):
    return (lax.broadcasted_iota(jnp.int32, shape, len(shape) - 1) % LANES) < HEAD_DIM


def _stack_pair(x):
    lo = _lane_lo(x.shape)
    z = jnp.zeros_like(x)
    return jnp.concatenate([jnp.where(lo, x, z), jnp.where(lo, z, x)], axis=0)


def _dup_half(x, g):
    lo = _lane_lo(x.shape)
    sw = pltpu.roll(x, HEAD_DIM, 1)
    return jnp.where(lo, x, sw) if g == 0 else jnp.where(lo, sw, x)


def _rope(x, cos, sin_signed):
    out = []
    even_q = (lax.broadcasted_iota(jnp.int32, cos.shape, 1) // (HEAD_DIM // 4)) % 2 == 0
    for g in range(x.shape[1] // LANES):
        xg = x[:, g * LANES:(g + 1) * LANES]
        up = pltpu.roll(xg, LANES - HEAD_DIM // 4, 1)
        dn = pltpu.roll(xg, HEAD_DIM // 4, 1)
        out.append(xg * cos + jnp.where(even_q, up, dn) * sin_signed)
    return out


def _mod_kernel(c_ref, w_ref, b_ref, o_ref):
    s = _silu(c_ref[...])
    o_ref[...] = _hdot(s, w_ref[...]) + b_ref[...]


def _modulation(cvecs, ada_w, ada_b):
    nrow = cvecs.shape[0]
    return pl.pallas_call(
        _mod_kernel,
        grid=(DEPTH, 3),
        in_specs=[pl.BlockSpec((nrow, D_MODEL), lambda l, j: (0, 0)),
                  pl.BlockSpec((None, D_MODEL, D_MODEL), lambda l, j: (l, 0, j)),
                  pl.BlockSpec((None, 1, D_MODEL), lambda l, j: (l, 0, j))],
        out_specs=pl.BlockSpec((None, nrow, D_MODEL), lambda l, j: (l, 0, j)),
        out_shape=jax.ShapeDtypeStruct((DEPTH, nrow, 3 * D_MODEL), F32),
        compiler_params=_params(2),
        name="modulation",
    )(cvecs, ada_w, ada_b.reshape(DEPTH, 1, 3 * D_MODEL))


def _mod_index(mod, tiles_per_seq):
    if mod.shape[0] == 1:
        return lambda i: (0, 0, 0)
    return lambda i: (i // tiles_per_seq, 0, 0)


def _proj_in_kernel(*refs, segs, rope_segs, use_rope, head_segs):
    if use_rope:
        x_ref, sh_ref, sc_ref, g_ref, w_ref, cos_ref, sin_ref = refs[:7]
        outs = refs[7:]
    else:
        x_ref, sh_ref, sc_ref, g_ref, w_ref = refs[:5]
        outs = refs[5:]
    x = x_ref[...]
    y = x * lax.rsqrt(jnp.mean(x * x, -1, keepdims=True) + NORM_EPS)
    h = (y * g_ref[...]) * (1.0 + sc_ref[...]) + sh_ref[...]
    hb = h.astype(BF16)
    for idx, (lo, hi) in enumerate(segs):
        o = jnp.dot(hb, w_ref[:, lo:hi], preferred_element_type=F32)
        post = (lambda a: (a * QUERY_SCALE).astype(BF16)) if idx == 0 else (lambda a: a)
        if use_rope and idx in rope_segs:
            parts = _rope(o, cos_ref[...], sin_ref[...])
            for g, part in enumerate(parts):
                outs[idx][:, g * LANES:(g + 1) * LANES] = post(part)
        elif idx in head_segs:
            outs[idx][...] = o.reshape(o.shape[0], (hi - lo) // LANES, LANES)
        else:
            outs[idx][...] = post(o)


def _proj_in(x2d, shift, scale, gain, w_bf16, segs, rope_segs, t, rope_tabs, head_segs=()):
    n = x2d.shape[0]
    tile = OUT_TILE
    assert shift.shape[0] == 1 or t % tile == 0
    nt = n // tile
    tiles_per_seq = max(t // tile, 1)
    use_rope = rope_tabs is not None
    row = lambda i: (i, 0)
    per_b = _mod_index(shift, tiles_per_seq)
    in_specs = [pl.BlockSpec((tile, D_MODEL), row),
                pl.BlockSpec((None, 1, D_MODEL), per_b),
                pl.BlockSpec((None, 1, D_MODEL), per_b),
                pl.BlockSpec((1, D_MODEL), lambda i: (0, 0)),
                pl.BlockSpec(w_bf16.shape, lambda i: (0, 0))]
    args = [x2d, shift, scale, gain, w_bf16]
    if use_rope:
        pos = lambda i: (i % tiles_per_seq, 0)
        in_specs += [pl.BlockSpec((tile, LANES), pos), pl.BlockSpec((tile, LANES), pos)]
        args += list(rope_tabs)
    widths = [hi - lo for lo, hi in segs]
    out_specs, out_shape = [], []
    for idx, w in enumerate(widths):
        if idx in head_segs:
            out_specs.append(pl.BlockSpec((tile, w // LANES, LANES), lambda i: (i, 0, 0)))
            out_shape.append(jax.ShapeDtypeStruct((n, w // LANES, LANES), F32))
        else:
            out_specs.append(pl.BlockSpec((tile, w), row))
            out_shape.append(jax.ShapeDtypeStruct((n, w), BF16 if idx == 0 else F32))
    return pl.pallas_call(
        functools.partial(_proj_in_kernel, segs=segs, rope_segs=rope_segs, use_rope=use_rope,
                          head_segs=tuple(head_segs)),
        grid=(nt,),
        in_specs=in_specs,
        out_specs=out_specs,
        out_shape=out_shape,
        compiler_params=_params(1),
        name="proj_in",
    )(*args)


def _sink_exp_pair(s2, sink_a, sink_b, n):
    rowi = lax.broadcasted_iota(jnp.int32, (2 * n, 1), 0)
    sink2 = jnp.where(rowi < n, sink_a, sink_b) * LOG2E
    m = jnp.maximum(jnp.max(s2, -1, keepdims=True), sink2)
    e = jnp.exp2(s2 - m)
    return e, 1.0 / (jnp.sum(e, -1, keepdims=True) + jnp.exp2(sink2 - m))


def _gqa_ctx_kernel(sink_ref, q_ref, k_ref, v_ref, o_ref):
    t = q_ref.shape[0]
    k = k_ref[...]
    v = v_ref[...]
    kd = [_dup_half(k, g).astype(BF16) for g in range(A_KV_HEADS)]
    vd = [_dup_half(v, g).astype(BF16) for g in range(A_KV_HEADS)]
    s2 = [_bdot(_stack_pair(q_ref[:, p * LANES:(p + 1) * LANES]),
                kd[p // 2], NT) for p in range(A_HEADS // 2)]
    for p in range(A_HEADS // 2):
        e, inv = _sink_exp_pair(s2[p], sink_ref[2 * p], sink_ref[2 * p + 1], t)
        o = _bdot(e, vd[p // 2]) * inv
        o_ref[:, p * LANES:(p + 1) * LANES] = jnp.where(_lane_lo((t, LANES)), o[:t], o[t:])


def _gqa_context(sink, q, k, v, bn, t):
    seq = lambda b: (b, 0, 0)
    return pl.pallas_call(
        _gqa_ctx_kernel,
        grid=(bn,),
        in_specs=[pl.BlockSpec(memory_space=pltpu.SMEM),
                  pl.BlockSpec((None, t, A_WIDTH), seq),
                  pl.BlockSpec((None, t, A_KV_WIDTH), seq),
                  pl.BlockSpec((None, t, A_KV_WIDTH), seq)],
        out_specs=pl.BlockSpec((None, t, A_WIDTH), seq),
        out_shape=jax.ShapeDtypeStruct((bn, t, A_WIDTH), F32),
        compiler_params=_params(1),
        name="gqa_context",
    )(sink, q.reshape(bn, t, A_WIDTH), k.reshape(bn, t, A_KV_WIDTH), v.reshape(bn, t, A_KV_WIDTH))


def _gqa_win_kernel(sink_ref, q_ref, kp_ref, kc_ref, kn_ref, vp_ref, vc_ref, vn_ref,
                    kx_ref, vx_ref, o_ref, *, t):
    nblk = pl.program_id(1)
    nctx = kx_ref.shape[1]
    keys = jnp.concatenate([kx_ref[...].T, kp_ref[...], kc_ref[...], kn_ref[...]], axis=0)
    vals = jnp.concatenate([vx_ref[...].T, vp_ref[...], vc_ref[...], vn_ref[...]], axis=0)
    nkeys = nctx + 3 * BLOCK
    rowi = lax.broadcasted_iota(jnp.int32, (2 * BLOCK, nkeys), 0)
    coli = lax.broadcasted_iota(jnp.int32, (2 * BLOCK, nkeys), 1)
    qpos = nblk * BLOCK + rowi % BLOCK
    kpos = nblk * BLOCK + coli - nctx - BLOCK
    valid = (coli < nctx) | ((jnp.abs(qpos - kpos) <= WINDOW) & (kpos >= 0) & (kpos < t))
    kd = [_dup_half(keys, g).astype(BF16) for g in range(A_KV_HEADS)]
    vd = [_dup_half(vals, g).astype(BF16) for g in range(A_KV_HEADS)]
    s2 = [_bdot(_stack_pair(q_ref[:, p * LANES:(p + 1) * LANES]),
                kd[p // 2], NT) for p in range(A_HEADS // 2)]
    for p in range(A_HEADS // 2):
        e, inv = _sink_exp_pair(jnp.where(valid, s2[p], NEG_INF), sink_ref[2 * p], sink_ref[2 * p + 1], BLOCK)
        o = _bdot(e, vd[p // 2]) * inv
        o_ref[:, p * LANES:(p + 1) * LANES] = jnp.where(_lane_lo((BLOCK, LANES)), o[:BLOCK], o[BLOCK:])


def _gqa_window(sink, q, k, v, k_ctx_t, v_ctx_t, bn, t):
    nb = t // BLOCK
    nctx = k_ctx_t.shape[2]
    cur = lambda b, n: (b, n, 0)
    prev = lambda b, n: (b, jnp.maximum(n - 1, 0), 0)
    nxt = lambda b, n: (b, jnp.minimum(n + 1, nb - 1), 0)
    ctx = lambda b, n: (b, 0, 0)
    kv = lambda f: pl.BlockSpec((None, BLOCK, A_KV_WIDTH), f)
    k3 = k.reshape(bn, t, A_KV_WIDTH)
    v3 = v.reshape(bn, t, A_KV_WIDTH)
    return pl.pallas_call(
        functools.partial(_gqa_win_kernel, t=t),
        grid=(bn, nb),
        in_specs=[pl.BlockSpec(memory_space=pltpu.SMEM),
                  pl.BlockSpec((None, BLOCK, A_WIDTH), cur),
                  kv(prev), kv(cur), kv(nxt), kv(prev), kv(cur), kv(nxt),
                  pl.BlockSpec((None, A_KV_WIDTH, nctx), ctx),
                  pl.BlockSpec((None, A_KV_WIDTH, nctx), ctx)],
        out_specs=pl.BlockSpec((None, BLOCK, A_WIDTH), cur),
        out_shape=jax.ShapeDtypeStruct((bn, t, A_WIDTH), F32),
        compiler_params=_params(2),
        name="gqa_window",
    )(sink, q.reshape(bn, t, A_WIDTH), k3, k3, k3, v3, v3, v3, k_ctx_t, v_ctx_t)


def _even_in_kernel(*refs, use_rope, emit_t, tiles_per_seq):
    x_ref, xp_ref, xn_ref, sh_ref, sc_ref, g_ref, w_ref = refs[:7]
    refs = refs[7:]
    if use_rope:
        cos_ref, sin_ref = refs[:2]
        refs = refs[2:]
    mu_ref, lora_ref, w0_ref, a0_ref, kk_ref, ka_ref, rk_ref, ones_ref = refs[:8]
    (q_out, k_out, v_out, ga_out, gb_out, r_out, kkn_out, vv_out, lw_out, b_out, ke_out,
     bonus_out) = refs[8:20]
    i = pl.program_id(0)
    tm = x_ref.shape[0]
    pos = i % tiles_per_seq

    def norm_mod(x):
        y = x * lax.rsqrt(jnp.mean(x * x, -1, keepdims=True) + NORM_EPS)
        return (y * g_ref[...]) * (1.0 + sc_ref[...]) + sh_ref[...]

    h = norm_mod(x_ref[...])
    hb = h.astype(BF16)
    seg = dict(zip(("q", "k", "v", "ga", "pb", "gb"), EVEN_SEGS))
    proj = lambda name, lhs=hb: jnp.dot(lhs, w_ref[:, seg[name][0]:seg[name][1]], preferred_element_type=F32)
    h_ext = jnp.concatenate([norm_mod(xp_ref[...]), h, norm_mod(xn_ref[...])], 0).astype(BF16)
    p_ext = proj("pb", h_ext)

    q = proj("q")
    if use_rope:
        q = jnp.concatenate(_rope(q, cos_ref[...], sin_ref[...]), 1)
    q_out[...] = (q * QUERY_SCALE).astype(BF16)

    p = p_ext[8:8 + tm]
    p_ext = jnp.concatenate([jnp.where(pos != 0, p_ext[:8], 0.0), p,
                             jnp.where(pos != tiles_per_seq - 1, p_ext[8 + tm:], 0.0)], 0)
    prev = pltpu.roll(p_ext, 1, 0)[8:8 + tm]
    nxt = pltpu.roll(p_ext, tm + 15, 0)[8:8 + tm]
    xf = p + (0.5 * (prev + nxt) - p) * mu_ref[...]
    r = xf[:, 0:B_WIDTH]
    k = xf[:, B_WIDTH:2 * B_WIDTH]
    v = xf[:, 2 * B_WIDTH:3 * B_WIDTH]
    g = xf[:, 3 * B_WIDTH:]
    r_out[...] = r
    vv_out[...] = v

    k_att = proj("k")
    if use_rope:
        k_att = _rope(k_att, cos_ref[...], sin_ref[...])[0]
    k_out[...] = k_att
    v_att = proj("v")
    v_out[...] = v_att
    if emit_t:
        refs[20][...] = k_att.T
        refs[21][...] = v_att.T

    g = jnp.where(_lane_lo(g.shape), jnp.tanh(g), g)
    lo = _bdot(g, lora_ref[...])
    ones = ones_ref[...]
    kkr = k * kk_ref[...]
    kkn = kkr / jnp.maximum(jnp.sqrt(_head_sum(kkr * kkr, ones)), 1e-12)
    kkn_out[...] = kkn

    ga_out[...] = proj("ga")

    ke_sum = jnp.zeros_like(k)
    for z in range(2):
        wz = w0_ref[z] + lo[:, z * B_WIDTH:(z + 1) * B_WIDTH]
        lw_out[z] = -math.exp(-0.5) * _sigmoid(wz)
        az = _sigmoid(a0_ref[z] + lo[:, (2 + z) * B_WIDTH:(3 + z) * B_WIDTH])
        ke = k * (1.0 + (az - 1.0) * ka_ref[...])
        ke_out[z] = ke
        b_out[z] = kkn * az
        ke_sum = ke_sum + ke
        if z == 0:
            gb_out[...] = proj("gb")
    bonus_out[...] = _head_sum(r * (0.5 * ke_sum) * rk_ref[...], ones) * v


def _even_in(x2d, shift, scale, gain, w_bf16, t, rope_tabs, emit_t, mu, lora, w0, a0, k_k, k_a, r_k, ones):
    n = x2d.shape[0]
    tile = ROW_TILE if emit_t else OUT_TILE
    assert t % tile == 0 and (not emit_t or t == tile)
    nt = n // tile
    tiles_per_seq = t // tile
    h8 = tile // 8
    use_rope = rope_tabs is not None
    row = lambda i: (i, 0)
    const2 = lambda i: (0, 0)
    const3 = lambda i: (0, 0, 0)
    per_b = _mod_index(shift, tiles_per_seq)
    vec = pl.BlockSpec((1, B_WIDTH), const2)
    in_specs = [pl.BlockSpec((tile, D_MODEL), row),
                pl.BlockSpec((8, D_MODEL), lambda i: (jnp.maximum(i * h8 - 1, 0), 0)),
                pl.BlockSpec((8, D_MODEL), lambda i: (jnp.minimum((i + 1) * h8, n // 8 - 1), 0)),
                pl.BlockSpec((None, 1, D_MODEL), per_b),
                pl.BlockSpec((None, 1, D_MODEL), per_b),
                pl.BlockSpec((1, D_MODEL), const2),
                pl.BlockSpec(w_bf16.shape, const2)]
    args = [x2d, x2d, x2d, shift, scale, gain, w_bf16]
    if use_rope:
        pos = lambda i: (i % tiles_per_seq, 0)
        in_specs += [pl.BlockSpec((tile, LANES), pos), pl.BlockSpec((tile, LANES), pos)]
        args += list(rope_tabs)
    in_specs += [pl.BlockSpec((1, B_SHIFT_WIDTH), const2),
                 pl.BlockSpec(lora.shape, const2),
                 pl.BlockSpec((2, 1, B_WIDTH), const3),
                 pl.BlockSpec((2, 1, B_WIDTH), const3),
                 vec, vec, vec,
                 pl.BlockSpec((B_WIDTH, B_WIDTH), const2)]
    args += [mu, lora, w0, a0, k_k, k_a, r_k, ones]
    wide = pl.BlockSpec((tile, B_WIDTH), row)
    narrow = pl.BlockSpec((tile, A_KV_WIDTH), row)
    dual = pl.BlockSpec((2, tile, B_WIDTH), lambda i: (0, i, 0))
    f_wide = jax.ShapeDtypeStruct((n, B_WIDTH), F32)
    f_narrow = jax.ShapeDtypeStruct((n, A_KV_WIDTH), F32)
    f_dual = jax.ShapeDtypeStruct((2, n, B_WIDTH), F32)
    out_specs = [wide, narrow, narrow, wide, wide, wide, wide, wide, dual, dual, dual, wide]
    out_shape = [jax.ShapeDtypeStruct((n, A_WIDTH), BF16), f_narrow, f_narrow, f_wide, f_wide,
                 f_wide, f_wide, f_wide, f_dual, f_dual, f_dual, f_wide]
    if emit_t:
        out_specs += [pl.BlockSpec((None, A_KV_WIDTH, tile), lambda i: (i, 0, 0))] * 2
        out_shape += [jax.ShapeDtypeStruct((nt, A_KV_WIDTH, tile), F32)] * 2
    return pl.pallas_call(
        functools.partial(_even_in_kernel, use_rope=use_rope, emit_t=emit_t, tiles_per_seq=tiles_per_seq),
        grid=(nt,),
        in_specs=in_specs,
        out_specs=out_specs,
        out_shape=out_shape,
        compiler_params=_params(1),
        name="even_in",
    )(*args)


def _rwkv_scan_kernel(*refs, n_chunks, from_zero):
    (rf_ref, kf_ref, vf_ref, rb_ref, kb_ref, vb_ref, lwf_ref, bf_ref, kef_ref,
     lwb_ref, bb_ref, keb_ref) = refs[:12]
    if from_zero:
        yf_ref, yb_ref, sout_ref, s_scr = refs[12:]
    else:
        s0_ref, yf_ref, yb_ref, s_scr = refs[12:]
    c = pl.program_id(1)

    @pl.when(c == 0)
    def _():
        if from_zero:
            s_scr[...] = jnp.zeros_like(s_scr)
        else:
            s_scr[...] = s0_ref[...]

    row = lax.broadcasted_iota(jnp.int32, (CHUNK, LANES), 0)
    li = lax.broadcasted_iota(jnp.int32, (CHUNK, LANES), 1) % CHUNK
    blk16 = (row // 16) == (li // 16)
    blk32 = (row // 32) == (li // 32)
    eye = jnp.where(row == li, 1.0, 0.0)
    srow = lax.broadcasted_iota(jnp.int32, (PAIR, PAIR), 0)
    scol = lax.broadcasted_iota(jnp.int32, (PAIR, PAIR), 1)
    same_head = (srow // CHUNK) == (scol // CHUNK)
    tr = lax.broadcasted_iota(jnp.int32, (CHUNK, CHUNK), 0)
    tc = lax.broadcasted_iota(jnp.int32, (CHUNK, CHUNK), 1)
    dirs = ((rf_ref, kf_ref, vf_ref, lwf_ref, bf_ref, kef_ref, yf_ref),
            (rb_ref, kb_ref, vb_ref, lwb_ref, bb_ref, keb_ref, yb_ref))

    def advance(s):
        chains = [(z, p) for z in range(2) for p in range(B_HEADS // 2)]
        sls = [slice(p * LANES, (p + 1) * LANES) for _, p in chains]
        before, before_eq, tot, kkd, rd, bg, kg, vv, sc = [], [], [], [], [], [], [], [], []
        for z in range(2):
            r_ref, kk_ref, v_ref, lw_ref, b_ref, ke_ref, _ = dirs[z]
            diff = (li - row) if z else (row - li)
            tri = jnp.where(((tc - tr) if z else (tr - tc)) >= 0, 1.0, 0.0).astype(BF16)
            lw_all = lw_ref[s]
            cum_all = _split_dot(tri, lw_all)
            for p in range(B_HEADS // 2):
                sl = slice(p * LANES, (p + 1) * LANES)
                lw = lw_all[:, sl]
                cum = cum_all[:, sl]
                tt = cum[0:1] if z else cum[CHUNK - 1:CHUNK]
                b = b_ref[s, :, sl]
                ke = ke_ref[s, :, sl]
                g_inv = jnp.exp(-cum)
                g_rest = jnp.exp(tt - cum)
                before.append(diff > 0)
                before_eq.append(diff >= 0)
                tot.append(tt)
                kkd.append(kk_ref[s, :, sl] * jnp.exp(cum - lw))
                rd.append(r_ref[s, :, sl] * jnp.exp(cum))
                bg.append(b * g_rest)
                kg.append(ke * g_rest)
                vv.append(v_ref[s, :, sl])
                sc.append(_bdot(jnp.concatenate([kkd[-1], rd[-1]], 0),
                                jnp.concatenate([_stack_pair(b * g_inv), _stack_pair(ke * g_inv)], 0), NT))
            yield
        lb = [jnp.where(m, x[:CHUNK, :LANES], 0.0) for m, x in zip(before, sc)]
        mb = [jnp.where(m, x[CHUNK:, :LANES], 0.0) for m, x in zip(before_eq, sc)]
        lv = [_bdot(jnp.concatenate([jnp.where(m, x[:CHUNK, LANES:], 0.0),
                                     jnp.where(me, x[CHUNK:, LANES:], 0.0)], 0), _stack_pair(v))
              for m, me, x, v in zip(before, before_eq, sc, vv)]
        yield

        pw = [jnp.where(blk16, x, 0.0) for x in lb]
        tinv = [eye - x for x in pw]
        for _ in range(3):
            pw = [_bdot(x, _stack_pair(x)) for x in pw]
            yield
            tinv = [t + _bdot(t, _stack_pair(x)) for t, x in zip(tinv, pw)]
            yield
        for off in (blk32 & ~blk16, ~blk32):
            tmp = [_bdot(t, _stack_pair(jnp.where(off, x, 0.0))) for t, x in zip(tinv, lb)]
            yield
            tinv = [t - _bdot(m, _stack_pair(t)) for t, m in zip(tinv, tmp)]
            yield

        w12 = [_bdot(t, jnp.concatenate([_stack_pair(k), _stack_pair(x[:CHUNK])], 1))
               for t, k, x in zip(tinv, kkd, lv)]
        yield
        s_old = [s_scr[s, z, p] for z, p in chains]
        ws = [_bdot(jnp.concatenate([w[:, :LANES], r], 0), st, NT) for w, r, st in zip(w12, rd, s_old)]
        yield
        u = [-(a[:CHUNK] + w[:, LANES:]) for a, w in zip(ws, w12)]
        mbu = [_bdot(m, _stack_pair(x)) for m, x in zip(mb, u)]
        upd = [_bdot(jnp.concatenate([x, v], 0).T, jnp.concatenate([b_, k_], 0))
               for x, v, b_, k_ in zip(u, vv, bg, kg)]
        yield
        for i, (z, p) in enumerate(chains):
            dirs[z][6][s, :, sls[i]] = ws[i][CHUNK:] + mbu[i] + lv[i][CHUNK:]
            s_scr[s, z, p] = s_old[i] * jnp.exp(tot[i]) + jnp.where(same_head, upd[i], 0.0)

    live = [advance(s) for s in range(rf_ref.shape[0])]
    while live:
        live = [gen for gen in live if next(gen, "done") != "done"]

    if from_zero:
        @pl.when(c == n_chunks - 1)
        def _():
            for s in range(s_scr.shape[0]):
                for z in range(2):
                    for p in range(B_HEADS // 2):
                        st = s_scr[s, z, p]
                        sout_ref[s, z, 2 * p] = st[:HEAD_DIM, :HEAD_DIM]
                        sout_ref[s, z, 2 * p + 1] = pltpu.roll(st[HEAD_DIM:, :], HEAD_DIM, 1)[:, :HEAD_DIM]


def _rwkv_scan(r, kkn, v, lw, b, ke, s0, bn, t):
    nc = t // CHUNK
    npair = B_HEADS // 2
    ns = min(SCAN_SEQS, bn)
    fwd = pl.BlockSpec((ns, CHUNK, B_WIDTH), lambda g, c: (g, c, 0))
    bwd = pl.BlockSpec((ns, CHUNK, B_WIDTH), lambda g, c: (g, nc - 1 - c, 0))
    dfwd = pl.BlockSpec((None, ns, CHUNK, B_WIDTH), lambda g, c: (0, g, c, 0))
    dbwd = pl.BlockSpec((None, ns, CHUNK, B_WIDTH), lambda g, c: (1, g, nc - 1 - c, 0))
    r3, k3, v3 = (a.reshape(bn, t, B_WIDTH) for a in (r, kkn, v))
    lw4, b4, ke4 = (a.reshape(2, bn, t, B_WIDTH) for a in (lw, b, ke))
    seq = jax.ShapeDtypeStruct((bn, t, B_WIDTH), F32)
    in_specs = [fwd, fwd, fwd, bwd, bwd, bwd, dfwd, dfwd, dfwd, dbwd, dbwd, dbwd]
    args = [r3, k3, v3, r3, k3, v3, lw4, b4, ke4, lw4, b4, ke4]
    out_specs = [fwd, bwd]
    out_shape = [seq, seq]
    if s0 is None:
        out_specs.append(pl.BlockSpec((ns, 2, B_HEADS, HEAD_DIM, HEAD_DIM), lambda g, c: (g, 0, 0, 0, 0)))
        out_shape.append(jax.ShapeDtypeStruct((bn, 2, B_HEADS, HEAD_DIM, HEAD_DIM), F32))
    else:
        in_specs.append(pl.BlockSpec((ns, 2, npair, PAIR, PAIR), lambda g, c: (g, 0, 0, 0, 0)))
        args.append(s0)
    return pl.pallas_call(
        functools.partial(_rwkv_scan_kernel, n_chunks=nc, from_zero=s0 is None),
        grid=(bn // ns, nc),
        in_specs=in_specs,
        out_specs=out_specs,
        out_shape=out_shape,
        scratch_shapes=[pltpu.VMEM((ns, 2, npair, PAIR, PAIR), F32)],
        compiler_params=_params(2),
        name="rwkv_scan",
    )(*args)


def _post(x, z, gate, gain):
    zn = z * lax.rsqrt(jnp.mean(z * z, -1, keepdims=True) + NORM_EPS)
    return x + gate * (zn * gain)


def _even_out_kernel(ya_ref, ga_ref, yf_ref, yb_ref, bonus_ref, gb_ref, lnw_ref, lnb_ref, ones_ref,
                     x_ref, gate_ref, gain_ref, w_ref, o_ref):
    y = yf_ref[...] + yb_ref[...]
    ones = ones_ref[...]
    mean = _head_sum(y, ones) * (1.0 / HEAD_DIM)
    d = y - mean
    var = _head_sum(d * d, ones) * (1.0 / HEAD_DIM)
    yb = (d * lax.rsqrt(var + RWKV_LN_EPS)) * lnw_ref[...] + lnb_ref[...] + bonus_ref[...]
    za = (ya_ref[...] * _silu(ga_ref[...])).astype(BF16)
    zb = (yb * _silu(gb_ref[...])).astype(BF16)
    z = (jnp.dot(za, w_ref[0:A_WIDTH, :], preferred_element_type=F32)
         + jnp.dot(zb, w_ref[A_WIDTH:, :], preferred_element_type=F32))
    o_ref[...] = _post(x_ref[...], z, gate_ref[...], gain_ref[...])


def _even_out(ya, ga, yf, yb, bonus, gb, ln_w, ln_b, ones, x2d, gate, gain, w_bf16, t):
    n = x2d.shape[0]
    row = lambda i: (i, 0)
    const2 = lambda i: (0, 0)
    half = pl.BlockSpec((OUT_TILE, B_WIDTH), row)
    full = pl.BlockSpec((OUT_TILE, D_MODEL), row)
    vec = pl.BlockSpec((1, B_WIDTH), const2)
    assert gate.shape[0] == 1 or t % OUT_TILE == 0
    return pl.pallas_call(
        _even_out_kernel,
        grid=(n // OUT_TILE,),
        in_specs=[half, half, half, half, half, half, vec, vec,
                  pl.BlockSpec((B_WIDTH, B_WIDTH), const2),
                  full,
                  pl.BlockSpec((None, 1, D_MODEL), _mod_index(gate, t // OUT_TILE)),
                  pl.BlockSpec((1, D_MODEL), const2),
                  pl.BlockSpec((D_MODEL, D_MODEL), const2)],
        out_specs=full,
        out_shape=jax.ShapeDtypeStruct((n, D_MODEL), F32),
        compiler_params=_params(1),
        name="even_out",
    )(ya, ga, yf, yb, bonus, gb, ln_w, ln_b, ones, x2d, gate, gain, w_bf16)


def _odd_out_kernel(o_ref_in, gc_ref, x_ref, gate_ref, gain_ref, w_ref, o_ref):
    zc = (o_ref_in[...] * _silu(gc_ref[...])).astype(BF16)
    z = jnp.dot(zc, w_ref[...], preferred_element_type=F32)
    o_ref[...] = _post(x_ref[...], z, gate_ref[...], gain_ref[...])


def _odd_out(o, gc, x2d, gate, gain, w_bf16, t):
    n = x2d.shape[0]
    row = lambda i: (i, 0)
    const2 = lambda i: (0, 0)
    full = pl.BlockSpec((OUT_TILE, D_MODEL), row)
    assert gate.shape[0] == 1 or t % OUT_TILE == 0
    return pl.pallas_call(
        _odd_out_kernel,
        grid=(n // OUT_TILE,),
        in_specs=[full, full, full,
                  pl.BlockSpec((None, 1, D_MODEL), _mod_index(gate, t // OUT_TILE)),
                  pl.BlockSpec((1, D_MODEL), const2),
                  pl.BlockSpec((D_MODEL, D_MODEL), const2)],
        out_specs=full,
        out_shape=jax.ShapeDtypeStruct((n, D_MODEL), F32),
        compiler_params=_params(1),
        name="odd_out",
    )(o, gc, x2d, gate, gain, w_bf16)


def _lambda(lq1_ref, lk1_ref, lq2_ref, lk2_ref, lam_init):
    s1 = jnp.sum(lq1_ref[...] * lk1_ref[...], -1, keepdims=True)
    s2 = jnp.sum(lq2_ref[...] * lk2_ref[...], -1, keepdims=True)
    return jnp.exp(s1) - jnp.exp(s2) + lam_init


def _subln(o, gain, lam_init):
    on = o * lax.rsqrt(jnp.mean(o * o, -1, keepdims=True) + SUBLN_EPS)
    return (on * gain) * (1.0 - lam_init)


def _diff_ctx_kernel(lq1_ref, lk1_ref, lq2_ref, lk2_ref, sub_ref, q_ref, k_ref, v_ref,
                     gc_ref, x_ref, gate_ref, gain_ref, w_ref, y_ref, o_scr, *, lam_init):
    t = q_ref.shape[0]
    lam = _lambda(lq1_ref, lk1_ref, lq2_ref, lk2_ref, lam_init)
    sls = [slice(h * LANES, (h + 1) * LANES) for h in range(C_HEADS)]
    k = k_ref[...].reshape(t, C_WIDTH)
    v = v_ref[...].reshape(t, C_WIDTH)
    s2 = [_bdot(_stack_pair(q_ref[:, sl]), k[:, sl], NT)
          for sl in sls]
    for h, sl in enumerate(sls):
        e = jnp.exp2(s2[h] - jnp.max(s2[h], -1, keepdims=True))
        ov = _bdot(e, v[:, sl]) * (1.0 / jnp.sum(e, -1, keepdims=True))
        o_scr[:, sl] = _subln(ov[:t] - lam * ov[t:], sub_ref[...], lam_init)
    zc = (o_scr[...] * _silu(gc_ref[...])).astype(BF16)
    z = jnp.dot(zc, w_ref[...], preferred_element_type=F32)
    y_ref[...] = _post(x_ref[...], z, gate_ref[...], gain_ref[...])


def _diff_context_out(lams, subln, q, k, v, gc, x2d, gate, gain, w_bf16, bn, t, lam_init):
    seq = pl.BlockSpec((None, t, C_WIDTH), lambda b: (b, 0, 0))
    heads = pl.BlockSpec((None, t, C_HEADS, LANES), lambda b: (b, 0, 0, 0))
    small = pl.BlockSpec((1, C_QK_DIM), lambda b: (0, 0))
    const2 = lambda b: (0, 0)
    to3 = lambda a: a.reshape((bn, t) + a.shape[1:])
    return pl.pallas_call(
        functools.partial(_diff_ctx_kernel, lam_init=lam_init),
        grid=(bn,),
        in_specs=[small, small, small, small, pl.BlockSpec((1, C_V_DIM), const2),
                  seq, heads, heads, seq, seq,
                  pl.BlockSpec((None, 1, D_MODEL), lambda b: (0, 0, 0)),
                  pl.BlockSpec((1, D_MODEL), const2),
                  pl.BlockSpec((D_MODEL, D_MODEL), const2)],
        out_specs=seq,
        out_shape=jax.ShapeDtypeStruct((bn, t, D_MODEL), F32),
        scratch_shapes=[pltpu.VMEM((t, C_WIDTH), F32)],
        compiler_params=_params(1),
        name="diff_context",
    )(*lams, subln, to3(q), to3(k), to3(v), to3(gc), to3(x2d), gate, gain, w_bf16).reshape(bn * t, D_MODEL)


def _diff_lat_kernel(lq1_ref, lk1_ref, lq2_ref, lk2_ref, sub_ref, q_ref, k_ref, v_ref, kx_ref, vx_ref,
                     o_ref, *, lam_init):
    lam = _lambda(lq1_ref, lk1_ref, lq2_ref, lk2_ref, lam_init)
    kx = kx_ref[...].astype(BF16)
    k = k_ref[...].astype(BF16)
    tq = DIFF_Q_SUB
    subs = range(q_ref.shape[0] // tq)
    scores = []
    for i in subs:
        qs = _stack_pair(q_ref[i * tq:(i + 1) * tq, :])
        scores.append((_bdot(qs, kx, NT), _bdot(qs, k, NT)))
    for i in subs:
        sx, sl = scores[i]
        m = jnp.maximum(jnp.max(sx, -1, keepdims=True), jnp.max(sl, -1, keepdims=True))
        ex = jnp.exp2(sx - m)
        el = jnp.exp2(sl - m)
        den = jnp.sum(ex, -1, keepdims=True) + jnp.sum(el, -1, keepdims=True)
        ratio = lam * den[:tq] / den[tq:]
        o = (_bdot(ex[:tq] - ex[tq:] * ratio, vx_ref[...])
             + _bdot(el[:tq] - el[tq:] * ratio, v_ref[...])) * (1.0 / den[:tq])
        o_ref[i * tq:(i + 1) * tq, :] = _subln(o, sub_ref[...], lam_init)


def _diff_latent(lams, subln, q, k, v, k_ctx, v_ctx, bn, t, lam_init):
    nb = t // DIFF_Q_TILE
    nctx = k_ctx.shape[1]
    small = pl.BlockSpec((1, C_QK_DIM), lambda b, h, n: (0, 0))
    qspec = pl.BlockSpec((None, DIFF_Q_TILE, LANES), lambda b, h, n: (b, n, h))
    kvspec = pl.BlockSpec((None, t, LANES), lambda b, h, n: (b, 0, h))
    cxspec = pl.BlockSpec((None, nctx, LANES), lambda b, h, n: (b, 0, h))
    return pl.pallas_call(
        functools.partial(_diff_lat_kernel, lam_init=lam_init),
        grid=(bn, C_HEADS, nb),
        in_specs=[small, small, small, small,
                  pl.BlockSpec((1, C_V_DIM), lambda b, h, n: (0, 0)),
                  qspec, kvspec, kvspec, cxspec, cxspec],
        out_specs=qspec,
        out_shape=jax.ShapeDtypeStruct((bn, t, C_WIDTH), F32),
        compiler_params=_params(3),
        name="diff_latent",
    )(*lams, subln, q.reshape(bn, t, C_WIDTH), k.reshape(bn, t, C_WIDTH), v.reshape(bn, t, C_WIDTH),
      k_ctx, v_ctx)


def _rope_tables(t):
    nf = HEAD_DIM // 4
    inv = 1.0 / (ROPE_BASE ** (jnp.arange(nf, dtype=F32) / nf))
    pos = jnp.arange(t)
    ang_r = (pos // GRID_W).astype(F32)[:, None] * inv[None]
    ang_c = (pos % GRID_W).astype(F32)[:, None] * inv[None]
    ang = jnp.concatenate([ang_r, ang_r, ang_c, ang_c], -1)
    cos, sin = jnp.cos(ang), jnp.sin(ang)
    sign = jnp.where((jnp.arange(HEAD_DIM) // nf) % 2 == 0, -1.0, 1.0).astype(F32)
    return jnp.tile(cos, (1, 2)), jnp.tile(sin * sign[None], (1, 2))


EVEN_SEGS = ((0, 512), (512, 640), (640, 768), (768, 1280), (1280, 2944), (2944, 3456))
ODD_SEGS = ((0, 1024), (1024, 2048), (2048, 3072), (3072, 4096))


def _pair_states(st):
    bn = st.shape[0]
    s = st.reshape(bn, 2, 4, 2, HEAD_DIM, HEAD_DIM)
    z = jnp.zeros_like(s[:, :, :, 0])
    top = jnp.concatenate([s[:, :, :, 0], z], -1)
    bot = jnp.concatenate([z, s[:, :, :, 1]], -1)
    return jnp.concatenate([top, bot], -2)


def kernel(x_prompt, x_sample, cache_a_k, cache_a_v, state_rwkv, cache_c_k, cache_c_v, c, c_ctx,
           ada_w, ada_b, norm_pre, norm_post, w_out, even_w_in, a_sink, b_mu, b_w0, b_w2, b_a0,
           b_a2, b_kk, b_ka, b_rk, b_ln_w, b_ln_b, odd_w_in, c_lq1, c_lk1, c_lq2, c_lk2, c_subln):
    bp, tp, _ = x_prompt.shape
    bs, ts, _ = x_sample.shape
    past = cache_a_k.shape[2]

    cvecs = jnp.concatenate([c_ctx[None], c, jnp.zeros((8 - 1 - bs, D_MODEL), F32)], 0)
    mods = _modulation(cvecs, ada_w, ada_b)
    rope_tabs = _rope_tables(ts)
    ones = (jnp.arange(B_WIDTH)[:, None] // HEAD_DIM == jnp.arange(B_WIDTH)[None] // HEAD_DIM).astype(BF16)

    xp = x_prompt.reshape(bp * tp, D_MODEL)
    xs = x_sample.reshape(bs * ts, D_MODEL)

    def mod_rows(layer, rows):
        m = mods[layer, rows[0]:rows[1]][:, None, :]
        return m[..., :D_MODEL], m[..., D_MODEL:2 * D_MODEL], m[..., 2 * D_MODEL:]

    e = 0
    w_in0 = even_w_in[e].astype(BF16)
    w_out0 = w_out[0].astype(BF16)
    gain_pre = norm_pre[0][None]
    gain_post = norm_post[0][None]
    zpad = jnp.zeros((DECAY_LORA, B_WIDTH), F32)
    lora = jnp.concatenate([
        jnp.concatenate([b_w2[e, 0], b_w2[e, 1], zpad, zpad], 1),
        jnp.concatenate([zpad, zpad, b_a2[e, 0], b_a2[e, 1]], 1)], 0).astype(BF16)
    feat_args = (b_mu[e][None], lora, b_w0[e][:, None, :], b_a0[e][:, None, :], b_kk[e][None],
                 b_ka[e][None], b_rk[e][None], ones)

    def even_layer(x2d, bn, t, rows, ctx):
        shift, scale, gate = mod_rows(0, rows)
        (qa, ka, va, ga, gb, r, kkn, v, lw, b, ke, bonus, *kv_t) = _even_in(
            x2d, shift, scale, gain_pre, w_in0, t, None if ctx is None else rope_tabs, ctx is None,
            *feat_args)
        if ctx is None:
            ya = _gqa_context(a_sink[e], qa, ka, va, bn, t)
            s0 = None
        else:
            k_ctx, v_ctx, st0 = ctx
            ya = _gqa_window(a_sink[e], qa, ka, va, k_ctx, v_ctx, bn, t)
            s0 = _pair_states(st0)
        yf, yb, *s_fin = _rwkv_scan(r, kkn, v, lw, b, ke, s0, bn, t)
        y = _even_out(ya.reshape(bn * t, A_WIDTH), ga, yf.reshape(bn * t, B_WIDTH),
                      yb.reshape(bn * t, B_WIDTH), bonus, gb,
                      b_ln_w[e][None], b_ln_b[e][None], ones, x2d, gate, gain_post, w_out0, t)
        return y, kv_t, s_fin

    yp, (ka_t, va_t), st_p = even_layer(xp, bp, tp, (0, 1), None)
    feat_major = lambda a: jnp.transpose(a, (0, 2, 3, 1)).reshape(bs, A_KV_WIDTH, past)
    ctx_a = (feat_major(cache_a_k[:, e]), feat_major(cache_a_v[:, e]), state_rwkv[:, e])
    ys, _, _ = even_layer(xs, bs, ts, (1, 1 + bs), ctx_a)

    def cache_a(x_t):
        return jnp.transpose(x_t.reshape(bp, A_KV_HEADS, HEAD_DIM, tp), (0, 3, 1, 2))[:, None]

    o = 0
    lam_init = 0.8 - 0.6 * math.exp(-0.3 * 1)
    w_in1 = odd_w_in[o].astype(BF16)
    w_out1 = w_out[1].astype(BF16)
    gain_pre1 = norm_pre[1][None]
    gain_post1 = norm_post[1][None]
    lams = (c_lq1[o][None], c_lk1[o][None], c_lq2[o][None], c_lk2[o][None])
    subln = c_subln[o][None]

    def odd_layer(x2d, bn, t, rows, ctx):
        shift, scale, gate = mod_rows(1, rows)
        qc, kc, vc, gc = _proj_in(x2d, shift, scale, gain_pre1, w_in1, ODD_SEGS, (0, 1), t,
                                  None if ctx is None else rope_tabs,
                                  head_segs=(1, 2) if ctx is None else ())
        if ctx is None:
            y = _diff_context_out(lams, subln, qc, kc, vc, gc, x2d, gate, gain_post1, w_out1, bn, t, lam_init)
        else:
            k_ctx, v_ctx = ctx
            oc = _diff_latent(lams, subln, qc, kc, vc, k_ctx, v_ctx, bn, t, lam_init)
            y = _odd_out(oc.reshape(bn * t, C_WIDTH), gc, x2d, gate, gain_post1, w_out1, t)
        return y, kc, vc

    yp2, kc_p, vc_p = odd_layer(yp, bp, tp, (0, 1), None)
    ctx_c = (cache_c_k[:, o].reshape(bs, past, C_WIDTH), cache_c_v[:, o].reshape(bs, past, C_WIDTH))
    ys2, _, _ = odd_layer(ys, bs, ts, (1, 1 + bs), ctx_c)

    return (yp2.reshape(bp, tp, D_MODEL),
            ys2.reshape(bs, ts, D_MODEL),
            cache_a(ka_t),
            cache_a(va_t),
            st_p[0][:, None],
            kc_p.reshape(bp, 1, tp, C_HEADS, 2 * C_QK_DIM),
            vc_p.reshape(bp, 1, tp, C_HEADS, C_V_DIM))
```

```python
import functools
import math

import jax
import jax.numpy as jnp
from jax import lax
from jax.experimental import pallas as pl
from jax.experimental.pallas import tpu as pltpu

F32 = jnp.float32
BF16 = jnp.bfloat16

D_MODEL = 1024
DEPTH = 2
GRID_W = 64
HEAD_DIM = 64
ROPE_BASE = 10000.0
A_HEADS = 8
A_KV_HEADS = 2
A_WIDTH = 512
A_KV_WIDTH = 128
WINDOW = 128
BLOCK = 128
B_HEADS = 8
B_WIDTH = 512
DECAY_LORA = 64
AAA_LORA = 64
B_SHIFT_WIDTH = 3 * B_WIDTH + DECAY_LORA + AAA_LORA
C_HEADS = 8
C_QK_DIM = 64
C_V_DIM = 128
C_WIDTH = 1024
EVEN_IN = 3456
ODD_IN = 4096
NORM_EPS = 1e-6
RWKV_LN_EPS = 64e-5
SUBLN_EPS = 1e-5
NEG_INF = -1e30

LANES = 128
HALO = 8
ROW_TILE = 256
OUT_TILE = 512
GQA_CTX_SEQS = 2
DIFF_Q_TILE = 1024
DIFF_Q_SUB = 128
CHUNK = 64
SCAN_SEQS = 4
PAIR = 2 * CHUNK
VMEM_LIMIT = 48 * 1024 * 1024

LOG2E = math.log2(math.e)
QUERY_SCALE = HEAD_DIM ** -0.5 * LOG2E

NN = ((1,), (0,))
NT = ((1,), (1,))


def _bdot(a, b, dims=NN):
    return lax.dot_general(a.astype(BF16), b.astype(BF16), (dims, ((), ())),
                           preferred_element_type=F32)


def _hdot(a, b):
    return lax.dot_general(a, b, (NN, ((), ())), precision=lax.Precision.HIGHEST,
                           preferred_element_type=F32)


def _split_dot(sel_bf16, x):
    x1 = x.astype(BF16)
    r1 = x - x1.astype(F32)
    x2 = r1.astype(BF16)
    x3 = (r1 - x2.astype(F32)).astype(BF16)
    dot = lambda p: jnp.dot(sel_bf16, p, preferred_element_type=F32)
    return dot(x1) + dot(x2) + dot(x3)


def _head_sum(x, ones_bf16):
    hi = x.astype(BF16)
    lo = (x - hi.astype(F32)).astype(BF16)
    return (jnp.dot(hi, ones_bf16, preferred_element_type=F32)
            + jnp.dot(lo, ones_bf16, preferred_element_type=F32))


def _sigmoid(x):
    return 1.0 / (1.0 + jnp.exp(-x))


def _silu(x):
    return x * _sigmoid(x)


def _params(n_axes):
    return pltpu.CompilerParams(dimension_semantics=("arbitrary",) * n_axes,
                                vmem_limit_bytes=VMEM_LIMIT)


def _lane_lo(shape):
    return (lax.broadcasted_iota(jnp.int32, shape, len(shape) - 1) % LANES) < HEAD_DIM


def _stack_pair(x):
    lo = _lane_lo(x.shape)
    z = jnp.zeros_like(x)
    return jnp.concatenate([jnp.where(lo, x, z), jnp.where(lo, z, x)], axis=0)


def _dup_half(x, g):
    lo = _lane_lo(x.shape)
    sw = pltpu.roll(x, HEAD_DIM, 1)
    return jnp.where(lo, x, sw) if g == 0 else jnp.where(lo, sw, x)


def _rope(x, cos, sin_signed):
    out = []
    even_q = (lax.broadcasted_iota(jnp.int32, cos.shape, 1) // (HEAD_DIM // 4)) % 2 == 0
    for g in range(x.shape[1] // LANES):
        xg = x[:, g * LANES:(g + 1) * LANES]
        up = pltpu.roll(xg, LANES - HEAD_DIM // 4, 1)
        dn = pltpu.roll(xg, HEAD_DIM // 4, 1)
        out.append(xg * cos + jnp.where(even_q, up, dn) * sin_signed)
    return out


def _mod_kernel(c_ref, w_ref, b_ref, o_ref):
    s = _silu(c_ref[...])
    o_ref[...] = _hdot(s, w_ref[...]) + b_ref[...]


def _modulation(cvecs, ada_w, ada_b):
    nrow = cvecs.shape[0]
    return pl.pallas_call(
        _mod_kernel,
        grid=(DEPTH, 3),
        in_specs=[pl.BlockSpec((nrow, D_MODEL), lambda l, j: (0, 0)),
                  pl.BlockSpec((None, D_MODEL, D_MODEL), lambda l, j: (l, 0, j)),
                  pl.BlockSpec((None, 1, D_MODEL), lambda l, j: (l, 0, j))],
        out_specs=pl.BlockSpec((None, nrow, D_MODEL), lambda l, j: (l, 0, j)),
        out_shape=jax.ShapeDtypeStruct((DEPTH, nrow, 3 * D_MODEL), F32),
        compiler_params=_params(2),
        name="modulation",
    )(cvecs, ada_w, ada_b.reshape(DEPTH, 1, 3 * D_MODEL))


def _mod_index(mod, tiles_per_seq):
    if mod.shape[0] == 1:
        return lambda i: (0, 0, 0)
    return lambda i: (i // tiles_per_seq, 0, 0)


def _proj_in_kernel(*refs, segs, rope_segs, use_rope, head_segs, bf16_segs):
    if use_rope:
        x_ref, sh_ref, sc_ref, g_ref, w_ref, cos_ref, sin_ref = refs[:7]
        outs = refs[7:]
    else:
        x_ref, sh_ref, sc_ref, g_ref, w_ref = refs[:5]
        outs = refs[5:]
    x = x_ref[...]
    y = x * lax.rsqrt(jnp.mean(x * x, -1, keepdims=True) + NORM_EPS)
    h = (y * g_ref[...]) * (1.0 + sc_ref[...]) + sh_ref[...]
    hb = h.astype(BF16)
    for idx, (lo, hi) in enumerate(segs):
        o = jnp.dot(hb, w_ref[:, lo:hi], preferred_element_type=F32)
        if idx == 0:
            post = lambda a: (a * QUERY_SCALE).astype(BF16)
        elif idx in bf16_segs:
            post = lambda a: a.astype(BF16)
        else:
            post = lambda a: a
        if use_rope and idx in rope_segs:
            parts = _rope(o, cos_ref[...], sin_ref[...])
            for g, part in enumerate(parts):
                outs[idx][:, g * LANES:(g + 1) * LANES] = post(part)
        elif idx in head_segs:
            outs[idx][...] = o.reshape(o.shape[0], (hi - lo) // LANES, LANES)
        else:
            outs[idx][...] = post(o)


def _proj_in(x2d, shift, scale, gain, w_bf16, segs, rope_segs, t, rope_tabs, head_segs=(), bf16_segs=()):
    n = x2d.shape[0]
    tile = OUT_TILE
    assert shift.shape[0] == 1 or t % tile == 0
    nt = n // tile
    tiles_per_seq = max(t // tile, 1)
    use_rope = rope_tabs is not None
    row = lambda i: (i, 0)
    per_b = _mod_index(shift, tiles_per_seq)
    in_specs = [pl.BlockSpec((tile, D_MODEL), row),
                pl.BlockSpec((None, 1, D_MODEL), per_b),
                pl.BlockSpec((None, 1, D_MODEL), per_b),
                pl.BlockSpec((1, D_MODEL), lambda i: (0, 0)),
                pl.BlockSpec(w_bf16.shape, lambda i: (0, 0))]
    args = [x2d, shift, scale, gain, w_bf16]
    if use_rope:
        pos = lambda i: (i % tiles_per_seq, 0)
        in_specs += [pl.BlockSpec((tile, LANES), pos), pl.BlockSpec((tile, LANES), pos)]
        args += list(rope_tabs)
    widths = [hi - lo for lo, hi in segs]
    out_specs, out_shape = [], []
    for idx, w in enumerate(widths):
        if idx in head_segs:
            out_specs.append(pl.BlockSpec((tile, w // LANES, LANES), lambda i: (i, 0, 0)))
            out_shape.append(jax.ShapeDtypeStruct((n, w // LANES, LANES), F32))
        else:
            out_specs.append(pl.BlockSpec((tile, w), row))
            out_shape.append(jax.ShapeDtypeStruct((n, w), BF16 if idx == 0 or idx in bf16_segs else F32))
    return pl.pallas_call(
        functools.partial(_proj_in_kernel, segs=segs, rope_segs=rope_segs, use_rope=use_rope,
                          head_segs=tuple(head_segs), bf16_segs=tuple(bf16_segs)),
        grid=(nt,),
        in_specs=in_specs,
        out_specs=out_specs,
        out_shape=out_shape,
        compiler_params=_params(1),
        name="proj_in",
    )(*args)


def _sink_exp_pair(s2, sink_a, sink_b, n):
    rowi = lax.broadcasted_iota(jnp.int32, (2 * n, 1), 0)
    sink2 = jnp.where(rowi < n, sink_a, sink_b) * LOG2E
    m = jnp.maximum(jnp.max(s2, -1, keepdims=True), sink2)
    e = jnp.exp2(s2 - m)
    return e, 1.0 / (jnp.sum(e, -1, keepdims=True) + jnp.exp2(sink2 - m))


def _gqa_ctx_kernel(sink_ref, q_ref, k_ref, v_ref, o_ref):
    nseq, t = q_ref.shape[:2]
    pairs = [(s, p) for s in range(nseq) for p in range(A_HEADS // 2)]
    kd = [[_dup_half(k_ref[s], g).astype(BF16) for g in range(A_KV_HEADS)] for s in range(nseq)]
    vd = [[_dup_half(v_ref[s], g).astype(BF16) for g in range(A_KV_HEADS)] for s in range(nseq)]
    s2 = [_bdot(_stack_pair(q_ref[s, :, p * LANES:(p + 1) * LANES]), kd[s][p // 2], NT)
          for s, p in pairs]
    for i, (s, p) in enumerate(pairs):
        e, inv = _sink_exp_pair(s2[i], sink_ref[2 * p], sink_ref[2 * p + 1], t)
        o = _bdot(e, vd[s][p // 2]) * inv
        o_ref[s, :, p * LANES:(p + 1) * LANES] = jnp.where(_lane_lo((t, LANES)), o[:t], o[t:])


def _gqa_context(sink, q, k, v, bn, t):
    seq = lambda b: (b, 0, 0)
    ns = GQA_CTX_SEQS
    return pl.pallas_call(
        _gqa_ctx_kernel,
        grid=(bn // ns,),
        in_specs=[pl.BlockSpec(memory_space=pltpu.SMEM),
                  pl.BlockSpec((ns, t, A_WIDTH), seq),
                  pl.BlockSpec((ns, t, A_KV_WIDTH), seq),
                  pl.BlockSpec((ns, t, A_KV_WIDTH), seq)],
        out_specs=pl.BlockSpec((ns, t, A_WIDTH), seq),
        out_shape=jax.ShapeDtypeStruct((bn, t, A_WIDTH), F32),
        compiler_params=_params(1),
        name="gqa_context",
    )(sink, q.reshape(bn, t, A_WIDTH), k.reshape(bn, t, A_KV_WIDTH), v.reshape(bn, t, A_KV_WIDTH))


def _gqa_win_kernel(sink_ref, q_ref, kp_ref, kc_ref, kn_ref, vp_ref, vc_ref, vn_ref,
                    kx_ref, vx_ref, o_ref, *, t):
    nblk = pl.program_id(1)
    nctx = kx_ref.shape[1]
    keys = jnp.concatenate([kx_ref[...].T, kp_ref[...], kc_ref[...], kn_ref[...]], axis=0)
    vals = jnp.concatenate([vx_ref[...].T, vp_ref[...], vc_ref[...], vn_ref[...]], axis=0)
    nkeys = nctx + 3 * BLOCK
    rowi = lax.broadcasted_iota(jnp.int32, (2 * BLOCK, nkeys), 0)
    coli = lax.broadcasted_iota(jnp.int32, (2 * BLOCK, nkeys), 1)
    qpos = nblk * BLOCK + rowi % BLOCK
    kpos = nblk * BLOCK + coli - nctx - BLOCK
    valid = (coli < nctx) | ((jnp.abs(qpos - kpos) <= WINDOW) & (kpos >= 0) & (kpos < t))
    kd = [_dup_half(keys, g).astype(BF16) for g in range(A_KV_HEADS)]
    vd = [_dup_half(vals, g).astype(BF16) for g in range(A_KV_HEADS)]
    s2 = [_bdot(_stack_pair(q_ref[:, p * LANES:(p + 1) * LANES]),
                kd[p // 2], NT) for p in range(A_HEADS // 2)]
    for p in range(A_HEADS // 2):
        e, inv = _sink_exp_pair(jnp.where(valid, s2[p], NEG_INF), sink_ref[2 * p], sink_ref[2 * p + 1], BLOCK)
        o = _bdot(e, vd[p // 2]) * inv
        o_ref[:, p * LANES:(p + 1) * LANES] = jnp.where(_lane_lo((BLOCK, LANES)), o[:BLOCK], o[BLOCK:])


def _gqa_window(sink, q, k, v, k_ctx_t, v_ctx_t, bn, t):
    nb = t // BLOCK
    nctx = k_ctx_t.shape[2]
    cur = lambda b, n: (b, n, 0)
    prev = lambda b, n: (b, jnp.maximum(n - 1, 0), 0)
    nxt = lambda b, n: (b, jnp.minimum(n + 1, nb - 1), 0)
    ctx = lambda b, n: (b, 0, 0)
    kv = lambda f: pl.BlockSpec((None, BLOCK, A_KV_WIDTH), f)
    k3 = k.reshape(bn, t, A_KV_WIDTH)
    v3 = v.reshape(bn, t, A_KV_WIDTH)
    return pl.pallas_call(
        functools.partial(_gqa_win_kernel, t=t),
        grid=(bn, nb),
        in_specs=[pl.BlockSpec(memory_space=pltpu.SMEM),
                  pl.BlockSpec((None, BLOCK, A_WIDTH), cur),
                  kv(prev), kv(cur), kv(nxt), kv(prev), kv(cur), kv(nxt),
                  pl.BlockSpec((None, A_KV_WIDTH, nctx), ctx),
                  pl.BlockSpec((None, A_KV_WIDTH, nctx), ctx)],
        out_specs=pl.BlockSpec((None, BLOCK, A_WIDTH), cur),
        out_shape=jax.ShapeDtypeStruct((bn, t, A_WIDTH), F32),
        compiler_params=_params(2),
        name="gqa_window",
    )(sink, q.reshape(bn, t, A_WIDTH), k3, k3, k3, v3, v3, v3, k_ctx_t, v_ctx_t)


def _even_in_kernel(*refs, use_rope, emit_t, tiles_per_seq):
    x_ref, xp_ref, xn_ref, sh_ref, sc_ref, g_ref, w_ref = refs[:7]
    refs = refs[7:]
    if use_rope:
        cos_ref, sin_ref = refs[:2]
        refs = refs[2:]
    mu_ref, lora_ref, w0_ref, a0_ref, kk_ref, ka_ref, rk_ref, ones_ref = refs[:8]
    (q_out, k_out, v_out, ga_out, gb_out, r_out, kkn_out, vv_out, lw_out, b_out, ke_out,
     bonus_out) = refs[8:20]
    i = pl.program_id(0)
    tm = x_ref.shape[0]
    pos = i % tiles_per_seq

    def norm_mod(x):
        y = x * lax.rsqrt(jnp.mean(x * x, -1, keepdims=True) + NORM_EPS)
        return (y * g_ref[...]) * (1.0 + sc_ref[...]) + sh_ref[...]

    h = norm_mod(x_ref[...])
    hb = h.astype(BF16)
    seg = dict(zip(("q", "k", "v", "ga", "pb", "gb"), EVEN_SEGS))
    proj = lambda name, lhs=hb: jnp.dot(lhs, w_ref[:, seg[name][0]:seg[name][1]], preferred_element_type=F32)
    h_ext = jnp.concatenate([norm_mod(xp_ref[...]), h, norm_mod(xn_ref[...])], 0).astype(BF16)
    p_ext = proj("pb", h_ext)

    q = proj("q")
    if use_rope:
        q = jnp.concatenate(_rope(q, cos_ref[...], sin_ref[...]), 1)
    q_out[...] = (q * QUERY_SCALE).astype(BF16)

    p = p_ext[HALO:HALO + tm]
    p_ext = jnp.concatenate([jnp.where(pos != 0, p_ext[:HALO], 0.0), p,
                             jnp.where(pos != tiles_per_seq - 1, p_ext[HALO + tm:], 0.0)], 0)
    prev = pltpu.roll(p_ext, 1, 0)[HALO:HALO + tm]
    nxt = pltpu.roll(p_ext, tm + 2 * HALO - 1, 0)[HALO:HALO + tm]
    xf = p + (0.5 * (prev + nxt) - p) * mu_ref[...]
    r = xf[:, 0:B_WIDTH]
    k = xf[:, B_WIDTH:2 * B_WIDTH]
    v = xf[:, 2 * B_WIDTH:3 * B_WIDTH]
    g = xf[:, 3 * B_WIDTH:]
    r_out[...] = r
    vv_out[...] = v

    k_att = proj("k")
    if use_rope:
        k_att = _rope(k_att, cos_ref[...], sin_ref[...])[0]
    k_out[...] = k_att
    v_att = proj("v")
    v_out[...] = v_att
    if emit_t:
        refs[20][...] = k_att.T
        refs[21][...] = v_att.T

    g = jnp.where(_lane_lo(g.shape), jnp.tanh(g), g)
    lo = _bdot(g, lora_ref[...])
    ones = ones_ref[...]
    kkr = k * kk_ref[...]
    kkn = kkr / jnp.maximum(jnp.sqrt(_head_sum(kkr * kkr, ones)), 1e-12)
    kkn_out[...] = kkn

    ga_out[...] = proj("ga")

    ke_sum = jnp.zeros_like(k)
    for z in range(2):
        wz = w0_ref[z] + lo[:, z * B_WIDTH:(z + 1) * B_WIDTH]
        lw_out[z] = -math.exp(-0.5) * _sigmoid(wz)
        az = _sigmoid(a0_ref[z] + lo[:, (2 + z) * B_WIDTH:(3 + z) * B_WIDTH])
        ke = k * (1.0 + (az - 1.0) * ka_ref[...])
        ke_out[z] = ke
        b_out[z] = kkn * az
        ke_sum = ke_sum + ke
        if z == 0:
            gb_out[...] = proj("gb")
    bonus_out[...] = _head_sum(r * (0.5 * ke_sum) * rk_ref[...], ones) * v


def _even_in(x2d, shift, scale, gain, w_bf16, t, rope_tabs, emit_t, mu, lora, w0, a0, k_k, k_a, r_k, ones):
    n = x2d.shape[0]
    tile = ROW_TILE if emit_t else OUT_TILE
    assert t % tile == 0 and (not emit_t or t == tile)
    nt = n // tile
    tiles_per_seq = t // tile
    h8 = tile // HALO
    use_rope = rope_tabs is not None
    row = lambda i: (i, 0)
    const2 = lambda i: (0, 0)
    const3 = lambda i: (0, 0, 0)
    per_b = _mod_index(shift, tiles_per_seq)
    vec = pl.BlockSpec((1, B_WIDTH), const2)
    in_specs = [pl.BlockSpec((tile, D_MODEL), row),
                pl.BlockSpec((HALO, D_MODEL), lambda i: (jnp.maximum(i * h8 - 1, 0), 0)),
                pl.BlockSpec((HALO, D_MODEL), lambda i: (jnp.minimum((i + 1) * h8, n // HALO - 1), 0)),
                pl.BlockSpec((None, 1, D_MODEL), per_b),
                pl.BlockSpec((None, 1, D_MODEL), per_b),
                pl.BlockSpec((1, D_MODEL), const2),
                pl.BlockSpec(w_bf16.shape, const2)]
    args = [x2d, x2d, x2d, shift, scale, gain, w_bf16]
    if use_rope:
        pos = lambda i: (i % tiles_per_seq, 0)
        in_specs += [pl.BlockSpec((tile, LANES), pos), pl.BlockSpec((tile, LANES), pos)]
        args += list(rope_tabs)
    in_specs += [pl.BlockSpec((1, B_SHIFT_WIDTH), const2),
                 pl.BlockSpec(lora.shape, const2),
                 pl.BlockSpec((2, 1, B_WIDTH), const3),
                 pl.BlockSpec((2, 1, B_WIDTH), const3),
                 vec, vec, vec,
                 pl.BlockSpec((B_WIDTH, B_WIDTH), const2)]
    args += [mu, lora, w0, a0, k_k, k_a, r_k, ones]
    wide = pl.BlockSpec((tile, B_WIDTH), row)
    narrow = pl.BlockSpec((tile, A_KV_WIDTH), row)
    dual = pl.BlockSpec((2, tile, B_WIDTH), lambda i: (0, i, 0))
    f_wide = jax.ShapeDtypeStruct((n, B_WIDTH), F32)
    f_narrow = jax.ShapeDtypeStruct((n, A_KV_WIDTH), F32)
    f_dual = jax.ShapeDtypeStruct((2, n, B_WIDTH), F32)
    out_specs = [wide, narrow, narrow, wide, wide, wide, wide, wide, dual, dual, dual, wide]
    out_shape = [jax.ShapeDtypeStruct((n, A_WIDTH), BF16), f_narrow, f_narrow, f_wide, f_wide,
                 f_wide, f_wide, f_wide, f_dual, f_dual, f_dual, f_wide]
    if emit_t:
        out_specs += [pl.BlockSpec((None, A_KV_WIDTH, tile), lambda i: (i, 0, 0))] * 2
        out_shape += [jax.ShapeDtypeStruct((nt, A_KV_WIDTH, tile), F32)] * 2
    return pl.pallas_call(
        functools.partial(_even_in_kernel, use_rope=use_rope, emit_t=emit_t, tiles_per_seq=tiles_per_seq),
        grid=(nt,),
        in_specs=in_specs,
        out_specs=out_specs,
        out_shape=out_shape,
        compiler_params=_params(1),
        name="even_in",
    )(*args)


def _rwkv_scan_kernel(*refs, n_chunks, from_zero):
    (rf_ref, kf_ref, vf_ref, rb_ref, kb_ref, vb_ref, lwf_ref, bf_ref, kef_ref,
     lwb_ref, bb_ref, keb_ref) = refs[:12]
    if from_zero:
        yf_ref, yb_ref, sout_ref, s_scr = refs[12:]
    else:
        s0_ref, yf_ref, yb_ref, s_scr = refs[12:]
    c = pl.program_id(1)

    @pl.when(c == 0)
    def _():
        if from_zero:
            s_scr[...] = jnp.zeros_like(s_scr)
        else:
            s_scr[...] = s0_ref[...]

    row = lax.broadcasted_iota(jnp.int32, (CHUNK, LANES), 0)
    li = lax.broadcasted_iota(jnp.int32, (CHUNK, LANES), 1) % CHUNK
    blk16 = (row // 16) == (li // 16)
    blk32 = (row // 32) == (li // 32)
    eye = jnp.where(row == li, 1.0, 0.0)
    srow = lax.broadcasted_iota(jnp.int32, (PAIR, PAIR), 0)
    scol = lax.broadcasted_iota(jnp.int32, (PAIR, PAIR), 1)
    same_head = (srow // CHUNK) == (scol // CHUNK)
    tr = lax.broadcasted_iota(jnp.int32, (CHUNK, CHUNK), 0)
    tc = lax.broadcasted_iota(jnp.int32, (CHUNK, CHUNK), 1)
    dirs = ((rf_ref, kf_ref, vf_ref, lwf_ref, bf_ref, kef_ref, yf_ref),
            (rb_ref, kb_ref, vb_ref, lwb_ref, bb_ref, keb_ref, yb_ref))

    def advance(s):
        chains = [(z, p) for z in range(2) for p in range(B_HEADS // 2)]
        sls = [slice(p * LANES, (p + 1) * LANES) for _, p in chains]
        before, before_eq, tot, kkd, rd, bg, kg, vv, sc = [], [], [], [], [], [], [], [], []
        for z in range(2):
            r_ref, kk_ref, v_ref, lw_ref, b_ref, ke_ref, _ = dirs[z]
            diff = (li - row) if z else (row - li)
            tri = jnp.where(((tc - tr) if z else (tr - tc)) >= 0, 1.0, 0.0).astype(BF16)
            lw_all = lw_ref[s]
            cum_all = _split_dot(tri, lw_all)
            for p in range(B_HEADS // 2):
                sl = slice(p * LANES, (p + 1) * LANES)
                lw = lw_all[:, sl]
                cum = cum_all[:, sl]
                tt = cum[0:1] if z else cum[CHUNK - 1:CHUNK]
                b = b_ref[s, :, sl]
                ke = ke_ref[s, :, sl]
                g_inv = jnp.exp(-cum)
                g_rest = jnp.exp(tt - cum)
                before.append(diff > 0)
                before_eq.append(diff >= 0)
                tot.append(tt)
                kkd.append(kk_ref[s, :, sl] * jnp.exp(cum - lw))
                rd.append(r_ref[s, :, sl] * jnp.exp(cum))
                bg.append(b * g_rest)
                kg.append(ke * g_rest)
                vv.append(v_ref[s, :, sl])
                sc.append(_bdot(jnp.concatenate([kkd[-1], rd[-1]], 0),
                                jnp.concatenate([_stack_pair(b * g_inv), _stack_pair(ke * g_inv)], 0), NT))
            yield
        lb = [jnp.where(m, x[:CHUNK, :LANES], 0.0) for m, x in zip(before, sc)]
        mb = [jnp.where(m, x[CHUNK:, :LANES], 0.0) for m, x in zip(before_eq, sc)]
        lv = [_bdot(jnp.concatenate([jnp.where(m, x[:CHUNK, LANES:], 0.0),
                                     jnp.where(me, x[CHUNK:, LANES:], 0.0)], 0), _stack_pair(v))
              for m, me, x, v in zip(before, before_eq, sc, vv)]
        yield

        pw = [jnp.where(blk16, x, 0.0) for x in lb]
        tinv = [eye - x for x in pw]
        for _ in range(3):
            pw = [_bdot(x, _stack_pair(x)) for x in pw]
            yield
            tinv = [t + _bdot(t, _stack_pair(x)) for t, x in zip(tinv, pw)]
            yield
        for off in (blk32 & ~blk16, ~blk32):
            tmp = [_bdot(t, _stack_pair(jnp.where(off, x, 0.0))) for t, x in zip(tinv, lb)]
            yield
            tinv = [t - _bdot(m, _stack_pair(t)) for t, m in zip(tinv, tmp)]
            yield

        w12 = [_bdot(t, jnp.concatenate([_stack_pair(k), _stack_pair(x[:CHUNK])], 1))
               for t, k, x in zip(tinv, kkd, lv)]
        yield
        s_old = [s_scr[s, z, p] for z, p in chains]
        ws = [_bdot(jnp.concatenate([w[:, :LANES], r], 0), st, NT) for w, r, st in zip(w12, rd, s_old)]
        yield
        u = [-(a[:CHUNK] + w[:, LANES:]) for a, w in zip(ws, w12)]
        mbu = [_bdot(m, _stack_pair(x)) for m, x in zip(mb, u)]
        upd = [_bdot(jnp.concatenate([x, v], 0).T, jnp.concatenate([b_, k_], 0))
               for x, v, b_, k_ in zip(u, vv, bg, kg)]
        yield
        for i, (z, p) in enumerate(chains):
            dirs[z][6][s, :, sls[i]] = ws[i][CHUNK:] + mbu[i] + lv[i][CHUNK:]
            s_scr[s, z, p] = s_old[i] * jnp.exp(tot[i]) + jnp.where(same_head, upd[i], 0.0)

    live = [advance(s) for s in range(rf_ref.shape[0])]
    while live:
        live = [gen for gen in live if next(gen, "done") != "done"]

    if from_zero:
        @pl.when(c == n_chunks - 1)
        def _():
            for s in range(s_scr.shape[0]):
                for z in range(2):
                    for p in range(B_HEADS // 2):
                        st = s_scr[s, z, p]
                        sout_ref[s, z, 2 * p] = st[:HEAD_DIM, :HEAD_DIM]
                        sout_ref[s, z, 2 * p + 1] = pltpu.roll(st[HEAD_DIM:, :], HEAD_DIM, 1)[:, :HEAD_DIM]


def _rwkv_scan(r, kkn, v, lw, b, ke, s0, bn, t):
    nc = t // CHUNK
    npair = B_HEADS // 2
    ns = min(SCAN_SEQS, bn)
    fwd = pl.BlockSpec((ns, CHUNK, B_WIDTH), lambda g, c: (g, c, 0))
    bwd = pl.BlockSpec((ns, CHUNK, B_WIDTH), lambda g, c: (g, nc - 1 - c, 0))
    dfwd = pl.BlockSpec((None, ns, CHUNK, B_WIDTH), lambda g, c: (0, g, c, 0))
    dbwd = pl.BlockSpec((None, ns, CHUNK, B_WIDTH), lambda g, c: (1, g, nc - 1 - c, 0))
    r3, k3, v3 = (a.reshape(bn, t, B_WIDTH) for a in (r, kkn, v))
    lw4, b4, ke4 = (a.reshape(2, bn, t, B_WIDTH) for a in (lw, b, ke))
    seq = jax.ShapeDtypeStruct((bn, t, B_WIDTH), F32)
    in_specs = [fwd, fwd, fwd, bwd, bwd, bwd, dfwd, dfwd, dfwd, dbwd, dbwd, dbwd]
    args = [r3, k3, v3, r3, k3, v3, lw4, b4, ke4, lw4, b4, ke4]
    out_specs = [fwd, bwd]
    out_shape = [seq, seq]
    if s0 is None:
        out_specs.append(pl.BlockSpec((ns, 2, B_HEADS, HEAD_DIM, HEAD_DIM), lambda g, c: (g, 0, 0, 0, 0)))
        out_shape.append(jax.ShapeDtypeStruct((bn, 2, B_HEADS, HEAD_DIM, HEAD_DIM), F32))
    else:
        in_specs.append(pl.BlockSpec((ns, 2, npair, PAIR, PAIR), lambda g, c: (g, 0, 0, 0, 0)))
        args.append(s0)
    return pl.pallas_call(
        functools.partial(_rwkv_scan_kernel, n_chunks=nc, from_zero=s0 is None),
        grid=(bn // ns, nc),
        in_specs=in_specs,
        out_specs=out_specs,
        out_shape=out_shape,
        scratch_shapes=[pltpu.VMEM((ns, 2, npair, PAIR, PAIR), F32)],
        compiler_params=_params(2),
        name="rwkv_scan",
    )(*args)


def _post(x, z, gate, gain):
    zn = z * lax.rsqrt(jnp.mean(z * z, -1, keepdims=True) + NORM_EPS)
    return x + gate * (zn * gain)


def _even_out_kernel(ya_ref, ga_ref, yf_ref, yb_ref, bonus_ref, gb_ref, lnw_ref, lnb_ref, ones_ref,
                     x_ref, gate_ref, gain_ref, w_ref, o_ref):
    y = yf_ref[...] + yb_ref[...]
    ones = ones_ref[...]
    mean = _head_sum(y, ones) * (1.0 / HEAD_DIM)
    d = y - mean
    var = _head_sum(d * d, ones) * (1.0 / HEAD_DIM)
    yb = (d * lax.rsqrt(var + RWKV_LN_EPS)) * lnw_ref[...] + lnb_ref[...] + bonus_ref[...]
    za = (ya_ref[...] * _silu(ga_ref[...])).astype(BF16)
    zb = (yb * _silu(gb_ref[...])).astype(BF16)
    z = (jnp.dot(za, w_ref[0:A_WIDTH, :], preferred_element_type=F32)
         + jnp.dot(zb, w_ref[A_WIDTH:, :], preferred_element_type=F32))
    o_ref[...] = _post(x_ref[...], z, gate_ref[...], gain_ref[...])


def _even_out(ya, ga, yf, yb, bonus, gb, ln_w, ln_b, ones, x2d, gate, gain, w_bf16, t):
    n = x2d.shape[0]
    row = lambda i: (i, 0)
    const2 = lambda i: (0, 0)
    half = pl.BlockSpec((OUT_TILE, B_WIDTH), row)
    full = pl.BlockSpec((OUT_TILE, D_MODEL), row)
    vec = pl.BlockSpec((1, B_WIDTH), const2)
    assert gate.shape[0] == 1 or t % OUT_TILE == 0
    return pl.pallas_call(
        _even_out_kernel,
        grid=(n // OUT_TILE,),
        in_specs=[half, half, half, half, half, half, vec, vec,
                  pl.BlockSpec((B_WIDTH, B_WIDTH), const2),
                  full,
                  pl.BlockSpec((None, 1, D_MODEL), _mod_index(gate, t // OUT_TILE)),
                  pl.BlockSpec((1, D_MODEL), const2),
                  pl.BlockSpec((D_MODEL, D_MODEL), const2)],
        out_specs=full,
        out_shape=jax.ShapeDtypeStruct((n, D_MODEL), F32),
        compiler_params=_params(1),
        name="even_out",
    )(ya, ga, yf, yb, bonus, gb, ln_w, ln_b, ones, x2d, gate, gain, w_bf16)


def _odd_out_kernel(o_ref_in, gc_ref, x_ref, gate_ref, gain_ref, w_ref, o_ref):
    zc = (o_ref_in[...] * _silu(gc_ref[...])).astype(BF16)
    z = jnp.dot(zc, w_ref[...], preferred_element_type=F32)
    o_ref[...] = _post(x_ref[...], z, gate_ref[...], gain_ref[...])


def _odd_out(o, gc, x2d, gate, gain, w_bf16, t):
    n = x2d.shape[0]
    row = lambda i: (i, 0)
    const2 = lambda i: (0, 0)
    full = pl.BlockSpec((OUT_TILE, D_MODEL), row)
    assert gate.shape[0] == 1 or t % OUT_TILE == 0
    return pl.pallas_call(
        _odd_out_kernel,
        grid=(n // OUT_TILE,),
        in_specs=[full, full, full,
                  pl.BlockSpec((None, 1, D_MODEL), _mod_index(gate, t // OUT_TILE)),
                  pl.BlockSpec((1, D_MODEL), const2),
                  pl.BlockSpec((D_MODEL, D_MODEL), const2)],
        out_specs=full,
        out_shape=jax.ShapeDtypeStruct((n, D_MODEL), F32),
        compiler_params=_params(1),
        name="odd_out",
    )(o, gc, x2d, gate, gain, w_bf16)


def _lambda(lq1_ref, lk1_ref, lq2_ref, lk2_ref, lam_init):
    s1 = jnp.sum(lq1_ref[...] * lk1_ref[...], -1, keepdims=True)
    s2 = jnp.sum(lq2_ref[...] * lk2_ref[...], -1, keepdims=True)
    return jnp.exp(s1) - jnp.exp(s2) + lam_init


def _subln(o, gain, lam_init):
    on = o * lax.rsqrt(jnp.mean(o * o, -1, keepdims=True) + SUBLN_EPS)
    return (on * gain) * (1.0 - lam_init)


def _diff_ctx_kernel(lq1_ref, lk1_ref, lq2_ref, lk2_ref, sub_ref, q_ref, k_ref, v_ref,
                     gc_ref, x_ref, gate_ref, gain_ref, w_ref, y_ref, o_scr, *, lam_init):
    t = q_ref.shape[0]
    lam = _lambda(lq1_ref, lk1_ref, lq2_ref, lk2_ref, lam_init)
    sls = [slice(h * LANES, (h + 1) * LANES) for h in range(C_HEADS)]
    k = k_ref[...].reshape(t, C_WIDTH)
    v = v_ref[...].reshape(t, C_WIDTH)
    s2 = [_bdot(_stack_pair(q_ref[:, sl]), k[:, sl], NT)
          for sl in sls]
    for h, sl in enumerate(sls):
        e = jnp.exp2(s2[h] - jnp.max(s2[h], -1, keepdims=True))
        ov = _bdot(e, v[:, sl]) * (1.0 / jnp.sum(e, -1, keepdims=True))
        o_scr[:, sl] = _subln(ov[:t] - lam * ov[t:], sub_ref[...], lam_init)
    zc = (o_scr[...] * _silu(gc_ref[...])).astype(BF16)
    z = jnp.dot(zc, w_ref[...], preferred_element_type=F32)
    y_ref[...] = _post(x_ref[...], z, gate_ref[...], gain_ref[...])


def _diff_context_out(lams, subln, q, k, v, gc, x2d, gate, gain, w_bf16, bn, t, lam_init):
    seq = pl.BlockSpec((None, t, C_WIDTH), lambda b: (b, 0, 0))
    heads = pl.BlockSpec((None, t, C_HEADS, LANES), lambda b: (b, 0, 0, 0))
    small = pl.BlockSpec((1, C_QK_DIM), lambda b: (0, 0))
    const2 = lambda b: (0, 0)
    to3 = lambda a: a.reshape((bn, t) + a.shape[1:])
    return pl.pallas_call(
        functools.partial(_diff_ctx_kernel, lam_init=lam_init),
        grid=(bn,),
        in_specs=[small, small, small, small, pl.BlockSpec((1, C_V_DIM), const2),
                  seq, heads, heads, seq, seq,
                  pl.BlockSpec((None, 1, D_MODEL), lambda b: (0, 0, 0)),
                  pl.BlockSpec((1, D_MODEL), const2),
                  pl.BlockSpec((D_MODEL, D_MODEL), const2)],
        out_specs=seq,
        out_shape=jax.ShapeDtypeStruct((bn, t, D_MODEL), F32),
        scratch_shapes=[pltpu.VMEM((t, C_WIDTH), F32)],
        compiler_params=_params(1),
        name="diff_context",
    )(*lams, subln, to3(q), to3(k), to3(v), to3(gc), to3(x2d), gate, gain, w_bf16).reshape(bn * t, D_MODEL)


def _diff_lat_kernel(lq1_ref, lk1_ref, lq2_ref, lk2_ref, sub_ref, q_ref, k_ref, v_ref, kx_ref, vx_ref,
                     o_ref, *, lam_init):
    lam = _lambda(lq1_ref, lk1_ref, lq2_ref, lk2_ref, lam_init)
    kx = kx_ref[...].astype(BF16)
    k = k_ref[...].astype(BF16)
    tq = DIFF_Q_SUB
    subs = range(q_ref.shape[0] // tq)
    scores = []
    for i in subs:
        qs = _stack_pair(q_ref[i * tq:(i + 1) * tq, :])
        scores.append((_bdot(qs, kx, NT), _bdot(qs, k, NT)))
    for i in subs:
        sx, sl = scores[i]
        m = jnp.maximum(jnp.max(sx, -1, keepdims=True), jnp.max(sl, -1, keepdims=True))
        ex = jnp.exp2(sx - m)
        el = jnp.exp2(sl - m)
        den = jnp.sum(ex, -1, keepdims=True) + jnp.sum(el, -1, keepdims=True)
        ratio = lam * den[:tq] / den[tq:]
        o = (_bdot(ex[:tq] - ex[tq:] * ratio, vx_ref[...])
             + _bdot(el[:tq] - el[tq:] * ratio, v_ref[...])) * (1.0 / den[:tq])
        o_ref[i * tq:(i + 1) * tq, :] = _subln(o, sub_ref[...], lam_init)


def _diff_latent(lams, subln, q, k, v, k_ctx, v_ctx, bn, t, lam_init):
    nb = t // DIFF_Q_TILE
    nctx = k_ctx.shape[1]
    small = pl.BlockSpec((1, C_QK_DIM), lambda b, h, n: (0, 0))
    qspec = pl.BlockSpec((None, DIFF_Q_TILE, LANES), lambda b, h, n: (b, n, h))
    kvspec = pl.BlockSpec((None, t, LANES), lambda b, h, n: (b, 0, h))
    cxspec = pl.BlockSpec((None, nctx, LANES), lambda b, h, n: (b, 0, h))
    return pl.pallas_call(
        functools.partial(_diff_lat_kernel, lam_init=lam_init),
        grid=(bn, C_HEADS, nb),
        in_specs=[small, small, small, small,
                  pl.BlockSpec((1, C_V_DIM), lambda b, h, n: (0, 0)),
                  qspec, kvspec, kvspec, cxspec, cxspec],
        out_specs=qspec,
        out_shape=jax.ShapeDtypeStruct((bn, t, C_WIDTH), F32),
        compiler_params=_params(3),
        name="diff_latent",
    )(*lams, subln, q.reshape(bn, t, C_WIDTH), k.reshape(bn, t, C_WIDTH), v.reshape(bn, t, C_WIDTH),
      k_ctx, v_ctx)


def _rope_tables(t):
    nf = HEAD_DIM // 4
    inv = 1.0 / (ROPE_BASE ** (jnp.arange(nf, dtype=F32) / nf))
    pos = jnp.arange(t)
    ang_r = (pos // GRID_W).astype(F32)[:, None] * inv[None]
    ang_c = (pos % GRID_W).astype(F32)[:, None] * inv[None]
    ang = jnp.concatenate([ang_r, ang_r, ang_c, ang_c], -1)
    cos, sin = jnp.cos(ang), jnp.sin(ang)
    sign = jnp.where((jnp.arange(HEAD_DIM) // nf) % 2 == 0, -1.0, 1.0).astype(F32)
    return jnp.tile(cos, (1, 2)), jnp.tile(sin * sign[None], (1, 2))


EVEN_SEGS = ((0, 512), (512, 640), (640, 768), (768, 1280), (1280, 2944), (2944, 3456))
ODD_SEGS = ((0, 1024), (1024, 2048), (2048, 3072), (3072, 4096))


def _pair_states(st):
    bn = st.shape[0]
    s = st.reshape(bn, 2, 4, 2, HEAD_DIM, HEAD_DIM)
    z = jnp.zeros_like(s[:, :, :, 0])
    top = jnp.concatenate([s[:, :, :, 0], z], -1)
    bot = jnp.concatenate([z, s[:, :, :, 1]], -1)
    return jnp.concatenate([top, bot], -2)


def kernel(x_prompt, x_sample, cache_a_k, cache_a_v, state_rwkv, cache_c_k, cache_c_v, c, c_ctx,
           ada_w, ada_b, norm_pre, norm_post, w_out, even_w_in, a_sink, b_mu, b_w0, b_w2, b_a0,
           b_a2, b_kk, b_ka, b_rk, b_ln_w, b_ln_b, odd_w_in, c_lq1, c_lk1, c_lq2, c_lk2, c_subln):
    bp, tp, _ = x_prompt.shape
    bs, ts, _ = x_sample.shape
    past = cache_a_k.shape[2]

    cvecs = jnp.concatenate([c_ctx[None], c, jnp.zeros((8 - 1 - bs, D_MODEL), F32)], 0)
    mods = _modulation(cvecs, ada_w, ada_b)
    rope_tabs = _rope_tables(ts)
    ones = (jnp.arange(B_WIDTH)[:, None] // HEAD_DIM == jnp.arange(B_WIDTH)[None] // HEAD_DIM).astype(BF16)

    xp = x_prompt.reshape(bp * tp, D_MODEL)
    xs = x_sample.reshape(bs * ts, D_MODEL)

    def mod_rows(layer, rows):
        m = mods[layer, rows[0]:rows[1]][:, None, :]
        return m[..., :D_MODEL], m[..., D_MODEL:2 * D_MODEL], m[..., 2 * D_MODEL:]

    e = 0
    w_in0 = even_w_in[e].astype(BF16)
    w_out0 = w_out[0].astype(BF16)
    gain_pre = norm_pre[0][None]
    gain_post = norm_post[0][None]
    zpad = jnp.zeros((DECAY_LORA, B_WIDTH), F32)
    lora = jnp.concatenate([
        jnp.concatenate([b_w2[e, 0], b_w2[e, 1], zpad, zpad], 1),
        jnp.concatenate([zpad, zpad, b_a2[e, 0], b_a2[e, 1]], 1)], 0).astype(BF16)
    feat_args = (b_mu[e][None], lora, b_w0[e][:, None, :], b_a0[e][:, None, :], b_kk[e][None],
                 b_ka[e][None], b_rk[e][None], ones)

    def even_layer(x2d, bn, t, rows, ctx):
        shift, scale, gate = mod_rows(0, rows)
        (qa, ka, va, ga, gb, r, kkn, v, lw, b, ke, bonus, *kv_t) = _even_in(
            x2d, shift, scale, gain_pre, w_in0, t, None if ctx is None else rope_tabs, ctx is None,
            *feat_args)
        if ctx is None:
            ya = _gqa_context(a_sink[e], qa, ka, va, bn, t)
            s0 = None
        else:
            k_ctx, v_ctx, st0 = ctx
            ya = _gqa_window(a_sink[e], qa, ka, va, k_ctx, v_ctx, bn, t)
            s0 = _pair_states(st0)
        yf, yb, *s_fin = _rwkv_scan(r, kkn, v, lw, b, ke, s0, bn, t)
        y = _even_out(ya.reshape(bn * t, A_WIDTH), ga, yf.reshape(bn * t, B_WIDTH),
                      yb.reshape(bn * t, B_WIDTH), bonus, gb,
                      b_ln_w[e][None], b_ln_b[e][None], ones, x2d, gate, gain_post, w_out0, t)
        return y, kv_t, s_fin

    yp, (ka_t, va_t), st_p = even_layer(xp, bp, tp, (0, 1), None)
    feat_major = lambda a: jnp.transpose(a, (0, 2, 3, 1)).reshape(bs, A_KV_WIDTH, past)
    ctx_a = (feat_major(cache_a_k[:, e]), feat_major(cache_a_v[:, e]), state_rwkv[:, e])
    ys, _, _ = even_layer(xs, bs, ts, (1, 1 + bs), ctx_a)

    def cache_a(x_t):
        return jnp.transpose(x_t.reshape(bp, A_KV_HEADS, HEAD_DIM, tp), (0, 3, 1, 2))[:, None]

    o = 0
    lam_init = 0.8 - 0.6 * math.exp(-0.3 * 1)
    w_in1 = odd_w_in[o].astype(BF16)
    w_out1 = w_out[1].astype(BF16)
    gain_pre1 = norm_pre[1][None]
    gain_post1 = norm_post[1][None]
    lams = (c_lq1[o][None], c_lk1[o][None], c_lq2[o][None], c_lk2[o][None])
    subln = c_subln[o][None]

    def odd_layer(x2d, bn, t, rows, ctx):
        shift, scale, gate = mod_rows(1, rows)
        qc, kc, vc, gc = _proj_in(x2d, shift, scale, gain_pre1, w_in1, ODD_SEGS, (0, 1), t,
                                  None if ctx is None else rope_tabs,
                                  head_segs=(1, 2) if ctx is None else (),
                                  bf16_segs=() if ctx is None else (1, 2))
        if ctx is None:
            y = _diff_context_out(lams, subln, qc, kc, vc, gc, x2d, gate, gain_post1, w_out1, bn, t, lam_init)
        else:
            k_ctx, v_ctx = ctx
            oc = _diff_latent(lams, subln, qc, kc, vc, k_ctx, v_ctx, bn, t, lam_init)
            y = _odd_out(oc.reshape(bn * t, C_WIDTH), gc, x2d, gate, gain_post1, w_out1, t)
        return y, kc, vc

    yp2, kc_p, vc_p = odd_layer(yp, bp, tp, (0, 1), None)
    ctx_c = (cache_c_k[:, o].reshape(bs, past, C_WIDTH), cache_c_v[:, o].reshape(bs, past, C_WIDTH))
    ys2, _, _ = odd_layer(ys, bs, ts, (1, 1 + bs), ctx_c)

    return (yp2.reshape(bp, tp, D_MODEL),
            ys2.reshape(bs, ts, D_MODEL),
            cache_a(ka_t),
            cache_a(va_t),
            st_p[0][:, None],
            kc_p.reshape(bp, 1, tp, C_HEADS, 2 * C_QK_DIM),
            vc_p.reshape(bp, 1, tp, C_HEADS, C_V_DIM))
```

```python
import functools
import math

import jax
import jax.numpy as jnp
from jax import lax
from jax.experimental import pallas as pl
from jax.experimental.pallas import tpu as pltpu

F32 = jnp.float32
BF16 = jnp.bfloat16

D_MODEL = 1024
DEPTH = 2
GRID_W = 64
HEAD_DIM = 64
ROPE_BASE = 10000.0
A_HEADS = 8
A_KV_HEADS = 2
A_WIDTH = 512
A_KV_WIDTH = 128
WINDOW = 128
BLOCK = 128
B_HEADS = 8
B_WIDTH = 512
DECAY_LORA = 64
AAA_LORA = 64
B_SHIFT_WIDTH = 3 * B_WIDTH + DECAY_LORA + AAA_LORA
C_HEADS = 8
C_QK_DIM = 64
C_V_DIM = 128
C_WIDTH = 1024
EVEN_IN = 3456
ODD_IN = 4096
NORM_EPS = 1e-6
RWKV_LN_EPS = 64e-5
SUBLN_EPS = 1e-5
NEG_INF = -1e30

LANES = 128
HALO = 8
ROW_TILE = 256
OUT_TILE = 512
GQA_CTX_SEQS = 4
DIFF_CTX_SEQS = 2
DIFF_Q_TILE = 1024
DIFF_Q_SUB = 128
CHUNK = 64
SCAN_SEQS = 4
PAIR = 2 * CHUNK
VMEM_LIMIT = 48 * 1024 * 1024

LOG2E = math.log2(math.e)
QUERY_SCALE = HEAD_DIM ** -0.5 * LOG2E

NN = ((1,), (0,))
NT = ((1,), (1,))


def _bdot(a, b, dims=NN):
    return lax.dot_general(a.astype(BF16), b.astype(BF16), (dims, ((), ())),
                           preferred_element_type=F32)


def _hdot(a, b):
    return lax.dot_general(a, b, (NN, ((), ())), precision=lax.Precision.HIGHEST,
                           preferred_element_type=F32)


def _split_dot(sel_bf16, x):
    x1 = x.astype(BF16)
    r1 = x - x1.astype(F32)
    x2 = r1.astype(BF16)
    x3 = (r1 - x2.astype(F32)).astype(BF16)
    dot = lambda p: jnp.dot(sel_bf16, p, preferred_element_type=F32)
    return dot(x1) + dot(x2) + dot(x3)


def _head_sum(x, ones_bf16):
    hi = x.astype(BF16)
    lo = (x - hi.astype(F32)).astype(BF16)
    return (jnp.dot(hi, ones_bf16, preferred_element_type=F32)
            + jnp.dot(lo, ones_bf16, preferred_element_type=F32))


def _sigmoid(x):
    return 1.0 / (1.0 + jnp.exp(-x))


def _silu(x):
    return x * _sigmoid(x)


def _params(n_axes):
    return pltpu.CompilerParams(dimension_semantics=("arbitrary",) * n_axes,
                                vmem_limit_bytes=VMEM_LIMIT)


def _lane_lo(shape):
    return (lax.broadcasted_iota(jnp.int32, shape, len(shape) - 1) % LANES) < HEAD_DIM


def _stack_pair(x):
    lo = _lane_lo(x.shape)
    z = jnp.zeros_like(x)
    return jnp.concatenate([jnp.where(lo, x, z), jnp.where(lo, z, x)], axis=0)


def _dup_half(x, g):
    lo = _lane_lo(x.shape)
    sw = pltpu.roll(x, HEAD_DIM, 1)
    return jnp.where(lo, x, sw) if g == 0 else jnp.where(lo, sw, x)


def _rope(x, cos, sin_signed):
    out = []
    even_q = (lax.broadcasted_iota(jnp.int32, cos.shape, 1) // (HEAD_DIM // 4)) % 2 == 0
    for g in range(x.shape[1] // LANES):
        xg = x[:, g * LANES:(g + 1) * LANES]
        up = pltpu.roll(xg, LANES - HEAD_DIM // 4, 1)
        dn = pltpu.roll(xg, HEAD_DIM // 4, 1)
        out.append(xg * cos + jnp.where(even_q, up, dn) * sin_signed)
    return out


def _mod_kernel(c_ref, w_ref, b_ref, o_ref):
    s = _silu(c_ref[...])
    o_ref[...] = _hdot(s, w_ref[...]) + b_ref[...]


def _modulation(cvecs, ada_w, ada_b):
    nrow = cvecs.shape[0]
    return pl.pallas_call(
        _mod_kernel,
        grid=(DEPTH, 3),
        in_specs=[pl.BlockSpec((nrow, D_MODEL), lambda l, j: (0, 0)),
                  pl.BlockSpec((None, D_MODEL, D_MODEL), lambda l, j: (l, 0, j)),
                  pl.BlockSpec((None, 1, D_MODEL), lambda l, j: (l, 0, j))],
        out_specs=pl.BlockSpec((None, nrow, D_MODEL), lambda l, j: (l, 0, j)),
        out_shape=jax.ShapeDtypeStruct((DEPTH, nrow, 3 * D_MODEL), F32),
        compiler_params=_params(2),
        name="modulation",
    )(cvecs, ada_w, ada_b.reshape(DEPTH, 1, 3 * D_MODEL))


def _mod_index(mod, tiles_per_seq):
    if mod.shape[0] == 1:
        return lambda i: (0, 0, 0)
    return lambda i: (i // tiles_per_seq, 0, 0)


def _proj_in_kernel(*refs, segs, rope_segs, use_rope, head_segs, bf16_segs):
    if use_rope:
        x_ref, sh_ref, sc_ref, g_ref, w_ref, cos_ref, sin_ref = refs[:7]
        outs = refs[7:]
    else:
        x_ref, sh_ref, sc_ref, g_ref, w_ref = refs[:5]
        outs = refs[5:]
    x = x_ref[...]
    y = x * lax.rsqrt(jnp.mean(x * x, -1, keepdims=True) + NORM_EPS)
    h = (y * g_ref[...]) * (1.0 + sc_ref[...]) + sh_ref[...]
    hb = h.astype(BF16)
    for idx, (lo, hi) in enumerate(segs):
        o = jnp.dot(hb, w_ref[:, lo:hi], preferred_element_type=F32)
        if idx == 0:
            post = lambda a: (a * QUERY_SCALE).astype(BF16)
        elif idx in bf16_segs:
            post = lambda a: a.astype(BF16)
        else:
            post = lambda a: a
        if use_rope and idx in rope_segs:
            parts = _rope(o, cos_ref[...], sin_ref[...])
            for g, part in enumerate(parts):
                outs[idx][:, g * LANES:(g + 1) * LANES] = post(part)
        elif idx in head_segs:
            outs[idx][...] = o.reshape(o.shape[0], (hi - lo) // LANES, LANES)
        else:
            outs[idx][...] = post(o)


def _proj_in(x2d, shift, scale, gain, w_bf16, segs, rope_segs, t, rope_tabs, head_segs=(), bf16_segs=()):
    n = x2d.shape[0]
    tile = OUT_TILE
    assert shift.shape[0] == 1 or t % tile == 0
    nt = n // tile
    tiles_per_seq = max(t // tile, 1)
    use_rope = rope_tabs is not None
    row = lambda i: (i, 0)
    per_b = _mod_index(shift, tiles_per_seq)
    in_specs = [pl.BlockSpec((tile, D_MODEL), row),
                pl.BlockSpec((None, 1, D_MODEL), per_b),
                pl.BlockSpec((None, 1, D_MODEL), per_b),
                pl.BlockSpec((1, D_MODEL), lambda i: (0, 0)),
                pl.BlockSpec(w_bf16.shape, lambda i: (0, 0))]
    args = [x2d, shift, scale, gain, w_bf16]
    if use_rope:
        pos = lambda i: (i % tiles_per_seq, 0)
        in_specs += [pl.BlockSpec((tile, LANES), pos), pl.BlockSpec((tile, LANES), pos)]
        args += list(rope_tabs)
    widths = [hi - lo for lo, hi in segs]
    out_specs, out_shape = [], []
    for idx, w in enumerate(widths):
        if idx in head_segs:
            out_specs.append(pl.BlockSpec((tile, w // LANES, LANES), lambda i: (i, 0, 0)))
            out_shape.append(jax.ShapeDtypeStruct((n, w // LANES, LANES), F32))
        else:
            out_specs.append(pl.BlockSpec((tile, w), row))
            out_shape.append(jax.ShapeDtypeStruct((n, w), BF16 if idx == 0 or idx in bf16_segs else F32))
    return pl.pallas_call(
        functools.partial(_proj_in_kernel, segs=segs, rope_segs=rope_segs, use_rope=use_rope,
                          head_segs=tuple(head_segs), bf16_segs=tuple(bf16_segs)),
        grid=(nt,),
        in_specs=in_specs,
        out_specs=out_specs,
        out_shape=out_shape,
        compiler_params=_params(1),
        name="proj_in",
    )(*args)


def _sink_exp_pair(s2, sink_a, sink_b, n):
    rowi = lax.broadcasted_iota(jnp.int32, (2 * n, 1), 0)
    sink2 = jnp.where(rowi < n, sink_a, sink_b) * LOG2E
    m = jnp.maximum(jnp.max(s2, -1, keepdims=True), sink2)
    e = jnp.exp2(s2 - m)
    return e, 1.0 / (jnp.sum(e, -1, keepdims=True) + jnp.exp2(sink2 - m))


def _gqa_ctx_kernel(sink_ref, q_ref, k_ref, v_ref, o_ref):
    nseq, t = q_ref.shape[:2]
    pairs = [(s, p) for s in range(nseq) for p in range(A_HEADS // 2)]
    kd = [[_dup_half(k_ref[s], g).astype(BF16) for g in range(A_KV_HEADS)] for s in range(nseq)]
    vd = [[_dup_half(v_ref[s], g).astype(BF16) for g in range(A_KV_HEADS)] for s in range(nseq)]
    s2 = [_bdot(_stack_pair(q_ref[s, :, p * LANES:(p + 1) * LANES]), kd[s][p // 2], NT)
          for s, p in pairs]
    for i, (s, p) in enumerate(pairs):
        e, inv = _sink_exp_pair(s2[i], sink_ref[2 * p], sink_ref[2 * p + 1], t)
        o = _bdot(e, vd[s][p // 2]) * inv
        o_ref[s, :, p * LANES:(p + 1) * LANES] = jnp.where(_lane_lo((t, LANES)), o[:t], o[t:])


def _gqa_context(sink, q, k, v, bn, t):
    seq = lambda b: (b, 0, 0)
    ns = GQA_CTX_SEQS
    return pl.pallas_call(
        _gqa_ctx_kernel,
        grid=(bn // ns,),
        in_specs=[pl.BlockSpec(memory_space=pltpu.SMEM),
                  pl.BlockSpec((ns, t, A_WIDTH), seq),
                  pl.BlockSpec((ns, t, A_KV_WIDTH), seq),
                  pl.BlockSpec((ns, t, A_KV_WIDTH), seq)],
        out_specs=pl.BlockSpec((ns, t, A_WIDTH), seq),
        out_shape=jax.ShapeDtypeStruct((bn, t, A_WIDTH), F32),
        compiler_params=_params(1),
        name="gqa_context",
    )(sink, q.reshape(bn, t, A_WIDTH), k.reshape(bn, t, A_KV_WIDTH), v.reshape(bn, t, A_KV_WIDTH))


def _gqa_win_kernel(sink_ref, q_ref, kp_ref, kc_ref, kn_ref, vp_ref, vc_ref, vn_ref,
                    kx_ref, vx_ref, o_ref, *, t):
    nblk = pl.program_id(1)
    nctx = kx_ref.shape[1]
    keys = jnp.concatenate([kx_ref[...].T, kp_ref[...], kc_ref[...], kn_ref[...]], axis=0)
    vals = jnp.concatenate([vx_ref[...].T, vp_ref[...], vc_ref[...], vn_ref[...]], axis=0)
    nkeys = nctx + 3 * BLOCK
    rowi = lax.broadcasted_iota(jnp.int32, (2 * BLOCK, nkeys), 0)
    coli = lax.broadcasted_iota(jnp.int32, (2 * BLOCK, nkeys), 1)
    qpos = nblk * BLOCK + rowi % BLOCK
    kpos = nblk * BLOCK + coli - nctx - BLOCK
    valid = (coli < nctx) | ((jnp.abs(qpos - kpos) <= WINDOW) & (kpos >= 0) & (kpos < t))
    kd = [_dup_half(keys, g).astype(BF16) for g in range(A_KV_HEADS)]
    vd = [_dup_half(vals, g).astype(BF16) for g in range(A_KV_HEADS)]
    s2 = [_bdot(_stack_pair(q_ref[:, p * LANES:(p + 1) * LANES]),
                kd[p // 2], NT) for p in range(A_HEADS // 2)]
    for p in range(A_HEADS // 2):
        e, inv = _sink_exp_pair(jnp.where(valid, s2[p], NEG_INF), sink_ref[2 * p], sink_ref[2 * p + 1], BLOCK)
        o = _bdot(e, vd[p // 2]) * inv
        o_ref[:, p * LANES:(p + 1) * LANES] = jnp.where(_lane_lo((BLOCK, LANES)), o[:BLOCK], o[BLOCK:])


def _gqa_window(sink, q, k, v, k_ctx_t, v_ctx_t, bn, t):
    nb = t // BLOCK
    nctx = k_ctx_t.shape[2]
    cur = lambda b, n: (b, n, 0)
    prev = lambda b, n: (b, jnp.maximum(n - 1, 0), 0)
    nxt = lambda b, n: (b, jnp.minimum(n + 1, nb - 1), 0)
    ctx = lambda b, n: (b, 0, 0)
    kv = lambda f: pl.BlockSpec((None, BLOCK, A_KV_WIDTH), f)
    k3 = k.reshape(bn, t, A_KV_WIDTH)
    v3 = v.reshape(bn, t, A_KV_WIDTH)
    return pl.pallas_call(
        functools.partial(_gqa_win_kernel, t=t),
        grid=(bn, nb),
        in_specs=[pl.BlockSpec(memory_space=pltpu.SMEM),
                  pl.BlockSpec((None, BLOCK, A_WIDTH), cur),
                  kv(prev), kv(cur), kv(nxt), kv(prev), kv(cur), kv(nxt),
                  pl.BlockSpec((None, A_KV_WIDTH, nctx), ctx),
                  pl.BlockSpec((None, A_KV_WIDTH, nctx), ctx)],
        out_specs=pl.BlockSpec((None, BLOCK, A_WIDTH), cur),
        out_shape=jax.ShapeDtypeStruct((bn, t, A_WIDTH), F32),
        compiler_params=_params(2),
        name="gqa_window",
    )(sink, q.reshape(bn, t, A_WIDTH), k3, k3, k3, v3, v3, v3, k_ctx_t, v_ctx_t)


def _even_in_kernel(*refs, use_rope, emit_t, tiles_per_seq):
    x_ref, xp_ref, xn_ref, sh_ref, sc_ref, g_ref, w_ref = refs[:7]
    refs = refs[7:]
    if use_rope:
        cos_ref, sin_ref = refs[:2]
        refs = refs[2:]
    mu_ref, lora_ref, w0_ref, a0_ref, kk_ref, ka_ref, rk_ref, ones_ref = refs[:8]
    (q_out, k_out, v_out, ga_out, gb_out, r_out, kkn_out, vv_out, lw_out, b_out, ke_out,
     bonus_out) = refs[8:20]
    i = pl.program_id(0)
    tm = x_ref.shape[0]
    pos = i % tiles_per_seq

    def norm_mod(x):
        y = x * lax.rsqrt(jnp.mean(x * x, -1, keepdims=True) + NORM_EPS)
        return (y * g_ref[...]) * (1.0 + sc_ref[...]) + sh_ref[...]

    h = norm_mod(x_ref[...])
    hb = h.astype(BF16)
    seg = dict(zip(("q", "k", "v", "ga", "pb", "gb"), EVEN_SEGS))
    proj = lambda name, lhs=hb: jnp.dot(lhs, w_ref[:, seg[name][0]:seg[name][1]], preferred_element_type=F32)
    h_ext = jnp.concatenate([norm_mod(xp_ref[...]), h, norm_mod(xn_ref[...])], 0).astype(BF16)
    p_ext = proj("pb", h_ext)

    q = proj("q")
    if use_rope:
        q = jnp.concatenate(_rope(q, cos_ref[...], sin_ref[...]), 1)
    q_out[...] = (q * QUERY_SCALE).astype(BF16)

    p = p_ext[HALO:HALO + tm]
    p_ext = jnp.concatenate([jnp.where(pos != 0, p_ext[:HALO], 0.0), p,
                             jnp.where(pos != tiles_per_seq - 1, p_ext[HALO + tm:], 0.0)], 0)
    prev = pltpu.roll(p_ext, 1, 0)[HALO:HALO + tm]
    nxt = pltpu.roll(p_ext, tm + 2 * HALO - 1, 0)[HALO:HALO + tm]
    xf = p + (0.5 * (prev + nxt) - p) * mu_ref[...]
    r = xf[:, 0:B_WIDTH]
    k = xf[:, B_WIDTH:2 * B_WIDTH]
    v = xf[:, 2 * B_WIDTH:3 * B_WIDTH]
    g = xf[:, 3 * B_WIDTH:]
    r_out[...] = r
    vv_out[...] = v

    k_att = proj("k")
    if use_rope:
        k_att = _rope(k_att, cos_ref[...], sin_ref[...])[0]
    k_out[...] = k_att
    v_att = proj("v")
    v_out[...] = v_att
    if emit_t:
        refs[20][...] = k_att.T
        refs[21][...] = v_att.T

    g = jnp.where(_lane_lo(g.shape), jnp.tanh(g), g)
    lo = _bdot(g, lora_ref[...])
    ones = ones_ref[...]
    kkr = k * kk_ref[...]
    kkn = kkr / jnp.maximum(jnp.sqrt(_head_sum(kkr * kkr, ones)), 1e-12)
    kkn_out[...] = kkn

    ga_out[...] = proj("ga")

    ke_sum = jnp.zeros_like(k)
    for z in range(2):
        wz = w0_ref[z] + lo[:, z * B_WIDTH:(z + 1) * B_WIDTH]
        lw_out[z] = -math.exp(-0.5) * _sigmoid(wz)
        az = _sigmoid(a0_ref[z] + lo[:, (2 + z) * B_WIDTH:(3 + z) * B_WIDTH])
        ke = k * (1.0 + (az - 1.0) * ka_ref[...])
        ke_out[z] = ke
        b_out[z] = kkn * az
        ke_sum = ke_sum + ke
        if z == 0:
            gb_out[...] = proj("gb")
    bonus_out[...] = _head_sum(r * (0.5 * ke_sum) * rk_ref[...], ones) * v


def _even_in(x2d, shift, scale, gain, w_bf16, t, rope_tabs, emit_t, mu, lora, w0, a0, k_k, k_a, r_k, ones):
    n = x2d.shape[0]
    tile = ROW_TILE if emit_t else OUT_TILE
    assert t % tile == 0 and (not emit_t or t == tile)
    nt = n // tile
    tiles_per_seq = t // tile
    h8 = tile // HALO
    use_rope = rope_tabs is not None
    row = lambda i: (i, 0)
    const2 = lambda i: (0, 0)
    const3 = lambda i: (0, 0, 0)
    per_b = _mod_index(shift, tiles_per_seq)
    vec = pl.BlockSpec((1, B_WIDTH), const2)
    in_specs = [pl.BlockSpec((tile, D_MODEL), row),
                pl.BlockSpec((HALO, D_MODEL), lambda i: (jnp.maximum(i * h8 - 1, 0), 0)),
                pl.BlockSpec((HALO, D_MODEL), lambda i: (jnp.minimum((i + 1) * h8, n // HALO - 1), 0)),
                pl.BlockSpec((None, 1, D_MODEL), per_b),
                pl.BlockSpec((None, 1, D_MODEL), per_b),
                pl.BlockSpec((1, D_MODEL), const2),
                pl.BlockSpec(w_bf16.shape, const2)]
    args = [x2d, x2d, x2d, shift, scale, gain, w_bf16]
    if use_rope:
        pos = lambda i: (i % tiles_per_seq, 0)
        in_specs += [pl.BlockSpec((tile, LANES), pos), pl.BlockSpec((tile, LANES), pos)]
        args += list(rope_tabs)
    in_specs += [pl.BlockSpec((1, B_SHIFT_WIDTH), const2),
                 pl.BlockSpec(lora.shape, const2),
                 pl.BlockSpec((2, 1, B_WIDTH), const3),
                 pl.BlockSpec((2, 1, B_WIDTH), const3),
                 vec, vec, vec,
                 pl.BlockSpec((B_WIDTH, B_WIDTH), const2)]
    args += [mu, lora, w0, a0, k_k, k_a, r_k, ones]
    wide = pl.BlockSpec((tile, B_WIDTH), row)
    narrow = pl.BlockSpec((tile, A_KV_WIDTH), row)
    dual = pl.BlockSpec((2, tile, B_WIDTH), lambda i: (0, i, 0))
    f_wide = jax.ShapeDtypeStruct((n, B_WIDTH), F32)
    f_narrow = jax.ShapeDtypeStruct((n, A_KV_WIDTH), F32)
    f_dual = jax.ShapeDtypeStruct((2, n, B_WIDTH), F32)
    out_specs = [wide, narrow, narrow, wide, wide, wide, wide, wide, dual, dual, dual, wide]
    out_shape = [jax.ShapeDtypeStruct((n, A_WIDTH), BF16), f_narrow, f_narrow, f_wide, f_wide,
                 f_wide, f_wide, f_wide, f_dual, f_dual, f_dual, f_wide]
    if emit_t:
        out_specs += [pl.BlockSpec((None, A_KV_WIDTH, tile), lambda i: (i, 0, 0))] * 2
        out_shape += [jax.ShapeDtypeStruct((nt, A_KV_WIDTH, tile), F32)] * 2
    return pl.pallas_call(
        functools.partial(_even_in_kernel, use_rope=use_rope, emit_t=emit_t, tiles_per_seq=tiles_per_seq),
        grid=(nt,),
        in_specs=in_specs,
        out_specs=out_specs,
        out_shape=out_shape,
        compiler_params=_params(1),
        name="even_in",
    )(*args)


def _rwkv_scan_kernel(*refs, n_chunks, from_zero):
    (rf_ref, kf_ref, vf_ref, rb_ref, kb_ref, vb_ref, lwf_ref, bf_ref, kef_ref,
     lwb_ref, bb_ref, keb_ref) = refs[:12]
    if from_zero:
        yf_ref, yb_ref, sout_ref, s_scr = refs[12:]
    else:
        s0_ref, yf_ref, yb_ref, s_scr = refs[12:]
    c = pl.program_id(1)

    @pl.when(c == 0)
    def _():
        if from_zero:
            s_scr[...] = jnp.zeros_like(s_scr)
        else:
            s_scr[...] = s0_ref[...]

    row = lax.broadcasted_iota(jnp.int32, (CHUNK, LANES), 0)
    li = lax.broadcasted_iota(jnp.int32, (CHUNK, LANES), 1) % CHUNK
    blk16 = (row // 16) == (li // 16)
    blk32 = (row // 32) == (li // 32)
    eye = jnp.where(row == li, 1.0, 0.0)
    srow = lax.broadcasted_iota(jnp.int32, (PAIR, PAIR), 0)
    scol = lax.broadcasted_iota(jnp.int32, (PAIR, PAIR), 1)
    same_head = (srow // CHUNK) == (scol // CHUNK)
    tr = lax.broadcasted_iota(jnp.int32, (CHUNK, CHUNK), 0)
    tc = lax.broadcasted_iota(jnp.int32, (CHUNK, CHUNK), 1)
    dirs = ((rf_ref, kf_ref, vf_ref, lwf_ref, bf_ref, kef_ref, yf_ref),
            (rb_ref, kb_ref, vb_ref, lwb_ref, bb_ref, keb_ref, yb_ref))

    def advance(s):
        chains = [(z, p) for z in range(2) for p in range(B_HEADS // 2)]
        sls = [slice(p * LANES, (p + 1) * LANES) for _, p in chains]
        before, before_eq, tot, kkd, rd, bg, kg, vv, sc = [], [], [], [], [], [], [], [], []
        for z in range(2):
            r_ref, kk_ref, v_ref, lw_ref, b_ref, ke_ref, _ = dirs[z]
            diff = (li - row) if z else (row - li)
            tri = jnp.where(((tc - tr) if z else (tr - tc)) >= 0, 1.0, 0.0).astype(BF16)
            lw_all = lw_ref[s]
            cum_all = _split_dot(tri, lw_all)
            for p in range(B_HEADS // 2):
                sl = slice(p * LANES, (p + 1) * LANES)
                lw = lw_all[:, sl]
                cum = cum_all[:, sl]
                tt = cum[0:1] if z else cum[CHUNK - 1:CHUNK]
                b = b_ref[s, :, sl]
                ke = ke_ref[s, :, sl]
                g_inv = jnp.exp(-cum)
                g_rest = jnp.exp(tt - cum)
                before.append(diff > 0)
                before_eq.append(diff >= 0)
                tot.append(tt)
                kkd.append(kk_ref[s, :, sl] * jnp.exp(cum - lw))
                rd.append(r_ref[s, :, sl] * jnp.exp(cum))
                bg.append(b * g_rest)
                kg.append(ke * g_rest)
                vv.append(v_ref[s, :, sl])
                sc.append(_bdot(jnp.concatenate([kkd[-1], rd[-1]], 0),
                                jnp.concatenate([_stack_pair(b * g_inv), _stack_pair(ke * g_inv)], 0), NT))
            yield
        lb = [jnp.where(m, x[:CHUNK, :LANES], 0.0) for m, x in zip(before, sc)]
        mb = [jnp.where(m, x[CHUNK:, :LANES], 0.0) for m, x in zip(before_eq, sc)]
        lv = [_bdot(jnp.concatenate([jnp.where(m, x[:CHUNK, LANES:], 0.0),
                                     jnp.where(me, x[CHUNK:, LANES:], 0.0)], 0), _stack_pair(v))
              for m, me, x, v in zip(before, before_eq, sc, vv)]
        yield

        pw = [jnp.where(blk16, x, 0.0) for x in lb]
        tinv = [eye - x for x in pw]
        for _ in range(3):
            pw = [_bdot(x, _stack_pair(x)) for x in pw]
            yield
            tinv = [t + _bdot(t, _stack_pair(x)) for t, x in zip(tinv, pw)]
            yield
        for off in (blk32 & ~blk16, ~blk32):
            tmp = [_bdot(t, _stack_pair(jnp.where(off, x, 0.0))) for t, x in zip(tinv, lb)]
            yield
            tinv = [t - _bdot(m, _stack_pair(t)) for t, m in zip(tinv, tmp)]
            yield

        w12 = [_bdot(t, jnp.concatenate([_stack_pair(k), _stack_pair(x[:CHUNK])], 1))
               for t, k, x in zip(tinv, kkd, lv)]
        yield
        s_old = [s_scr[s, z, p] for z, p in chains]
        ws = [_bdot(jnp.concatenate([w[:, :LANES], r], 0), st, NT) for w, r, st in zip(w12, rd, s_old)]
        yield
        u = [-(a[:CHUNK] + w[:, LANES:]) for a, w in zip(ws, w12)]
        mbu = [_bdot(m, _stack_pair(x)) for m, x in zip(mb, u)]
        upd = [_bdot(jnp.concatenate([x, v], 0).T, jnp.concatenate([b_, k_], 0))
               for x, v, b_, k_ in zip(u, vv, bg, kg)]
        yield
        for i, (z, p) in enumerate(chains):
            dirs[z][6][s, :, sls[i]] = ws[i][CHUNK:] + mbu[i] + lv[i][CHUNK:]
            s_scr[s, z, p] = s_old[i] * jnp.exp(tot[i]) + jnp.where(same_head, upd[i], 0.0)

    live = [advance(s) for s in range(rf_ref.shape[0])]
    while live:
        live = [gen for gen in live if next(gen, "done") != "done"]

    if from_zero:
        @pl.when(c == n_chunks - 1)
        def _():
            for s in range(s_scr.shape[0]):
                for z in range(2):
                    for p in range(B_HEADS // 2):
                        st = s_scr[s, z, p]
                        sout_ref[s, z, 2 * p] = st[:HEAD_DIM, :HEAD_DIM]
                        sout_ref[s, z, 2 * p + 1] = pltpu.roll(st[HEAD_DIM:, :], HEAD_DIM, 1)[:, :HEAD_DIM]


def _rwkv_scan(r, kkn, v, lw, b, ke, s0, bn, t):
    nc = t // CHUNK
    npair = B_HEADS // 2
    ns = min(SCAN_SEQS, bn)
    fwd = pl.BlockSpec((ns, CHUNK, B_WIDTH), lambda g, c: (g, c, 0))
    bwd = pl.BlockSpec((ns, CHUNK, B_WIDTH), lambda g, c: (g, nc - 1 - c, 0))
    dfwd = pl.BlockSpec((None, ns, CHUNK, B_WIDTH), lambda g, c: (0, g, c, 0))
    dbwd = pl.BlockSpec((None, ns, CHUNK, B_WIDTH), lambda g, c: (1, g, nc - 1 - c, 0))
    r3, k3, v3 = (a.reshape(bn, t, B_WIDTH) for a in (r, kkn, v))
    lw4, b4, ke4 = (a.reshape(2, bn, t, B_WIDTH) for a in (lw, b, ke))
    seq = jax.ShapeDtypeStruct((bn, t, B_WIDTH), F32)
    in_specs = [fwd, fwd, fwd, bwd, bwd, bwd, dfwd, dfwd, dfwd, dbwd, dbwd, dbwd]
    args = [r3, k3, v3, r3, k3, v3, lw4, b4, ke4, lw4, b4, ke4]
    out_specs = [fwd, bwd]
    out_shape = [seq, seq]
    if s0 is None:
        out_specs.append(pl.BlockSpec((ns, 2, B_HEADS, HEAD_DIM, HEAD_DIM), lambda g, c: (g, 0, 0, 0, 0)))
        out_shape.append(jax.ShapeDtypeStruct((bn, 2, B_HEADS, HEAD_DIM, HEAD_DIM), F32))
    else:
        in_specs.append(pl.BlockSpec((ns, 2, npair, PAIR, PAIR), lambda g, c: (g, 0, 0, 0, 0)))
        args.append(s0)
    return pl.pallas_call(
        functools.partial(_rwkv_scan_kernel, n_chunks=nc, from_zero=s0 is None),
        grid=(bn // ns, nc),
        in_specs=in_specs,
        out_specs=out_specs,
        out_shape=out_shape,
        scratch_shapes=[pltpu.VMEM((ns, 2, npair, PAIR, PAIR), F32)],
        compiler_params=_params(2),
        name="rwkv_scan",
    )(*args)


def _post(x, z, gate, gain):
    zn = z * lax.rsqrt(jnp.mean(z * z, -1, keepdims=True) + NORM_EPS)
    return x + gate * (zn * gain)


def _even_out_kernel(ya_ref, ga_ref, yf_ref, yb_ref, bonus_ref, gb_ref, lnw_ref, lnb_ref, ones_ref,
                     x_ref, gate_ref, gain_ref, w_ref, o_ref):
    y = yf_ref[...] + yb_ref[...]
    ones = ones_ref[...]
    mean = _head_sum(y, ones) * (1.0 / HEAD_DIM)
    d = y - mean
    var = _head_sum(d * d, ones) * (1.0 / HEAD_DIM)
    yb = (d * lax.rsqrt(var + RWKV_LN_EPS)) * lnw_ref[...] + lnb_ref[...] + bonus_ref[...]
    za = (ya_ref[...] * _silu(ga_ref[...])).astype(BF16)
    zb = (yb * _silu(gb_ref[...])).astype(BF16)
    z = (jnp.dot(za, w_ref[0:A_WIDTH, :], preferred_element_type=F32)
         + jnp.dot(zb, w_ref[A_WIDTH:, :], preferred_element_type=F32))
    o_ref[...] = _post(x_ref[...], z, gate_ref[...], gain_ref[...])


def _even_out(ya, ga, yf, yb, bonus, gb, ln_w, ln_b, ones, x2d, gate, gain, w_bf16, t):
    n = x2d.shape[0]
    row = lambda i: (i, 0)
    const2 = lambda i: (0, 0)
    half = pl.BlockSpec((OUT_TILE, B_WIDTH), row)
    full = pl.BlockSpec((OUT_TILE, D_MODEL), row)
    vec = pl.BlockSpec((1, B_WIDTH), const2)
    assert gate.shape[0] == 1 or t % OUT_TILE == 0
    return pl.pallas_call(
        _even_out_kernel,
        grid=(n // OUT_TILE,),
        in_specs=[half, half, half, half, half, half, vec, vec,
                  pl.BlockSpec((B_WIDTH, B_WIDTH), const2),
                  full,
                  pl.BlockSpec((None, 1, D_MODEL), _mod_index(gate, t // OUT_TILE)),
                  pl.BlockSpec((1, D_MODEL), const2),
                  pl.BlockSpec((D_MODEL, D_MODEL), const2)],
        out_specs=full,
        out_shape=jax.ShapeDtypeStruct((n, D_MODEL), F32),
        compiler_params=_params(1),
        name="even_out",
    )(ya, ga, yf, yb, bonus, gb, ln_w, ln_b, ones, x2d, gate, gain, w_bf16)


def _odd_out_kernel(o_ref_in, gc_ref, x_ref, gate_ref, gain_ref, w_ref, o_ref):
    zc = (o_ref_in[...] * _silu(gc_ref[...])).astype(BF16)
    z = jnp.dot(zc, w_ref[...], preferred_element_type=F32)
    o_ref[...] = _post(x_ref[...], z, gate_ref[...], gain_ref[...])


def _odd_out(o, gc, x2d, gate, gain, w_bf16, t):
    n = x2d.shape[0]
    row = lambda i: (i, 0)
    const2 = lambda i: (0, 0)
    full = pl.BlockSpec((OUT_TILE, D_MODEL), row)
    assert gate.shape[0] == 1 or t % OUT_TILE == 0
    return pl.pallas_call(
        _odd_out_kernel,
        grid=(n // OUT_TILE,),
        in_specs=[full, full, full,
                  pl.BlockSpec((None, 1, D_MODEL), _mod_index(gate, t // OUT_TILE)),
                  pl.BlockSpec((1, D_MODEL), const2),
                  pl.BlockSpec((D_MODEL, D_MODEL), const2)],
        out_specs=full,
        out_shape=jax.ShapeDtypeStruct((n, D_MODEL), F32),
        compiler_params=_params(1),
        name="odd_out",
    )(o, gc, x2d, gate, gain, w_bf16)


def _lambda(lq1_ref, lk1_ref, lq2_ref, lk2_ref, lam_init):
    s1 = jnp.sum(lq1_ref[...] * lk1_ref[...], -1, keepdims=True)
    s2 = jnp.sum(lq2_ref[...] * lk2_ref[...], -1, keepdims=True)
    return jnp.exp(s1) - jnp.exp(s2) + lam_init


def _subln(o, gain, lam_init):
    on = o * lax.rsqrt(jnp.mean(o * o, -1, keepdims=True) + SUBLN_EPS)
    return (on * gain) * (1.0 - lam_init)


def _diff_ctx_kernel(lq1_ref, lk1_ref, lq2_ref, lk2_ref, sub_ref, q_ref, k_ref, v_ref,
                     gc_ref, x_ref, gate_ref, gain_ref, w_ref, y_ref, o_scr, *, lam_init):
    nseq, t = q_ref.shape[:2]
    lam = _lambda(lq1_ref, lk1_ref, lq2_ref, lk2_ref, lam_init)
    units = [(s, slice(h * LANES, (h + 1) * LANES)) for s in range(nseq) for h in range(C_HEADS)]
    k = [k_ref[s].reshape(t, C_WIDTH) for s in range(nseq)]
    v = [v_ref[s].reshape(t, C_WIDTH) for s in range(nseq)]
    s2 = [_bdot(_stack_pair(q_ref[s, :, sl]), k[s][:, sl], NT)
          for s, sl in units]
    for i, (s, sl) in enumerate(units):
        e = jnp.exp2(s2[i] - jnp.max(s2[i], -1, keepdims=True))
        ov = _bdot(e, v[s][:, sl]) * (1.0 / jnp.sum(e, -1, keepdims=True))
        o_scr[s * t:(s + 1) * t, sl] = _subln(ov[:t] - lam * ov[t:], sub_ref[...], lam_init)
    gc = gc_ref[...].reshape(nseq * t, C_WIDTH)
    zc = (o_scr[...] * _silu(gc)).astype(BF16)
    z = jnp.dot(zc, w_ref[...], preferred_element_type=F32)
    y = _post(x_ref[...].reshape(nseq * t, D_MODEL), z, gate_ref[...], gain_ref[...])
    y_ref[...] = y.reshape(nseq, t, D_MODEL)


def _diff_context_out(lams, subln, q, k, v, gc, x2d, gate, gain, w_bf16, bn, t, lam_init):
    ns = DIFF_CTX_SEQS
    seq = pl.BlockSpec((ns, t, C_WIDTH), lambda b: (b, 0, 0))
    heads = pl.BlockSpec((ns, t, C_HEADS, LANES), lambda b: (b, 0, 0, 0))
    small = pl.BlockSpec((1, C_QK_DIM), lambda b: (0, 0))
    const2 = lambda b: (0, 0)
    to3 = lambda a: a.reshape((bn, t) + a.shape[1:])
    return pl.pallas_call(
        functools.partial(_diff_ctx_kernel, lam_init=lam_init),
        grid=(bn // ns,),
        in_specs=[small, small, small, small, pl.BlockSpec((1, C_V_DIM), const2),
                  seq, heads, heads, seq, seq,
                  pl.BlockSpec((None, 1, D_MODEL), lambda b: (0, 0, 0)),
                  pl.BlockSpec((1, D_MODEL), const2),
                  pl.BlockSpec((D_MODEL, D_MODEL), const2)],
        out_specs=seq,
        out_shape=jax.ShapeDtypeStruct((bn, t, D_MODEL), F32),
        scratch_shapes=[pltpu.VMEM((ns * t, C_WIDTH), F32)],
        compiler_params=_params(1),
        name="diff_context",
    )(*lams, subln, to3(q), to3(k), to3(v), to3(gc), to3(x2d), gate, gain, w_bf16).reshape(bn * t, D_MODEL)


def _diff_lat_kernel(lq1_ref, lk1_ref, lq2_ref, lk2_ref, sub_ref, q_ref, k_ref, v_ref, kx_ref, vx_ref,
                     o_ref, *, lam_init):
    lam = _lambda(lq1_ref, lk1_ref, lq2_ref, lk2_ref, lam_init)
    kx = kx_ref[...].astype(BF16)
    k = k_ref[...].astype(BF16)
    tq = DIFF_Q_SUB
    subs = range(q_ref.shape[0] // tq)
    scores = []
    for i in subs:
        qs = _stack_pair(q_ref[i * tq:(i + 1) * tq, :])
        scores.append((_bdot(qs, kx, NT), _bdot(qs, k, NT)))
    for i in subs:
        sx, sl = scores[i]
        m = jnp.maximum(jnp.max(sx, -1, keepdims=True), jnp.max(sl, -1, keepdims=True))
        ex = jnp.exp2(sx - m)
        el = jnp.exp2(sl - m)
        den = jnp.sum(ex, -1, keepdims=True) + jnp.sum(el, -1, keepdims=True)
        ratio = lam * den[:tq] / den[tq:]
        o = (_bdot(ex[:tq] - ex[tq:] * ratio, vx_ref[...])
             + _bdot(el[:tq] - el[tq:] * ratio, v_ref[...])) * (1.0 / den[:tq])
        o_ref[i * tq:(i + 1) * tq, :] = _subln(o, sub_ref[...], lam_init)


def _diff_latent(lams, subln, q, k, v, k_ctx, v_ctx, bn, t, lam_init):
    nb = t // DIFF_Q_TILE
    nctx = k_ctx.shape[1]
    small = pl.BlockSpec((1, C_QK_DIM), lambda b, h, n: (0, 0))
    qspec = pl.BlockSpec((None, DIFF_Q_TILE, LANES), lambda b, h, n: (b, n, h))
    kvspec = pl.BlockSpec((None, t, LANES), lambda b, h, n: (b, 0, h))
    cxspec = pl.BlockSpec((None, nctx, LANES), lambda b, h, n: (b, 0, h))
    return pl.pallas_call(
        functools.partial(_diff_lat_kernel, lam_init=lam_init),
        grid=(bn, C_HEADS, nb),
        in_specs=[small, small, small, small,
                  pl.BlockSpec((1, C_V_DIM), lambda b, h, n: (0, 0)),
                  qspec, kvspec, kvspec, cxspec, cxspec],
        out_specs=qspec,
        out_shape=jax.ShapeDtypeStruct((bn, t, C_WIDTH), F32),
        compiler_params=_params(3),
        name="diff_latent",
    )(*lams, subln, q.reshape(bn, t, C_WIDTH), k.reshape(bn, t, C_WIDTH), v.reshape(bn, t, C_WIDTH),
      k_ctx, v_ctx)


def _rope_tables(t):
    nf = HEAD_DIM // 4
    inv = 1.0 / (ROPE_BASE ** (jnp.arange(nf, dtype=F32) / nf))
    pos = jnp.arange(t)
    ang_r = (pos // GRID_W).astype(F32)[:, None] * inv[None]
    ang_c = (pos % GRID_W).astype(F32)[:, None] * inv[None]
    ang = jnp.concatenate([ang_r, ang_r, ang_c, ang_c], -1)
    cos, sin = jnp.cos(ang), jnp.sin(ang)
    sign = jnp.where((jnp.arange(HEAD_DIM) // nf) % 2 == 0, -1.0, 1.0).astype(F32)
    return jnp.tile(cos, (1, 2)), jnp.tile(sin * sign[None], (1, 2))


EVEN_SEGS = ((0, 512), (512, 640), (640, 768), (768, 1280), (1280, 2944), (2944, 3456))
ODD_SEGS = ((0, 1024), (1024, 2048), (2048, 3072), (3072, 4096))


def _pair_states(st):
    bn = st.shape[0]
    s = st.reshape(bn, 2, 4, 2, HEAD_DIM, HEAD_DIM)
    z = jnp.zeros_like(s[:, :, :, 0])
    top = jnp.concatenate([s[:, :, :, 0], z], -1)
    bot = jnp.concatenate([z, s[:, :, :, 1]], -1)
    return jnp.concatenate([top, bot], -2)


def kernel(x_prompt, x_sample, cache_a_k, cache_a_v, state_rwkv, cache_c_k, cache_c_v, c, c_ctx,
           ada_w, ada_b, norm_pre, norm_post, w_out, even_w_in, a_sink, b_mu, b_w0, b_w2, b_a0,
           b_a2, b_kk, b_ka, b_rk, b_ln_w, b_ln_b, odd_w_in, c_lq1, c_lk1, c_lq2, c_lk2, c_subln):
    bp, tp, _ = x_prompt.shape
    bs, ts, _ = x_sample.shape
    past = cache_a_k.shape[2]

    cvecs = jnp.concatenate([c_ctx[None], c, jnp.zeros((8 - 1 - bs, D_MODEL), F32)], 0)
    mods = _modulation(cvecs, ada_w, ada_b)
    rope_tabs = _rope_tables(ts)
    ones = (jnp.arange(B_WIDTH)[:, None] // HEAD_DIM == jnp.arange(B_WIDTH)[None] // HEAD_DIM).astype(BF16)

    xp = x_prompt.reshape(bp * tp, D_MODEL)
    xs = x_sample.reshape(bs * ts, D_MODEL)

    def mod_rows(layer, rows):
        m = mods[layer, rows[0]:rows[1]][:, None, :]
        return m[..., :D_MODEL], m[..., D_MODEL:2 * D_MODEL], m[..., 2 * D_MODEL:]

    e = 0
    w_in0 = even_w_in[e].astype(BF16)
    w_out0 = w_out[0].astype(BF16)
    gain_pre = norm_pre[0][None]
    gain_post = norm_post[0][None]
    zpad = jnp.zeros((DECAY_LORA, B_WIDTH), F32)
    lora = jnp.concatenate([
        jnp.concatenate([b_w2[e, 0], b_w2[e, 1], zpad, zpad], 1),
        jnp.concatenate([zpad, zpad, b_a2[e, 0], b_a2[e, 1]], 1)], 0).astype(BF16)
    feat_args = (b_mu[e][None], lora, b_w0[e][:, None, :], b_a0[e][:, None, :], b_kk[e][None],
                 b_ka[e][None], b_rk[e][None], ones)

    def even_layer(x2d, bn, t, rows, ctx):
        shift, scale, gate = mod_rows(0, rows)
        (qa, ka, va, ga, gb, r, kkn, v, lw, b, ke, bonus, *kv_t) = _even_in(
            x2d, shift, scale, gain_pre, w_in0, t, None if ctx is None else rope_tabs, ctx is None,
            *feat_args)
        if ctx is None:
            ya = _gqa_context(a_sink[e], qa, ka, va, bn, t)
            s0 = None
        else:
            k_ctx, v_ctx, st0 = ctx
            ya = _gqa_window(a_sink[e], qa, ka, va, k_ctx, v_ctx, bn, t)
            s0 = _pair_states(st0)
        yf, yb, *s_fin = _rwkv_scan(r, kkn, v, lw, b, ke, s0, bn, t)
        y = _even_out(ya.reshape(bn * t, A_WIDTH), ga, yf.reshape(bn * t, B_WIDTH),
                      yb.reshape(bn * t, B_WIDTH), bonus, gb,
                      b_ln_w[e][None], b_ln_b[e][None], ones, x2d, gate, gain_post, w_out0, t)
        return y, kv_t, s_fin

    yp, (ka_t, va_t), st_p = even_layer(xp, bp, tp, (0, 1), None)
    feat_major = lambda a: jnp.transpose(a, (0, 2, 3, 1)).reshape(bs, A_KV_WIDTH, past)
    ctx_a = (feat_major(cache_a_k[:, e]), feat_major(cache_a_v[:, e]), state_rwkv[:, e])
    ys, _, _ = even_layer(xs, bs, ts, (1, 1 + bs), ctx_a)

    def cache_a(x_t):
        return jnp.transpose(x_t.reshape(bp, A_KV_HEADS, HEAD_DIM, tp), (0, 3, 1, 2))[:, None]

    o = 0
    lam_init = 0.8 - 0.6 * math.exp(-0.3 * 1)
    w_in1 = odd_w_in[o].astype(BF16)
    w_out1 = w_out[1].astype(BF16)
    gain_pre1 = norm_pre[1][None]
    gain_post1 = norm_post[1][None]
    lams = (c_lq1[o][None], c_lk1[o][None], c_lq2[o][None], c_lk2[o][None])
    subln = c_subln[o][None]

    def odd_layer(x2d, bn, t, rows, ctx):
        shift, scale, gate = mod_rows(1, rows)
        qc, kc, vc, gc = _proj_in(x2d, shift, scale, gain_pre1, w_in1, ODD_SEGS, (0, 1), t,
                                  None if ctx is None else rope_tabs,
                                  head_segs=(1, 2) if ctx is None else (),
                                  bf16_segs=() if ctx is None else (1, 2))
        if ctx is None:
            y = _diff_context_out(lams, subln, qc, kc, vc, gc, x2d, gate, gain_post1, w_out1, bn, t, lam_init)
        else:
            k_ctx, v_ctx = ctx
            oc = _diff_latent(lams, subln, qc, kc, vc, k_ctx, v_ctx, bn, t, lam_init)
            y = _odd_out(oc.reshape(bn * t, C_WIDTH), gc, x2d, gate, gain_post1, w_out1, t)
        return y, kc, vc

    yp2, kc_p, vc_p = odd_layer(yp, bp, tp, (0, 1), None)
    ctx_c = (cache_c_k[:, o].reshape(bs, past, C_WIDTH), cache_c_v[:, o].reshape(bs, past, C_WIDTH))
    ys2, _, _ = odd_layer(ys, bs, ts, (1, 1 + bs), ctx_c)

    return (yp2.reshape(bp, tp, D_MODEL),
            ys2.reshape(bs, ts, D_MODEL),
            cache_a(ka_t),
            cache_a(va_t),
            st_p[0][:, None],
            kc_p.reshape(bp, 1, tp, C_HEADS, 2 * C_QK_DIM),
            vc_p.reshape(bp, 1, tp, C_HEADS, C_V_DIM))
```

```python
import functools
import math

import jax
import jax.numpy as jnp
from jax import lax
from jax.experimental import pallas as pl
from jax.experimental.pallas import tpu as pltpu

F32 = jnp.float32
BF16 = jnp.bfloat16

D_MODEL = 1024
DEPTH = 2
GRID_W = 64
HEAD_DIM = 64
ROPE_BASE = 10000.0
A_HEADS = 8
A_KV_HEADS = 2
A_WIDTH = 512
A_KV_WIDTH = 128
WINDOW = 128
BLOCK = 128
B_HEADS = 8
B_WIDTH = 512
DECAY_LORA = 64
AAA_LORA = 64
B_SHIFT_WIDTH = 3 * B_WIDTH + DECAY_LORA + AAA_LORA
C_HEADS = 8
C_QK_DIM = 64
C_V_DIM = 128
C_WIDTH = 1024
EVEN_IN = 3456
ODD_IN = 4096
NORM_EPS = 1e-6
RWKV_LN_EPS = 64e-5
SUBLN_EPS = 1e-5
NEG_INF = -1e30

LANES = 128
HALO = 8
ROW_TILE = 256
OUT_TILE = 512
GQA_CTX_SEQS = 4
DIFF_CTX_SEQS = 2
DIFF_Q_TILE = 1024
DIFF_Q_SUB = 128
CHUNK = 64
SCAN_SEQS = 4
PAIR = 2 * CHUNK
VMEM_LIMIT = 48 * 1024 * 1024

LOG2E = math.log2(math.e)
QUERY_SCALE = HEAD_DIM ** -0.5 * LOG2E

NN = ((1,), (0,))
NT = ((1,), (1,))


def _bdot(a, b, dims=NN):
    return lax.dot_general(a.astype(BF16), b.astype(BF16), (dims, ((), ())),
                           preferred_element_type=F32)


def _split_dot(sel_bf16, x):
    x1 = x.astype(BF16)
    r1 = x - x1.astype(F32)
    x2 = r1.astype(BF16)
    x3 = (r1 - x2.astype(F32)).astype(BF16)
    dot = lambda p: jnp.dot(sel_bf16, p, preferred_element_type=F32)
    return dot(x1) + dot(x2) + dot(x3)


def _head_sum(x, ones_bf16):
    hi = x.astype(BF16)
    lo = (x - hi.astype(F32)).astype(BF16)
    return (jnp.dot(hi, ones_bf16, preferred_element_type=F32)
            + jnp.dot(lo, ones_bf16, preferred_element_type=F32))


def _sigmoid(x):
    return 1.0 / (1.0 + jnp.exp(-x))


def _silu(x):
    return x * _sigmoid(x)


def _params(n_axes):
    return pltpu.CompilerParams(dimension_semantics=("arbitrary",) * n_axes,
                                vmem_limit_bytes=VMEM_LIMIT)


def _lane_lo(shape):
    return (lax.broadcasted_iota(jnp.int32, shape, len(shape) - 1) % LANES) < HEAD_DIM


def _stack_pair(x):
    lo = _lane_lo(x.shape)
    z = jnp.zeros_like(x)
    return jnp.concatenate([jnp.where(lo, x, z), jnp.where(lo, z, x)], axis=0)


def _dup_half(x, g):
    lo = _lane_lo(x.shape)
    sw = pltpu.roll(x, HEAD_DIM, 1)
    return jnp.where(lo, x, sw) if g == 0 else jnp.where(lo, sw, x)


def _rope(x, cos, sin_signed):
    out = []
    even_q = (lax.broadcasted_iota(jnp.int32, cos.shape, 1) // (HEAD_DIM // 4)) % 2 == 0
    for g in range(x.shape[1] // LANES):
        xg = x[:, g * LANES:(g + 1) * LANES]
        up = pltpu.roll(xg, LANES - HEAD_DIM // 4, 1)
        dn = pltpu.roll(xg, HEAD_DIM // 4, 1)
        out.append(xg * cos + jnp.where(even_q, up, dn) * sin_signed)
    return out


def _mod_kernel(c_ref, w_ref, b_ref, o_ref, *, n_used):
    s_t = _silu(c_ref[...]).T
    w = w_ref[...]
    rows = [jnp.sum(w * s_t[:, r:r + 1], axis=0, keepdims=True) for r in range(n_used)]
    rows.append(jnp.zeros((c_ref.shape[0] - n_used, w.shape[1]), F32))
    o_ref[...] = jnp.concatenate(rows, axis=0) + b_ref[...]


def _modulation(cvecs, n_used, ada_w, ada_b):
    nrow = cvecs.shape[0]
    return pl.pallas_call(
        functools.partial(_mod_kernel, n_used=n_used),
        grid=(DEPTH, 3),
        in_specs=[pl.BlockSpec((nrow, D_MODEL), lambda l, j: (0, 0)),
                  pl.BlockSpec((None, D_MODEL, D_MODEL), lambda l, j: (l, 0, j)),
                  pl.BlockSpec((None, 1, D_MODEL), lambda l, j: (l, 0, j))],
        out_specs=pl.BlockSpec((None, nrow, D_MODEL), lambda l, j: (l, 0, j)),
        out_shape=jax.ShapeDtypeStruct((DEPTH, nrow, 3 * D_MODEL), F32),
        compiler_params=_params(2),
        name="modulation",
    )(cvecs, ada_w, ada_b.reshape(DEPTH, 1, 3 * D_MODEL))


def _mod_index(mod, tiles_per_seq):
    if mod.shape[0] == 1:
        return lambda i: (0, 0, 0)
    return lambda i: (i // tiles_per_seq, 0, 0)


def _proj_in_kernel(*refs, segs, rope_segs, use_rope, head_segs, bf16_segs):
    if use_rope:
        x_ref, sh_ref, sc_ref, g_ref, w_ref, cos_ref, sin_ref = refs[:7]
        outs = refs[7:]
    else:
        x_ref, sh_ref, sc_ref, g_ref, w_ref = refs[:5]
        outs = refs[5:]
    x = x_ref[...]
    y = x * lax.rsqrt(jnp.mean(x * x, -1, keepdims=True) + NORM_EPS)
    h = (y * g_ref[...]) * (1.0 + sc_ref[...]) + sh_ref[...]
    hb = h.astype(BF16)
    for idx, (lo, hi) in enumerate(segs):
        o = jnp.dot(hb, w_ref[:, lo:hi], preferred_element_type=F32)
        if idx == 0:
            post = lambda a: (a * QUERY_SCALE).astype(BF16)
        elif idx in bf16_segs:
            post = lambda a: a.astype(BF16)
        else:
            post = lambda a: a
        if use_rope and idx in rope_segs:
            parts = _rope(o, cos_ref[...], sin_ref[...])
            for g, part in enumerate(parts):
                outs[idx][:, g * LANES:(g + 1) * LANES] = post(part)
        elif idx in head_segs:
            outs[idx][...] = o.reshape(o.shape[0], (hi - lo) // LANES, LANES)
        else:
            outs[idx][...] = post(o)


def _proj_in(x2d, shift, scale, gain, w_bf16, segs, rope_segs, t, rope_tabs, head_segs=(), bf16_segs=()):
    n = x2d.shape[0]
    tile = OUT_TILE
    assert shift.shape[0] == 1 or t % tile == 0
    nt = n // tile
    tiles_per_seq = max(t // tile, 1)
    use_rope = rope_tabs is not None
    row = lambda i: (i, 0)
    per_b = _mod_index(shift, tiles_per_seq)
    in_specs = [pl.BlockSpec((tile, D_MODEL), row),
                pl.BlockSpec((None, 1, D_MODEL), per_b),
                pl.BlockSpec((None, 1, D_MODEL), per_b),
                pl.BlockSpec((1, D_MODEL), lambda i: (0, 0)),
                pl.BlockSpec(w_bf16.shape, lambda i: (0, 0))]
    args = [x2d, shift, scale, gain, w_bf16]
    if use_rope:
        pos = lambda i: (i % tiles_per_seq, 0)
        in_specs += [pl.BlockSpec((tile, LANES), pos), pl.BlockSpec((tile, LANES), pos)]
        args += list(rope_tabs)
    widths = [hi - lo for lo, hi in segs]
    out_specs, out_shape = [], []
    for idx, w in enumerate(widths):
        if idx in head_segs:
            out_specs.append(pl.BlockSpec((tile, w // LANES, LANES), lambda i: (i, 0, 0)))
            out_shape.append(jax.ShapeDtypeStruct((n, w // LANES, LANES), F32))
        else:
            out_specs.append(pl.BlockSpec((tile, w), row))
            out_shape.append(jax.ShapeDtypeStruct((n, w), BF16 if idx == 0 or idx in bf16_segs else F32))
    return pl.pallas_call(
        functools.partial(_proj_in_kernel, segs=segs, rope_segs=rope_segs, use_rope=use_rope,
                          head_segs=tuple(head_segs), bf16_segs=tuple(bf16_segs)),
        grid=(nt,),
        in_specs=in_specs,
        out_specs=out_specs,
        out_shape=out_shape,
        compiler_params=_params(1),
        name="proj_in",
    )(*args)


def _sink_exp_pair(s2, sink_a, sink_b, n):
    rowi = lax.broadcasted_iota(jnp.int32, (2 * n, 1), 0)
    sink2 = jnp.where(rowi < n, sink_a, sink_b) * LOG2E
    m = jnp.maximum(jnp.max(s2, -1, keepdims=True), sink2)
    e = jnp.exp2(s2 - m)
    return e, 1.0 / (jnp.sum(e, -1, keepdims=True) + jnp.exp2(sink2 - m))


def _gqa_ctx_kernel(sink_ref, q_ref, k_ref, v_ref, o_ref):
    nseq, t = q_ref.shape[:2]
    pairs = [(s, p) for s in range(nseq) for p in range(A_HEADS // 2)]
    kd = [[_dup_half(k_ref[s], g).astype(BF16) for g in range(A_KV_HEADS)] for s in range(nseq)]
    vd = [[_dup_half(v_ref[s], g).astype(BF16) for g in range(A_KV_HEADS)] for s in range(nseq)]
    s2 = [_bdot(_stack_pair(q_ref[s, :, p * LANES:(p + 1) * LANES]), kd[s][p // 2], NT)
          for s, p in pairs]
    for i, (s, p) in enumerate(pairs):
        e, inv = _sink_exp_pair(s2[i], sink_ref[2 * p], sink_ref[2 * p + 1], t)
        o = _bdot(e, vd[s][p // 2]) * inv
        o_ref[s, :, p * LANES:(p + 1) * LANES] = jnp.where(_lane_lo((t, LANES)), o[:t], o[t:])


def _gqa_context(sink, q, k, v, bn, t):
    seq = lambda b: (b, 0, 0)
    ns = GQA_CTX_SEQS
    return pl.pallas_call(
        _gqa_ctx_kernel,
        grid=(bn // ns,),
        in_specs=[pl.BlockSpec(memory_space=pltpu.SMEM),
                  pl.BlockSpec((ns, t, A_WIDTH), seq),
                  pl.BlockSpec((ns, t, A_KV_WIDTH), seq),
                  pl.BlockSpec((ns, t, A_KV_WIDTH), seq)],
        out_specs=pl.BlockSpec((ns, t, A_WIDTH), seq),
        out_shape=jax.ShapeDtypeStruct((bn, t, A_WIDTH), F32),
        compiler_params=_params(1),
        name="gqa_context",
    )(sink, q.reshape(bn, t, A_WIDTH), k.reshape(bn, t, A_KV_WIDTH), v.reshape(bn, t, A_KV_WIDTH))


def _gqa_win_kernel(sink_ref, q_ref, kp_ref, kc_ref, kn_ref, vp_ref, vc_ref, vn_ref,
                    kx_ref, vx_ref, o_ref, *, t):
    nblk = pl.program_id(1)
    nctx = kx_ref.shape[1]
    keys = jnp.concatenate([kx_ref[...].T, kp_ref[...], kc_ref[...], kn_ref[...]], axis=0)
    vals = jnp.concatenate([vx_ref[...].T, vp_ref[...], vc_ref[...], vn_ref[...]], axis=0)
    nkeys = nctx + 3 * BLOCK
    rowi = lax.broadcasted_iota(jnp.int32, (2 * BLOCK, nkeys), 0)
    coli = lax.broadcasted_iota(jnp.int32, (2 * BLOCK, nkeys), 1)
    qpos = nblk * BLOCK + rowi % BLOCK
    kpos = nblk * BLOCK + coli - nctx - BLOCK
    valid = (coli < nctx) | ((jnp.abs(qpos - kpos) <= WINDOW) & (kpos >= 0) & (kpos < t))
    kd = [_dup_half(keys, g).astype(BF16) for g in range(A_KV_HEADS)]
    vd = [_dup_half(vals, g).astype(BF16) for g in range(A_KV_HEADS)]
    s2 = [_bdot(_stack_pair(q_ref[:, p * LANES:(p + 1) * LANES]),
                kd[p // 2], NT) for p in range(A_HEADS // 2)]
    for p in range(A_HEADS // 2):
        e, inv = _sink_exp_pair(jnp.where(valid, s2[p], NEG_INF), sink_ref[2 * p], sink_ref[2 * p + 1], BLOCK)
        o = _bdot(e, vd[p // 2]) * inv
        o_ref[:, p * LANES:(p + 1) * LANES] = jnp.where(_lane_lo((BLOCK, LANES)), o[:BLOCK], o[BLOCK:])


def _gqa_window(sink, q, k, v, k_ctx_t, v_ctx_t, bn, t):
    nb = t // BLOCK
    nctx = k_ctx_t.shape[2]
    cur = lambda b, n: (b, n, 0)
    prev = lambda b, n: (b, jnp.maximum(n - 1, 0), 0)
    nxt = lambda b, n: (b, jnp.minimum(n + 1, nb - 1), 0)
    ctx = lambda b, n: (b, 0, 0)
    kv = lambda f: pl.BlockSpec((None, BLOCK, A_KV_WIDTH), f)
    k3 = k.reshape(bn, t, A_KV_WIDTH)
    v3 = v.reshape(bn, t, A_KV_WIDTH)
    return pl.pallas_call(
        functools.partial(_gqa_win_kernel, t=t),
        grid=(bn, nb),
        in_specs=[pl.BlockSpec(memory_space=pltpu.SMEM),
                  pl.BlockSpec((None, BLOCK, A_WIDTH), cur),
                  kv(prev), kv(cur), kv(nxt), kv(prev), kv(cur), kv(nxt),
                  pl.BlockSpec((None, A_KV_WIDTH, nctx), ctx),
                  pl.BlockSpec((None, A_KV_WIDTH, nctx), ctx)],
        out_specs=pl.BlockSpec((None, BLOCK, A_WIDTH), cur),
        out_shape=jax.ShapeDtypeStruct((bn, t, A_WIDTH), F32),
        compiler_params=_params(2),
        name="gqa_window",
    )(sink, q.reshape(bn, t, A_WIDTH), k3, k3, k3, v3, v3, v3, k_ctx_t, v_ctx_t)


def _even_in_kernel(*refs, use_rope, emit_t, tiles_per_seq):
    x_ref, xp_ref, xn_ref, sh_ref, sc_ref, g_ref, w_ref = refs[:7]
    refs = refs[7:]
    if use_rope:
        cos_ref, sin_ref = refs[:2]
        refs = refs[2:]
    mu_ref, lora_ref, w0_ref, a0_ref, kk_ref, ka_ref, rk_ref, ones_ref = refs[:8]
    (q_out, k_out, v_out, ga_out, gb_out, r_out, kkn_out, vv_out, lw_out, b_out, ke_out,
     bonus_out) = refs[8:20]
    i = pl.program_id(0)
    tm = x_ref.shape[0]
    pos = i % tiles_per_seq

    def norm_mod(x):
        y = x * lax.rsqrt(jnp.mean(x * x, -1, keepdims=True) + NORM_EPS)
        return (y * g_ref[...]) * (1.0 + sc_ref[...]) + sh_ref[...]

    h = norm_mod(x_ref[...])
    hb = h.astype(BF16)
    seg = dict(zip(("q", "k", "v", "ga", "pb", "gb"), EVEN_SEGS))
    proj = lambda name, lhs=hb: jnp.dot(lhs, w_ref[:, seg[name][0]:seg[name][1]], preferred_element_type=F32)
    h_ext = jnp.concatenate([norm_mod(xp_ref[...]), h, norm_mod(xn_ref[...])], 0).astype(BF16)
    p_ext = proj("pb", h_ext)

    q = proj("q")
    if use_rope:
        q = jnp.concatenate(_rope(q, cos_ref[...], sin_ref[...]), 1)
    q_out[...] = (q * QUERY_SCALE).astype(BF16)

    p = p_ext[HALO:HALO + tm]
    p_ext = jnp.concatenate([jnp.where(pos != 0, p_ext[:HALO], 0.0), p,
                             jnp.where(pos != tiles_per_seq - 1, p_ext[HALO + tm:], 0.0)], 0)
    prev = pltpu.roll(p_ext, 1, 0)[HALO:HALO + tm]
    nxt = pltpu.roll(p_ext, tm + 2 * HALO - 1, 0)[HALO:HALO + tm]
    xf = p + (0.5 * (prev + nxt) - p) * mu_ref[...]
    r = xf[:, 0:B_WIDTH]
    k = xf[:, B_WIDTH:2 * B_WIDTH]
    v = xf[:, 2 * B_WIDTH:3 * B_WIDTH]
    g = xf[:, 3 * B_WIDTH:]
    r_out[...] = r
    vv_out[...] = v

    kv_att = jnp.dot(hb, w_ref[:, seg["k"][0]:seg["v"][1]], preferred_element_type=F32)
    k_att, v_att = kv_att[:, :A_KV_WIDTH], kv_att[:, A_KV_WIDTH:]
    if use_rope:
        k_att = _rope(k_att, cos_ref[...], sin_ref[...])[0]
    k_out[...] = k_att
    v_out[...] = v_att
    if emit_t:
        refs[20][...] = k_att.T
        refs[21][...] = v_att.T

    g = jnp.where(_lane_lo(g.shape), jnp.tanh(g), g)
    lo = _bdot(g, lora_ref[...])
    ones = ones_ref[...]
    kkr = k * kk_ref[...]
    kkn = kkr / jnp.maximum(jnp.sqrt(_head_sum(kkr * kkr, ones)), 1e-12)
    kkn_out[...] = kkn

    ga_out[...] = proj("ga")

    ke_sum = jnp.zeros_like(k)
    for z in range(2):
        wz = w0_ref[z] + lo[:, z * B_WIDTH:(z + 1) * B_WIDTH]
        lw_out[z] = -math.exp(-0.5) * _sigmoid(wz)
        az = _sigmoid(a0_ref[z] + lo[:, (2 + z) * B_WIDTH:(3 + z) * B_WIDTH])
        ke = k * (1.0 + (az - 1.0) * ka_ref[...])
        ke_out[z] = ke
        b_out[z] = kkn * az
        ke_sum = ke_sum + ke
        if z == 0:
            gb_out[...] = proj("gb")
    bonus_out[...] = _head_sum(r * (0.5 * ke_sum) * rk_ref[...], ones) * v


def _even_in(x2d, shift, scale, gain, w_bf16, t, rope_tabs, emit_t, mu, lora, w0, a0, k_k, k_a, r_k, ones):
    n = x2d.shape[0]
    tile = ROW_TILE if emit_t else OUT_TILE
    assert t % tile == 0 and (not emit_t or t == tile)
    nt = n // tile
    tiles_per_seq = t // tile
    h8 = tile // HALO
    use_rope = rope_tabs is not None
    row = lambda i: (i, 0)
    const2 = lambda i: (0, 0)
    const3 = lambda i: (0, 0, 0)
    per_b = _mod_index(shift, tiles_per_seq)
    vec = pl.BlockSpec((1, B_WIDTH), const2)
    in_specs = [pl.BlockSpec((tile, D_MODEL), row),
                pl.BlockSpec((HALO, D_MODEL), lambda i: (jnp.maximum(i * h8 - 1, 0), 0)),
                pl.BlockSpec((HALO, D_MODEL), lambda i: (jnp.minimum((i + 1) * h8, n // HALO - 1), 0)),
                pl.BlockSpec((None, 1, D_MODEL), per_b),
                pl.BlockSpec((None, 1, D_MODEL), per_b),
                pl.BlockSpec((1, D_MODEL), const2),
                pl.BlockSpec(w_bf16.shape, const2)]
    args = [x2d, x2d, x2d, shift, scale, gain, w_bf16]
    if use_rope:
        pos = lambda i: (i % tiles_per_seq, 0)
        in_specs += [pl.BlockSpec((tile, LANES), pos), pl.BlockSpec((tile, LANES), pos)]
        args += list(rope_tabs)
    in_specs += [pl.BlockSpec((1, B_SHIFT_WIDTH), const2),
                 pl.BlockSpec(lora.shape, const2),
                 pl.BlockSpec((2, 1, B_WIDTH), const3),
                 pl.BlockSpec((2, 1, B_WIDTH), const3),
                 vec, vec, vec,
                 pl.BlockSpec((B_WIDTH, B_WIDTH), const2)]
    args += [mu, lora, w0, a0, k_k, k_a, r_k, ones]
    wide = pl.BlockSpec((tile, B_WIDTH), row)
    narrow = pl.BlockSpec((tile, A_KV_WIDTH), row)
    dual = pl.BlockSpec((2, tile, B_WIDTH), lambda i: (0, i, 0))
    f_wide = jax.ShapeDtypeStruct((n, B_WIDTH), F32)
    f_narrow = jax.ShapeDtypeStruct((n, A_KV_WIDTH), F32)
    f_dual = jax.ShapeDtypeStruct((2, n, B_WIDTH), F32)
    out_specs = [wide, narrow, narrow, wide, wide, wide, wide, wide, dual, dual, dual, wide]
    out_shape = [jax.ShapeDtypeStruct((n, A_WIDTH), BF16), f_narrow, f_narrow, f_wide, f_wide,
                 f_wide, f_wide, f_wide, f_dual, f_dual, f_dual, f_wide]
    if emit_t:
        out_specs += [pl.BlockSpec((None, A_KV_WIDTH, tile), lambda i: (i, 0, 0))] * 2
        out_shape += [jax.ShapeDtypeStruct((nt, A_KV_WIDTH, tile), F32)] * 2
    return pl.pallas_call(
        functools.partial(_even_in_kernel, use_rope=use_rope, emit_t=emit_t, tiles_per_seq=tiles_per_seq),
        grid=(nt,),
        in_specs=in_specs,
        out_specs=out_specs,
        out_shape=out_shape,
        compiler_params=_params(1),
        name="even_in",
    )(*args)


def _rwkv_scan_kernel(*refs, n_chunks, from_zero):
    (rf_ref, kf_ref, vf_ref, rb_ref, kb_ref, vb_ref, lwf_ref, bf_ref, kef_ref,
     lwb_ref, bb_ref, keb_ref) = refs[:12]
    if from_zero:
        yf_ref, yb_ref, sout_ref, s_scr = refs[12:]
    else:
        s0_ref, yf_ref, yb_ref, s_scr = refs[12:]
    c = pl.program_id(1)

    @pl.when(c == 0)
    def _():
        if from_zero:
            s_scr[...] = jnp.zeros_like(s_scr)
        else:
            s_scr[...] = s0_ref[...]

    row = lax.broadcasted_iota(jnp.int32, (CHUNK, LANES), 0)
    li = lax.broadcasted_iota(jnp.int32, (CHUNK, LANES), 1) % CHUNK
    blk16 = (row // 16) == (li // 16)
    blk32 = (row // 32) == (li // 32)
    eye = jnp.where(row == li, 1.0, 0.0)
    srow = lax.broadcasted_iota(jnp.int32, (PAIR, PAIR), 0)
    scol = lax.broadcasted_iota(jnp.int32, (PAIR, PAIR), 1)
    same_head = (srow // CHUNK) == (scol // CHUNK)
    tr = lax.broadcasted_iota(jnp.int32, (CHUNK, CHUNK), 0)
    tc = lax.broadcasted_iota(jnp.int32, (CHUNK, CHUNK), 1)
    dirs = ((rf_ref, kf_ref, vf_ref, lwf_ref, bf_ref, kef_ref, yf_ref),
            (rb_ref, kb_ref, vb_ref, lwb_ref, bb_ref, keb_ref, yb_ref))

    def advance(s):
        chains = [(z, p) for z in range(2) for p in range(B_HEADS // 2)]
        sls = [slice(p * LANES, (p + 1) * LANES) for _, p in chains]
        before, before_eq, tot, kkd, rd, bg, kg, vv, sc = [], [], [], [], [], [], [], [], []
        for z in range(2):
            r_ref, kk_ref, v_ref, lw_ref, b_ref, ke_ref, _ = dirs[z]
            diff = (li - row) if z else (row - li)
            tri = jnp.where(((tc - tr) if z else (tr - tc)) >= 0, 1.0, 0.0).astype(BF16)
            lw_all = lw_ref[s]
            cum_all = _split_dot(tri, lw_all)
            for p in range(B_HEADS // 2):
                sl = slice(p * LANES, (p + 1) * LANES)
                lw = lw_all[:, sl]
                cum = cum_all[:, sl]
                tt = cum[0:1] if z else cum[CHUNK - 1:CHUNK]
                b = b_ref[s, :, sl]
                ke = ke_ref[s, :, sl]
                g_inv = jnp.exp(-cum)
                g_rest = jnp.exp(tt - cum)
                before.append(diff > 0)
                before_eq.append(diff >= 0)
                tot.append(tt)
                kkd.append(kk_ref[s, :, sl] * jnp.exp(cum - lw))
                rd.append(r_ref[s, :, sl] * jnp.exp(cum))
                bg.append(b * g_rest)
                kg.append(ke * g_rest)
                vv.append(v_ref[s, :, sl])
                sc.append(_bdot(jnp.concatenate([kkd[-1], rd[-1]], 0),
                                jnp.concatenate([_stack_pair(b * g_inv), _stack_pair(ke * g_inv)], 0), NT))
            yield
        lb = [jnp.where(m, x[:CHUNK, :LANES], 0.0) for m, x in zip(before, sc)]
        mb = [jnp.where(m, x[CHUNK:, :LANES], 0.0) for m, x in zip(before_eq, sc)]
        lv = [_bdot(jnp.concatenate([jnp.where(m, x[:CHUNK, LANES:], 0.0),
                                     jnp.where(me, x[CHUNK:, LANES:], 0.0)], 0), _stack_pair(v))
              for m, me, x, v in zip(before, before_eq, sc, vv)]
        yield

        pw = [jnp.where(blk16, x, 0.0) for x in lb]
        tinv = [eye - x for x in pw]
        for _ in range(3):
            pw = [_bdot(x, _stack_pair(x)) for x in pw]
            yield
            tinv = [t + _bdot(t, _stack_pair(x)) for t, x in zip(tinv, pw)]
            yield
        for off in (blk32 & ~blk16, ~blk32):
            tmp = [_bdot(t, _stack_pair(jnp.where(off, x, 0.0))) for t, x in zip(tinv, lb)]
            yield
            tinv = [t - _bdot(m, _stack_pair(t)) for t, m in zip(tinv, tmp)]
            yield

        w12 = [_bdot(t, jnp.concatenate([_stack_pair(k), _stack_pair(x[:CHUNK])], 1))
               for t, k, x in zip(tinv, kkd, lv)]
        yield
        s_old = [s_scr[s, z, p] for z, p in chains]
        ws = [_bdot(jnp.concatenate([w[:, :LANES], r], 0), st, NT) for w, r, st in zip(w12, rd, s_old)]
        yield
        u = [-(a[:CHUNK] + w[:, LANES:]) for a, w in zip(ws, w12)]
        mbu = [_bdot(m, _stack_pair(x)) for m, x in zip(mb, u)]
        upd = [_bdot(jnp.concatenate([x, v], 0).T, jnp.concatenate([b_, k_], 0))
               for x, v, b_, k_ in zip(u, vv, bg, kg)]
        yield
        for i, (z, p) in enumerate(chains):
            dirs[z][6][s, :, sls[i]] = ws[i][CHUNK:] + mbu[i] + lv[i][CHUNK:]
            s_scr[s, z, p] = s_old[i] * jnp.exp(tot[i]) + jnp.where(same_head, upd[i], 0.0)

    live = [advance(s) for s in range(rf_ref.shape[0])]
    while live:
        live = [gen for gen in live if next(gen, "done") != "done"]

    if from_zero:
        @pl.when(c == n_chunks - 1)
        def _():
            for s in range(s_scr.shape[0]):
                for z in range(2):
                    for p in range(B_HEADS // 2):
                        st = s_scr[s, z, p]
                        sout_ref[s, z, 2 * p] = st[:HEAD_DIM, :HEAD_DIM]
                        sout_ref[s, z, 2 * p + 1] = pltpu.roll(st[HEAD_DIM:, :], HEAD_DIM, 1)[:, :HEAD_DIM]


def _rwkv_scan(r, kkn, v, lw, b, ke, s0, bn, t):
    nc = t // CHUNK
    npair = B_HEADS // 2
    ns = min(SCAN_SEQS, bn)
    fwd = pl.BlockSpec((ns, CHUNK, B_WIDTH), lambda g, c: (g, c, 0))
    bwd = pl.BlockSpec((ns, CHUNK, B_WIDTH), lambda g, c: (g, nc - 1 - c, 0))
    dfwd = pl.BlockSpec((None, ns, CHUNK, B_WIDTH), lambda g, c: (0, g, c, 0))
    dbwd = pl.BlockSpec((None, ns, CHUNK, B_WIDTH), lambda g, c: (1, g, nc - 1 - c, 0))
    r3, k3, v3 = (a.reshape(bn, t, B_WIDTH) for a in (r, kkn, v))
    lw4, b4, ke4 = (a.reshape(2, bn, t, B_WIDTH) for a in (lw, b, ke))
    seq = jax.ShapeDtypeStruct((bn, t, B_WIDTH), F32)
    in_specs = [fwd, fwd, fwd, bwd, bwd, bwd, dfwd, dfwd, dfwd, dbwd, dbwd, dbwd]
    args = [r3, k3, v3, r3, k3, v3, lw4, b4, ke4, lw4, b4, ke4]
    out_specs = [fwd, bwd]
    out_shape = [seq, seq]
    if s0 is None:
        out_specs.append(pl.BlockSpec((ns, 2, B_HEADS, HEAD_DIM, HEAD_DIM), lambda g, c: (g, 0, 0, 0, 0)))
        out_shape.append(jax.ShapeDtypeStruct((bn, 2, B_HEADS, HEAD_DIM, HEAD_DIM), F32))
    else:
        in_specs.append(pl.BlockSpec((ns, 2, npair, PAIR, PAIR), lambda g, c: (g, 0, 0, 0, 0)))
        args.append(s0)
    return pl.pallas_call(
        functools.partial(_rwkv_scan_kernel, n_chunks=nc, from_zero=s0 is None),
        grid=(bn // ns, nc),
        in_specs=in_specs,
        out_specs=out_specs,
        out_shape=out_shape,
        scratch_shapes=[pltpu.VMEM((ns, 2, npair, PAIR, PAIR), F32)],
        compiler_params=_params(2),
        name="rwkv_scan",
    )(*args)


def _post(x, z, gate, gain):
    zn = z * lax.rsqrt(jnp.mean(z * z, -1, keepdims=True) + NORM_EPS)
    return x + gate * (zn * gain)


def _even_out_kernel(ya_ref, ga_ref, yf_ref, yb_ref, bonus_ref, gb_ref, lnw_ref, lnb_ref, ones_ref,
                     x_ref, gate_ref, gain_ref, w_ref, o_ref):
    y = yf_ref[...] + yb_ref[...]
    ones = ones_ref[...]
    mean = _head_sum(y, ones) * (1.0 / HEAD_DIM)
    d = y - mean
    var = _head_sum(d * d, ones) * (1.0 / HEAD_DIM)
    yb = (d * lax.rsqrt(var + RWKV_LN_EPS)) * lnw_ref[...] + lnb_ref[...] + bonus_ref[...]
    za = (ya_ref[...] * _silu(ga_ref[...])).astype(BF16)
    zb = (yb * _silu(gb_ref[...])).astype(BF16)
    z = (jnp.dot(za, w_ref[0:A_WIDTH, :], preferred_element_type=F32)
         + jnp.dot(zb, w_ref[A_WIDTH:, :], preferred_element_type=F32))
    o_ref[...] = _post(x_ref[...], z, gate_ref[...], gain_ref[...])


def _even_out(ya, ga, yf, yb, bonus, gb, ln_w, ln_b, ones, x2d, gate, gain, w_bf16, t):
    n = x2d.shape[0]
    row = lambda i: (i, 0)
    const2 = lambda i: (0, 0)
    half = pl.BlockSpec((OUT_TILE, B_WIDTH), row)
    full = pl.BlockSpec((OUT_TILE, D_MODEL), row)
    vec = pl.BlockSpec((1, B_WIDTH), const2)
    assert gate.shape[0] == 1 or t % OUT_TILE == 0
    return pl.pallas_call(
        _even_out_kernel,
        grid=(n // OUT_TILE,),
        in_specs=[half, half, half, half, half, half, vec, vec,
                  pl.BlockSpec((B_WIDTH, B_WIDTH), const2),
                  full,
                  pl.BlockSpec((None, 1, D_MODEL), _mod_index(gate, t // OUT_TILE)),
                  pl.BlockSpec((1, D_MODEL), const2),
                  pl.BlockSpec((D_MODEL, D_MODEL), const2)],
        out_specs=full,
        out_shape=jax.ShapeDtypeStruct((n, D_MODEL), F32),
        compiler_params=_params(1),
        name="even_out",
    )(ya, ga, yf, yb, bonus, gb, ln_w, ln_b, ones, x2d, gate, gain, w_bf16)


def _odd_out_kernel(o_ref_in, gc_ref, x_ref, gate_ref, gain_ref, w_ref, o_ref):
    zc = (o_ref_in[...] * _silu(gc_ref[...])).astype(BF16)
    z = jnp.dot(zc, w_ref[...], preferred_element_type=F32)
    o_ref[...] = _post(x_ref[...], z, gate_ref[...], gain_ref[...])


def _odd_out(o, gc, x2d, gate, gain, w_bf16, t):
    n = x2d.shape[0]
    row = lambda i: (i, 0)
    const2 = lambda i: (0, 0)
    full = pl.BlockSpec((OUT_TILE, D_MODEL), row)
    assert gate.shape[0] == 1 or t % OUT_TILE == 0
    return pl.pallas_call(
        _odd_out_kernel,
        grid=(n // OUT_TILE,),
        in_specs=[full, full, full,
                  pl.BlockSpec((None, 1, D_MODEL), _mod_index(gate, t // OUT_TILE)),
                  pl.BlockSpec((1, D_MODEL), const2),
                  pl.BlockSpec((D_MODEL, D_MODEL), const2)],
        out_specs=full,
        out_shape=jax.ShapeDtypeStruct((n, D_MODEL), F32),
        compiler_params=_params(1),
        name="odd_out",
    )(o, gc, x2d, gate, gain, w_bf16)


def _lambda(lq1_ref, lk1_ref, lq2_ref, lk2_ref, lam_init):
    s1 = jnp.sum(lq1_ref[...] * lk1_ref[...], -1, keepdims=True)
    s2 = jnp.sum(lq2_ref[...] * lk2_ref[...], -1, keepdims=True)
    return jnp.exp(s1) - jnp.exp(s2) + lam_init


def _subln(o, gain, lam_init):
    on = o * lax.rsqrt(jnp.mean(o * o, -1, keepdims=True) + SUBLN_EPS)
    return (on * gain) * (1.0 - lam_init)


def _diff_ctx_kernel(lq1_ref, lk1_ref, lq2_ref, lk2_ref, sub_ref, q_ref, k_ref, v_ref,
                     gc_ref, x_ref, gate_ref, gain_ref, w_ref, y_ref, o_scr, *, lam_init):
    nseq, t = q_ref.shape[:2]
    lam = _lambda(lq1_ref, lk1_ref, lq2_ref, lk2_ref, lam_init)
    units = [(s, slice(h * LANES, (h + 1) * LANES)) for s in range(nseq) for h in range(C_HEADS)]
    k = [k_ref[s].reshape(t, C_WIDTH) for s in range(nseq)]
    v = [v_ref[s].reshape(t, C_WIDTH) for s in range(nseq)]
    s2 = [_bdot(_stack_pair(q_ref[s, :, sl]), k[s][:, sl], NT)
          for s, sl in units]
    for i, (s, sl) in enumerate(units):
        e = jnp.exp2(s2[i] - jnp.max(s2[i], -1, keepdims=True))
        ov = _bdot(e, v[s][:, sl]) * (1.0 / jnp.sum(e, -1, keepdims=True))
        o_scr[s * t:(s + 1) * t, sl] = _subln(ov[:t] - lam * ov[t:], sub_ref[...], lam_init)
    gc = gc_ref[...].reshape(nseq * t, C_WIDTH)
    zc = (o_scr[...] * _silu(gc)).astype(BF16)
    z = jnp.dot(zc, w_ref[...], preferred_element_type=F32)
    y = _post(x_ref[...].reshape(nseq * t, D_MODEL), z, gate_ref[...], gain_ref[...])
    y_ref[...] = y.reshape(nseq, t, D_MODEL)


def _diff_context_out(lams, subln, q, k, v, gc, x2d, gate, gain, w_bf16, bn, t, lam_init):
    ns = DIFF_CTX_SEQS
    seq = pl.BlockSpec((ns, t, C_WIDTH), lambda b: (b, 0, 0))
    heads = pl.BlockSpec((ns, t, C_HEADS, LANES), lambda b: (b, 0, 0, 0))
    small = pl.BlockSpec((1, C_QK_DIM), lambda b: (0, 0))
    const2 = lambda b: (0, 0)
    to3 = lambda a: a.reshape((bn, t) + a.shape[1:])
    return pl.pallas_call(
        functools.partial(_diff_ctx_kernel, lam_init=lam_init),
        grid=(bn // ns,),
        in_specs=[small, small, small, small, pl.BlockSpec((1, C_V_DIM), const2),
                  seq, heads, heads, seq, seq,
                  pl.BlockSpec((None, 1, D_MODEL), lambda b: (0, 0, 0)),
                  pl.BlockSpec((1, D_MODEL), const2),
                  pl.BlockSpec((D_MODEL, D_MODEL), const2)],
        out_specs=seq,
        out_shape=jax.ShapeDtypeStruct((bn, t, D_MODEL), F32),
        scratch_shapes=[pltpu.VMEM((ns * t, C_WIDTH), F32)],
        compiler_params=_params(1),
        name="diff_context",
    )(*lams, subln, to3(q), to3(k), to3(v), to3(gc), to3(x2d), gate, gain, w_bf16).reshape(bn * t, D_MODEL)


def _diff_lat_kernel(lq1_ref, lk1_ref, lq2_ref, lk2_ref, sub_ref, q_ref, k_ref, v_ref, kx_ref, vx_ref,
                     o_ref, *, lam_init):
    lam = _lambda(lq1_ref, lk1_ref, lq2_ref, lk2_ref, lam_init)
    kx = kx_ref[...].astype(BF16)
    k = k_ref[...].astype(BF16)
    tq = DIFF_Q_SUB
    subs = range(q_ref.shape[0] // tq)
    scores = []
    for i in subs:
        qs = _stack_pair(q_ref[i * tq:(i + 1) * tq, :])
        scores.append((_bdot(qs, kx, NT), _bdot(qs, k, NT)))
    for i in subs:
        sx, sl = scores[i]
        m = jnp.maximum(jnp.max(sx, -1, keepdims=True), jnp.max(sl, -1, keepdims=True))
        ex = jnp.exp2(sx - m)
        el = jnp.exp2(sl - m)
        den = jnp.sum(ex, -1, keepdims=True) + jnp.sum(el, -1, keepdims=True)
        ratio = lam * den[:tq] / den[tq:]
        o = (_bdot(ex[:tq] - ex[tq:] * ratio, vx_ref[...])
             + _bdot(el[:tq] - el[tq:] * ratio, v_ref[...])) * (1.0 / den[:tq])
        o_ref[i * tq:(i + 1) * tq, :] = _subln(o, sub_ref[...], lam_init)


def _diff_latent(lams, subln, q, k, v, k_ctx, v_ctx, bn, t, lam_init):
    nb = t // DIFF_Q_TILE
    nctx = k_ctx.shape[1]
    small = pl.BlockSpec((1, C_QK_DIM), lambda b, h, n: (0, 0))
    qspec = pl.BlockSpec((None, DIFF_Q_TILE, LANES), lambda b, h, n: (b, n, h))
    kvspec = pl.BlockSpec((None, t, LANES), lambda b, h, n: (b, 0, h))
    cxspec = pl.BlockSpec((None, nctx, LANES), lambda b, h, n: (b, 0, h))
    return pl.pallas_call(
        functools.partial(_diff_lat_kernel, lam_init=lam_init),
        grid=(bn, C_HEADS, nb),
        in_specs=[small, small, small, small,
                  pl.BlockSpec((1, C_V_DIM), lambda b, h, n: (0, 0)),
                  qspec, kvspec, kvspec, cxspec, cxspec],
        out_specs=qspec,
        out_shape=jax.ShapeDtypeStruct((bn, t, C_WIDTH), F32),
        compiler_params=_params(3),
        name="diff_latent",
    )(*lams, subln, q.reshape(bn, t, C_WIDTH), k.reshape(bn, t, C_WIDTH), v.reshape(bn, t, C_WIDTH),
      k_ctx, v_ctx)


def _rope_tables(t):
    nf = HEAD_DIM // 4
    inv = 1.0 / (ROPE_BASE ** (jnp.arange(nf, dtype=F32) / nf))
    pos = jnp.arange(t)
    ang_r = (pos // GRID_W).astype(F32)[:, None] * inv[None]
    ang_c = (pos % GRID_W).astype(F32)[:, None] * inv[None]
    ang = jnp.concatenate([ang_r, ang_r, ang_c, ang_c], -1)
    cos, sin = jnp.cos(ang), jnp.sin(ang)
    sign = jnp.where((jnp.arange(HEAD_DIM) // nf) % 2 == 0, -1.0, 1.0).astype(F32)
    return jnp.tile(cos, (1, 2)), jnp.tile(sin * sign[None], (1, 2))


EVEN_SEGS = ((0, 512), (512, 640), (640, 768), (768, 1280), (1280, 2944), (2944, 3456))
ODD_SEGS = ((0, 1024), (1024, 2048), (2048, 3072), (3072, 4096))


def _pair_states(st):
    bn = st.shape[0]
    s = st.reshape(bn, 2, 4, 2, HEAD_DIM, HEAD_DIM)
    z = jnp.zeros_like(s[:, :, :, 0])
    top = jnp.concatenate([s[:, :, :, 0], z], -1)
    bot = jnp.concatenate([z, s[:, :, :, 1]], -1)
    return jnp.concatenate([top, bot], -2)


def kernel(x_prompt, x_sample, cache_a_k, cache_a_v, state_rwkv, cache_c_k, cache_c_v, c, c_ctx,
           ada_w, ada_b, norm_pre, norm_post, w_out, even_w_in, a_sink, b_mu, b_w0, b_w2, b_a0,
           b_a2, b_kk, b_ka, b_rk, b_ln_w, b_ln_b, odd_w_in, c_lq1, c_lk1, c_lq2, c_lk2, c_subln):
    bp, tp, _ = x_prompt.shape
    bs, ts, _ = x_sample.shape
    past = cache_a_k.shape[2]

    cvecs = jnp.concatenate([c_ctx[None], c, jnp.zeros((8 - 1 - bs, D_MODEL), F32)], 0)
    mods = _modulation(cvecs, 1 + bs, ada_w, ada_b)
    rope_tabs = _rope_tables(ts)
    ones = (jnp.arange(B_WIDTH)[:, None] // HEAD_DIM == jnp.arange(B_WIDTH)[None] // HEAD_DIM).astype(BF16)

    xp = x_prompt.reshape(bp * tp, D_MODEL)
    xs = x_sample.reshape(bs * ts, D_MODEL)

    def mod_rows(layer, rows):
        m = mods[layer, rows[0]:rows[1]][:, None, :]
        return m[..., :D_MODEL], m[..., D_MODEL:2 * D_MODEL], m[..., 2 * D_MODEL:]

    e = 0
    w_in0 = even_w_in[e].astype(BF16)
    w_out0 = w_out[0].astype(BF16)
    gain_pre = norm_pre[0][None]
    gain_post = norm_post[0][None]
    zpad = jnp.zeros((DECAY_LORA, B_WIDTH), F32)
    lora = jnp.concatenate([
        jnp.concatenate([b_w2[e, 0], b_w2[e, 1], zpad, zpad], 1),
        jnp.concatenate([zpad, zpad, b_a2[e, 0], b_a2[e, 1]], 1)], 0).astype(BF16)
    feat_args = (b_mu[e][None], lora, b_w0[e][:, None, :], b_a0[e][:, None, :], b_kk[e][None],
                 b_ka[e][None], b_rk[e][None], ones)

    def even_layer(x2d, bn, t, rows, ctx):
        shift, scale, gate = mod_rows(0, rows)
        (qa, ka, va, ga, gb, r, kkn, v, lw, b, ke, bonus, *kv_t) = _even_in(
            x2d, shift, scale, gain_pre, w_in0, t, None if ctx is None else rope_tabs, ctx is None,
            *feat_args)
        if ctx is None:
            ya = _gqa_context(a_sink[e], qa, ka, va, bn, t)
            s0 = None
        else:
            k_ctx, v_ctx, st0 = ctx
            ya = _gqa_window(a_sink[e], qa, ka, va, k_ctx, v_ctx, bn, t)
            s0 = _pair_states(st0)
        yf, yb, *s_fin = _rwkv_scan(r, kkn, v, lw, b, ke, s0, bn, t)
        y = _even_out(ya.reshape(bn * t, A_WIDTH), ga, yf.reshape(bn * t, B_WIDTH),
                      yb.reshape(bn * t, B_WIDTH), bonus, gb,
                      b_ln_w[e][None], b_ln_b[e][None], ones, x2d, gate, gain_post, w_out0, t)
        return y, kv_t, s_fin

    yp, (ka_t, va_t), st_p = even_layer(xp, bp, tp, (0, 1), None)
    feat_major = lambda a: jnp.transpose(a, (0, 2, 3, 1)).reshape(bs, A_KV_WIDTH, past)
    ctx_a = (feat_major(cache_a_k[:, e]), feat_major(cache_a_v[:, e]), state_rwkv[:, e])
    ys, _, _ = even_layer(xs, bs, ts, (1, 1 + bs), ctx_a)

    def cache_a(x_t):
        return jnp.transpose(x_t.reshape(bp, A_KV_HEADS, HEAD_DIM, tp), (0, 3, 1, 2))[:, None]

    o = 0
    lam_init = 0.8 - 0.6 * math.exp(-0.3 * 1)
    w_in1 = odd_w_in[o].astype(BF16)
    w_out1 = w_out[1].astype(BF16)
    gain_pre1 = norm_pre[1][None]
    gain_post1 = norm_post[1][None]
    lams = (c_lq1[o][None], c_lk1[o][None], c_lq2[o][None], c_lk2[o][None])
    subln = c_subln[o][None]

    def odd_layer(x2d, bn, t, rows, ctx):
        shift, scale, gate = mod_rows(1, rows)
        qc, kc, vc, gc = _proj_in(x2d, shift, scale, gain_pre1, w_in1, ODD_SEGS, (0, 1), t,
                                  None if ctx is None else rope_tabs,
                                  head_segs=(1, 2) if ctx is None else (),
                                  bf16_segs=() if ctx is None else (1, 2))
        if ctx is None:
            y = _diff_context_out(lams, subln, qc, kc, vc, gc, x2d, gate, gain_post1, w_out1, bn, t, lam_init)
        else:
            k_ctx, v_ctx = ctx
            oc = _diff_latent(lams, subln, qc, kc, vc, k_ctx, v_ctx, bn, t, lam_init)
            y = _odd_out(oc.reshape(bn * t, C_WIDTH), gc, x2d, gate, gain_post1, w_out1, t)
        return y, kc, vc

    yp2, kc_p, vc_p = odd_layer(yp, bp, tp, (0, 1), None)
    ctx_c = (cache_c_k[:, o].reshape(bs, past, C_WIDTH), cache_c_v[:, o].reshape(bs, past, C_WIDTH))
    ys2, _, _ = odd_layer(ys, bs, ts, (1, 1 + bs), ctx_c)

    return (yp2.reshape(bp, tp, D_MODEL),
            ys2.reshape(bs, ts, D_MODEL),
            cache_a(ka_t),
            cache_a(va_t),
            st_p[0][:, None],
            kc_p.reshape(bp, 1, tp, C_HEADS, 2 * C_QK_DIM),
            vc_p.reshape(bp, 1, tp, C_HEADS, C_V_DIM))
```

```python
import functools
import math

import jax
import jax.numpy as jnp
from jax import lax
from jax.experimental import pallas as pl
from jax.experimental.pallas import tpu as pltpu

F32 = jnp.float32
BF16 = jnp.bfloat16

D_MODEL = 1024
DEPTH = 2
GRID_W = 64
HEAD_DIM = 64
ROPE_BASE = 10000.0
A_HEADS = 8
A_KV_HEADS = 2
A_WIDTH = 512
A_KV_WIDTH = 128
WINDOW = 128
BLOCK = 128
B_HEADS = 8
B_WIDTH = 512
DECAY_LORA = 64
AAA_LORA = 64
B_SHIFT_WIDTH = 3 * B_WIDTH + DECAY_LORA + AAA_LORA
C_HEADS = 8
C_QK_DIM = 64
C_V_DIM = 128
C_WIDTH = 1024
EVEN_IN = 3456
ODD_IN = 4096
NORM_EPS = 1e-6
RWKV_LN_EPS = 64e-5
SUBLN_EPS = 1e-5
NEG_INF = -1e30

LANES = 128
HALO = 8
ROW_TILE = 256
OUT_TILE = 512
GQA_CTX_SEQS = 4
DIFF_CTX_SEQS = 2
DIFF_Q_TILE = 1024
DIFF_Q_SUB = 128
CHUNK = 64
SCAN_SEQS = 4
PAIR = 2 * CHUNK
VMEM_LIMIT = 48 * 1024 * 1024

LOG2E = math.log2(math.e)
QUERY_SCALE = HEAD_DIM ** -0.5 * LOG2E

NN = ((1,), (0,))
NT = ((1,), (1,))


def _bdot(a, b, dims=NN):
    return lax.dot_general(a.astype(BF16), b.astype(BF16), (dims, ((), ())),
                           preferred_element_type=F32)


def _split_dot(sel_bf16, x):
    x1 = x.astype(BF16)
    x2 = (x - x1.astype(F32)).astype(BF16)
    dot = lambda p: jnp.dot(sel_bf16, p, preferred_element_type=F32)
    return dot(x1) + dot(x2)


def _head_sum(x, ones_bf16):
    hi = x.astype(BF16)
    lo = (x - hi.astype(F32)).astype(BF16)
    return (jnp.dot(hi, ones_bf16, preferred_element_type=F32)
            + jnp.dot(lo, ones_bf16, preferred_element_type=F32))


def _sigmoid(x):
    return 1.0 / (1.0 + jnp.exp(-x))


def _silu(x):
    return x * _sigmoid(x)


def _params(n_axes):
    return pltpu.CompilerParams(dimension_semantics=("arbitrary",) * n_axes,
                                vmem_limit_bytes=VMEM_LIMIT)


def _lane_lo(shape):
    return (lax.broadcasted_iota(jnp.int32, shape, len(shape) - 1) % LANES) < HEAD_DIM


def _stack_pair(x):
    lo = _lane_lo(x.shape)
    z = jnp.zeros_like(x)
    return jnp.concatenate([jnp.where(lo, x, z), jnp.where(lo, z, x)], axis=0)


def _dup_half(x, g):
    lo = _lane_lo(x.shape)
    sw = pltpu.roll(x, HEAD_DIM, 1)
    return jnp.where(lo, x, sw) if g == 0 else jnp.where(lo, sw, x)


def _rope(x, cos, sin_signed):
    out = []
    even_q = (lax.broadcasted_iota(jnp.int32, cos.shape, 1) // (HEAD_DIM // 4)) % 2 == 0
    for g in range(x.shape[1] // LANES):
        xg = x[:, g * LANES:(g + 1) * LANES]
        up = pltpu.roll(xg, LANES - HEAD_DIM // 4, 1)
        dn = pltpu.roll(xg, HEAD_DIM // 4, 1)
        out.append(xg * cos + jnp.where(even_q, up, dn) * sin_signed)
    return out


def _mod_kernel(c_ref, w_ref, b_ref, o_ref, *, n_used):
    s_t = _silu(c_ref[...]).T
    w = w_ref[...]
    rows = [jnp.sum(w * s_t[:, r:r + 1], axis=0, keepdims=True) for r in range(n_used)]
    rows.append(jnp.zeros((c_ref.shape[0] - n_used, w.shape[1]), F32))
    o_ref[...] = jnp.concatenate(rows, axis=0) + b_ref[...]


def _modulation(cvecs, n_used, ada_w, ada_b):
    nrow = cvecs.shape[0]
    return pl.pallas_call(
        functools.partial(_mod_kernel, n_used=n_used),
        grid=(DEPTH, 3),
        in_specs=[pl.BlockSpec((nrow, D_MODEL), lambda l, j: (0, 0)),
                  pl.BlockSpec((None, D_MODEL, D_MODEL), lambda l, j: (l, 0, j)),
                  pl.BlockSpec((None, 1, D_MODEL), lambda l, j: (l, 0, j))],
        out_specs=pl.BlockSpec((None, nrow, D_MODEL), lambda l, j: (l, 0, j)),
        out_shape=jax.ShapeDtypeStruct((DEPTH, nrow, 3 * D_MODEL), F32),
        compiler_params=_params(2),
        name="modulation",
    )(cvecs, ada_w, ada_b.reshape(DEPTH, 1, 3 * D_MODEL))


def _mod_index(mod, tiles_per_seq):
    if mod.shape[0] == 1:
        return lambda i: (0, 0, 0)
    return lambda i: (i // tiles_per_seq, 0, 0)


def _proj_in_kernel(*refs, segs, rope_segs, use_rope, head_segs, bf16_segs):
    if use_rope:
        x_ref, sh_ref, sc_ref, g_ref, w_ref, cos_ref, sin_ref = refs[:7]
        outs = refs[7:]
    else:
        x_ref, sh_ref, sc_ref, g_ref, w_ref = refs[:5]
        outs = refs[5:]
    x = x_ref[...]
    y = x * lax.rsqrt(jnp.mean(x * x, -1, keepdims=True) + NORM_EPS)
    h = (y * g_ref[...]) * (1.0 + sc_ref[...]) + sh_ref[...]
    hb = h.astype(BF16)
    for idx, (lo, hi) in enumerate(segs):
        o = jnp.dot(hb, w_ref[:, lo:hi], preferred_element_type=F32)
        if idx == 0:
            post = lambda a: (a * QUERY_SCALE).astype(BF16)
        elif idx in bf16_segs:
            post = lambda a: a.astype(BF16)
        else:
            post = lambda a: a
        if use_rope and idx in rope_segs:
            parts = _rope(o, cos_ref[...], sin_ref[...])
            for g, part in enumerate(parts):
                outs[idx][:, g * LANES:(g + 1) * LANES] = post(part)
        elif idx in head_segs:
            outs[idx][...] = o.reshape(o.shape[0], (hi - lo) // LANES, LANES)
        else:
            outs[idx][...] = post(o)


def _proj_in(x2d, shift, scale, gain, w_bf16, segs, rope_segs, t, rope_tabs, head_segs=(), bf16_segs=()):
    n = x2d.shape[0]
    tile = OUT_TILE
    assert shift.shape[0] == 1 or t % tile == 0
    nt = n // tile
    tiles_per_seq = max(t // tile, 1)
    use_rope = rope_tabs is not None
    row = lambda i: (i, 0)
    per_b = _mod_index(shift, tiles_per_seq)
    in_specs = [pl.BlockSpec((tile, D_MODEL), row),
                pl.BlockSpec((None, 1, D_MODEL), per_b),
                pl.BlockSpec((None, 1, D_MODEL), per_b),
                pl.BlockSpec((1, D_MODEL), lambda i: (0, 0)),
                pl.BlockSpec(w_bf16.shape, lambda i: (0, 0))]
    args = [x2d, shift, scale, gain, w_bf16]
    if use_rope:
        pos = lambda i: (i % tiles_per_seq, 0)
        in_specs += [pl.BlockSpec((tile, LANES), pos), pl.BlockSpec((tile, LANES), pos)]
        args += list(rope_tabs)
    widths = [hi - lo for lo, hi in segs]
    out_specs, out_shape = [], []
    for idx, w in enumerate(widths):
        if idx in head_segs:
            out_specs.append(pl.BlockSpec((tile, w // LANES, LANES), lambda i: (i, 0, 0)))
            out_shape.append(jax.ShapeDtypeStruct((n, w // LANES, LANES), F32))
        else:
            out_specs.append(pl.BlockSpec((tile, w), row))
            out_shape.append(jax.ShapeDtypeStruct((n, w), BF16 if idx == 0 or idx in bf16_segs else F32))
    return pl.pallas_call(
        functools.partial(_proj_in_kernel, segs=segs, rope_segs=rope_segs, use_rope=use_rope,
                          head_segs=tuple(head_segs), bf16_segs=tuple(bf16_segs)),
        grid=(nt,),
        in_specs=in_specs,
        out_specs=out_specs,
        out_shape=out_shape,
        compiler_params=_params(1),
        name="proj_in",
    )(*args)


def _sink_exp_pair(s2, sink_a, sink_b, n):
    rowi = lax.broadcasted_iota(jnp.int32, (2 * n, 1), 0)
    sink2 = jnp.where(rowi < n, sink_a, sink_b) * LOG2E
    m = jnp.maximum(jnp.max(s2, -1, keepdims=True), sink2)
    e = jnp.exp2(s2 - m)
    return e, 1.0 / (jnp.sum(e, -1, keepdims=True) + jnp.exp2(sink2 - m))


def _gqa_ctx_kernel(sink_ref, q_ref, k_ref, v_ref, o_ref):
    nseq, t = q_ref.shape[:2]
    pairs = [(s, p) for s in range(nseq) for p in range(A_HEADS // 2)]
    kd = [[_dup_half(k_ref[s], g).astype(BF16) for g in range(A_KV_HEADS)] for s in range(nseq)]
    vd = [[_dup_half(v_ref[s], g).astype(BF16) for g in range(A_KV_HEADS)] for s in range(nseq)]
    s2 = [_bdot(_stack_pair(q_ref[s, :, p * LANES:(p + 1) * LANES]), kd[s][p // 2], NT)
          for s, p in pairs]
    for i, (s, p) in enumerate(pairs):
        e, inv = _sink_exp_pair(s2[i], sink_ref[2 * p], sink_ref[2 * p + 1], t)
        o = _bdot(e, vd[s][p // 2]) * inv
        o_ref[s, :, p * LANES:(p + 1) * LANES] = jnp.where(_lane_lo((t, LANES)), o[:t], o[t:])


def _gqa_context(sink, q, k, v, bn, t):
    seq = lambda b: (b, 0, 0)
    ns = GQA_CTX_SEQS
    return pl.pallas_call(
        _gqa_ctx_kernel,
        grid=(bn // ns,),
        in_specs=[pl.BlockSpec(memory_space=pltpu.SMEM),
                  pl.BlockSpec((ns, t, A_WIDTH), seq),
                  pl.BlockSpec((ns, t, A_KV_WIDTH), seq),
                  pl.BlockSpec((ns, t, A_KV_WIDTH), seq)],
        out_specs=pl.BlockSpec((ns, t, A_WIDTH), seq),
        out_shape=jax.ShapeDtypeStruct((bn, t, A_WIDTH), F32),
        compiler_params=_params(1),
        name="gqa_context",
    )(sink, q.reshape(bn, t, A_WIDTH), k.reshape(bn, t, A_KV_WIDTH), v.reshape(bn, t, A_KV_WIDTH))


def _gqa_win_kernel(sink_ref, q_ref, kp_ref, kc_ref, kn_ref, vp_ref, vc_ref, vn_ref,
                    kx_ref, vx_ref, o_ref, *, t):
    nblk = pl.program_id(1)
    nctx = kx_ref.shape[1]
    keys = jnp.concatenate([kx_ref[...].T, kp_ref[...], kc_ref[...], kn_ref[...]], axis=0)
    vals = jnp.concatenate([vx_ref[...].T, vp_ref[...], vc_ref[...], vn_ref[...]], axis=0)
    nkeys = nctx + 3 * BLOCK
    rowi = lax.broadcasted_iota(jnp.int32, (2 * BLOCK, nkeys), 0)
    coli = lax.broadcasted_iota(jnp.int32, (2 * BLOCK, nkeys), 1)
    qpos = nblk * BLOCK + rowi % BLOCK
    kpos = nblk * BLOCK + coli - nctx - BLOCK
    valid = (coli < nctx) | ((jnp.abs(qpos - kpos) <= WINDOW) & (kpos >= 0) & (kpos < t))
    kd = [_dup_half(keys, g).astype(BF16) for g in range(A_KV_HEADS)]
    vd = [_dup_half(vals, g).astype(BF16) for g in range(A_KV_HEADS)]
    s2 = [_bdot(_stack_pair(q_ref[:, p * LANES:(p + 1) * LANES]),
                kd[p // 2], NT) for p in range(A_HEADS // 2)]
    for p in range(A_HEADS // 2):
        e, inv = _sink_exp_pair(jnp.where(valid, s2[p], NEG_INF), sink_ref[2 * p], sink_ref[2 * p + 1], BLOCK)
        o = _bdot(e, vd[p // 2]) * inv
        o_ref[:, p * LANES:(p + 1) * LANES] = jnp.where(_lane_lo((BLOCK, LANES)), o[:BLOCK], o[BLOCK:])


def _gqa_window(sink, q, k, v, k_ctx_t, v_ctx_t, bn, t):
    nb = t // BLOCK
    nctx = k_ctx_t.shape[2]
    cur = lambda b, n: (b, n, 0)
    prev = lambda b, n: (b, jnp.maximum(n - 1, 0), 0)
    nxt = lambda b, n: (b, jnp.minimum(n + 1, nb - 1), 0)
    ctx = lambda b, n: (b, 0, 0)
    kv = lambda f: pl.BlockSpec((None, BLOCK, A_KV_WIDTH), f)
    k3 = k.reshape(bn, t, A_KV_WIDTH)
    v3 = v.reshape(bn, t, A_KV_WIDTH)
    return pl.pallas_call(
        functools.partial(_gqa_win_kernel, t=t),
        grid=(bn, nb),
        in_specs=[pl.BlockSpec(memory_space=pltpu.SMEM),
                  pl.BlockSpec((None, BLOCK, A_WIDTH), cur),
                  kv(prev), kv(cur), kv(nxt), kv(prev), kv(cur), kv(nxt),
                  pl.BlockSpec((None, A_KV_WIDTH, nctx), ctx),
                  pl.BlockSpec((None, A_KV_WIDTH, nctx), ctx)],
        out_specs=pl.BlockSpec((None, BLOCK, A_WIDTH), cur),
        out_shape=jax.ShapeDtypeStruct((bn, t, A_WIDTH), F32),
        compiler_params=_params(2),
        name="gqa_window",
    )(sink, q.reshape(bn, t, A_WIDTH), k3, k3, k3, v3, v3, v3, k_ctx_t, v_ctx_t)


def _even_in_kernel(*refs, use_rope, emit_t, tiles_per_seq):
    x_ref, xp_ref, xn_ref, sh_ref, sc_ref, g_ref, w_ref = refs[:7]
    refs = refs[7:]
    if use_rope:
        cos_ref, sin_ref = refs[:2]
        refs = refs[2:]
    mu_ref, lora_ref, w0_ref, a0_ref, kk_ref, ka_ref, rk_ref, ones_ref = refs[:8]
    (q_out, k_out, v_out, ga_out, gb_out, r_out, kkn_out, vv_out, lw_out, b_out, ke_out,
     bonus_out) = refs[8:20]
    i = pl.program_id(0)
    tm = x_ref.shape[0]
    pos = i % tiles_per_seq

    def norm_mod(x):
        y = x * lax.rsqrt(jnp.mean(x * x, -1, keepdims=True) + NORM_EPS)
        return (y * g_ref[...]) * (1.0 + sc_ref[...]) + sh_ref[...]

    h = norm_mod(x_ref[...])
    hb = h.astype(BF16)
    seg = dict(zip(("q", "k", "v", "ga", "pb", "gb"), EVEN_SEGS))
    proj = lambda name, lhs=hb: jnp.dot(lhs, w_ref[:, seg[name][0]:seg[name][1]], preferred_element_type=F32)
    h_ext = jnp.concatenate([norm_mod(xp_ref[...]), h, norm_mod(xn_ref[...])], 0).astype(BF16)
    p_ext = proj("pb", h_ext)

    q = proj("q")
    if use_rope:
        q = jnp.concatenate(_rope(q, cos_ref[...], sin_ref[...]), 1)
    q_out[...] = (q * QUERY_SCALE).astype(BF16)

    p = p_ext[HALO:HALO + tm]
    p_ext = jnp.concatenate([jnp.where(pos != 0, p_ext[:HALO], 0.0), p,
                             jnp.where(pos != tiles_per_seq - 1, p_ext[HALO + tm:], 0.0)], 0)
    prev = pltpu.roll(p_ext, 1, 0)[HALO:HALO + tm]
    nxt = pltpu.roll(p_ext, tm + 2 * HALO - 1, 0)[HALO:HALO + tm]
    xf = p + (0.5 * (prev + nxt) - p) * mu_ref[...]
    r = xf[:, 0:B_WIDTH]
    k = xf[:, B_WIDTH:2 * B_WIDTH]
    v = xf[:, 2 * B_WIDTH:3 * B_WIDTH]
    g = xf[:, 3 * B_WIDTH:]
    r_out[...] = r
    vv_out[...] = v

    kv_att = jnp.dot(hb, w_ref[:, seg["k"][0]:seg["v"][1]], preferred_element_type=F32)
    k_att, v_att = kv_att[:, :A_KV_WIDTH], kv_att[:, A_KV_WIDTH:]
    if use_rope:
        k_att = _rope(k_att, cos_ref[...], sin_ref[...])[0]
    k_out[...] = k_att
    v_out[...] = v_att
    if emit_t:
        refs[20][...] = k_att.T
        refs[21][...] = v_att.T

    g = jnp.where(_lane_lo(g.shape), jnp.tanh(g), g)
    lo = _bdot(g, lora_ref[...])
    ones = ones_ref[...]
    kkr = k * kk_ref[...]
    kkn = kkr / jnp.maximum(jnp.sqrt(_head_sum(kkr * kkr, ones)), 1e-12)
    kkn_out[...] = kkn

    ga_out[...] = proj("ga")

    ke_sum = jnp.zeros_like(k)
    for z in range(2):
        wz = w0_ref[z] + lo[:, z * B_WIDTH:(z + 1) * B_WIDTH]
        lw_out[z] = -math.exp(-0.5) * _sigmoid(wz)
        az = _sigmoid(a0_ref[z] + lo[:, (2 + z) * B_WIDTH:(3 + z) * B_WIDTH])
        ke = k * (1.0 + (az - 1.0) * ka_ref[...])
        ke_out[z] = ke
        b_out[z] = kkn * az
        ke_sum = ke_sum + ke
        if z == 0:
            gb_out[...] = proj("gb")
    bonus_out[...] = _head_sum(r * (0.5 * ke_sum) * rk_ref[...], ones) * v


def _even_in(x2d, shift, scale, gain, w_bf16, t, rope_tabs, emit_t, mu, lora, w0, a0, k_k, k_a, r_k, ones):
    n = x2d.shape[0]
    tile = ROW_TILE if emit_t else OUT_TILE
    assert t % tile == 0 and (not emit_t or t == tile)
    nt = n // tile
    tiles_per_seq = t // tile
    h8 = tile // HALO
    use_rope = rope_tabs is not None
    row = lambda i: (i, 0)
    const2 = lambda i: (0, 0)
    const3 = lambda i: (0, 0, 0)
    per_b = _mod_index(shift, tiles_per_seq)
    vec = pl.BlockSpec((1, B_WIDTH), const2)
    in_specs = [pl.BlockSpec((tile, D_MODEL), row),
                pl.BlockSpec((HALO, D_MODEL), lambda i: (jnp.maximum(i * h8 - 1, 0), 0)),
                pl.BlockSpec((HALO, D_MODEL), lambda i: (jnp.minimum((i + 1) * h8, n // HALO - 1), 0)),
                pl.BlockSpec((None, 1, D_MODEL), per_b),
                pl.BlockSpec((None, 1, D_MODEL), per_b),
                pl.BlockSpec((1, D_MODEL), const2),
                pl.BlockSpec(w_bf16.shape, const2)]
    args = [x2d, x2d, x2d, shift, scale, gain, w_bf16]
    if use_rope:
        pos = lambda i: (i % tiles_per_seq, 0)
        in_specs += [pl.BlockSpec((tile, LANES), pos), pl.BlockSpec((tile, LANES), pos)]
        args += list(rope_tabs)
    in_specs += [pl.BlockSpec((1, B_SHIFT_WIDTH), const2),
                 pl.BlockSpec(lora.shape, const2),
                 pl.BlockSpec((2, 1, B_WIDTH), const3),
                 pl.BlockSpec((2, 1, B_WIDTH), const3),
                 vec, vec, vec,
                 pl.BlockSpec((B_WIDTH, B_WIDTH), const2)]
    args += [mu, lora, w0, a0, k_k, k_a, r_k, ones]
    wide = pl.BlockSpec((tile, B_WIDTH), row)
    narrow = pl.BlockSpec((tile, A_KV_WIDTH), row)
    dual = pl.BlockSpec((2, tile, B_WIDTH), lambda i: (0, i, 0))
    f_wide = jax.ShapeDtypeStruct((n, B_WIDTH), F32)
    f_narrow = jax.ShapeDtypeStruct((n, A_KV_WIDTH), F32)
    f_dual = jax.ShapeDtypeStruct((2, n, B_WIDTH), F32)
    out_specs = [wide, narrow, narrow, wide, wide, wide, wide, wide, dual, dual, dual, wide]
    out_shape = [jax.ShapeDtypeStruct((n, A_WIDTH), BF16), f_narrow, f_narrow, f_wide, f_wide,
                 f_wide, f_wide, f_wide, f_dual, f_dual, f_dual, f_wide]
    if emit_t:
        out_specs += [pl.BlockSpec((None, A_KV_WIDTH, tile), lambda i: (i, 0, 0))] * 2
        out_shape += [jax.ShapeDtypeStruct((nt, A_KV_WIDTH, tile), F32)] * 2
    return pl.pallas_call(
        functools.partial(_even_in_kernel, use_rope=use_rope, emit_t=emit_t, tiles_per_seq=tiles_per_seq),
        grid=(nt,),
        in_specs=in_specs,
        out_specs=out_specs,
        out_shape=out_shape,
        compiler_params=_params(1),
        name="even_in",
    )(*args)


def _rwkv_scan_kernel(*refs, n_chunks, from_zero):
    (rf_ref, kf_ref, vf_ref, rb_ref, kb_ref, vb_ref, lwf_ref, bf_ref, kef_ref,
     lwb_ref, bb_ref, keb_ref) = refs[:12]
    if from_zero:
        yf_ref, yb_ref, sout_ref, s_scr = refs[12:]
    else:
        s0_ref, yf_ref, yb_ref, s_scr = refs[12:]
    c = pl.program_id(1)

    @pl.when(c == 0)
    def _():
        if from_zero:
            s_scr[...] = jnp.zeros_like(s_scr)
        else:
            s_scr[...] = s0_ref[...]

    row = lax.broadcasted_iota(jnp.int32, (CHUNK, LANES), 0)
    li = lax.broadcasted_iota(jnp.int32, (CHUNK, LANES), 1) % CHUNK
    blk16 = (row // 16) == (li // 16)
    blk32 = (row // 32) == (li // 32)
    eye = jnp.where(row == li, 1.0, 0.0)
    srow = lax.broadcasted_iota(jnp.int32, (PAIR, PAIR), 0)
    scol = lax.broadcasted_iota(jnp.int32, (PAIR, PAIR), 1)
    same_head = (srow // CHUNK) == (scol // CHUNK)
    tr = lax.broadcasted_iota(jnp.int32, (CHUNK, CHUNK), 0)
    tc = lax.broadcasted_iota(jnp.int32, (CHUNK, CHUNK), 1)
    dirs = ((rf_ref, kf_ref, vf_ref, lwf_ref, bf_ref, kef_ref, yf_ref),
            (rb_ref, kb_ref, vb_ref, lwb_ref, bb_ref, keb_ref, yb_ref))

    def advance(s):
        chains = [(z, p) for z in range(2) for p in range(B_HEADS // 2)]
        sls = [slice(p * LANES, (p + 1) * LANES) for _, p in chains]
        before, before_eq, tot, kkd, rd, bg, kg, vv, sc = [], [], [], [], [], [], [], [], []
        for z in range(2):
            r_ref, kk_ref, v_ref, lw_ref, b_ref, ke_ref, _ = dirs[z]
            diff = (li - row) if z else (row - li)
            tri = jnp.where(((tc - tr) if z else (tr - tc)) >= 0, 1.0, 0.0).astype(BF16)
            lw_all = lw_ref[s]
            cum_all = _split_dot(tri, lw_all)
            for p in range(B_HEADS // 2):
                sl = slice(p * LANES, (p + 1) * LANES)
                lw = lw_all[:, sl]
                cum = cum_all[:, sl]
                tt = cum[0:1] if z else cum[CHUNK - 1:CHUNK]
                b = b_ref[s, :, sl]
                ke = ke_ref[s, :, sl]
                g_inv = jnp.exp(-cum)
                g_rest = jnp.exp(tt - cum)
                before.append(diff > 0)
                before_eq.append(diff >= 0)
                tot.append(tt)
                kkd.append(kk_ref[s, :, sl] * jnp.exp(cum - lw))
                rd.append(r_ref[s, :, sl] * jnp.exp(cum))
                bg.append(b * g_rest)
                kg.append(ke * g_rest)
                vv.append(v_ref[s, :, sl])
                sc.append(_bdot(jnp.concatenate([kkd[-1], rd[-1]], 0),
                                jnp.concatenate([_stack_pair(b * g_inv), _stack_pair(ke * g_inv)], 0), NT))
            yield
        lb = [jnp.where(m, x[:CHUNK, :LANES], 0.0) for m, x in zip(before, sc)]
        mb = [jnp.where(m, x[CHUNK:, :LANES], 0.0) for m, x in zip(before_eq, sc)]
        lv = [_bdot(jnp.concatenate([jnp.where(m, x[:CHUNK, LANES:], 0.0),
                                     jnp.where(me, x[CHUNK:, LANES:], 0.0)], 0), _stack_pair(v))
              for m, me, x, v in zip(before, before_eq, sc, vv)]
        yield

        pw = [jnp.where(blk16, x, 0.0) for x in lb]
        tinv = [eye - x for x in pw]
        for _ in range(3):
            pw = [_bdot(x, _stack_pair(x)) for x in pw]
            yield
            tinv = [t + _bdot(t, _stack_pair(x)) for t, x in zip(tinv, pw)]
            yield
        for off in (blk32 & ~blk16, ~blk32):
            tmp = [_bdot(t, _stack_pair(jnp.where(off, x, 0.0))) for t, x in zip(tinv, lb)]
            yield
            tinv = [t - _bdot(m, _stack_pair(t)) for t, m in zip(tinv, tmp)]
            yield

        w12 = [_bdot(t, jnp.concatenate([_stack_pair(k), _stack_pair(x[:CHUNK])], 1))
               for t, k, x in zip(tinv, kkd, lv)]
        yield
        s_old = [s_scr[s, z, p] for z, p in chains]
        ws = [_bdot(jnp.concatenate([w[:, :LANES], r], 0), st, NT) for w, r, st in zip(w12, rd, s_old)]
        yield
        u = [-(a[:CHUNK] + w[:, LANES:]) for a, w in zip(ws, w12)]
        mbu = [_bdot(m, _stack_pair(x)) for m, x in zip(mb, u)]
        upd = [_bdot(jnp.concatenate([x, v], 0).T, jnp.concatenate([b_, k_], 0))
               for x, v, b_, k_ in zip(u, vv, bg, kg)]
        yield
        for i, (z, p) in enumerate(chains):
            dirs[z][6][s, :, sls[i]] = ws[i][CHUNK:] + mbu[i] + lv[i][CHUNK:]
            s_scr[s, z, p] = s_old[i] * jnp.exp(tot[i]) + jnp.where(same_head, upd[i], 0.0)

    live = [advance(s) for s in range(rf_ref.shape[0])]
    while live:
        live = [gen for gen in live if next(gen, "done") != "done"]

    if from_zero:
        @pl.when(c == n_chunks - 1)
        def _():
            for s in range(s_scr.shape[0]):
                for z in range(2):
                    for p in range(B_HEADS // 2):
                        st = s_scr[s, z, p]
                        sout_ref[s, z, 2 * p] = st[:HEAD_DIM, :HEAD_DIM]
                        sout_ref[s, z, 2 * p + 1] = pltpu.roll(st[HEAD_DIM:, :], HEAD_DIM, 1)[:, :HEAD_DIM]


def _rwkv_scan(r, kkn, v, lw, b, ke, s0, bn, t):
    nc = t // CHUNK
    npair = B_HEADS // 2
    ns = min(SCAN_SEQS, bn)
    fwd = pl.BlockSpec((ns, CHUNK, B_WIDTH), lambda g, c: (g, c, 0))
    bwd = pl.BlockSpec((ns, CHUNK, B_WIDTH), lambda g, c: (g, nc - 1 - c, 0))
    dfwd = pl.BlockSpec((None, ns, CHUNK, B_WIDTH), lambda g, c: (0, g, c, 0))
    dbwd = pl.BlockSpec((None, ns, CHUNK, B_WIDTH), lambda g, c: (1, g, nc - 1 - c, 0))
    r3, k3, v3 = (a.reshape(bn, t, B_WIDTH) for a in (r, kkn, v))
    lw4, b4, ke4 = (a.reshape(2, bn, t, B_WIDTH) for a in (lw, b, ke))
    seq = jax.ShapeDtypeStruct((bn, t, B_WIDTH), F32)
    in_specs = [fwd, fwd, fwd, bwd, bwd, bwd, dfwd, dfwd, dfwd, dbwd, dbwd, dbwd]
    args = [r3, k3, v3, r3, k3, v3, lw4, b4, ke4, lw4, b4, ke4]
    out_specs = [fwd, bwd]
    out_shape = [seq, seq]
    if s0 is None:
        out_specs.append(pl.BlockSpec((ns, 2, B_HEADS, HEAD_DIM, HEAD_DIM), lambda g, c: (g, 0, 0, 0, 0)))
        out_shape.append(jax.ShapeDtypeStruct((bn, 2, B_HEADS, HEAD_DIM, HEAD_DIM), F32))
    else:
        in_specs.append(pl.BlockSpec((ns, 2, npair, PAIR, PAIR), lambda g, c: (g, 0, 0, 0, 0)))
        args.append(s0)
    return pl.pallas_call(
        functools.partial(_rwkv_scan_kernel, n_chunks=nc, from_zero=s0 is None),
        grid=(bn // ns, nc),
        in_specs=in_specs,
        out_specs=out_specs,
        out_shape=out_shape,
        scratch_shapes=[pltpu.VMEM((ns, 2, npair, PAIR, PAIR), F32)],
        compiler_params=_params(2),
        name="rwkv_scan",
    )(*args)


def _post(x, z, gate, gain):
    zn = z * lax.rsqrt(jnp.mean(z * z, -1, keepdims=True) + NORM_EPS)
    return x + gate * (zn * gain)


def _even_out_kernel(ya_ref, ga_ref, yf_ref, yb_ref, bonus_ref, gb_ref, lnw_ref, lnb_ref, ones_ref,
                     x_ref, gate_ref, gain_ref, w_ref, o_ref):
    y = yf_ref[...] + yb_ref[...]
    ones = ones_ref[...]
    mean = _head_sum(y, ones) * (1.0 / HEAD_DIM)
    d = y - mean
    var = _head_sum(d * d, ones) * (1.0 / HEAD_DIM)
    yb = (d * lax.rsqrt(var + RWKV_LN_EPS)) * lnw_ref[...] + lnb_ref[...] + bonus_ref[...]
    za = (ya_ref[...] * _silu(ga_ref[...])).astype(BF16)
    zb = (yb * _silu(gb_ref[...])).astype(BF16)
    z = (jnp.dot(za, w_ref[0:A_WIDTH, :], preferred_element_type=F32)
         + jnp.dot(zb, w_ref[A_WIDTH:, :], preferred_element_type=F32))
    o_ref[...] = _post(x_ref[...], z, gate_ref[...], gain_ref[...])


def _even_out(ya, ga, yf, yb, bonus, gb, ln_w, ln_b, ones, x2d, gate, gain, w_bf16, t):
    n = x2d.shape[0]
    row = lambda i: (i, 0)
    const2 = lambda i: (0, 0)
    half = pl.BlockSpec((OUT_TILE, B_WIDTH), row)
    full = pl.BlockSpec((OUT_TILE, D_MODEL), row)
    vec = pl.BlockSpec((1, B_WIDTH), const2)
    assert gate.shape[0] == 1 or t % OUT_TILE == 0
    return pl.pallas_call(
        _even_out_kernel,
        grid=(n // OUT_TILE,),
        in_specs=[half, half, half, half, half, half, vec, vec,
                  pl.BlockSpec((B_WIDTH, B_WIDTH), const2),
                  full,
                  pl.BlockSpec((None, 1, D_MODEL), _mod_index(gate, t // OUT_TILE)),
                  pl.BlockSpec((1, D_MODEL), const2),
                  pl.BlockSpec((D_MODEL, D_MODEL), const2)],
        out_specs=full,
        out_shape=jax.ShapeDtypeStruct((n, D_MODEL), F32),
        compiler_params=_params(1),
        name="even_out",
    )(ya, ga, yf, yb, bonus, gb, ln_w, ln_b, ones, x2d, gate, gain, w_bf16)


def _odd_out_kernel(o_ref_in, gc_ref, x_ref, gate_ref, gain_ref, w_ref, o_ref):
    zc = (o_ref_in[...] * _silu(gc_ref[...])).astype(BF16)
    z = jnp.dot(zc, w_ref[...], preferred_element_type=F32)
    o_ref[...] = _post(x_ref[...], z, gate_ref[...], gain_ref[...])


def _odd_out(o, gc, x2d, gate, gain, w_bf16, t):
    n = x2d.shape[0]
    row = lambda i: (i, 0)
    const2 = lambda i: (0, 0)
    full = pl.BlockSpec((OUT_TILE, D_MODEL), row)
    assert gate.shape[0] == 1 or t % OUT_TILE == 0
    return pl.pallas_call(
        _odd_out_kernel,
        grid=(n // OUT_TILE,),
        in_specs=[full, full, full,
                  pl.BlockSpec((None, 1, D_MODEL), _mod_index(gate, t // OUT_TILE)),
                  pl.BlockSpec((1, D_MODEL), const2),
                  pl.BlockSpec((D_MODEL, D_MODEL), const2)],
        out_specs=full,
        out_shape=jax.ShapeDtypeStruct((n, D_MODEL), F32),
        compiler_params=_params(1),
        name="odd_out",
    )(o, gc, x2d, gate, gain, w_bf16)


def _lambda(lq1_ref, lk1_ref, lq2_ref, lk2_ref, lam_init):
    s1 = jnp.sum(lq1_ref[...] * lk1_ref[...], -1, keepdims=True)
    s2 = jnp.sum(lq2_ref[...] * lk2_ref[...], -1, keepdims=True)
    return jnp.exp(s1) - jnp.exp(s2) + lam_init


def _subln(o, gain, lam_init):
    on = o * lax.rsqrt(jnp.mean(o * o, -1, keepdims=True) + SUBLN_EPS)
    return (on * gain) * (1.0 - lam_init)


def _diff_ctx_kernel(lq1_ref, lk1_ref, lq2_ref, lk2_ref, sub_ref, q_ref, k_ref, v_ref,
                     gc_ref, x_ref, gate_ref, gain_ref, w_ref, y_ref, o_scr, *, lam_init):
    nseq, t = q_ref.shape[:2]
    lam = _lambda(lq1_ref, lk1_ref, lq2_ref, lk2_ref, lam_init)
    units = [(s, slice(h * LANES, (h + 1) * LANES)) for s in range(nseq) for h in range(C_HEADS)]
    k = [k_ref[s].reshape(t, C_WIDTH) for s in range(nseq)]
    v = [v_ref[s].reshape(t, C_WIDTH) for s in range(nseq)]
    s2 = [_bdot(_stack_pair(q_ref[s, :, sl]), k[s][:, sl], NT)
          for s, sl in units]
    for i, (s, sl) in enumerate(units):
        e = jnp.exp2(s2[i] - jnp.max(s2[i], -1, keepdims=True))
        ov = _bdot(e, v[s][:, sl]) * (1.0 / jnp.sum(e, -1, keepdims=True))
        o_scr[s * t:(s + 1) * t, sl] = _subln(ov[:t] - lam * ov[t:], sub_ref[...], lam_init)
    gc = gc_ref[...].reshape(nseq * t, C_WIDTH)
    zc = (o_scr[...] * _silu(gc)).astype(BF16)
    z = jnp.dot(zc, w_ref[...], preferred_element_type=F32)
    y = _post(x_ref[...].reshape(nseq * t, D_MODEL), z, gate_ref[...], gain_ref[...])
    y_ref[...] = y.reshape(nseq, t, D_MODEL)


def _diff_context_out(lams, subln, q, k, v, gc, x2d, gate, gain, w_bf16, bn, t, lam_init):
    ns = DIFF_CTX_SEQS
    seq = pl.BlockSpec((ns, t, C_WIDTH), lambda b: (b, 0, 0))
    heads = pl.BlockSpec((ns, t, C_HEADS, LANES), lambda b: (b, 0, 0, 0))
    small = pl.BlockSpec((1, C_QK_DIM), lambda b: (0, 0))
    const2 = lambda b: (0, 0)
    to3 = lambda a: a.reshape((bn, t) + a.shape[1:])
    return pl.pallas_call(
        functools.partial(_diff_ctx_kernel, lam_init=lam_init),
        grid=(bn // ns,),
        in_specs=[small, small, small, small, pl.BlockSpec((1, C_V_DIM), const2),
                  seq, heads, heads, seq, seq,
                  pl.BlockSpec((None, 1, D_MODEL), lambda b: (0, 0, 0)),
                  pl.BlockSpec((1, D_MODEL), const2),
                  pl.BlockSpec((D_MODEL, D_MODEL), const2)],
        out_specs=seq,
        out_shape=jax.ShapeDtypeStruct((bn, t, D_MODEL), F32),
        scratch_shapes=[pltpu.VMEM((ns * t, C_WIDTH), F32)],
        compiler_params=_params(1),
        name="diff_context",
    )(*lams, subln, to3(q), to3(k), to3(v), to3(gc), to3(x2d), gate, gain, w_bf16).reshape(bn * t, D_MODEL)


def _diff_lat_kernel(lq1_ref, lk1_ref, lq2_ref, lk2_ref, sub_ref, q_ref, k_ref, v_ref, kx_ref, vx_ref,
                     o_ref, *, lam_init):
    lam = _lambda(lq1_ref, lk1_ref, lq2_ref, lk2_ref, lam_init)
    kx = kx_ref[...].astype(BF16)
    k = k_ref[...].astype(BF16)
    tq = DIFF_Q_SUB
    subs = range(q_ref.shape[0] // tq)
    scores = []
    for i in subs:
        qs = _stack_pair(q_ref[i * tq:(i + 1) * tq, :])
        scores.append((_bdot(qs, kx, NT), _bdot(qs, k, NT)))
    for i in subs:
        sx, sl = scores[i]
        m = jnp.maximum(jnp.max(sx, -1, keepdims=True), jnp.max(sl, -1, keepdims=True))
        ex = jnp.exp2(sx - m)
        el = jnp.exp2(sl - m)
        den = jnp.sum(ex, -1, keepdims=True) + jnp.sum(el, -1, keepdims=True)
        ratio = lam * den[:tq] / den[tq:]
        o = (_bdot(ex[:tq] - ex[tq:] * ratio, vx_ref[...])
             + _bdot(el[:tq] - el[tq:] * ratio, v_ref[...])) * (1.0 / den[:tq])
        o_ref[i * tq:(i + 1) * tq, :] = _subln(o, sub_ref[...], lam_init)


def _diff_latent(lams, subln, q, k, v, k_ctx, v_ctx, bn, t, lam_init):
    nb = t // DIFF_Q_TILE
    nctx = k_ctx.shape[1]
    small = pl.BlockSpec((1, C_QK_DIM), lambda b, h, n: (0, 0))
    qspec = pl.BlockSpec((None, DIFF_Q_TILE, LANES), lambda b, h, n: (b, n, h))
    kvspec = pl.BlockSpec((None, t, LANES), lambda b, h, n: (b, 0, h))
    cxspec = pl.BlockSpec((None, nctx, LANES), lambda b, h, n: (b, 0, h))
    return pl.pallas_call(
        functools.partial(_diff_lat_kernel, lam_init=lam_init),
        grid=(bn, C_HEADS, nb),
        in_specs=[small, small, small, small,
                  pl.BlockSpec((1, C_V_DIM), lambda b, h, n: (0, 0)),
                  qspec, kvspec, kvspec, cxspec, cxspec],
        out_specs=qspec,
        out_shape=jax.ShapeDtypeStruct((bn, t, C_WIDTH), F32),
        compiler_params=_params(3),
        name="diff_latent",
    )(*lams, subln, q.reshape(bn, t, C_WIDTH), k.reshape(bn, t, C_WIDTH), v.reshape(bn, t, C_WIDTH),
      k_ctx, v_ctx)


def _rope_tables(t):
    nf = HEAD_DIM // 4
    inv = 1.0 / (ROPE_BASE ** (jnp.arange(nf, dtype=F32) / nf))
    pos = jnp.arange(t)
    ang_r = (pos // GRID_W).astype(F32)[:, None] * inv[None]
    ang_c = (pos % GRID_W).astype(F32)[:, None] * inv[None]
    ang = jnp.concatenate([ang_r, ang_r, ang_c, ang_c], -1)
    cos, sin = jnp.cos(ang), jnp.sin(ang)
    sign = jnp.where((jnp.arange(HEAD_DIM) // nf) % 2 == 0, -1.0, 1.0).astype(F32)
    return jnp.tile(cos, (1, 2)), jnp.tile(sin * sign[None], (1, 2))


EVEN_SEGS = ((0, 512), (512, 640), (640, 768), (768, 1280), (1280, 2944), (2944, 3456))
ODD_SEGS = ((0, 1024), (1024, 2048), (2048, 3072), (3072, 4096))


def _pair_states(st):
    bn = st.shape[0]
    s = st.reshape(bn, 2, 4, 2, HEAD_DIM, HEAD_DIM)
    z = jnp.zeros_like(s[:, :, :, 0])
    top = jnp.concatenate([s[:, :, :, 0], z], -1)
    bot = jnp.concatenate([z, s[:, :, :, 1]], -1)
    return jnp.concatenate([top, bot], -2)


def kernel(x_prompt, x_sample, cache_a_k, cache_a_v, state_rwkv, cache_c_k, cache_c_v, c, c_ctx,
           ada_w, ada_b, norm_pre, norm_post, w_out, even_w_in, a_sink, b_mu, b_w0, b_w2, b_a0,
           b_a2, b_kk, b_ka, b_rk, b_ln_w, b_ln_b, odd_w_in, c_lq1, c_lk1, c_lq2, c_lk2, c_subln):
    bp, tp, _ = x_prompt.shape
    bs, ts, _ = x_sample.shape
    past = cache_a_k.shape[2]

    cvecs = jnp.concatenate([c_ctx[None], c, jnp.zeros((8 - 1 - bs, D_MODEL), F32)], 0)
    mods = _modulation(cvecs, 1 + bs, ada_w, ada_b)
    rope_tabs = _rope_tables(ts)
    ones = (jnp.arange(B_WIDTH)[:, None] // HEAD_DIM == jnp.arange(B_WIDTH)[None] // HEAD_DIM).astype(BF16)

    xp = x_prompt.reshape(bp * tp, D_MODEL)
    xs = x_sample.reshape(bs * ts, D_MODEL)

    def mod_rows(layer, rows):
        m = mods[layer, rows[0]:rows[1]][:, None, :]
        return m[..., :D_MODEL], m[..., D_MODEL:2 * D_MODEL], m[..., 2 * D_MODEL:]

    e = 0
    w_in0 = even_w_in[e].astype(BF16)
    w_out0 = w_out[0].astype(BF16)
    gain_pre = norm_pre[0][None]
    gain_post = norm_post[0][None]
    zpad = jnp.zeros((DECAY_LORA, B_WIDTH), F32)
    lora = jnp.concatenate([
        jnp.concatenate([b_w2[e, 0], b_w2[e, 1], zpad, zpad], 1),
        jnp.concatenate([zpad, zpad, b_a2[e, 0], b_a2[e, 1]], 1)], 0).astype(BF16)
    feat_args = (b_mu[e][None], lora, b_w0[e][:, None, :], b_a0[e][:, None, :], b_kk[e][None],
                 b_ka[e][None], b_rk[e][None], ones)

    def even_layer(x2d, bn, t, rows, ctx):
        shift, scale, gate = mod_rows(0, rows)
        (qa, ka, va, ga, gb, r, kkn, v, lw, b, ke, bonus, *kv_t) = _even_in(
            x2d, shift, scale, gain_pre, w_in0, t, None if ctx is None else rope_tabs, ctx is None,
            *feat_args)
        if ctx is None:
            ya = _gqa_context(a_sink[e], qa, ka, va, bn, t)
            s0 = None
        else:
            k_ctx, v_ctx, st0 = ctx
            ya = _gqa_window(a_sink[e], qa, ka, va, k_ctx, v_ctx, bn, t)
            s0 = _pair_states(st0)
        yf, yb, *s_fin = _rwkv_scan(r, kkn, v, lw, b, ke, s0, bn, t)
        y = _even_out(ya.reshape(bn * t, A_WIDTH), ga, yf.reshape(bn * t, B_WIDTH),
                      yb.reshape(bn * t, B_WIDTH), bonus, gb,
                      b_ln_w[e][None], b_ln_b[e][None], ones, x2d, gate, gain_post, w_out0, t)
        return y, kv_t, s_fin

    yp, (ka_t, va_t), st_p = even_layer(xp, bp, tp, (0, 1), None)
    feat_major = lambda a: jnp.transpose(a, (0, 2, 3, 1)).reshape(bs, A_KV_WIDTH, past)
    ctx_a = (feat_major(cache_a_k[:, e]), feat_major(cache_a_v[:, e]), state_rwkv[:, e])
    ys, _, _ = even_layer(xs, bs, ts, (1, 1 + bs), ctx_a)

    def cache_a(x_t):
        return jnp.transpose(x_t.reshape(bp, A_KV_HEADS, HEAD_DIM, tp), (0, 3, 1, 2))[:, None]

    o = 0
    lam_init = 0.8 - 0.6 * math.exp(-0.3 * 1)
    w_in1 = odd_w_in[o].astype(BF16)
    w_out1 = w_out[1].astype(BF16)
    gain_pre1 = norm_pre[1][None]
    gain_post1 = norm_post[1][None]
    lams = (c_lq1[o][None], c_lk1[o][None], c_lq2[o][None], c_lk2[o][None])
    subln = c_subln[o][None]

    def odd_layer(x2d, bn, t, rows, ctx):
        shift, scale, gate = mod_rows(1, rows)
        qc, kc, vc, gc = _proj_in(x2d, shift, scale, gain_pre1, w_in1, ODD_SEGS, (0, 1), t,
                                  None if ctx is None else rope_tabs,
                                  head_segs=(1, 2) if ctx is None else (),
                                  bf16_segs=() if ctx is None else (1, 2))
        if ctx is None:
            y = _diff_context_out(lams, subln, qc, kc, vc, gc, x2d, gate, gain_post1, w_out1, bn, t, lam_init)
        else:
            k_ctx, v_ctx = ctx
            oc = _diff_latent(lams, subln, qc, kc, vc, k_ctx, v_ctx, bn, t, lam_init)
            y = _odd_out(oc.reshape(bn * t, C_WIDTH), gc, x2d, gate, gain_post1, w_out1, t)
        return y, kc, vc

    yp2, kc_p, vc_p = odd_layer(yp, bp, tp, (0, 1), None)
    ctx_c = (cache_c_k[:, o].reshape(bs, past, C_WIDTH), cache_c_v[:, o].reshape(bs, past, C_WIDTH))
    ys2, _, _ = odd_layer(ys, bs, ts, (1, 1 + bs), ctx_c)

    return (yp2.reshape(bp, tp, D_MODEL),
            ys2.reshape(bs, ts, D_MODEL),
            cache_a(ka_t),
            cache_a(va_t),
            st_p[0][:, None],
            kc_p.reshape(bp, 1, tp, C_HEADS, 2 * C_QK_DIM),
            vc_p.reshape(bp, 1, tp, C_HEADS, C_V_DIM))
```

```python
import functools
import math

import jax
import jax.numpy as jnp
from jax import lax
from jax.experimental import pallas as pl
from jax.experimental.pallas import tpu as pltpu

F32 = jnp.float32
BF16 = jnp.bfloat16

D_MODEL = 1024
DEPTH = 2
GRID_W = 64
HEAD_DIM = 64
ROPE_BASE = 10000.0
A_HEADS = 8
A_KV_HEADS = 2
A_WIDTH = 512
A_KV_WIDTH = 128
WINDOW = 128
BLOCK = 128
B_HEADS = 8
B_WIDTH = 512
DECAY_LORA = 64
AAA_LORA = 64
B_SHIFT_WIDTH = 3 * B_WIDTH + DECAY_LORA + AAA_LORA
C_HEADS = 8
C_QK_DIM = 64
C_V_DIM = 128
C_WIDTH = 1024
EVEN_IN = 3456
ODD_IN = 4096
NORM_EPS = 1e-6
RWKV_LN_EPS = 64e-5
SUBLN_EPS = 1e-5
NEG_INF = -1e30

LANES = 128
HALO = 8
ROW_TILE = 256
OUT_TILE = 512
GQA_CTX_SEQS = 4
DIFF_CTX_SEQS = 2
DIFF_Q_TILE = 1024
DIFF_Q_SUB = 128
CHUNK = 64
SCAN_SEQS = 4
PAIR = 2 * CHUNK
VMEM_LIMIT = 48 * 1024 * 1024

LOG2E = math.log2(math.e)
QUERY_SCALE = HEAD_DIM ** -0.5 * LOG2E

NN = ((1,), (0,))
NT = ((1,), (1,))


def _bdot(a, b, dims=NN):
    return lax.dot_general(a.astype(BF16), b.astype(BF16), (dims, ((), ())),
                           preferred_element_type=F32)


def _split_dot(sel_bf16, x):
    x1 = x.astype(BF16)
    x2 = (x - x1.astype(F32)).astype(BF16)
    dot = lambda p: jnp.dot(sel_bf16, p, preferred_element_type=F32)
    return dot(x1) + dot(x2)


def _head_sum(x, ones_bf16):
    hi = x.astype(BF16)
    lo = (x - hi.astype(F32)).astype(BF16)
    return (jnp.dot(hi, ones_bf16, preferred_element_type=F32)
            + jnp.dot(lo, ones_bf16, preferred_element_type=F32))


def _sigmoid(x):
    return 1.0 / (1.0 + jnp.exp(-x))


def _silu(x):
    return x * _sigmoid(x)


def _params(n_axes):
    return pltpu.CompilerParams(dimension_semantics=("arbitrary",) * n_axes,
                                vmem_limit_bytes=VMEM_LIMIT)


def _lane_lo(shape):
    return (lax.broadcasted_iota(jnp.int32, shape, len(shape) - 1) % LANES) < HEAD_DIM


def _stack_pair(x):
    lo = _lane_lo(x.shape)
    z = jnp.zeros_like(x)
    return jnp.concatenate([jnp.where(lo, x, z), jnp.where(lo, z, x)], axis=0)


def _dup_half(x, g):
    lo = _lane_lo(x.shape)
    sw = pltpu.roll(x, HEAD_DIM, 1)
    return jnp.where(lo, x, sw) if g == 0 else jnp.where(lo, sw, x)


def _rope(x, cos, sin_signed):
    out = []
    even_q = (lax.broadcasted_iota(jnp.int32, cos.shape, 1) // (HEAD_DIM // 4)) % 2 == 0
    for g in range(x.shape[1] // LANES):
        xg = x[:, g * LANES:(g + 1) * LANES]
        up = pltpu.roll(xg, LANES - HEAD_DIM // 4, 1)
        dn = pltpu.roll(xg, HEAD_DIM // 4, 1)
        out.append(xg * cos + jnp.where(even_q, up, dn) * sin_signed)
    return out


def _mod_kernel(c_ref, w_ref, b_ref, o_ref, *, n_used):
    s_t = _silu(c_ref[...]).T
    w = w_ref[...]
    rows = [jnp.sum(w * s_t[:, r:r + 1], axis=0, keepdims=True) for r in range(n_used)]
    rows.append(jnp.zeros((c_ref.shape[0] - n_used, w.shape[1]), F32))
    o_ref[...] = jnp.concatenate(rows, axis=0) + b_ref[...]


def _modulation(cvecs, n_used, ada_w, ada_b):
    nrow = cvecs.shape[0]
    return pl.pallas_call(
        functools.partial(_mod_kernel, n_used=n_used),
        grid=(DEPTH, 3),
        in_specs=[pl.BlockSpec((nrow, D_MODEL), lambda l, j: (0, 0)),
                  pl.BlockSpec((None, D_MODEL, D_MODEL), lambda l, j: (l, 0, j)),
                  pl.BlockSpec((None, 1, D_MODEL), lambda l, j: (l, 0, j))],
        out_specs=pl.BlockSpec((None, nrow, D_MODEL), lambda l, j: (l, 0, j)),
        out_shape=jax.ShapeDtypeStruct((DEPTH, nrow, 3 * D_MODEL), F32),
        compiler_params=_params(2),
        name="modulation",
    )(cvecs, ada_w, ada_b.reshape(DEPTH, 1, 3 * D_MODEL))


def _mod_index(mod, tiles_per_seq):
    if mod.shape[0] == 1:
        return lambda i: (0, 0, 0)
    return lambda i: (i // tiles_per_seq, 0, 0)


def _proj_in_kernel(*refs, segs, rope_segs, use_rope, head_segs, bf16_segs):
    if use_rope:
        x_ref, sh_ref, sc_ref, g_ref, w_ref, cos_ref, sin_ref = refs[:7]
        outs = refs[7:]
    else:
        x_ref, sh_ref, sc_ref, g_ref, w_ref = refs[:5]
        outs = refs[5:]
    x = x_ref[...]
    y = x * lax.rsqrt(jnp.mean(x * x, -1, keepdims=True) + NORM_EPS)
    h = (y * g_ref[...]) * (1.0 + sc_ref[...]) + sh_ref[...]
    hb = h.astype(BF16)
    for idx, (lo, hi) in enumerate(segs):
        o = jnp.dot(hb, w_ref[:, lo:hi], preferred_element_type=F32)
        if idx == 0:
            post = lambda a: (a * QUERY_SCALE).astype(BF16)
        elif idx in bf16_segs:
            post = lambda a: a.astype(BF16)
        else:
            post = lambda a: a
        if use_rope and idx in rope_segs:
            parts = _rope(o, cos_ref[...], sin_ref[...])
            for g, part in enumerate(parts):
                outs[idx][:, g * LANES:(g + 1) * LANES] = post(part)
        elif idx in head_segs:
            outs[idx][...] = o.reshape(o.shape[0], (hi - lo) // LANES, LANES)
        else:
            outs[idx][...] = post(o)


def _proj_in(x2d, shift, scale, gain, w_bf16, segs, rope_segs, t, rope_tabs, head_segs=(), bf16_segs=()):
    n = x2d.shape[0]
    tile = OUT_TILE
    assert shift.shape[0] == 1 or t % tile == 0
    nt = n // tile
    tiles_per_seq = max(t // tile, 1)
    use_rope = rope_tabs is not None
    row = lambda i: (i, 0)
    per_b = _mod_index(shift, tiles_per_seq)
    in_specs = [pl.BlockSpec((tile, D_MODEL), row),
                pl.BlockSpec((None, 1, D_MODEL), per_b),
                pl.BlockSpec((None, 1, D_MODEL), per_b),
                pl.BlockSpec((1, D_MODEL), lambda i: (0, 0)),
                pl.BlockSpec(w_bf16.shape, lambda i: (0, 0))]
    args = [x2d, shift, scale, gain, w_bf16]
    if use_rope:
        pos = lambda i: (i % tiles_per_seq, 0)
        in_specs += [pl.BlockSpec((tile, LANES), pos), pl.BlockSpec((tile, LANES), pos)]
        args += list(rope_tabs)
    widths = [hi - lo for lo, hi in segs]
    out_specs, out_shape = [], []
    for idx, w in enumerate(widths):
        if idx in head_segs:
            out_specs.append(pl.BlockSpec((tile, w // LANES, LANES), lambda i: (i, 0, 0)))
            out_shape.append(jax.ShapeDtypeStruct((n, w // LANES, LANES), F32))
        else:
            out_specs.append(pl.BlockSpec((tile, w), row))
            out_shape.append(jax.ShapeDtypeStruct((n, w), BF16 if idx == 0 or idx in bf16_segs else F32))
    return pl.pallas_call(
        functools.partial(_proj_in_kernel, segs=segs, rope_segs=rope_segs, use_rope=use_rope,
                          head_segs=tuple(head_segs), bf16_segs=tuple(bf16_segs)),
        grid=(nt,),
        in_specs=in_specs,
        out_specs=out_specs,
        out_shape=out_shape,
        compiler_params=_params(1),
        name="proj_in",
    )(*args)


def _sink_exp_pair(s2, sink_a, sink_b, n):
    rowi = lax.broadcasted_iota(jnp.int32, (2 * n, 1), 0)
    sink2 = jnp.where(rowi < n, sink_a, sink_b) * LOG2E
    m = jnp.maximum(jnp.max(s2, -1, keepdims=True), sink2)
    e = jnp.exp2(s2 - m)
    return e, 1.0 / (jnp.sum(e, -1, keepdims=True) + jnp.exp2(sink2 - m))


def _gqa_ctx_kernel(sink_ref, q_ref, k_ref, v_ref, o_ref):
    nseq, t = q_ref.shape[:2]
    pairs = [(s, p) for s in range(nseq) for p in range(A_HEADS // 2)]
    kd = [[_dup_half(k_ref[s], g).astype(BF16) for g in range(A_KV_HEADS)] for s in range(nseq)]
    vd = [[_dup_half(v_ref[s], g).astype(BF16) for g in range(A_KV_HEADS)] for s in range(nseq)]
    s2 = [_bdot(_stack_pair(q_ref[s, :, p * LANES:(p + 1) * LANES]), kd[s][p // 2], NT)
          for s, p in pairs]
    for i, (s, p) in enumerate(pairs):
        e, inv = _sink_exp_pair(s2[i], sink_ref[2 * p], sink_ref[2 * p + 1], t)
        o = _bdot(e, vd[s][p // 2]) * inv
        o_ref[s, :, p * LANES:(p + 1) * LANES] = jnp.where(_lane_lo((t, LANES)), o[:t], o[t:])


def _gqa_context(sink, q, k, v, bn, t):
    seq = lambda b: (b, 0, 0)
    ns = GQA_CTX_SEQS
    return pl.pallas_call(
        _gqa_ctx_kernel,
        grid=(bn // ns,),
        in_specs=[pl.BlockSpec(memory_space=pltpu.SMEM),
                  pl.BlockSpec((ns, t, A_WIDTH), seq),
                  pl.BlockSpec((ns, t, A_KV_WIDTH), seq),
                  pl.BlockSpec((ns, t, A_KV_WIDTH), seq)],
        out_specs=pl.BlockSpec((ns, t, A_WIDTH), seq),
        out_shape=jax.ShapeDtypeStruct((bn, t, A_WIDTH), F32),
        compiler_params=_params(1),
        name="gqa_context",
    )(sink, q.reshape(bn, t, A_WIDTH), k.reshape(bn, t, A_KV_WIDTH), v.reshape(bn, t, A_KV_WIDTH))


def _gqa_win_kernel(sink_ref, q_ref, kp_ref, kc_ref, kn_ref, vp_ref, vc_ref, vn_ref,
                    kx_ref, vx_ref, o_ref, *, t):
    nblk = pl.program_id(1)
    nctx = kx_ref.shape[1]
    keys = jnp.concatenate([kx_ref[...].T, kp_ref[...], kc_ref[...], kn_ref[...]], axis=0)
    vals = jnp.concatenate([vx_ref[...].T, vp_ref[...], vc_ref[...], vn_ref[...]], axis=0)
    nkeys = nctx + 3 * BLOCK
    rowi = lax.broadcasted_iota(jnp.int32, (2 * BLOCK, nkeys), 0)
    coli = lax.broadcasted_iota(jnp.int32, (2 * BLOCK, nkeys), 1)
    qpos = nblk * BLOCK + rowi % BLOCK
    kpos = nblk * BLOCK + coli - nctx - BLOCK
    valid = (coli < nctx) | ((jnp.abs(qpos - kpos) <= WINDOW) & (kpos >= 0) & (kpos < t))
    kd = [_dup_half(keys, g).astype(BF16) for g in range(A_KV_HEADS)]
    vd = [_dup_half(vals, g).astype(BF16) for g in range(A_KV_HEADS)]
    n2 = 2 * BLOCK
    s2 = [_bdot(jnp.concatenate([_stack_pair(q_ref[:, p * LANES:(p + 1) * LANES])
                                 for p in (2 * g, 2 * g + 1)], 0), kd[g], NT)
          for g in range(A_KV_HEADS)]
    for g in range(A_KV_HEADS):
        es, invs = [], []
        for j, p in enumerate((2 * g, 2 * g + 1)):
            e, inv = _sink_exp_pair(jnp.where(valid, s2[g][j * n2:(j + 1) * n2], NEG_INF),
                                    sink_ref[2 * p], sink_ref[2 * p + 1], BLOCK)
            es.append(e)
            invs.append(inv)
        og = _bdot(jnp.concatenate(es, 0), vd[g])
        for j, p in enumerate((2 * g, 2 * g + 1)):
            o = og[j * n2:(j + 1) * n2] * invs[j]
            o_ref[:, p * LANES:(p + 1) * LANES] = jnp.where(_lane_lo((BLOCK, LANES)), o[:BLOCK], o[BLOCK:])


def _gqa_window(sink, q, k, v, k_ctx_t, v_ctx_t, bn, t):
    nb = t // BLOCK
    nctx = k_ctx_t.shape[2]
    cur = lambda b, n: (b, n, 0)
    prev = lambda b, n: (b, jnp.maximum(n - 1, 0), 0)
    nxt = lambda b, n: (b, jnp.minimum(n + 1, nb - 1), 0)
    ctx = lambda b, n: (b, 0, 0)
    kv = lambda f: pl.BlockSpec((None, BLOCK, A_KV_WIDTH), f)
    k3 = k.reshape(bn, t, A_KV_WIDTH)
    v3 = v.reshape(bn, t, A_KV_WIDTH)
    return pl.pallas_call(
        functools.partial(_gqa_win_kernel, t=t),
        grid=(bn, nb),
        in_specs=[pl.BlockSpec(memory_space=pltpu.SMEM),
                  pl.BlockSpec((None, BLOCK, A_WIDTH), cur),
                  kv(prev), kv(cur), kv(nxt), kv(prev), kv(cur), kv(nxt),
                  pl.BlockSpec((None, A_KV_WIDTH, nctx), ctx),
                  pl.BlockSpec((None, A_KV_WIDTH, nctx), ctx)],
        out_specs=pl.BlockSpec((None, BLOCK, A_WIDTH), cur),
        out_shape=jax.ShapeDtypeStruct((bn, t, A_WIDTH), F32),
        compiler_params=_params(2),
        name="gqa_window",
    )(sink, q.reshape(bn, t, A_WIDTH), k3, k3, k3, v3, v3, v3, k_ctx_t, v_ctx_t)


def _even_in_kernel(*refs, use_rope, emit_t, tiles_per_seq):
    x_ref, xp_ref, xn_ref, sh_ref, sc_ref, g_ref, w_ref = refs[:7]
    refs = refs[7:]
    if use_rope:
        cos_ref, sin_ref = refs[:2]
        refs = refs[2:]
    mu_ref, lora_ref, w0_ref, a0_ref, kk_ref, ka_ref, rk_ref, ones_ref = refs[:8]
    (q_out, k_out, v_out, ga_out, gb_out, r_out, kkn_out, vv_out, lw_out, b_out, ke_out,
     bonus_out) = refs[8:20]
    i = pl.program_id(0)
    tm = x_ref.shape[0]
    pos = i % tiles_per_seq

    def norm_mod(x):
        y = x * lax.rsqrt(jnp.mean(x * x, -1, keepdims=True) + NORM_EPS)
        return (y * g_ref[...]) * (1.0 + sc_ref[...]) + sh_ref[...]

    h = norm_mod(x_ref[...])
    hb = h.astype(BF16)
    seg = dict(zip(("q", "k", "v", "ga", "pb", "gb"), EVEN_SEGS))
    proj = lambda name, lhs=hb: jnp.dot(lhs, w_ref[:, seg[name][0]:seg[name][1]], preferred_element_type=F32)
    h_ext = jnp.concatenate([norm_mod(xp_ref[...]), h, norm_mod(xn_ref[...])], 0).astype(BF16)
    p_ext = proj("pb", h_ext)

    q = proj("q")
    if use_rope:
        q = jnp.concatenate(_rope(q, cos_ref[...], sin_ref[...]), 1)
    q_out[...] = (q * QUERY_SCALE).astype(BF16)

    p = p_ext[HALO:HALO + tm]
    p_ext = jnp.concatenate([jnp.where(pos != 0, p_ext[:HALO], 0.0), p,
                             jnp.where(pos != tiles_per_seq - 1, p_ext[HALO + tm:], 0.0)], 0)
    prev = pltpu.roll(p_ext, 1, 0)[HALO:HALO + tm]
    nxt = pltpu.roll(p_ext, tm + 2 * HALO - 1, 0)[HALO:HALO + tm]
    xf = p + (0.5 * (prev + nxt) - p) * mu_ref[...]
    r = xf[:, 0:B_WIDTH]
    k = xf[:, B_WIDTH:2 * B_WIDTH]
    v = xf[:, 2 * B_WIDTH:3 * B_WIDTH]
    g = xf[:, 3 * B_WIDTH:]
    r_out[...] = r
    vv_out[...] = v

    kv_att = jnp.dot(hb, w_ref[:, seg["k"][0]:seg["v"][1]], preferred_element_type=F32)
    k_att, v_att = kv_att[:, :A_KV_WIDTH], kv_att[:, A_KV_WIDTH:]
    if use_rope:
        k_att = _rope(k_att, cos_ref[...], sin_ref[...])[0]
    k_out[...] = k_att
    v_out[...] = v_att
    if emit_t:
        refs[20][...] = k_att.T
        refs[21][...] = v_att.T

    g = jnp.where(_lane_lo(g.shape), jnp.tanh(g), g)
    lo = _bdot(g, lora_ref[...])
    ones = ones_ref[...]
    kkr = k * kk_ref[...]
    kkn = kkr / jnp.maximum(jnp.sqrt(_head_sum(kkr * kkr, ones)), 1e-12)
    kkn_out[...] = kkn

    ga_out[...] = proj("ga")

    ke_sum = jnp.zeros_like(k)
    for z in range(2):
        wz = w0_ref[z] + lo[:, z * B_WIDTH:(z + 1) * B_WIDTH]
        lw_out[z] = -math.exp(-0.5) * _sigmoid(wz)
        az = _sigmoid(a0_ref[z] + lo[:, (2 + z) * B_WIDTH:(3 + z) * B_WIDTH])
        ke = k * (1.0 + (az - 1.0) * ka_ref[...])
        ke_out[z] = ke
        b_out[z] = kkn * az
        ke_sum = ke_sum + ke
        if z == 0:
            gb_out[...] = proj("gb")
    bonus_out[...] = _head_sum(r * (0.5 * ke_sum) * rk_ref[...], ones) * v


def _even_in(x2d, shift, scale, gain, w_bf16, t, rope_tabs, emit_t, mu, lora, w0, a0, k_k, k_a, r_k, ones):
    n = x2d.shape[0]
    tile = ROW_TILE if emit_t else OUT_TILE
    assert t % tile == 0 and (not emit_t or t == tile)
    nt = n // tile
    tiles_per_seq = t // tile
    h8 = tile // HALO
    use_rope = rope_tabs is not None
    row = lambda i: (i, 0)
    const2 = lambda i: (0, 0)
    const3 = lambda i: (0, 0, 0)
    per_b = _mod_index(shift, tiles_per_seq)
    vec = pl.BlockSpec((1, B_WIDTH), const2)
    in_specs = [pl.BlockSpec((tile, D_MODEL), row),
                pl.BlockSpec((HALO, D_MODEL), lambda i: (jnp.maximum(i * h8 - 1, 0), 0)),
                pl.BlockSpec((HALO, D_MODEL), lambda i: (jnp.minimum((i + 1) * h8, n // HALO - 1), 0)),
                pl.BlockSpec((None, 1, D_MODEL), per_b),
                pl.BlockSpec((None, 1, D_MODEL), per_b),
                pl.BlockSpec((1, D_MODEL), const2),
                pl.BlockSpec(w_bf16.shape, const2)]
    args = [x2d, x2d, x2d, shift, scale, gain, w_bf16]
    if use_rope:
        pos = lambda i: (i % tiles_per_seq, 0)
        in_specs += [pl.BlockSpec((tile, LANES), pos), pl.BlockSpec((tile, LANES), pos)]
        args += list(rope_tabs)
    in_specs += [pl.BlockSpec((1, B_SHIFT_WIDTH), const2),
                 pl.BlockSpec(lora.shape, const2),
                 pl.BlockSpec((2, 1, B_WIDTH), const3),
                 pl.BlockSpec((2, 1, B_WIDTH), const3),
                 vec, vec, vec,
                 pl.BlockSpec((B_WIDTH, B_WIDTH), const2)]
    args += [mu, lora, w0, a0, k_k, k_a, r_k, ones]
    wide = pl.BlockSpec((tile, B_WIDTH), row)
    narrow = pl.BlockSpec((tile, A_KV_WIDTH), row)
    dual = pl.BlockSpec((2, tile, B_WIDTH), lambda i: (0, i, 0))
    f_wide = jax.ShapeDtypeStruct((n, B_WIDTH), F32)
    f_narrow = jax.ShapeDtypeStruct((n, A_KV_WIDTH), F32)
    f_dual = jax.ShapeDtypeStruct((2, n, B_WIDTH), F32)
    out_specs = [wide, narrow, narrow, wide, wide, wide, wide, wide, dual, dual, dual, wide]
    out_shape = [jax.ShapeDtypeStruct((n, A_WIDTH), BF16), f_narrow, f_narrow, f_wide, f_wide,
                 f_wide, f_wide, f_wide, f_dual, f_dual, f_dual, f_wide]
    if emit_t:
        out_specs += [pl.BlockSpec((None, A_KV_WIDTH, tile), lambda i: (i, 0, 0))] * 2
        out_shape += [jax.ShapeDtypeStruct((nt, A_KV_WIDTH, tile), F32)] * 2
    return pl.pallas_call(
        functools.partial(_even_in_kernel, use_rope=use_rope, emit_t=emit_t, tiles_per_seq=tiles_per_seq),
        grid=(nt,),
        in_specs=in_specs,
        out_specs=out_specs,
        out_shape=out_shape,
        compiler_params=_params(1),
        name="even_in",
    )(*args)


def _rwkv_scan_kernel(*refs, n_chunks, from_zero):
    (rf_ref, kf_ref, vf_ref, rb_ref, kb_ref, vb_ref, lwf_ref, bf_ref, kef_ref,
     lwb_ref, bb_ref, keb_ref) = refs[:12]
    if from_zero:
        yf_ref, yb_ref, sout_ref, s_scr = refs[12:]
    else:
        s0_ref, yf_ref, yb_ref, s_scr = refs[12:]
    c = pl.program_id(1)

    @pl.when(c == 0)
    def _():
        if from_zero:
            s_scr[...] = jnp.zeros_like(s_scr)
        else:
            s_scr[...] = s0_ref[...]

    row = lax.broadcasted_iota(jnp.int32, (CHUNK, LANES), 0)
    li = lax.broadcasted_iota(jnp.int32, (CHUNK, LANES), 1) % CHUNK
    blk16 = (row // 16) == (li // 16)
    blk32 = (row // 32) == (li // 32)
    eye = jnp.where(row == li, 1.0, 0.0)
    srow = lax.broadcasted_iota(jnp.int32, (PAIR, PAIR), 0)
    scol = lax.broadcasted_iota(jnp.int32, (PAIR, PAIR), 1)
    same_head = (srow // CHUNK) == (scol // CHUNK)
    tr = lax.broadcasted_iota(jnp.int32, (CHUNK, CHUNK), 0)
    tc = lax.broadcasted_iota(jnp.int32, (CHUNK, CHUNK), 1)
    dirs = ((rf_ref, kf_ref, vf_ref, lwf_ref, bf_ref, kef_ref, yf_ref),
            (rb_ref, kb_ref, vb_ref, lwb_ref, bb_ref, keb_ref, yb_ref))

    def advance(s):
        chains = [(z, p) for z in range(2) for p in range(B_HEADS // 2)]
        sls = [slice(p * LANES, (p + 1) * LANES) for _, p in chains]
        before, before_eq, tot, kkd, rd, bg, kg, vv, sc = [], [], [], [], [], [], [], [], []
        for z in range(2):
            r_ref, kk_ref, v_ref, lw_ref, b_ref, ke_ref, _ = dirs[z]
            diff = (li - row) if z else (row - li)
            tri = jnp.where(((tc - tr) if z else (tr - tc)) >= 0, 1.0, 0.0).astype(BF16)
            lw_all = lw_ref[s]
            cum_all = _split_dot(tri, lw_all)
            for p in range(B_HEADS // 2):
                sl = slice(p * LANES, (p + 1) * LANES)
                lw = lw_all[:, sl]
                cum = cum_all[:, sl]
                tt = cum[0:1] if z else cum[CHUNK - 1:CHUNK]
                b = b_ref[s, :, sl]
                ke = ke_ref[s, :, sl]
                g_inv = jnp.exp(-cum)
                g_rest = jnp.exp(tt - cum)
                before.append(diff > 0)
                before_eq.append(diff >= 0)
                tot.append(tt)
                kkd.append(kk_ref[s, :, sl] * jnp.exp(cum - lw))
                rd.append(r_ref[s, :, sl] * jnp.exp(cum))
                bg.append(b * g_rest)
                kg.append(ke * g_rest)
                vv.append(v_ref[s, :, sl])
                sc.append(_bdot(jnp.concatenate([kkd[-1], rd[-1]], 0),
                                jnp.concatenate([_stack_pair(b * g_inv), _stack_pair(ke * g_inv)], 0), NT))
            yield
        lb = [jnp.where(m, x[:CHUNK, :LANES], 0.0) for m, x in zip(before, sc)]
        mb = [jnp.where(m, x[CHUNK:, :LANES], 0.0) for m, x in zip(before_eq, sc)]
        lv = [_bdot(jnp.concatenate([jnp.where(m, x[:CHUNK, LANES:], 0.0),
                                     jnp.where(me, x[CHUNK:, LANES:], 0.0)], 0), _stack_pair(v))
              for m, me, x, v in zip(before, before_eq, sc, vv)]
        yield

        pw = [jnp.where(blk16, x, 0.0) for x in lb]
        tinv = [eye - x for x in pw]
        for _ in range(3):
            pw = [_bdot(x, _stack_pair(x)) for x in pw]
            yield
            tinv = [t + _bdot(t, _stack_pair(x)) for t, x in zip(tinv, pw)]
            yield
        for off in (blk32 & ~blk16, ~blk32):
            tmp = [_bdot(t, _stack_pair(jnp.where(off, x, 0.0))) for t, x in zip(tinv, lb)]
            yield
            tinv = [t - _bdot(m, _stack_pair(t)) for t, m in zip(tinv, tmp)]
            yield

        w12 = [_bdot(t, jnp.concatenate([_stack_pair(k), _stack_pair(x[:CHUNK])], 1))
               for t, k, x in zip(tinv, kkd, lv)]
        yield
        s_old = [s_scr[s, z, p] for z, p in chains]
        ws = [_bdot(jnp.concatenate([w[:, :LANES], r], 0), st, NT) for w, r, st in zip(w12, rd, s_old)]
        yield
        u = [-(a[:CHUNK] + w[:, LANES:]) for a, w in zip(ws, w12)]
        mbu = [_bdot(m, _stack_pair(x)) for m, x in zip(mb, u)]
        upd = [_bdot(jnp.concatenate([x, v], 0).T, jnp.concatenate([b_, k_], 0))
               for x, v, b_, k_ in zip(u, vv, bg, kg)]
        yield
        for i, (z, p) in enumerate(chains):
            dirs[z][6][s, :, sls[i]] = ws[i][CHUNK:] + mbu[i] + lv[i][CHUNK:]
            s_scr[s, z, p] = s_old[i] * jnp.exp(tot[i]) + jnp.where(same_head, upd[i], 0.0)

    live = [advance(s) for s in range(rf_ref.shape[0])]
    while live:
        live = [gen for gen in live if next(gen, "done") != "done"]

    if from_zero:
        @pl.when(c == n_chunks - 1)
        def _():
            for s in range(s_scr.shape[0]):
                for z in range(2):
                    for p in range(B_HEADS // 2):
                        st = s_scr[s, z, p]
                        sout_ref[s, z, 2 * p] = st[:HEAD_DIM, :HEAD_DIM]
                        sout_ref[s, z, 2 * p + 1] = pltpu.roll(st[HEAD_DIM:, :], HEAD_DIM, 1)[:, :HEAD_DIM]


def _rwkv_scan(r, kkn, v, lw, b, ke, s0, bn, t):
    nc = t // CHUNK
    npair = B_HEADS // 2
    ns = min(SCAN_SEQS, bn)
    fwd = pl.BlockSpec((ns, CHUNK, B_WIDTH), lambda g, c: (g, c, 0))
    bwd = pl.BlockSpec((ns, CHUNK, B_WIDTH), lambda g, c: (g, nc - 1 - c, 0))
    dfwd = pl.BlockSpec((None, ns, CHUNK, B_WIDTH), lambda g, c: (0, g, c, 0))
    dbwd = pl.BlockSpec((None, ns, CHUNK, B_WIDTH), lambda g, c: (1, g, nc - 1 - c, 0))
    r3, k3, v3 = (a.reshape(bn, t, B_WIDTH) for a in (r, kkn, v))
    lw4, b4, ke4 = (a.reshape(2, bn, t, B_WIDTH) for a in (lw, b, ke))
    seq = jax.ShapeDtypeStruct((bn, t, B_WIDTH), F32)
    in_specs = [fwd, fwd, fwd, bwd, bwd, bwd, dfwd, dfwd, dfwd, dbwd, dbwd, dbwd]
    args = [r3, k3, v3, r3, k3, v3, lw4, b4, ke4, lw4, b4, ke4]
    out_specs = [fwd, bwd]
    out_shape = [seq, seq]
    if s0 is None:
        out_specs.append(pl.BlockSpec((ns, 2, B_HEADS, HEAD_DIM, HEAD_DIM), lambda g, c: (g, 0, 0, 0, 0)))
        out_shape.append(jax.ShapeDtypeStruct((bn, 2, B_HEADS, HEAD_DIM, HEAD_DIM), F32))
    else:
        in_specs.append(pl.BlockSpec((ns, 2, npair, PAIR, PAIR), lambda g, c: (g, 0, 0, 0, 0)))
        args.append(s0)
    return pl.pallas_call(
        functools.partial(_rwkv_scan_kernel, n_chunks=nc, from_zero=s0 is None),
        grid=(bn // ns, nc),
        in_specs=in_specs,
        out_specs=out_specs,
        out_shape=out_shape,
        scratch_shapes=[pltpu.VMEM((ns, 2, npair, PAIR, PAIR), F32)],
        compiler_params=_params(2),
        name="rwkv_scan",
    )(*args)


def _post(x, z, gate, gain):
    zn = z * lax.rsqrt(jnp.mean(z * z, -1, keepdims=True) + NORM_EPS)
    return x + gate * (zn * gain)


def _even_out_kernel(ya_ref, ga_ref, yf_ref, yb_ref, bonus_ref, gb_ref, lnw_ref, lnb_ref, ones_ref,
                     x_ref, gate_ref, gain_ref, w_ref, o_ref):
    y = yf_ref[...] + yb_ref[...]
    ones = ones_ref[...]
    mean = _head_sum(y, ones) * (1.0 / HEAD_DIM)
    d = y - mean
    var = _head_sum(d * d, ones) * (1.0 / HEAD_DIM)
    yb = (d * lax.rsqrt(var + RWKV_LN_EPS)) * lnw_ref[...] + lnb_ref[...] + bonus_ref[...]
    za = (ya_ref[...] * _silu(ga_ref[...])).astype(BF16)
    zb = (yb * _silu(gb_ref[...])).astype(BF16)
    z = (jnp.dot(za, w_ref[0:A_WIDTH, :], preferred_element_type=F32)
         + jnp.dot(zb, w_ref[A_WIDTH:, :], preferred_element_type=F32))
    o_ref[...] = _post(x_ref[...], z, gate_ref[...], gain_ref[...])


def _even_out(ya, ga, yf, yb, bonus, gb, ln_w, ln_b, ones, x2d, gate, gain, w_bf16, t):
    n = x2d.shape[0]
    row = lambda i: (i, 0)
    const2 = lambda i: (0, 0)
    half = pl.BlockSpec((OUT_TILE, B_WIDTH), row)
    full = pl.BlockSpec((OUT_TILE, D_MODEL), row)
    vec = pl.BlockSpec((1, B_WIDTH), const2)
    assert gate.shape[0] == 1 or t % OUT_TILE == 0
    return pl.pallas_call(
        _even_out_kernel,
        grid=(n // OUT_TILE,),
        in_specs=[half, half, half, half, half, half, vec, vec,
                  pl.BlockSpec((B_WIDTH, B_WIDTH), const2),
                  full,
                  pl.BlockSpec((None, 1, D_MODEL), _mod_index(gate, t // OUT_TILE)),
                  pl.BlockSpec((1, D_MODEL), const2),
                  pl.BlockSpec((D_MODEL, D_MODEL), const2)],
        out_specs=full,
        out_shape=jax.ShapeDtypeStruct((n, D_MODEL), F32),
        compiler_params=_params(1),
        name="even_out",
    )(ya, ga, yf, yb, bonus, gb, ln_w, ln_b, ones, x2d, gate, gain, w_bf16)


def _odd_out_kernel(o_ref_in, gc_ref, x_ref, gate_ref, gain_ref, w_ref, o_ref):
    zc = (o_ref_in[...] * _silu(gc_ref[...])).astype(BF16)
    z = jnp.dot(zc, w_ref[...], preferred_element_type=F32)
    o_ref[...] = _post(x_ref[...], z, gate_ref[...], gain_ref[...])


def _odd_out(o, gc, x2d, gate, gain, w_bf16, t):
    n = x2d.shape[0]
    row = lambda i: (i, 0)
    const2 = lambda i: (0, 0)
    full = pl.BlockSpec((OUT_TILE, D_MODEL), row)
    assert gate.shape[0] == 1 or t % OUT_TILE == 0
    return pl.pallas_call(
        _odd_out_kernel,
        grid=(n // OUT_TILE,),
        in_specs=[full, full, full,
                  pl.BlockSpec((None, 1, D_MODEL), _mod_index(gate, t // OUT_TILE)),
                  pl.BlockSpec((1, D_MODEL), const2),
                  pl.BlockSpec((D_MODEL, D_MODEL), const2)],
        out_specs=full,
        out_shape=jax.ShapeDtypeStruct((n, D_MODEL), F32),
        compiler_params=_params(1),
        name="odd_out",
    )(o, gc, x2d, gate, gain, w_bf16)


def _lambda(lq1_ref, lk1_ref, lq2_ref, lk2_ref, lam_init):
    s1 = jnp.sum(lq1_ref[...] * lk1_ref[...], -1, keepdims=True)
    s2 = jnp.sum(lq2_ref[...] * lk2_ref[...], -1, keepdims=True)
    return jnp.exp(s1) - jnp.exp(s2) + lam_init


def _subln(o, gain, lam_init):
    on = o * lax.rsqrt(jnp.mean(o * o, -1, keepdims=True) + SUBLN_EPS)
    return (on * gain) * (1.0 - lam_init)


def _diff_ctx_kernel(lq1_ref, lk1_ref, lq2_ref, lk2_ref, sub_ref, q_ref, k_ref, v_ref,
                     gc_ref, x_ref, gate_ref, gain_ref, w_ref, y_ref, o_scr, *, lam_init):
    nseq, t = q_ref.shape[:2]
    lam = _lambda(lq1_ref, lk1_ref, lq2_ref, lk2_ref, lam_init)
    units = [(s, slice(h * LANES, (h + 1) * LANES)) for s in range(nseq) for h in range(C_HEADS)]
    k = [k_ref[s].reshape(t, C_WIDTH) for s in range(nseq)]
    v = [v_ref[s].reshape(t, C_WIDTH) for s in range(nseq)]
    s2 = [_bdot(_stack_pair(q_ref[s, :, sl]), k[s][:, sl], NT)
          for s, sl in units]
    for i, (s, sl) in enumerate(units):
        e = jnp.exp2(s2[i] - jnp.max(s2[i], -1, keepdims=True))
        ov = _bdot(e, v[s][:, sl]) * (1.0 / jnp.sum(e, -1, keepdims=True))
        o_scr[s * t:(s + 1) * t, sl] = _subln(ov[:t] - lam * ov[t:], sub_ref[...], lam_init)
    gc = gc_ref[...].reshape(nseq * t, C_WIDTH)
    zc = (o_scr[...] * _silu(gc)).astype(BF16)
    z = jnp.dot(zc, w_ref[...], preferred_element_type=F32)
    y = _post(x_ref[...].reshape(nseq * t, D_MODEL), z, gate_ref[...], gain_ref[...])
    y_ref[...] = y.reshape(nseq, t, D_MODEL)


def _diff_context_out(lams, subln, q, k, v, gc, x2d, gate, gain, w_bf16, bn, t, lam_init):
    ns = DIFF_CTX_SEQS
    seq = pl.BlockSpec((ns, t, C_WIDTH), lambda b: (b, 0, 0))
    heads = pl.BlockSpec((ns, t, C_HEADS, LANES), lambda b: (b, 0, 0, 0))
    small = pl.BlockSpec((1, C_QK_DIM), lambda b: (0, 0))
    const2 = lambda b: (0, 0)
    to3 = lambda a: a.reshape((bn, t) + a.shape[1:])
    return pl.pallas_call(
        functools.partial(_diff_ctx_kernel, lam_init=lam_init),
        grid=(bn // ns,),
        in_specs=[small, small, small, small, pl.BlockSpec((1, C_V_DIM), const2),
                  seq, heads, heads, seq, seq,
                  pl.BlockSpec((None, 1, D_MODEL), lambda b: (0, 0, 0)),
                  pl.BlockSpec((1, D_MODEL), const2),
                  pl.BlockSpec((D_MODEL, D_MODEL), const2)],
        out_specs=seq,
        out_shape=jax.ShapeDtypeStruct((bn, t, D_MODEL), F32),
        scratch_shapes=[pltpu.VMEM((ns * t, C_WIDTH), F32)],
        compiler_params=_params(1),
        name="diff_context",
    )(*lams, subln, to3(q), to3(k), to3(v), to3(gc), to3(x2d), gate, gain, w_bf16).reshape(bn * t, D_MODEL)


def _diff_lat_kernel(lq1_ref, lk1_ref, lq2_ref, lk2_ref, sub_ref, q_ref, k_ref, v_ref, kx_ref, vx_ref,
                     o_ref, *, lam_init):
    lam = _lambda(lq1_ref, lk1_ref, lq2_ref, lk2_ref, lam_init)
    kx = kx_ref[...].astype(BF16)
    k = k_ref[...].astype(BF16)
    tq = DIFF_Q_SUB
    subs = range(q_ref.shape[0] // tq)
    scores = []
    for i in subs:
        qs = _stack_pair(q_ref[i * tq:(i + 1) * tq, :])
        scores.append((_bdot(qs, kx, NT), _bdot(qs, k, NT)))
    for i in subs:
        sx, sl = scores[i]
        m = jnp.maximum(jnp.max(sx, -1, keepdims=True), jnp.max(sl, -1, keepdims=True))
        ex = jnp.exp2(sx - m)
        el = jnp.exp2(sl - m)
        den = jnp.sum(ex, -1, keepdims=True) + jnp.sum(el, -1, keepdims=True)
        ratio = lam * den[:tq] / den[tq:]
        o = (_bdot(ex[:tq] - ex[tq:] * ratio, vx_ref[...])
             + _bdot(el[:tq] - el[tq:] * ratio, v_ref[...])) * (1.0 / den[:tq])
        o_ref[i * tq:(i + 1) * tq, :] = _subln(o, sub_ref[...], lam_init)


def _diff_latent(lams, subln, q, k, v, k_ctx, v_ctx, bn, t, lam_init):
    nb = t // DIFF_Q_TILE
    nctx = k_ctx.shape[1]
    small = pl.BlockSpec((1, C_QK_DIM), lambda b, h, n: (0, 0))
    qspec = pl.BlockSpec((None, DIFF_Q_TILE, LANES), lambda b, h, n: (b, n, h))
    kvspec = pl.BlockSpec((None, t, LANES), lambda b, h, n: (b, 0, h))
    cxspec = pl.BlockSpec((None, nctx, LANES), lambda b, h, n: (b, 0, h))
    return pl.pallas_call(
        functools.partial(_diff_lat_kernel, lam_init=lam_init),
        grid=(bn, C_HEADS, nb),
        in_specs=[small, small, small, small,
                  pl.BlockSpec((1, C_V_DIM), lambda b, h, n: (0, 0)),
                  qspec, kvspec, kvspec, cxspec, cxspec],
        out_specs=qspec,
        out_shape=jax.ShapeDtypeStruct((bn, t, C_WIDTH), F32),
        compiler_params=_params(3),
        name="diff_latent",
    )(*lams, subln, q.reshape(bn, t, C_WIDTH), k.reshape(bn, t, C_WIDTH), v.reshape(bn, t, C_WIDTH),
      k_ctx, v_ctx)


def _rope_tables(t):
    nf = HEAD_DIM // 4
    inv = 1.0 / (ROPE_BASE ** (jnp.arange(nf, dtype=F32) / nf))
    pos = jnp.arange(t)
    ang_r = (pos // GRID_W).astype(F32)[:, None] * inv[None]
    ang_c = (pos % GRID_W).astype(F32)[:, None] * inv[None]
    ang = jnp.concatenate([ang_r, ang_r, ang_c, ang_c], -1)
    cos, sin = jnp.cos(ang), jnp.sin(ang)
    sign = jnp.where((jnp.arange(HEAD_DIM) // nf) % 2 == 0, -1.0, 1.0).astype(F32)
    return jnp.tile(cos, (1, 2)), jnp.tile(sin * sign[None], (1, 2))


EVEN_SEGS = ((0, 512), (512, 640), (640, 768), (768, 1280), (1280, 2944), (2944, 3456))
ODD_SEGS = ((0, 1024), (1024, 2048), (2048, 3072), (3072, 4096))


def _pair_states(st):
    bn = st.shape[0]
    s = st.reshape(bn, 2, 4, 2, HEAD_DIM, HEAD_DIM)
    z = jnp.zeros_like(s[:, :, :, 0])
    top = jnp.concatenate([s[:, :, :, 0], z], -1)
    bot = jnp.concatenate([z, s[:, :, :, 1]], -1)
    return jnp.concatenate([top, bot], -2)


def kernel(x_prompt, x_sample, cache_a_k, cache_a_v, state_rwkv, cache_c_k, cache_c_v, c, c_ctx,
           ada_w, ada_b, norm_pre, norm_post, w_out, even_w_in, a_sink, b_mu, b_w0, b_w2, b_a0,
           b_a2, b_kk, b_ka, b_rk, b_ln_w, b_ln_b, odd_w_in, c_lq1, c_lk1, c_lq2, c_lk2, c_subln):
    bp, tp, _ = x_prompt.shape
    bs, ts, _ = x_sample.shape
    past = cache_a_k.shape[2]

    cvecs = jnp.concatenate([c_ctx[None], c, jnp.zeros((8 - 1 - bs, D_MODEL), F32)], 0)
    mods = _modulation(cvecs, 1 + bs, ada_w, ada_b)
    rope_tabs = _rope_tables(ts)
    ones = (jnp.arange(B_WIDTH)[:, None] // HEAD_DIM == jnp.arange(B_WIDTH)[None] // HEAD_DIM).astype(BF16)

    xp = x_prompt.reshape(bp * tp, D_MODEL)
    xs = x_sample.reshape(bs * ts, D_MODEL)

    def mod_rows(layer, rows):
        m = mods[layer, rows[0]:rows[1]][:, None, :]
        return m[..., :D_MODEL], m[..., D_MODEL:2 * D_MODEL], m[..., 2 * D_MODEL:]

    e = 0
    w_in0 = even_w_in[e].astype(BF16)
    w_out0 = w_out[0].astype(BF16)
    gain_pre = norm_pre[0][None]
    gain_post = norm_post[0][None]
    zpad = jnp.zeros((DECAY_LORA, B_WIDTH), F32)
    lora = jnp.concatenate([
        jnp.concatenate([b_w2[e, 0], b_w2[e, 1], zpad, zpad], 1),
        jnp.concatenate([zpad, zpad, b_a2[e, 0], b_a2[e, 1]], 1)], 0).astype(BF16)
    feat_args = (b_mu[e][None], lora, b_w0[e][:, None, :], b_a0[e][:, None, :], b_kk[e][None],
                 b_ka[e][None], b_rk[e][None], ones)

    def even_layer(x2d, bn, t, rows, ctx):
        shift, scale, gate = mod_rows(0, rows)
        (qa, ka, va, ga, gb, r, kkn, v, lw, b, ke, bonus, *kv_t) = _even_in(
            x2d, shift, scale, gain_pre, w_in0, t, None if ctx is None else rope_tabs, ctx is None,
            *feat_args)
        if ctx is None:
            ya = _gqa_context(a_sink[e], qa, ka, va, bn, t)
            s0 = None
        else:
            k_ctx, v_ctx, st0 = ctx
            ya = _gqa_window(a_sink[e], qa, ka, va, k_ctx, v_ctx, bn, t)
            s0 = _pair_states(st0)
        yf, yb, *s_fin = _rwkv_scan(r, kkn, v, lw, b, ke, s0, bn, t)
        y = _even_out(ya.reshape(bn * t, A_WIDTH), ga, yf.reshape(bn * t, B_WIDTH),
                      yb.reshape(bn * t, B_WIDTH), bonus, gb,
                      b_ln_w[e][None], b_ln_b[e][None], ones, x2d, gate, gain_post, w_out0, t)
        return y, kv_t, s_fin

    yp, (ka_t, va_t), st_p = even_layer(xp, bp, tp, (0, 1), None)
    feat_major = lambda a: jnp.transpose(a, (0, 2, 3, 1)).reshape(bs, A_KV_WIDTH, past)
    ctx_a = (feat_major(cache_a_k[:, e]), feat_major(cache_a_v[:, e]), state_rwkv[:, e])
    ys, _, _ = even_layer(xs, bs, ts, (1, 1 + bs), ctx_a)

    def cache_a(x_t):
        return jnp.transpose(x_t.reshape(bp, A_KV_HEADS, HEAD_DIM, tp), (0, 3, 1, 2))[:, None]

    o = 0
    lam_init = 0.8 - 0.6 * math.exp(-0.3 * 1)
    w_in1 = odd_w_in[o].astype(BF16)
    w_out1 = w_out[1].astype(BF16)
    gain_pre1 = norm_pre[1][None]
    gain_post1 = norm_post[1][None]
    lams = (c_lq1[o][None], c_lk1[o][None], c_lq2[o][None], c_lk2[o][None])
    subln = c_subln[o][None]

    def odd_layer(x2d, bn, t, rows, ctx):
        shift, scale, gate = mod_rows(1, rows)
        qc, kc, vc, gc = _proj_in(x2d, shift, scale, gain_pre1, w_in1, ODD_SEGS, (0, 1), t,
                                  None if ctx is None else rope_tabs,
                                  head_segs=(1, 2) if ctx is None else (),
                                  bf16_segs=() if ctx is None else (1, 2))
        if ctx is None:
            y = _diff_context_out(lams, subln, qc, kc, vc, gc, x2d, gate, gain_post1, w_out1, bn, t, lam_init)
        else:
            k_ctx, v_ctx = ctx
            oc = _diff_latent(lams, subln, qc, kc, vc, k_ctx, v_ctx, bn, t, lam_init)
            y = _odd_out(oc.reshape(bn * t, C_WIDTH), gc, x2d, gate, gain_post1, w_out1, t)
        return y, kc, vc

    yp2, kc_p, vc_p = odd_layer(yp, bp, tp, (0, 1), None)
    ctx_c = (cache_c_k[:, o].reshape(bs, past, C_WIDTH), cache_c_v[:, o].reshape(bs, past, C_WIDTH))
    ys2, _, _ = odd_layer(ys, bs, ts, (1, 1 + bs), ctx_c)

    return (yp2.reshape(bp, tp, D_MODEL),
            ys2.reshape(bs, ts, D_MODEL),
            cache_a(ka_t),
            cache_a(va_t),
            st_p[0][:, None],
            kc_p.reshape(bp, 1, tp, C_HEADS, 2 * C_QK_DIM),
            vc_p.reshape(bp, 1, tp, C_HEADS, C_V_DIM))
```

```python
import functools
import math

import jax
import jax.numpy as jnp
from jax import lax
from jax.experimental import pallas as pl
from jax.experimental.pallas import tpu as pltpu

F32 = jnp.float32
BF16 = jnp.bfloat16

D_MODEL = 1024
DEPTH = 2
GRID_W = 64
HEAD_DIM = 64
ROPE_BASE = 10000.0
A_HEADS = 8
A_KV_HEADS = 2
A_WIDTH = 512
A_KV_WIDTH = 128
WINDOW = 128
BLOCK = 128
B_HEADS = 8
B_WIDTH = 512
DECAY_LORA = 64
AAA_LORA = 64
B_SHIFT_WIDTH = 3 * B_WIDTH + DECAY_LORA + AAA_LORA
C_HEADS = 8
C_QK_DIM = 64
C_V_DIM = 128
C_WIDTH = 1024
EVEN_IN = 3456
ODD_IN = 4096
NORM_EPS = 1e-6
RWKV_LN_EPS = 64e-5
SUBLN_EPS = 1e-5
NEG_INF = -1e30

LANES = 128
HALO = 8
ROW_TILE = 256
OUT_TILE = 512
GQA_CTX_SEQS = 4
DIFF_CTX_SEQS = 2
DIFF_Q_TILE = 1024
DIFF_Q_SUB = 128
CHUNK = 64
SCAN_SEQS = 4
PAIR = 2 * CHUNK
VMEM_LIMIT = 48 * 1024 * 1024

LOG2E = math.log2(math.e)
QUERY_SCALE = HEAD_DIM ** -0.5 * LOG2E

NN = ((1,), (0,))
NT = ((1,), (1,))


def _bdot(a, b, dims=NN):
    return lax.dot_general(a.astype(BF16), b.astype(BF16), (dims, ((), ())),
                           preferred_element_type=F32)


def _split_dot(sel_bf16, x):
    x1 = x.astype(BF16)
    x2 = (x - x1.astype(F32)).astype(BF16)
    dot = lambda p: jnp.dot(sel_bf16, p, preferred_element_type=F32)
    return dot(x1) + dot(x2)


def _head_sum(x, ones_bf16):
    hi = x.astype(BF16)
    lo = (x - hi.astype(F32)).astype(BF16)
    return (jnp.dot(hi, ones_bf16, preferred_element_type=F32)
            + jnp.dot(lo, ones_bf16, preferred_element_type=F32))


def _sigmoid(x):
    return 1.0 / (1.0 + jnp.exp(-x))


def _silu(x):
    return x * _sigmoid(x)


def _params(n_axes):
    return pltpu.CompilerParams(dimension_semantics=("arbitrary",) * n_axes,
                                vmem_limit_bytes=VMEM_LIMIT)


def _lane_lo(shape):
    return (lax.broadcasted_iota(jnp.int32, shape, len(shape) - 1) % LANES) < HEAD_DIM


def _stack_pair(x):
    lo = _lane_lo(x.shape)
    z = jnp.zeros_like(x)
    return jnp.concatenate([jnp.where(lo, x, z), jnp.where(lo, z, x)], axis=0)


def _dup_half(x, g):
    lo = _lane_lo(x.shape)
    sw = pltpu.roll(x, HEAD_DIM, 1)
    return jnp.where(lo, x, sw) if g == 0 else jnp.where(lo, sw, x)


def _rope(x, cos, sin_signed):
    out = []
    even_q = (lax.broadcasted_iota(jnp.int32, cos.shape, 1) // (HEAD_DIM // 4)) % 2 == 0
    for g in range(x.shape[1] // LANES):
        xg = x[:, g * LANES:(g + 1) * LANES]
        up = pltpu.roll(xg, LANES - HEAD_DIM // 4, 1)
        dn = pltpu.roll(xg, HEAD_DIM // 4, 1)
        out.append(xg * cos + jnp.where(even_q, up, dn) * sin_signed)
    return out


def _mod_kernel(c_ref, w_ref, b_ref, o_ref, *, n_used):
    s_t = _silu(c_ref[...]).T
    w = w_ref[...]
    rows = [jnp.sum(w * s_t[:, r:r + 1], axis=0, keepdims=True) for r in range(n_used)]
    rows.append(jnp.zeros((c_ref.shape[0] - n_used, w.shape[1]), F32))
    o_ref[...] = jnp.concatenate(rows, axis=0) + b_ref[...]


def _modulation(cvecs, n_used, ada_w, ada_b):
    nrow = cvecs.shape[0]
    return pl.pallas_call(
        functools.partial(_mod_kernel, n_used=n_used),
        grid=(DEPTH, 3),
        in_specs=[pl.BlockSpec((nrow, D_MODEL), lambda l, j: (0, 0)),
                  pl.BlockSpec((None, D_MODEL, D_MODEL), lambda l, j: (l, 0, j)),
                  pl.BlockSpec((None, 1, D_MODEL), lambda l, j: (l, 0, j))],
        out_specs=pl.BlockSpec((None, nrow, D_MODEL), lambda l, j: (l, 0, j)),
        out_shape=jax.ShapeDtypeStruct((DEPTH, nrow, 3 * D_MODEL), F32),
        compiler_params=_params(2),
        name="modulation",
    )(cvecs, ada_w, ada_b.reshape(DEPTH, 1, 3 * D_MODEL))


def _mod_index(mod, tiles_per_seq):
    if mod.shape[0] == 1:
        return lambda i: (0, 0, 0)
    return lambda i: (i // tiles_per_seq, 0, 0)


def _proj_in_kernel(*refs, segs, rope_segs, use_rope, head_segs, bf16_segs):
    if use_rope:
        x_ref, sh_ref, sc_ref, g_ref, w_ref, cos_ref, sin_ref = refs[:7]
        outs = refs[7:]
    else:
        x_ref, sh_ref, sc_ref, g_ref, w_ref = refs[:5]
        outs = refs[5:]
    x = x_ref[...]
    y = x * lax.rsqrt(jnp.mean(x * x, -1, keepdims=True) + NORM_EPS)
    h = (y * g_ref[...]) * (1.0 + sc_ref[...]) + sh_ref[...]
    hb = h.astype(BF16)
    for idx, (lo, hi) in enumerate(segs):
        o = jnp.dot(hb, w_ref[:, lo:hi], preferred_element_type=F32)
        if idx == 0:
            post = lambda a: (a * QUERY_SCALE).astype(BF16)
        elif idx in bf16_segs:
            post = lambda a: a.astype(BF16)
        else:
            post = lambda a: a
        if use_rope and idx in rope_segs:
            parts = _rope(o, cos_ref[...], sin_ref[...])
            for g, part in enumerate(parts):
                outs[idx][:, g * LANES:(g + 1) * LANES] = post(part)
        elif idx in head_segs:
            outs[idx][...] = o.reshape(o.shape[0], (hi - lo) // LANES, LANES)
        else:
            outs[idx][...] = post(o)


def _proj_in(x2d, shift, scale, gain, w_bf16, segs, rope_segs, t, rope_tabs, head_segs=(), bf16_segs=()):
    n = x2d.shape[0]
    tile = OUT_TILE
    assert shift.shape[0] == 1 or t % tile == 0
    nt = n // tile
    tiles_per_seq = max(t // tile, 1)
    use_rope = rope_tabs is not None
    row = lambda i: (i, 0)
    per_b = _mod_index(shift, tiles_per_seq)
    in_specs = [pl.BlockSpec((tile, D_MODEL), row),
                pl.BlockSpec((None, 1, D_MODEL), per_b),
                pl.BlockSpec((None, 1, D_MODEL), per_b),
                pl.BlockSpec((1, D_MODEL), lambda i: (0, 0)),
                pl.BlockSpec(w_bf16.shape, lambda i: (0, 0))]
    args = [x2d, shift, scale, gain, w_bf16]
    if use_rope:
        pos = lambda i: (i % tiles_per_seq, 0)
        in_specs += [pl.BlockSpec((tile, LANES), pos), pl.BlockSpec((tile, LANES), pos)]
        args += list(rope_tabs)
    widths = [hi - lo for lo, hi in segs]
    out_specs, out_shape = [], []
    for idx, w in enumerate(widths):
        if idx in head_segs:
            out_specs.append(pl.BlockSpec((tile, w // LANES, LANES), lambda i: (i, 0, 0)))
            out_shape.append(jax.ShapeDtypeStruct((n, w // LANES, LANES), F32))
        else:
            out_specs.append(pl.BlockSpec((tile, w), row))
            out_shape.append(jax.ShapeDtypeStruct((n, w), BF16 if idx == 0 or idx in bf16_segs else F32))
    return pl.pallas_call(
        functools.partial(_proj_in_kernel, segs=segs, rope_segs=rope_segs, use_rope=use_rope,
                          head_segs=tuple(head_segs), bf16_segs=tuple(bf16_segs)),
        grid=(nt,),
        in_specs=in_specs,
        out_specs=out_specs,
        out_shape=out_shape,
        compiler_params=_params(1),
        name="proj_in",
    )(*args)


def _sink_exp_pair(s2, sink_a, sink_b, n):
    rowi = lax.broadcasted_iota(jnp.int32, (2 * n, 1), 0)
    sink2 = jnp.where(rowi < n, sink_a, sink_b) * LOG2E
    m = jnp.maximum(jnp.max(s2, -1, keepdims=True), sink2)
    e = jnp.exp2(s2 - m)
    return e, 1.0 / (jnp.sum(e, -1, keepdims=True) + jnp.exp2(sink2 - m))


def _gqa_ctx_kernel(sink_ref, q_ref, k_ref, v_ref, o_ref):
    nseq, t = q_ref.shape[:2]
    groups = [(s, g) for s in range(nseq) for g in range(A_KV_HEADS)]
    kd = [[_dup_half(k_ref[s], g).astype(BF16) for g in range(A_KV_HEADS)] for s in range(nseq)]
    vd = [[_dup_half(v_ref[s], g).astype(BF16) for g in range(A_KV_HEADS)] for s in range(nseq)]
    n2 = 2 * t
    s2 = [_bdot(jnp.concatenate([_stack_pair(q_ref[s, :, p * LANES:(p + 1) * LANES])
                                 for p in (2 * g, 2 * g + 1)], 0), kd[s][g], NT)
          for s, g in groups]
    for i, (s, g) in enumerate(groups):
        es, invs = [], []
        for j, p in enumerate((2 * g, 2 * g + 1)):
            e, inv = _sink_exp_pair(s2[i][j * n2:(j + 1) * n2], sink_ref[2 * p], sink_ref[2 * p + 1], t)
            es.append(e)
            invs.append(inv)
        og = _bdot(jnp.concatenate(es, 0), vd[s][g])
        for j, p in enumerate((2 * g, 2 * g + 1)):
            o = og[j * n2:(j + 1) * n2] * invs[j]
            o_ref[s, :, p * LANES:(p + 1) * LANES] = jnp.where(_lane_lo((t, LANES)), o[:t], o[t:])


def _gqa_context(sink, q, k, v, bn, t):
    seq = lambda b: (b, 0, 0)
    ns = GQA_CTX_SEQS
    return pl.pallas_call(
        _gqa_ctx_kernel,
        grid=(bn // ns,),
        in_specs=[pl.BlockSpec(memory_space=pltpu.SMEM),
                  pl.BlockSpec((ns, t, A_WIDTH), seq),
                  pl.BlockSpec((ns, t, A_KV_WIDTH), seq),
                  pl.BlockSpec((ns, t, A_KV_WIDTH), seq)],
        out_specs=pl.BlockSpec((ns, t, A_WIDTH), seq),
        out_shape=jax.ShapeDtypeStruct((bn, t, A_WIDTH), F32),
        compiler_params=_params(1),
        name="gqa_context",
    )(sink, q.reshape(bn, t, A_WIDTH), k.reshape(bn, t, A_KV_WIDTH), v.reshape(bn, t, A_KV_WIDTH))


def _gqa_win_kernel(sink_ref, q_ref, kp_ref, kc_ref, kn_ref, vp_ref, vc_ref, vn_ref,
                    kx_ref, vx_ref, o_ref, *, t):
    nblk = pl.program_id(1)
    nctx = kx_ref.shape[1]
    keys = jnp.concatenate([kx_ref[...].T, kp_ref[...], kc_ref[...], kn_ref[...]], axis=0)
    vals = jnp.concatenate([vx_ref[...].T, vp_ref[...], vc_ref[...], vn_ref[...]], axis=0)
    nkeys = nctx + 3 * BLOCK
    rowi = lax.broadcasted_iota(jnp.int32, (2 * BLOCK, nkeys), 0)
    coli = lax.broadcasted_iota(jnp.int32, (2 * BLOCK, nkeys), 1)
    qpos = nblk * BLOCK + rowi % BLOCK
    kpos = nblk * BLOCK + coli - nctx - BLOCK
    valid = (coli < nctx) | ((jnp.abs(qpos - kpos) <= WINDOW) & (kpos >= 0) & (kpos < t))
    kd = [_dup_half(keys, g).astype(BF16) for g in range(A_KV_HEADS)]
    vd = [_dup_half(vals, g).astype(BF16) for g in range(A_KV_HEADS)]
    n2 = 2 * BLOCK
    s2 = [_bdot(jnp.concatenate([_stack_pair(q_ref[:, p * LANES:(p + 1) * LANES])
                                 for p in (2 * g, 2 * g + 1)], 0), kd[g], NT)
          for g in range(A_KV_HEADS)]
    for g in range(A_KV_HEADS):
        es, invs = [], []
        for j, p in enumerate((2 * g, 2 * g + 1)):
            e, inv = _sink_exp_pair(jnp.where(valid, s2[g][j * n2:(j + 1) * n2], NEG_INF),
                                    sink_ref[2 * p], sink_ref[2 * p + 1], BLOCK)
            es.append(e)
            invs.append(inv)
        og = _bdot(jnp.concatenate(es, 0), vd[g])
        for j, p in enumerate((2 * g, 2 * g + 1)):
            o = og[j * n2:(j + 1) * n2] * invs[j]
            o_ref[:, p * LANES:(p + 1) * LANES] = jnp.where(_lane_lo((BLOCK, LANES)), o[:BLOCK], o[BLOCK:])


def _gqa_window(sink, q, k, v, k_ctx_t, v_ctx_t, bn, t):
    nb = t // BLOCK
    nctx = k_ctx_t.shape[2]
    cur = lambda b, n: (b, n, 0)
    prev = lambda b, n: (b, jnp.maximum(n - 1, 0), 0)
    nxt = lambda b, n: (b, jnp.minimum(n + 1, nb - 1), 0)
    ctx = lambda b, n: (b, 0, 0)
    kv = lambda f: pl.BlockSpec((None, BLOCK, A_KV_WIDTH), f)
    k3 = k.reshape(bn, t, A_KV_WIDTH)
    v3 = v.reshape(bn, t, A_KV_WIDTH)
    return pl.pallas_call(
        functools.partial(_gqa_win_kernel, t=t),
        grid=(bn, nb),
        in_specs=[pl.BlockSpec(memory_space=pltpu.SMEM),
                  pl.BlockSpec((None, BLOCK, A_WIDTH), cur),
                  kv(prev), kv(cur), kv(nxt), kv(prev), kv(cur), kv(nxt),
                  pl.BlockSpec((None, A_KV_WIDTH, nctx), ctx),
                  pl.BlockSpec((None, A_KV_WIDTH, nctx), ctx)],
        out_specs=pl.BlockSpec((None, BLOCK, A_WIDTH), cur),
        out_shape=jax.ShapeDtypeStruct((bn, t, A_WIDTH), F32),
        compiler_params=_params(2),
        name="gqa_window",
    )(sink, q.reshape(bn, t, A_WIDTH), k3, k3, k3, v3, v3, v3, k_ctx_t, v_ctx_t)


def _even_in_kernel(*refs, use_rope, emit_t, tiles_per_seq):
    x_ref, xp_ref, xn_ref, sh_ref, sc_ref, g_ref, w_ref = refs[:7]
    refs = refs[7:]
    if use_rope:
        cos_ref, sin_ref = refs[:2]
        refs = refs[2:]
    mu_ref, lora_ref, w0_ref, a0_ref, kk_ref, ka_ref, rk_ref, ones_ref = refs[:8]
    (q_out, k_out, v_out, ga_out, gb_out, r_out, kkn_out, vv_out, lw_out, b_out, ke_out,
     bonus_out) = refs[8:20]
    i = pl.program_id(0)
    tm = x_ref.shape[0]
    pos = i % tiles_per_seq

    def norm_mod(x):
        y = x * lax.rsqrt(jnp.mean(x * x, -1, keepdims=True) + NORM_EPS)
        return (y * g_ref[...]) * (1.0 + sc_ref[...]) + sh_ref[...]

    h = norm_mod(x_ref[...])
    hb = h.astype(BF16)
    seg = dict(zip(("q", "k", "v", "ga", "pb", "gb"), EVEN_SEGS))
    proj = lambda name, lhs=hb: jnp.dot(lhs, w_ref[:, seg[name][0]:seg[name][1]], preferred_element_type=F32)
    h_ext = jnp.concatenate([norm_mod(xp_ref[...]), h, norm_mod(xn_ref[...])], 0).astype(BF16)
    p_ext = proj("pb", h_ext)

    q = proj("q")
    if use_rope:
        q = jnp.concatenate(_rope(q, cos_ref[...], sin_ref[...]), 1)
    q_out[...] = (q * QUERY_SCALE).astype(BF16)

    p = p_ext[HALO:HALO + tm]
    p_ext = jnp.concatenate([jnp.where(pos != 0, p_ext[:HALO], 0.0), p,
                             jnp.where(pos != tiles_per_seq - 1, p_ext[HALO + tm:], 0.0)], 0)
    prev = pltpu.roll(p_ext, 1, 0)[HALO:HALO + tm]
    nxt = pltpu.roll(p_ext, tm + 2 * HALO - 1, 0)[HALO:HALO + tm]
    xf = p + (0.5 * (prev + nxt) - p) * mu_ref[...]
    r = xf[:, 0:B_WIDTH]
    k = xf[:, B_WIDTH:2 * B_WIDTH]
    v = xf[:, 2 * B_WIDTH:3 * B_WIDTH]
    g = xf[:, 3 * B_WIDTH:]
    r_out[...] = r
    vv_out[...] = v

    kv_att = jnp.dot(hb, w_ref[:, seg["k"][0]:seg["v"][1]], preferred_element_type=F32)
    k_att, v_att = kv_att[:, :A_KV_WIDTH], kv_att[:, A_KV_WIDTH:]
    if use_rope:
        k_att = _rope(k_att, cos_ref[...], sin_ref[...])[0]
    k_out[...] = k_att
    v_out[...] = v_att
    if emit_t:
        refs[20][...] = k_att.T
        refs[21][...] = v_att.T

    g = jnp.where(_lane_lo(g.shape), jnp.tanh(g), g)
    lo = _bdot(g, lora_ref[...])
    ones = ones_ref[...]
    kkr = k * kk_ref[...]
    kkn = kkr / jnp.maximum(jnp.sqrt(_head_sum(kkr * kkr, ones)), 1e-12)
    kkn_out[...] = kkn

    ga_out[...] = proj("ga")

    ke_sum = jnp.zeros_like(k)
    for z in range(2):
        wz = w0_ref[z] + lo[:, z * B_WIDTH:(z + 1) * B_WIDTH]
        lw_out[z] = -math.exp(-0.5) * _sigmoid(wz)
        az = _sigmoid(a0_ref[z] + lo[:, (2 + z) * B_WIDTH:(3 + z) * B_WIDTH])
        ke = k * (1.0 + (az - 1.0) * ka_ref[...])
        ke_out[z] = ke
        b_out[z] = kkn * az
        ke_sum = ke_sum + ke
        if z == 0:
            gb_out[...] = proj("gb")
    bonus_out[...] = _head_sum(r * (0.5 * ke_sum) * rk_ref[...], ones) * v


def _even_in(x2d, shift, scale, gain, w_bf16, t, rope_tabs, emit_t, mu, lora, w0, a0, k_k, k_a, r_k, ones):
    n = x2d.shape[0]
    tile = ROW_TILE if emit_t else OUT_TILE
    assert t % tile == 0 and (not emit_t or t == tile)
    nt = n // tile
    tiles_per_seq = t // tile
    h8 = tile // HALO
    use_rope = rope_tabs is not None
    row = lambda i: (i, 0)
    const2 = lambda i: (0, 0)
    const3 = lambda i: (0, 0, 0)
    per_b = _mod_index(shift, tiles_per_seq)
    vec = pl.BlockSpec((1, B_WIDTH), const2)
    in_specs = [pl.BlockSpec((tile, D_MODEL), row),
                pl.BlockSpec((HALO, D_MODEL), lambda i: (jnp.maximum(i * h8 - 1, 0), 0)),
                pl.BlockSpec((HALO, D_MODEL), lambda i: (jnp.minimum((i + 1) * h8, n // HALO - 1), 0)),
                pl.BlockSpec((None, 1, D_MODEL), per_b),
                pl.BlockSpec((None, 1, D_MODEL), per_b),
                pl.BlockSpec((1, D_MODEL), const2),
                pl.BlockSpec(w_bf16.shape, const2)]
    args = [x2d, x2d, x2d, shift, scale, gain, w_bf16]
    if use_rope:
        pos = lambda i: (i % tiles_per_seq, 0)
        in_specs += [pl.BlockSpec((tile, LANES), pos), pl.BlockSpec((tile, LANES), pos)]
        args += list(rope_tabs)
    in_specs += [pl.BlockSpec((1, B_SHIFT_WIDTH), const2),
                 pl.BlockSpec(lora.shape, const2),
                 pl.BlockSpec((2, 1, B_WIDTH), const3),
                 pl.BlockSpec((2, 1, B_WIDTH), const3),
                 vec, vec, vec,
                 pl.BlockSpec((B_WIDTH, B_WIDTH), const2)]
    args += [mu, lora, w0, a0, k_k, k_a, r_k, ones]
    wide = pl.BlockSpec((tile, B_WIDTH), row)
    narrow = pl.BlockSpec((tile, A_KV_WIDTH), row)
    dual = pl.BlockSpec((2, tile, B_WIDTH), lambda i: (0, i, 0))
    f_wide = jax.ShapeDtypeStruct((n, B_WIDTH), F32)
    f_narrow = jax.ShapeDtypeStruct((n, A_KV_WIDTH), F32)
    f_dual = jax.ShapeDtypeStruct((2, n, B_WIDTH), F32)
    out_specs = [wide, narrow, narrow, wide, wide, wide, wide, wide, dual, dual, dual, wide]
    out_shape = [jax.ShapeDtypeStruct((n, A_WIDTH), BF16), f_narrow, f_narrow, f_wide, f_wide,
                 f_wide, f_wide, f_wide, f_dual, f_dual, f_dual, f_wide]
    if emit_t:
        out_specs += [pl.BlockSpec((None, A_KV_WIDTH, tile), lambda i: (i, 0, 0))] * 2
        out_shape += [jax.ShapeDtypeStruct((nt, A_KV_WIDTH, tile), F32)] * 2
    return pl.pallas_call(
        functools.partial(_even_in_kernel, use_rope=use_rope, emit_t=emit_t, tiles_per_seq=tiles_per_seq),
        grid=(nt,),
        in_specs=in_specs,
        out_specs=out_specs,
        out_shape=out_shape,
        compiler_params=_params(1),
        name="even_in",
    )(*args)


def _rwkv_scan_kernel(*refs, n_chunks, from_zero):
    (rf_ref, kf_ref, vf_ref, rb_ref, kb_ref, vb_ref, lwf_ref, bf_ref, kef_ref,
     lwb_ref, bb_ref, keb_ref) = refs[:12]
    if from_zero:
        yf_ref, yb_ref, sout_ref, s_scr = refs[12:]
    else:
        s0_ref, yf_ref, yb_ref, s_scr = refs[12:]
    c = pl.program_id(1)

    @pl.when(c == 0)
    def _():
        if from_zero:
            s_scr[...] = jnp.zeros_like(s_scr)
        else:
            s_scr[...] = s0_ref[...]

    row = lax.broadcasted_iota(jnp.int32, (CHUNK, LANES), 0)
    li = lax.broadcasted_iota(jnp.int32, (CHUNK, LANES), 1) % CHUNK
    blk16 = (row // 16) == (li // 16)
    blk32 = (row // 32) == (li // 32)
    eye = jnp.where(row == li, 1.0, 0.0)
    srow = lax.broadcasted_iota(jnp.int32, (PAIR, PAIR), 0)
    scol = lax.broadcasted_iota(jnp.int32, (PAIR, PAIR), 1)
    same_head = (srow // CHUNK) == (scol // CHUNK)
    tr = lax.broadcasted_iota(jnp.int32, (CHUNK, CHUNK), 0)
    tc = lax.broadcasted_iota(jnp.int32, (CHUNK, CHUNK), 1)
    dirs = ((rf_ref, kf_ref, vf_ref, lwf_ref, bf_ref, kef_ref, yf_ref),
            (rb_ref, kb_ref, vb_ref, lwb_ref, bb_ref, keb_ref, yb_ref))

    def advance(s):
        chains = [(z, p) for z in range(2) for p in range(B_HEADS // 2)]
        sls = [slice(p * LANES, (p + 1) * LANES) for _, p in chains]
        before, before_eq, tot, kkd, rd, bg, kg, vv, sc = [], [], [], [], [], [], [], [], []
        for z in range(2):
            r_ref, kk_ref, v_ref, lw_ref, b_ref, ke_ref, _ = dirs[z]
            diff = (li - row) if z else (row - li)
            tri = jnp.where(((tc - tr) if z else (tr - tc)) >= 0, 1.0, 0.0).astype(BF16)
            lw_all = lw_ref[s]
            cum_all = _split_dot(tri, lw_all)
            for p in range(B_HEADS // 2):
                sl = slice(p * LANES, (p + 1) * LANES)
                lw = lw_all[:, sl]
                cum = cum_all[:, sl]
                tt = cum[0:1] if z else cum[CHUNK - 1:CHUNK]
                b = b_ref[s, :, sl]
                ke = ke_ref[s, :, sl]
                g_inv = jnp.exp(-cum)
                g_rest = jnp.exp(tt - cum)
                before.append(diff > 0)
                before_eq.append(diff >= 0)
                tot.append(tt)
                kkd.append(kk_ref[s, :, sl] * jnp.exp(cum - lw))
                rd.append(r_ref[s, :, sl] * jnp.exp(cum))
                bg.append(b * g_rest)
                kg.append(ke * g_rest)
                vv.append(v_ref[s, :, sl])
                sc.append(_bdot(jnp.concatenate([kkd[-1], rd[-1]], 0),
                                jnp.concatenate([_stack_pair(b * g_inv), _stack_pair(ke * g_inv)], 0), NT))
            yield
        lb = [jnp.where(m, x[:CHUNK, :LANES], 0.0) for m, x in zip(before, sc)]
        mb = [jnp.where(m, x[CHUNK:, :LANES], 0.0) for m, x in zip(before_eq, sc)]
        lv = [_bdot(jnp.concatenate([jnp.where(m, x[:CHUNK, LANES:], 0.0),
                                     jnp.where(me, x[CHUNK:, LANES:], 0.0)], 0), _stack_pair(v))
              for m, me, x, v in zip(before, before_eq, sc, vv)]
        yield

        pw = [jnp.where(blk16, x, 0.0) for x in lb]
        tinv = [eye - x for x in pw]
        for _ in range(3):
            pw = [_bdot(x, _stack_pair(x)) for x in pw]
            yield
            tinv = [t + _bdot(t, _stack_pair(x)) for t, x in zip(tinv, pw)]
            yield
        for off in (blk32 & ~blk16, ~blk32):
            tmp = [_bdot(t, _stack_pair(jnp.where(off, x, 0.0))) for t, x in zip(tinv, lb)]
            yield
            tinv = [t - _bdot(m, _stack_pair(t)) for t, m in zip(tinv, tmp)]
            yield

        w12 = [_bdot(t, jnp.concatenate([_stack_pair(k), _stack_pair(x[:CHUNK])], 1))
               for t, k, x in zip(tinv, kkd, lv)]
        yield
        s_old = [s_scr[s, z, p] for z, p in chains]
        ws = [_bdot(jnp.concatenate([w[:, :LANES], r], 0), st, NT) for w, r, st in zip(w12, rd, s_old)]
        yield
        u = [-(a[:CHUNK] + w[:, LANES:]) for a, w in zip(ws, w12)]
        mbu = [_bdot(m, _stack_pair(x)) for m, x in zip(mb, u)]
        upd = [_bdot(jnp.concatenate([x, v], 0).T, jnp.concatenate([b_, k_], 0))
               for x, v, b_, k_ in zip(u, vv, bg, kg)]
        yield
        for i, (z, p) in enumerate(chains):
            dirs[z][6][s, :, sls[i]] = ws[i][CHUNK:] + mbu[i] + lv[i][CHUNK:]
            s_scr[s, z, p] = s_old[i] * jnp.exp(tot[i]) + jnp.where(same_head, upd[i], 0.0)

    live = [advance(s) for s in range(rf_ref.shape[0])]
    while live:
        live = [gen for gen in live if next(gen, "done") != "done"]

    if from_zero:
        @pl.when(c == n_chunks - 1)
        def _():
            for s in range(s_scr.shape[0]):
                for z in range(2):
                    for p in range(B_HEADS // 2):
                        st = s_scr[s, z, p]
                        sout_ref[s, z, 2 * p] = st[:HEAD_DIM, :HEAD_DIM]
                        sout_ref[s, z, 2 * p + 1] = pltpu.roll(st[HEAD_DIM:, :], HEAD_DIM, 1)[:, :HEAD_DIM]


def _rwkv_scan(r, kkn, v, lw, b, ke, s0, bn, t):
    nc = t // CHUNK
    npair = B_HEADS // 2
    ns = min(SCAN_SEQS, bn)
    fwd = pl.BlockSpec((ns, CHUNK, B_WIDTH), lambda g, c: (g, c, 0))
    bwd = pl.BlockSpec((ns, CHUNK, B_WIDTH), lambda g, c: (g, nc - 1 - c, 0))
    dfwd = pl.BlockSpec((None, ns, CHUNK, B_WIDTH), lambda g, c: (0, g, c, 0))
    dbwd = pl.BlockSpec((None, ns, CHUNK, B_WIDTH), lambda g, c: (1, g, nc - 1 - c, 0))
    r3, k3, v3 = (a.reshape(bn, t, B_WIDTH) for a in (r, kkn, v))
    lw4, b4, ke4 = (a.reshape(2, bn, t, B_WIDTH) for a in (lw, b, ke))
    seq = jax.ShapeDtypeStruct((bn, t, B_WIDTH), F32)
    in_specs = [fwd, fwd, fwd, bwd, bwd, bwd, dfwd, dfwd, dfwd, dbwd, dbwd, dbwd]
    args = [r3, k3, v3, r3, k3, v3, lw4, b4, ke4, lw4, b4, ke4]
    out_specs = [fwd, bwd]
    out_shape = [seq, seq]
    if s0 is None:
        out_specs.append(pl.BlockSpec((ns, 2, B_HEADS, HEAD_DIM, HEAD_DIM), lambda g, c: (g, 0, 0, 0, 0)))
        out_shape.append(jax.ShapeDtypeStruct((bn, 2, B_HEADS, HEAD_DIM, HEAD_DIM), F32))
    else:
        in_specs.append(pl.BlockSpec((ns, 2, npair, PAIR, PAIR), lambda g, c: (g, 0, 0, 0, 0)))
        args.append(s0)
    return pl.pallas_call(
        functools.partial(_rwkv_scan_kernel, n_chunks=nc, from_zero=s0 is None),
        grid=(bn // ns, nc),
        in_specs=in_specs,
        out_specs=out_specs,
        out_shape=out_shape,
        scratch_shapes=[pltpu.VMEM((ns, 2, npair, PAIR, PAIR), F32)],
        compiler_params=_params(2),
        name="rwkv_scan",
    )(*args)


def _post(x, z, gate, gain):
    zn = z * lax.rsqrt(jnp.mean(z * z, -1, keepdims=True) + NORM_EPS)
    return x + gate * (zn * gain)


def _even_out_kernel(ya_ref, ga_ref, yf_ref, yb_ref, bonus_ref, gb_ref, lnw_ref, lnb_ref, ones_ref,
                     x_ref, gate_ref, gain_ref, w_ref, o_ref):
    y = yf_ref[...] + yb_ref[...]
    ones = ones_ref[...]
    mean = _head_sum(y, ones) * (1.0 / HEAD_DIM)
    d = y - mean
    var = _head_sum(d * d, ones) * (1.0 / HEAD_DIM)
    yb = (d * lax.rsqrt(var + RWKV_LN_EPS)) * lnw_ref[...] + lnb_ref[...] + bonus_ref[...]
    za = (ya_ref[...] * _silu(ga_ref[...])).astype(BF16)
    zb = (yb * _silu(gb_ref[...])).astype(BF16)
    z = (jnp.dot(za, w_ref[0:A_WIDTH, :], preferred_element_type=F32)
         + jnp.dot(zb, w_ref[A_WIDTH:, :], preferred_element_type=F32))
    o_ref[...] = _post(x_ref[...], z, gate_ref[...], gain_ref[...])


def _even_out(ya, ga, yf, yb, bonus, gb, ln_w, ln_b, ones, x2d, gate, gain, w_bf16, t):
    n = x2d.shape[0]
    row = lambda i: (i, 0)
    const2 = lambda i: (0, 0)
    half = pl.BlockSpec((OUT_TILE, B_WIDTH), row)
    full = pl.BlockSpec((OUT_TILE, D_MODEL), row)
    vec = pl.BlockSpec((1, B_WIDTH), const2)
    assert gate.shape[0] == 1 or t % OUT_TILE == 0
    return pl.pallas_call(
        _even_out_kernel,
        grid=(n // OUT_TILE,),
        in_specs=[half, half, half, half, half, half, vec, vec,
                  pl.BlockSpec((B_WIDTH, B_WIDTH), const2),
                  full,
                  pl.BlockSpec((None, 1, D_MODEL), _mod_index(gate, t // OUT_TILE)),
                  pl.BlockSpec((1, D_MODEL), const2),
                  pl.BlockSpec((D_MODEL, D_MODEL), const2)],
        out_specs=full,
        out_shape=jax.ShapeDtypeStruct((n, D_MODEL), F32),
        compiler_params=_params(1),
        name="even_out",
    )(ya, ga, yf, yb, bonus, gb, ln_w, ln_b, ones, x2d, gate, gain, w_bf16)


def _odd_out_kernel(o_ref_in, gc_ref, x_ref, gate_ref, gain_ref, w_ref, o_ref):
    zc = (o_ref_in[...] * _silu(gc_ref[...])).astype(BF16)
    z = jnp.dot(zc, w_ref[...], preferred_element_type=F32)
    o_ref[...] = _post(x_ref[...], z, gate_ref[...], gain_ref[...])


def _odd_out(o, gc, x2d, gate, gain, w_bf16, t):
    n = x2d.shape[0]
    row = lambda i: (i, 0)
    const2 = lambda i: (0, 0)
    full = pl.BlockSpec((OUT_TILE, D_MODEL), row)
    assert gate.shape[0] == 1 or t % OUT_TILE == 0
    return pl.pallas_call(
        _odd_out_kernel,
        grid=(n // OUT_TILE,),
        in_specs=[full, full, full,
                  pl.BlockSpec((None, 1, D_MODEL), _mod_index(gate, t // OUT_TILE)),
                  pl.BlockSpec((1, D_MODEL), const2),
                  pl.BlockSpec((D_MODEL, D_MODEL), const2)],
        out_specs=full,
        out_shape=jax.ShapeDtypeStruct((n, D_MODEL), F32),
        compiler_params=_params(1),
        name="odd_out",
    )(o, gc, x2d, gate, gain, w_bf16)


def _lambda(lq1_ref, lk1_ref, lq2_ref, lk2_ref, lam_init):
    s1 = jnp.sum(lq1_ref[...] * lk1_ref[...], -1, keepdims=True)
    s2 = jnp.sum(lq2_ref[...] * lk2_ref[...], -1, keepdims=True)
    return jnp.exp(s1) - jnp.exp(s2) + lam_init


def _subln(o, gain, lam_init):
    on = o * lax.rsqrt(jnp.mean(o * o, -1, keepdims=True) + SUBLN_EPS)
    return (on * gain) * (1.0 - lam_init)


def _diff_ctx_kernel(lq1_ref, lk1_ref, lq2_ref, lk2_ref, sub_ref, q_ref, k_ref, v_ref,
                     gc_ref, x_ref, gate_ref, gain_ref, w_ref, y_ref, o_scr, *, lam_init):
    nseq, t = q_ref.shape[:2]
    lam = _lambda(lq1_ref, lk1_ref, lq2_ref, lk2_ref, lam_init)
    units = [(s, slice(h * LANES, (h + 1) * LANES)) for s in range(nseq) for h in range(C_HEADS)]
    k = [k_ref[s].reshape(t, C_WIDTH) for s in range(nseq)]
    v = [v_ref[s].reshape(t, C_WIDTH) for s in range(nseq)]
    s2 = [_bdot(_stack_pair(q_ref[s, :, sl]), k[s][:, sl], NT)
          for s, sl in units]
    for i, (s, sl) in enumerate(units):
        e = jnp.exp2(s2[i] - jnp.max(s2[i], -1, keepdims=True))
        ov = _bdot(e, v[s][:, sl]) * (1.0 / jnp.sum(e, -1, keepdims=True))
        o_scr[s * t:(s + 1) * t, sl] = _subln(ov[:t] - lam * ov[t:], sub_ref[...], lam_init)
    gc = gc_ref[...].reshape(nseq * t, C_WIDTH)
    zc = (o_scr[...] * _silu(gc)).astype(BF16)
    z = jnp.dot(zc, w_ref[...], preferred_element_type=F32)
    y = _post(x_ref[...].reshape(nseq * t, D_MODEL), z, gate_ref[...], gain_ref[...])
    y_ref[...] = y.reshape(nseq, t, D_MODEL)


def _diff_context_out(lams, subln, q, k, v, gc, x2d, gate, gain, w_bf16, bn, t, lam_init):
    ns = DIFF_CTX_SEQS
    seq = pl.BlockSpec((ns, t, C_WIDTH), lambda b: (b, 0, 0))
    heads = pl.BlockSpec((ns, t, C_HEADS, LANES), lambda b: (b, 0, 0, 0))
    small = pl.BlockSpec((1, C_QK_DIM), lambda b: (0, 0))
    const2 = lambda b: (0, 0)
    to3 = lambda a: a.reshape((bn, t) + a.shape[1:])
    return pl.pallas_call(
        functools.partial(_diff_ctx_kernel, lam_init=lam_init),
        grid=(bn // ns,),
        in_specs=[small, small, small, small, pl.BlockSpec((1, C_V_DIM), const2),
                  seq, heads, heads, seq, seq,
                  pl.BlockSpec((None, 1, D_MODEL), lambda b: (0, 0, 0)),
                  pl.BlockSpec((1, D_MODEL), const2),
                  pl.BlockSpec((D_MODEL, D_MODEL), const2)],
        out_specs=seq,
        out_shape=jax.ShapeDtypeStruct((bn, t, D_MODEL), F32),
        scratch_shapes=[pltpu.VMEM((ns * t, C_WIDTH), F32)],
        compiler_params=_params(1),
        name="diff_context",
    )(*lams, subln, to3(q), to3(k), to3(v), to3(gc), to3(x2d), gate, gain, w_bf16).reshape(bn * t, D_MODEL)


def _diff_lat_kernel(lq1_ref, lk1_ref, lq2_ref, lk2_ref, sub_ref, q_ref, k_ref, v_ref, kx_ref, vx_ref,
                     o_ref, *, lam_init):
    lam = _lambda(lq1_ref, lk1_ref, lq2_ref, lk2_ref, lam_init)
    kx = kx_ref[...].astype(BF16)
    k = k_ref[...].astype(BF16)
    tq = DIFF_Q_SUB
    subs = range(q_ref.shape[0] // tq)
    scores = []
    for i in subs:
        qs = _stack_pair(q_ref[i * tq:(i + 1) * tq, :])
        scores.append((_bdot(qs, kx, NT), _bdot(qs, k, NT)))
    for i in subs:
        sx, sl = scores[i]
        m = jnp.maximum(jnp.max(sx, -1, keepdims=True), jnp.max(sl, -1, keepdims=True))
        ex = jnp.exp2(sx - m)
        el = jnp.exp2(sl - m)
        den = jnp.sum(ex, -1, keepdims=True) + jnp.sum(el, -1, keepdims=True)
        ratio = lam * den[:tq] / den[tq:]
        o = (_bdot(ex[:tq] - ex[tq:] * ratio, vx_ref[...])
             + _bdot(el[:tq] - el[tq:] * ratio, v_ref[...])) * (1.0 / den[:tq])
        o_ref[i * tq:(i + 1) * tq, :] = _subln(o, sub_ref[...], lam_init)


def _diff_latent(lams, subln, q, k, v, k_ctx, v_ctx, bn, t, lam_init):
    nb = t // DIFF_Q_TILE
    nctx = k_ctx.shape[1]
    small = pl.BlockSpec((1, C_QK_DIM), lambda b, h, n: (0, 0))
    qspec = pl.BlockSpec((None, DIFF_Q_TILE, LANES), lambda b, h, n: (b, n, h))
    kvspec = pl.BlockSpec((None, t, LANES), lambda b, h, n: (b, 0, h))
    cxspec = pl.BlockSpec((None, nctx, LANES), lambda b, h, n: (b, 0, h))
    return pl.pallas_call(
        functools.partial(_diff_lat_kernel, lam_init=lam_init),
        grid=(bn, C_HEADS, nb),
        in_specs=[small, small, small, small,
                  pl.BlockSpec((1, C_V_DIM), lambda b, h, n: (0, 0)),
                  qspec, kvspec, kvspec, cxspec, cxspec],
        out_specs=qspec,
        out_shape=jax.ShapeDtypeStruct((bn, t, C_WIDTH), F32),
        compiler_params=_params(3),
        name="diff_latent",
    )(*lams, subln, q.reshape(bn, t, C_WIDTH), k.reshape(bn, t, C_WIDTH), v.reshape(bn, t, C_WIDTH),
      k_ctx, v_ctx)


def _rope_tables(t):
    nf = HEAD_DIM // 4
    inv = 1.0 / (ROPE_BASE ** (jnp.arange(nf, dtype=F32) / nf))
    pos = jnp.arange(t)
    ang_r = (pos // GRID_W).astype(F32)[:, None] * inv[None]
    ang_c = (pos % GRID_W).astype(F32)[:, None] * inv[None]
    ang = jnp.concatenate([ang_r, ang_r, ang_c, ang_c], -1)
    cos, sin = jnp.cos(ang), jnp.sin(ang)
    sign = jnp.where((jnp.arange(HEAD_DIM) // nf) % 2 == 0, -1.0, 1.0).astype(F32)
    return jnp.tile(cos, (1, 2)), jnp.tile(sin * sign[None], (1, 2))


EVEN_SEGS = ((0, 512), (512, 640), (640, 768), (768, 1280), (1280, 2944), (2944, 3456))
ODD_SEGS = ((0, 1024), (1024, 2048), (2048, 3072), (3072, 4096))


def _pair_states(st):
    bn = st.shape[0]
    s = st.reshape(bn, 2, 4, 2, HEAD_DIM, HEAD_DIM)
    z = jnp.zeros_like(s[:, :, :, 0])
    top = jnp.concatenate([s[:, :, :, 0], z], -1)
    bot = jnp.concatenate([z, s[:, :, :, 1]], -1)
    return jnp.concatenate([top, bot], -2)


def kernel(x_prompt, x_sample, cache_a_k, cache_a_v, state_rwkv, cache_c_k, cache_c_v, c, c_ctx,
           ada_w, ada_b, norm_pre, norm_post, w_out, even_w_in, a_sink, b_mu, b_w0, b_w2, b_a0,
           b_a2, b_kk, b_ka, b_rk, b_ln_w, b_ln_b, odd_w_in, c_lq1, c_lk1, c_lq2, c_lk2, c_subln):
    bp, tp, _ = x_prompt.shape
    bs, ts, _ = x_sample.shape
    past = cache_a_k.shape[2]

    cvecs = jnp.concatenate([c_ctx[None], c, jnp.zeros((8 - 1 - bs, D_MODEL), F32)], 0)
    mods = _modulation(cvecs, 1 + bs, ada_w, ada_b)
    rope_tabs = _rope_tables(ts)
    ones = (jnp.arange(B_WIDTH)[:, None] // HEAD_DIM == jnp.arange(B_WIDTH)[None] // HEAD_DIM).astype(BF16)

    xp = x_prompt.reshape(bp * tp, D_MODEL)
    xs = x_sample.reshape(bs * ts, D_MODEL)

    def mod_rows(layer, rows):
        m = mods[layer, rows[0]:rows[1]][:, None, :]
        return m[..., :D_MODEL], m[..., D_MODEL:2 * D_MODEL], m[..., 2 * D_MODEL:]

    e = 0
    w_in0 = even_w_in[e].astype(BF16)
    w_out0 = w_out[0].astype(BF16)
    gain_pre = norm_pre[0][None]
    gain_post = norm_post[0][None]
    zpad = jnp.zeros((DECAY_LORA, B_WIDTH), F32)
    lora = jnp.concatenate([
        jnp.concatenate([b_w2[e, 0], b_w2[e, 1], zpad, zpad], 1),
        jnp.concatenate([zpad, zpad, b_a2[e, 0], b_a2[e, 1]], 1)], 0).astype(BF16)
    feat_args = (b_mu[e][None], lora, b_w0[e][:, None, :], b_a0[e][:, None, :], b_kk[e][None],
                 b_ka[e][None], b_rk[e][None], ones)

    def even_layer(x2d, bn, t, rows, ctx):
        shift, scale, gate = mod_rows(0, rows)
        (qa, ka, va, ga, gb, r, kkn, v, lw, b, ke, bonus, *kv_t) = _even_in(
            x2d, shift, scale, gain_pre, w_in0, t, None if ctx is None else rope_tabs, ctx is None,
            *feat_args)
        if ctx is None:
            ya = _gqa_context(a_sink[e], qa, ka, va, bn, t)
            s0 = None
        else:
            k_ctx, v_ctx, st0 = ctx
            ya = _gqa_window(a_sink[e], qa, ka, va, k_ctx, v_ctx, bn, t)
            s0 = _pair_states(st0)
        yf, yb, *s_fin = _rwkv_scan(r, kkn, v, lw, b, ke, s0, bn, t)
        y = _even_out(ya.reshape(bn * t, A_WIDTH), ga, yf.reshape(bn * t, B_WIDTH),
                      yb.reshape(bn * t, B_WIDTH), bonus, gb,
                      b_ln_w[e][None], b_ln_b[e][None], ones, x2d, gate, gain_post, w_out0, t)
        return y, kv_t, s_fin

    yp, (ka_t, va_t), st_p = even_layer(xp, bp, tp, (0, 1), None)
    feat_major = lambda a: jnp.transpose(a, (0, 2, 3, 1)).reshape(bs, A_KV_WIDTH, past)
    ctx_a = (feat_major(cache_a_k[:, e]), feat_major(cache_a_v[:, e]), state_rwkv[:, e])
    ys, _, _ = even_layer(xs, bs, ts, (1, 1 + bs), ctx_a)

    def cache_a(x_t):
        return jnp.transpose(x_t.reshape(bp, A_KV_HEADS, HEAD_DIM, tp), (0, 3, 1, 2))[:, None]

    o = 0
    lam_init = 0.8 - 0.6 * math.exp(-0.3 * 1)
    w_in1 = odd_w_in[o].astype(BF16)
    w_out1 = w_out[1].astype(BF16)
    gain_pre1 = norm_pre[1][None]
    gain_post1 = norm_post[1][None]
    lams = (c_lq1[o][None], c_lk1[o][None], c_lq2[o][None], c_lk2[o][None])
    subln = c_subln[o][None]

    def odd_layer(x2d, bn, t, rows, ctx):
        shift, scale, gate = mod_rows(1, rows)
        qc, kc, vc, gc = _proj_in(x2d, shift, scale, gain_pre1, w_in1, ODD_SEGS, (0, 1), t,
                                  None if ctx is None else rope_tabs,
                                  head_segs=(1, 2) if ctx is None else (),
                                  bf16_segs=() if ctx is None else (1, 2))
        if ctx is None:
            y = _diff_context_out(lams, subln, qc, kc, vc, gc, x2d, gate, gain_post1, w_out1, bn, t, lam_init)
        else:
            k_ctx, v_ctx = ctx
            oc = _diff_latent(lams, subln, qc, kc, vc, k_ctx, v_ctx, bn, t, lam_init)
            y = _odd_out(oc.reshape(bn * t, C_WIDTH), gc, x2d, gate, gain_post1, w_out1, t)
        return y, kc, vc

    yp2, kc_p, vc_p = odd_layer(yp, bp, tp, (0, 1), None)
    ctx_c = (cache_c_k[:, o].reshape(bs, past, C_WIDTH), cache_c_v[:, o].reshape(bs, past, C_WIDTH))
    ys2, _, _ = odd_layer(ys, bs, ts, (1, 1 + bs), ctx_c)

    return (yp2.reshape(bp, tp, D_MODEL),
            ys2.reshape(bs, ts, D_MODEL),
            cache_a(ka_t),
            cache_a(va_t),
            st_p[0][:, None],
            kc_p.reshape(bp, 1, tp, C_HEADS, 2 * C_QK_DIM),
            vc_p.reshape(bp, 1, tp, C_HEADS, C_V_DIM))
```
